```python
import math
import jax
import jax.numpy as jnp
from jax import lax
import numpy as np

D_MODEL = 1024
BATCH = 8
SEQ = 2048
DEPTH = 2

GRID_W = 64
CTX_LEN = 256
HEAD_DIM = 64
BLOCK = 128
WINDOW = 128
ROPE_BASE = 10000.0
AX_FREQS = HEAD_DIM // 4
NEG_INF = -1e30
A_HEADS = D_MODEL // 128
A_KV_HEADS = 2
A_WIDTH = A_HEADS * HEAD_DIM
A_KV_WIDTH = A_KV_HEADS * HEAD_DIM
B_WIDTH = D_MODEL // 2
S5_GROUP = 16
S5_GROUPS = B_WIDTH // S5_GROUP
S5_STATE = 64
C_HEADS = D_MODEL // 256
C_WIDTH = C_HEADS * 2 * HEAD_DIM
N_BRANCH = 3
IN_WIDTHS = (A_WIDTH, A_KV_WIDTH, A_KV_WIDTH, A_WIDTH,
             B_WIDTH, B_WIDTH,
             C_WIDTH, C_WIDTH, C_WIDTH, C_WIDTH,
             N_BRANCH * D_MODEL)
IN_WIDTH = sum(IN_WIDTHS)

kernel_name = "hybrid_swa_s5_diffattn_prefix_block"


def rms_norm(x, w, eps=1e-6):
    xf = x.astype(jnp.float32)
    xf = xf * lax.rsqrt(jnp.mean(xf * xf, axis=-1, keepdims=True) + eps)
    return (xf * w.astype(jnp.float32)).astype(x.dtype)


def modulated_projection(h, shift, scale, norm_w, w_in):
    hn = rms_norm(h, norm_w) * (1.0 + scale) + shift
    z = hn @ w_in
    parts = []
    start = 0
    for width in IN_WIDTHS:
        parts.append(z[..., start:start + width])
        start += width
    return parts


def heads(t, n):
    return t.reshape(t.shape[:-1] + (n, -1))


def axial_rope_tables(n_tokens):
    rows = n_tokens // GRID_W
    pos_r = jnp.repeat(jnp.arange(rows, dtype=jnp.float32), GRID_W)
    pos_c = jnp.tile(jnp.arange(GRID_W, dtype=jnp.float32), rows)
    inv = ROPE_BASE ** (-jnp.arange(AX_FREQS, dtype=jnp.float32) / AX_FREQS)
    ang_r = pos_r[:, None] * inv[None]
    ang_c = pos_c[:, None] * inv[None]
    ang = jnp.concatenate([ang_r, ang_r, ang_c, ang_c], axis=-1)
    return jnp.cos(ang), jnp.sin(ang)


def apply_rope(t, cos, sin):
    shape = (t.shape[1],) + (1,) * (t.ndim - 3) + (HEAD_DIM,)
    cs = cos.reshape(shape)
    sn = sin.reshape(shape)
    tf = t.astype(jnp.float32)
    th = tf.reshape(tf.shape[:-1] + (2, 2, AX_FREQS))
    rot = jnp.stack([-th[..., 1, :], th[..., 0, :]], axis=-2).reshape(tf.shape)
    return (tf * cs + rot * sn).astype(t.dtype)


def swa_context(q, k, v, sink):
    b, n, h, dh = q.shape
    hk = k.shape[2]
    g = h // hk
    qg = q.reshape(b, n, hk, g, dh)
    s = jnp.einsum('bqkgd,bckd->bkgqc', qg, k).astype(jnp.float32) * (dh ** -0.5)
    s_sink = jnp.broadcast_to(sink.astype(jnp.float32).reshape(1, hk, g, 1, 1), s.shape[:-1] + (1,))
    p = jax.nn.softmax(jnp.concatenate([s, s_sink], axis=-1), axis=-1)[..., :-1]
    o = jnp.einsum('bkgqc,bckd->bqkgd', p.astype(v.dtype), v)
    return o.reshape(b, n, h * dh)


def swa_latent(q, k, v, k_ctx, v_ctx, sink):
    b, n, h, dh = q.shape
    hk = k.shape[2]
    g = h // hk
    nb = n // BLOCK
    scale = dh ** -0.5
    qb = q.reshape(b, nb, BLOCK, hk, g, dh)

    def band(t):
        tp = jnp.pad(t, ((0, 0), (BLOCK, BLOCK), (0, 0), (0, 0))).reshape(b, nb + 2, BLOCK, hk, dh)
        return jnp.concatenate([tp[:, :-2], tp[:, 1:-1], tp[:, 2:]], axis=2)

    kb, vb = band(k), band(v)
    s_loc = jnp.einsum('bnqkgd,bnmkd->bnkgqm', qb, kb).astype(jnp.float32) * scale
    s_ctx = jnp.einsum('bnqkgd,bckd->bnkgqc', qb, k_ctx).astype(jnp.float32) * scale
    r = jnp.arange(BLOCK)[:, None]
    m = jnp.arange(3 * BLOCK)[None, :]
    rel = m - r
    in_band = (rel >= BLOCK - WINDOW) & (rel <= BLOCK + WINDOW)
    key_pos = (jnp.arange(nb) * BLOCK - BLOCK)[:, None, None] + m[None]
    mask = in_band[None] & (key_pos >= 0) & (key_pos < n)
    s_loc = jnp.where(mask[None, :, None, None], s_loc, NEG_INF)
    s_sink = jnp.broadcast_to(sink.astype(jnp.float32).reshape(1, 1, hk, g, 1, 1), s_loc.shape[:-1] + (1,))
    p = jax.nn.softmax(jnp.concatenate([s_loc, s_ctx, s_sink], axis=-1), axis=-1)
    n_ctx = k_ctx.shape[1]
    w_loc = p[..., :3 * BLOCK].astype(v.dtype)
    w_ctx = p[..., 3 * BLOCK:3 * BLOCK + n_ctx].astype(v.dtype)
    o = (jnp.einsum('bnkgqm,bnmkd->bnqkgd', w_loc, vb)
         + jnp.einsum('bnkgqc,bckd->bnqkgd', w_ctx, v_ctx))
    return o.reshape(b, n, h * dh)


def diff_attention(q, k, v, lam, subln_w, lam_init):
    b, lq, h, _, dh = q.shape
    nb = lq // BLOCK
    scale = dh ** -0.5
    qb = jnp.moveaxis(q.reshape(b, nb, BLOCK, h, 2, dh), 1, 0)

    def one_block(qblk):
        s = jnp.einsum('bqhjd,bkhjd->bhjqk', qblk, k).astype(jnp.float32) * scale
        p = jax.nn.softmax(s, axis=-1)
        a = p[:, :, 0] - lam * p[:, :, 1]
        return jnp.einsum('bhqk,bkhe->bqhe', a.astype(v.dtype), v)

    o = lax.map(one_block, qb)
    o = jnp.moveaxis(o, 0, 1).reshape(b, lq, h, 2 * dh)
    o = rms_norm(o, subln_w) * (1.0 - lam_init)
    return o.reshape(b, lq, h * 2 * dh)


def s5_discretize(a_re, a_im, log_dt, b_re, b_im):
    a_re = a_re.astype(jnp.float32)
    a_im = a_im.astype(jnp.float32)
    dt = jnp.exp(log_dt.astype(jnp.float32))[:, None]
    mag = jnp.exp(a_re * dt)
    abar_re = mag * jnp.cos(a_im * dt)
    abar_im = mag * jnp.sin(a_im * dt)
    den = a_re * a_re + a_im * a_im
    nr = abar_re - 1.0
    ni = abar_im
    f_re = ((nr * a_re + ni * a_im) / den)[..., None]
    f_im = ((ni * a_re - nr * a_im) / den)[..., None]
    b_re = b_re.astype(jnp.float32)
    b_im = b_im.astype(jnp.float32)
    bbar_re = f_re * b_re - f_im * b_im
    bbar_im = f_re * b_im + f_im * b_re
    return abar_re, abar_im, bbar_re, bbar_im


def s5_states(u, abar_re, abar_im, bbar_re, bbar_im, h0, reverse):
    bu_re = jnp.einsum('blgh,gph->blgp', u, bbar_re)
    bu_im = jnp.einsum('blgh,gph->blgp', u, bbar_im)
    if h0 is not None:
        h0_re, h0_im = h0
        start = -1 if reverse else 0
        bu_re = bu_re.at[:, start].add(abar_re * h0_re - abar_im * h0_im)
        bu_im = bu_im.at[:, start].add(abar_re * h0_im + abar_im * h0_re)
    n = u.shape[1]
    a_re = jnp.broadcast_to(abar_re, (1, n) + abar_re.shape)
    a_im = jnp.broadcast_to(abar_im, (1, n) + abar_im.shape)

    def combine(e1, e2):
        a1r, a1i, b1r, b1i = e1
        a2r, a2i, b2r, b2i = e2
        return (a2r * a1r - a2i * a1i,
                a2r * a1i + a2i * a1r,
                a2r * b1r - a2i * b1i + b2r,
                a2r * b1i + a2i * b1r + b2i)

    _, _, h_re, h_im = lax.associative_scan(combine, (a_re, a_im, bu_re, bu_im), reverse=reverse, axis=1)
    return h_re, h_im


def s5_final(h, reverse):
    idx = 0 if reverse else -1
    return (h[0][:, idx], h[1][:, idx])


def s5_readout(h, c_re, c_im):
    return (jnp.einsum('blgp,ghp->blgh', h[0], c_re.astype(jnp.float32))
            - jnp.einsum('blgp,ghp->blgh', h[1], c_im.astype(jnp.float32)))


def s5_output(u, h_f, h_b, c_f, c_b, d_skip, w_glu, b_glu, dtype):
    b, n = u.shape[:2]
    y = (s5_readout(h_f, c_f[0], c_f[1]) + s5_readout(h_b, c_b[0], c_b[1])
         + d_skip.astype(jnp.float32).reshape(S5_GROUPS, S5_GROUP) * u)
    y = jax.nn.gelu(y.reshape(b, n, B_WIDTH).astype(dtype))
    return y * jax.nn.sigmoid(y @ w_glu + b_glu)


def merge_branches(o_a, o_b, o_c, g_a, g_b, g_c, z_merge, w_o_a, w_o_b, w_o_c, w_out):
    gm = jax.nn.sigmoid(z_merge.astype(jnp.float32)).astype(o_a.dtype)
    m_a = gm[..., :D_MODEL]
    m_b = gm[..., D_MODEL:2 * D_MODEL]
    m_c = gm[..., 2 * D_MODEL:]
    y = (m_a * ((o_a * jax.nn.silu(g_a)) @ w_o_a)
         + m_b * ((o_b * jax.nn.silu(g_b)) @ w_o_b)
         + m_c * ((o_c * jax.nn.silu(g_c)) @ w_o_c))
    return y @ w_out


def setup_inputs(seed: int = 0) -> dict:
    key = jax.random.key(seed)
    ks = jax.random.split(key, 32)
    f32 = jnp.float32
    nrm = lambda k, s: jax.random.normal(k, s, dtype=f32)
    G, P, H = S5_GROUPS, S5_STATE, S5_GROUP
    a_im_init = jnp.broadcast_to(jnp.pi * jnp.arange(P, dtype=f32), (DEPTH, 2, G, P))
    return {
        "x": nrm(ks[0], (BATCH, SEQ, D_MODEL)),
        "c": nrm(ks[1], (BATCH, D_MODEL)),
        "ctx": nrm(ks[2], (BATCH, CTX_LEN, D_MODEL)),
        "c_ctx": nrm(ks[3], (D_MODEL,)),
        "w_mod": nrm(ks[4], (DEPTH, D_MODEL, 3 * D_MODEL)) * (0.5 * D_MODEL ** -0.5),
        "b_mod": 0.02 * nrm(ks[5], (DEPTH, 3 * D_MODEL)),
        "norm_pre": 1.0 + 0.02 * nrm(ks[6], (DEPTH, D_MODEL)),
        "norm_post": 1.0 + 0.02 * nrm(ks[7], (DEPTH, D_MODEL)),
        "w_in": nrm(ks[8], (DEPTH, D_MODEL, IN_WIDTH)) * D_MODEL ** -0.5,
        "swa_sink": 0.5 * nrm(ks[9], (DEPTH, A_HEADS)),
        "s5_a_re": -0.5 + 0.01 * nrm(ks[10], (DEPTH, 2, G, P)),
        "s5_a_im": a_im_init + 0.01 * nrm(ks[11], (DEPTH, 2, G, P)),
        "s5_log_dt": jax.random.uniform(ks[12], (DEPTH, 2, G), dtype=f32,
                                        minval=math.log(1e-3), maxval=math.log(1e-1)),
        "s5_b_re": nrm(ks[13], (DEPTH, 2, G, P, H)) * (2 * H) ** -0.5,
        "s5_b_im": nrm(ks[14], (DEPTH, 2, G, P, H)) * (2 * H) ** -0.5,
        "s5_c_re": nrm(ks[15], (DEPTH, 2, G, H, P)) * P ** -0.5,
        "s5_c_im": nrm(ks[16], (DEPTH, 2, G, H, P)) * P ** -0.5,
        "s5_d": nrm(ks[17], (DEPTH, B_WIDTH)),
        "s5_w_glu": nrm(ks[18], (DEPTH, B_WIDTH, B_WIDTH)) * B_WIDTH ** -0.5,
        "s5_b_glu": 0.02 * nrm(ks[19], (DEPTH, B_WIDTH)),
        "diff_lq1": 0.1 * nrm(ks[20], (DEPTH, HEAD_DIM)),
        "diff_lk1": 0.1 * nrm(ks[21], (DEPTH, HEAD_DIM)),
        "diff_lq2": 0.1 * nrm(ks[22], (DEPTH, HEAD_DIM)),
        "diff_lk2": 0.1 * nrm(ks[23], (DEPTH, HEAD_DIM)),
        "diff_subln": 1.0 + 0.02 * nrm(ks[24], (DEPTH, 2 * HEAD_DIM)),
        "w_o_a": nrm(ks[25], (DEPTH, A_WIDTH, D_MODEL)) * A_WIDTH ** -0.5,
        "w_o_b": nrm(ks[26], (DEPTH, B_WIDTH, D_MODEL)) * B_WIDTH ** -0.5,
        "w_o_c": nrm(ks[27], (DEPTH, C_WIDTH, D_MODEL)) * C_WIDTH ** -0.5,
        "w_out": nrm(ks[28], (DEPTH, D_MODEL, D_MODEL)) * D_MODEL ** -0.5,
    }


def reference(x, c, ctx, c_ctx, w_mod, b_mod, norm_pre, norm_post, w_in, swa_sink,
              s5_a_re, s5_a_im, s5_log_dt, s5_b_re, s5_b_im, s5_c_re, s5_c_im,
              s5_d, s5_w_glu, s5_b_glu, diff_lq1, diff_lk1, diff_lq2, diff_lk2, diff_subln,
              w_o_a, w_o_b, w_o_c, w_out):
    bsz, n_lat, _ = x.shape
    n_ctx = ctx.shape[1]
    cos, sin = axial_rope_tables(n_lat)
    h_lat, h_ctx = x, ctx
    for l in range(DEPTH):
        lam_init = 0.8 - 0.6 * math.exp(-0.3 * l)
        mod_lat = jax.nn.silu(c) @ w_mod[l] + b_mod[l]
        mod_ctx = jax.nn.silu(c_ctx) @ w_mod[l] + b_mod[l]
        sh_l = mod_lat[:, None, :D_MODEL]
        sc_l = mod_lat[:, None, D_MODEL:2 * D_MODEL]
        gt_l = mod_lat[:, None, 2 * D_MODEL:]
        sh_c = mod_ctx[:D_MODEL]
        sc_c = mod_ctx[D_MODEL:2 * D_MODEL]
        gt_c = mod_ctx[2 * D_MODEL:]
        (aq_l, ak_l, av_l, ag_l, bu_l, bg_l, cq_l, ck_l, cv_l, cg_l, mg_l) = modulated_projection(
            h_lat, sh_l, sc_l, norm_pre[l], w_in[l])
        (aq_c, ak_c, av_c, ag_c, bu_c, bg_c, cq_c, ck_c, cv_c, cg_c, mg_c) = modulated_projection(
            h_ctx, sh_c, sc_c, norm_pre[l], w_in[l])

        a_kc = heads(ak_c, A_KV_HEADS)
        a_vc = heads(av_c, A_KV_HEADS)
        a_q = apply_rope(heads(aq_l, A_HEADS), cos, sin)
        a_k = apply_rope(heads(ak_l, A_KV_HEADS), cos, sin)
        o_a_l = swa_latent(a_q, a_k, heads(av_l, A_KV_HEADS), a_kc, a_vc, swa_sink[l])

        disc = [s5_discretize(s5_a_re[l, d], s5_a_im[l, d], s5_log_dt[l, d], s5_b_re[l, d], s5_b_im[l, d])
                for d in range(2)]
        u_c = bu_c.astype(jnp.float32).reshape(bsz, n_ctx, S5_GROUPS, S5_GROUP)
        u_l = bu_l.astype(jnp.float32).reshape(bsz, n_lat, S5_GROUPS, S5_GROUP)
        hc_f = s5_states(u_c, *disc[0], None, False)
        hc_b = s5_states(u_c, *disc[1], None, True)
        hl_f = s5_states(u_l, *disc[0], s5_final(hc_f, False), False)
        hl_b = s5_states(u_l, *disc[1], s5_final(hc_b, True), True)
        c_f = (s5_c_re[l, 0], s5_c_im[l, 0])
        c_b = (s5_c_re[l, 1], s5_c_im[l, 1])
        o_b_l = s5_output(u_l, hl_f, hl_b, c_f, c_b, s5_d[l], s5_w_glu[l], s5_b_glu[l], x.dtype)

        lam = (jnp.exp(jnp.sum(diff_lq1[l].astype(jnp.float32) * diff_lk1[l].astype(jnp.float32)))
               - jnp.exp(jnp.sum(diff_lq2[l].astype(jnp.float32) * diff_lk2[l].astype(jnp.float32)))
               + lam_init)
        c_kc = ck_c.reshape(bsz, n_ctx, C_HEADS, 2, HEAD_DIM)
        c_vc = cv_c.reshape(bsz, n_ctx, C_HEADS, 2 * HEAD_DIM)
        c_q = apply_rope(cq_l.reshape(bsz, n_lat, C_HEADS, 2, HEAD_DIM), cos, sin)
        c_k = apply_rope(ck_l.reshape(bsz, n_lat, C_HEADS, 2, HEAD_DIM), cos, sin)
        c_v = cv_l.reshape(bsz, n_lat, C_HEADS, 2 * HEAD_DIM)
        o_c_l = diff_attention(c_q, jnp.concatenate([c_kc, c_k], axis=1),
                               jnp.concatenate([c_vc, c_v], axis=1), lam, diff_subln[l], lam_init)

        y_l = merge_branches(o_a_l, o_b_l, o_c_l, ag_l, bg_l, cg_l, mg_l,
                             w_o_a[l], w_o_b[l], w_o_c[l], w_out[l])
        h_lat_new = h_lat + gt_l * rms_norm(y_l, norm_post[l])

        if l < DEPTH - 1:
            o_a_c = swa_context(heads(aq_c, A_HEADS), a_kc, a_vc, swa_sink[l])
            o_b_c = s5_output(u_c, hc_f, hc_b, c_f, c_b, s5_d[l], s5_w_glu[l], s5_b_glu[l], x.dtype)
            o_c_c = diff_attention(cq_c.reshape(bsz, n_ctx, C_HEADS, 2, HEAD_DIM), c_kc, c_vc,
                                   lam, diff_subln[l], lam_init)
            y_c = merge_branches(o_a_c, o_b_c, o_c_c, ag_c, bg_c, cg_c, mg_c,
                                 w_o_a[l], w_o_b[l], w_o_c[l], w_out[l])
            h_ctx = h_ctx + gt_c * rms_norm(y_c, norm_post[l])
        h_lat = h_lat_new
    return h_lat
```

```python
import functools
import math

import jax
import jax.numpy as jnp
from jax import lax
from jax.experimental import pallas as pl
from jax.experimental.pallas import tpu as pltpu

F32 = jnp.float32
BF16 = jnp.bfloat16

D_MODEL = 1024
DEPTH = 2
GRID_W = 64
HEAD_DIM = 64
WINDOW = 128
ROPE_BASE = 10000.0
AX_FREQS = HEAD_DIM // 4
NEG_INF = -1e30
A_HEADS = 8
A_KV_HEADS = 2
A_GROUP = A_HEADS // A_KV_HEADS
A_WIDTH = A_HEADS * HEAD_DIM
A_KV_WIDTH = A_KV_HEADS * HEAD_DIM
B_WIDTH = 512
S5_GROUP = 16
S5_GROUPS = B_WIDTH // S5_GROUP
S5_STATE = 64
C_HEADS = 4
C_WIDTH = C_HEADS * 2 * HEAD_DIM
EPS = 1e-6

LANES = 128
SUBLANES = 8

ROW_TILE = 256
PROJ_COLS = 512
Q_BLOCK = 128
S5_CHUNK = 16
S5_CW = S5_CHUNK * S5_GROUP
VMEM_LIMIT = 56 * 1024 * 1024

_SEG_NAMES = ("aq", "ak", "av", "ag", "bu", "bg", "cq", "ck", "cv", "cg", "mg")
_SEG_WIDTHS = (A_WIDTH, A_KV_WIDTH, A_KV_WIDTH, A_WIDTH, B_WIDTH, B_WIDTH,
               C_WIDTH, C_WIDTH, C_WIDTH, C_WIDTH, 3 * D_MODEL)
SEGS = {}
_off = 0
for _n, _w in zip(_SEG_NAMES, _SEG_WIDTHS):
    SEGS[_n] = (_off, _w)
    _off += _w
IN_WIDTH = _off
ROPE_SEGS = {"aq": HEAD_DIM ** -0.5, "ak": 1.0, "cq": HEAD_DIM ** -0.5, "ck": 1.0}


def _const_spec(shape):
    nd = len(shape)
    return pl.BlockSpec(shape, lambda *_: (0,) * nd, pipeline_mode=pl.Buffered(1))


def _mod_kernel(c_ref, w_ref, b_ref, o_ref):
    c = c_ref[...]
    a = c * jax.nn.sigmoid(c)
    o_ref[...] = jnp.dot(a, w_ref[...], preferred_element_type=F32) + b_ref[...]


def _modulation(cc, w_mod, b_mod):
    n_tiles = 3
    return pl.pallas_call(
        _mod_kernel,
        grid=(DEPTH, n_tiles),
        in_specs=[
            pl.BlockSpec((16, D_MODEL), lambda l, j: (0, 0)),
            pl.BlockSpec((None, D_MODEL, D_MODEL), lambda l, j: (l, 0, j)),
            pl.BlockSpec((None, 1, D_MODEL), lambda l, j: (l, 0, j)),
        ],
        out_specs=pl.BlockSpec((None, 16, D_MODEL), lambda l, j: (l, 0, j)),
        out_shape=jax.ShapeDtypeStruct((DEPTH, 16, 3 * D_MODEL), F32),
        name="modulation",
    )(cc, w_mod, b_mod.reshape(DEPTH, 1, 3 * D_MODEL))


def _inproj_kernel(*refs, names, use_rope):
    h_ref, mod_ref, nw_ref, w_ref = refs[:4]
    if use_rope:
        cos_ref, sa_ref, sb_ref = refs[4:7]
        outs = refs[7:]
    else:
        outs = refs[4:]
    x = h_ref[...]
    ms = jnp.mean(x * x, axis=-1, keepdims=True)
    xn = x * lax.rsqrt(ms + EPS) * nw_ref[...]
    shift = mod_ref[:, 0:D_MODEL]
    scale = mod_ref[:, D_MODEL:2 * D_MODEL]
    hn = (xn * (1.0 + scale) + shift).astype(BF16)
    for name, o_ref in zip(names, outs):
        off, width = SEGS[name]
        if use_rope and name in ROPE_SEGS:
            mul = ROPE_SEGS[name]
            cos = cos_ref[...]
            sa = sa_ref[...]
            sb = sb_ref[...]
            for c in range(width // LANES):
                t = jnp.dot(hn, w_ref[:, off + c * LANES: off + (c + 1) * LANES],
                            preferred_element_type=F32)
                r = t * cos + pltpu.roll(t, LANES - AX_FREQS, 1) * sa + pltpu.roll(t, AX_FREQS, 1) * sb
                if mul != 1.0:
                    r = r * mul
                o_ref[:, c * LANES:(c + 1) * LANES] = r.astype(o_ref.dtype)
        else:
            mul = ROPE_SEGS.get(name, 1.0)
            for c in range(0, width, PROJ_COLS):
                cw = min(PROJ_COLS, width - c)
                z = jnp.dot(hn, w_ref[:, off + c: off + c + cw], preferred_element_type=F32)
                if mul != 1.0:
                    z = z * mul
                o_ref[:, c:c + cw] = z.astype(o_ref.dtype)


def _in_projection(h, mod, mod_row, norm_w, w_bf16, names, rope_tabs):
    bsz, rows, _ = h.shape
    use_rope = rope_tabs is not None
    if mod_row is None:
        mod_map = lambda b, i: (b, 0, 0)
    else:
        mod_map = lambda b, i: (mod_row, 0, 0)
    in_specs = [
        pl.BlockSpec((None, ROW_TILE, D_MODEL), lambda b, i: (b, i, 0)),
        pl.BlockSpec((None, 1, 3 * D_MODEL), mod_map),
        _const_spec((1, D_MODEL)),
        _const_spec((D_MODEL, IN_WIDTH)),
    ]
    args = [h, mod, norm_w.reshape(1, D_MODEL), w_bf16]
    if use_rope:
        for t in rope_tabs:
            in_specs.append(pl.BlockSpec((ROW_TILE, LANES), lambda b, i: (i, 0)))
            args.append(t)
    out_specs = [pl.BlockSpec((None, ROW_TILE, SEGS[n][1]), lambda b, i: (b, i, 0)) for n in names]
    out_shape = [jax.ShapeDtypeStruct((bsz, rows, SEGS[n][1]), BF16) for n in names]
    outs = pl.pallas_call(
        functools.partial(_inproj_kernel, names=tuple(names), use_rope=use_rope),
        grid=(bsz, rows // ROW_TILE),
        in_specs=in_specs,
        out_specs=out_specs,
        out_shape=out_shape,
        compiler_params=pltpu.CompilerParams(
            dimension_semantics=("parallel", "parallel"), vmem_limit_bytes=VMEM_LIMIT),
        name="in_projection_rope" if use_rope else "in_projection",
    )(*args)
    return dict(zip(names, outs))


def _dot_nt(a, b):
    return lax.dot_general(a, b, (((1,), (1,)), ((), ())), preferred_element_type=F32)


def _swa_kernel(*refs, has_local, n_lat):
    if has_local:
        sink_ref, q_ref, kc_ref, vc_ref, kl_ref, vl_ref, o_ref = refs
    else:
        sink_ref, q_ref, kc_ref, vc_ref, o_ref = refs
    rows = A_GROUP * Q_BLOCK
    band = 3 * Q_BLOCK
    if has_local:
        n = pl.program_id(1)
        start = pl.multiple_of(jnp.clip((n - 1) * Q_BLOCK, 0, n_lat - band), Q_BLOCK)
        kl = kl_ref[pl.ds(start, band), :]
        vl = vl_ref[pl.ds(start, band), :]
        qpos = n * Q_BLOCK + (lax.broadcasted_iota(jnp.int32, (rows, band), 0) & (Q_BLOCK - 1))
        kpos = start + lax.broadcasted_iota(jnp.int32, (rows, band), 1)
        valid = jnp.abs(qpos - kpos) <= WINDOW
    kc = kc_ref[...]
    vc = vc_ref[...]
    for kh in range(A_KV_HEADS):
        lo, hi = kh * HEAD_DIM, (kh + 1) * HEAD_DIM
        q4 = jnp.concatenate(
            [q_ref[:, (kh * A_GROUP + g) * HEAD_DIM:(kh * A_GROUP + g + 1) * HEAD_DIM] for g in range(A_GROUP)],
            axis=0)
        sink = jnp.concatenate(
            [jnp.full((Q_BLOCK, 1), sink_ref[kh * A_GROUP + g], F32) for g in range(A_GROUP)], axis=0)
        s_c = _dot_nt(q4, kc[:, lo:hi])
        m = jnp.maximum(jnp.max(s_c, axis=1, keepdims=True), sink)
        if has_local:
            s_l = jnp.where(valid, _dot_nt(q4, kl[:, lo:hi]), NEG_INF)
            m = jnp.maximum(m, jnp.max(s_l, axis=1, keepdims=True))
        p_c = jnp.exp(s_c - m)
        den = jnp.sum(p_c, axis=1, keepdims=True) + jnp.exp(sink - m)
        o = jnp.dot(p_c.astype(BF16), vc[:, lo:hi], preferred_element_type=F32)
        if has_local:
            p_l = jnp.exp(s_l - m)
            den = den + jnp.sum(p_l, axis=1, keepdims=True)
            o = o + jnp.dot(p_l.astype(BF16), vl[:, lo:hi], preferred_element_type=F32)
        o = o * (1.0 / den)
        for g in range(A_GROUP):
            h = kh * A_GROUP + g
            o_ref[:, h * HEAD_DIM:(h + 1) * HEAD_DIM] = o[g * Q_BLOCK:(g + 1) * Q_BLOCK].astype(o_ref.dtype)


def _swa(sink, q, k_ctx, v_ctx, k_lat=None, v_lat=None):
    bsz, rows, _ = q.shape
    n_ctx = k_ctx.shape[1]
    has_local = k_lat is not None
    in_specs = [
        pl.BlockSpec(memory_space=pltpu.SMEM),
        pl.BlockSpec((None, Q_BLOCK, A_WIDTH), lambda b, n: (b, n, 0)),
        pl.BlockSpec((None, n_ctx, A_KV_WIDTH), lambda b, n: (b, 0, 0)),
        pl.BlockSpec((None, n_ctx, A_KV_WIDTH), lambda b, n: (b, 0, 0)),
    ]
    args = [sink, q, k_ctx, v_ctx]
    n_lat = 0
    if has_local:
        n_lat = k_lat.shape[1]
        in_specs += [pl.BlockSpec((None, n_lat, A_KV_WIDTH), lambda b, n: (b, 0, 0))] * 2
        args += [k_lat, v_lat]
    return pl.pallas_call(
        functools.partial(_swa_kernel, has_local=has_local, n_lat=n_lat),
        grid=(bsz, rows // Q_BLOCK),
        in_specs=in_specs,
        out_specs=pl.BlockSpec((None, Q_BLOCK, A_WIDTH), lambda b, n: (b, n, 0)),
        out_shape=jax.ShapeDtypeStruct((bsz, rows, A_WIDTH), BF16),
        compiler_params=pltpu.CompilerParams(
            dimension_semantics=("parallel", "arbitrary"), vmem_limit_bytes=VMEM_LIMIT),
        name="swa_latent" if has_local else "swa_context",
    )(*args)


def _diff_kernel(*refs, has_lat, lam_init):
    if has_lat:
        lam_ref, sub_ref, q_ref, kc_ref, vc_ref, kl_ref, vl_ref, o_ref = refs
    else:
        lam_ref, sub_ref, q_ref, kc_ref, vc_ref, o_ref = refs
    lp = lam_ref[...]
    lam = (jnp.exp(jnp.sum(lp[0:1] * lp[1:2], axis=1, keepdims=True))
           - jnp.exp(jnp.sum(lp[2:3] * lp[3:4], axis=1, keepdims=True)) + lam_init)
    sub_w = sub_ref[...] * (1.0 - lam_init)
    width = 2 * HEAD_DIM
    for h in range(C_HEADS):
        a_c = None
        a_l = None
        for j in range(2):
            lo = h * width + j * HEAD_DIM
            q = q_ref[:, lo:lo + HEAD_DIM]
            s_c = _dot_nt(q, kc_ref[:, lo:lo + HEAD_DIM])
            m = jnp.max(s_c, axis=1, keepdims=True)
            if has_lat:
                s_l = _dot_nt(q, kl_ref[:, lo:lo + HEAD_DIM])
                m = jnp.maximum(m, jnp.max(s_l, axis=1, keepdims=True))
            p_c = jnp.exp(s_c - m)
            den = jnp.sum(p_c, axis=1, keepdims=True)
            if has_lat:
                p_l = jnp.exp(s_l - m)
                den = den + jnp.sum(p_l, axis=1, keepdims=True)
            coef = 1.0 / den
            if j == 1:
                coef = -lam * coef
            a_c = p_c * coef if a_c is None else a_c + p_c * coef
            if has_lat:
                a_l = p_l * coef if a_l is None else a_l + p_l * coef
        o = jnp.dot(a_c.astype(BF16), vc_ref[:, h * width:(h + 1) * width], preferred_element_type=F32)
        if has_lat:
            o = o + jnp.dot(a_l.astype(BF16), vl_ref[:, h * width:(h + 1) * width],
                            preferred_element_type=F32)
        ms = jnp.mean(o * o, axis=1, keepdims=True)
        o_ref[:, h * width:(h + 1) * width] = (o * lax.rsqrt(ms + EPS) * sub_w).astype(o_ref.dtype)


def _diff_attention(lam_params, subln, lam_init, q, k_ctx, v_ctx, k_lat=None, v_lat=None):
    bsz, rows, _ = q.shape
    n_ctx = k_ctx.shape[1]
    has_lat = k_lat is not None
    in_specs = [
        _const_spec((4, HEAD_DIM)),
        _const_spec((1, 2 * HEAD_DIM)),
        pl.BlockSpec((None, Q_BLOCK, C_WIDTH), lambda b, n: (b, n, 0)),
        pl.BlockSpec((None, n_ctx, C_WIDTH), lambda b, n: (b, 0, 0)),
        pl.BlockSpec((None, n_ctx, C_WIDTH), lambda b, n: (b, 0, 0)),
    ]
    args = [lam_params, subln.reshape(1, 2 * HEAD_DIM), q, k_ctx, v_ctx]
    if has_lat:
        n_lat = k_lat.shape[1]
        in_specs += [pl.BlockSpec((None, n_lat, C_WIDTH), lambda b, n: (b, 0, 0))] * 2
        args += [k_lat, v_lat]
    return pl.pallas_call(
        functools.partial(_diff_kernel, has_lat=has_lat, lam_init=lam_init),
        grid=(bsz, rows // Q_BLOCK),
        in_specs=in_specs,
        out_specs=pl.BlockSpec((None, Q_BLOCK, C_WIDTH), lambda b, n: (b, n, 0)),
        out_shape=jax.ShapeDtypeStruct((bsz, rows, C_WIDTH), BF16),
        compiler_params=pltpu.CompilerParams(
            dimension_semantics=("parallel", "arbitrary"), vmem_limit_bytes=VMEM_LIMIT),
        name="diff_latent" if has_lat else "diff_context",
    )(*args)


def _s5_operators(a_re, a_im, log_dt, b_re, b_im, c_re, c_im, d_skip):
    hp = lax.Precision.HIGHEST
    t_len = S5_CHUNK
    a_re = a_re.astype(F32)
    a_im = a_im.astype(F32)
    dt = jnp.exp(log_dt.astype(F32))[..., None]
    mag = jnp.exp(a_re * dt)
    abar_re = mag * jnp.cos(a_im * dt)
    abar_im = mag * jnp.sin(a_im * dt)
    den = a_re * a_re + a_im * a_im
    nr = abar_re - 1.0
    ni = abar_im
    f_re = ((nr * a_re + ni * a_im) / den)[..., None]
    f_im = ((ni * a_re - nr * a_im) / den)[..., None]
    b_re = b_re.astype(F32)
    b_im = b_im.astype(F32)
    bb_re = f_re * b_re - f_im * b_im
    bb_im = f_re * b_im + f_im * b_re
    tau = jnp.arange(t_len + 1, dtype=F32)[:, None, None, None]
    pmag = jnp.exp(tau * (a_re * dt)[None])
    pw_re = pmag * jnp.cos(tau * (a_im * dt)[None])
    pw_im = pmag * jnp.sin(tau * (a_im * dt)[None])
    c_re = c_re.astype(F32)
    c_im = c_im.astype(F32)
    ca_re = c_re[None] * pw_re[:, :, :, None, :] - c_im[None] * pw_im[:, :, :, None, :]
    ca_im = c_re[None] * pw_im[:, :, :, None, :] + c_im[None] * pw_re[:, :, :, None, :]
    k = (jnp.einsum('tdgop,dgpi->tdgoi', ca_re[:t_len], bb_re, precision=hp)
         - jnp.einsum('tdgop,dgpi->tdgoi', ca_im[:t_len], bb_im, precision=hp))
    jj = jnp.arange(t_len)[:, None]
    ii = jnp.arange(t_len)[None, :]
    lag_f = ii - jj
    kf = jnp.where((lag_f >= 0)[:, :, None, None, None], k[jnp.clip(lag_f, 0, t_len - 1), 0], 0.0)
    kb = jnp.where((lag_f <= 0)[:, :, None, None, None], k[jnp.clip(-lag_f, 0, t_len - 1), 1], 0.0)
    eye_t = (lag_f == 0)[:, :, None, None, None]
    eye_h = jnp.eye(S5_GROUP, dtype=F32)[None, None, None]
    dsk = d_skip.astype(F32).reshape(S5_GROUPS, 1, S5_GROUP)[None, None]
    m_tot = kf + kb + jnp.where(eye_t, eye_h * dsk, 0.0)
    m_tot = jnp.transpose(m_tot, (2, 0, 4, 1, 3)).reshape(S5_GROUPS, S5_CW, S5_CW)

    def local_state(pre, pim, d):
        sre = pre[..., None] * bb_re[d][None] - pim[..., None] * bb_im[d][None]
        sim = pre[..., None] * bb_im[d][None] + pim[..., None] * bb_re[d][None]
        sre = jnp.transpose(sre, (1, 0, 3, 2)).reshape(S5_GROUPS, S5_CW, S5_STATE)
        sim = jnp.transpose(sim, (1, 0, 3, 2)).reshape(S5_GROUPS, S5_CW, S5_STATE)
        return jnp.concatenate([sre, sim, sim, sre], axis=-1)

    s_f = local_state(pw_re[:t_len, 0][::-1], pw_im[:t_len, 0][::-1], 0)
    s_b = local_state(pw_re[:t_len, 1], pw_im[:t_len, 1], 1)
    w1 = jnp.concatenate([m_tot, s_f, s_b], axis=-1).astype(BF16)

    def readout(cre, cim):
        r_re = jnp.transpose(cre, (1, 3, 0, 2)).reshape(S5_GROUPS, S5_STATE, S5_CW)
        r_im = jnp.transpose(-cim, (1, 3, 0, 2)).reshape(S5_GROUPS, S5_STATE, S5_CW)
        return jnp.concatenate([r_re, r_im], axis=1)

    r_f = readout(ca_re[1:, 0], ca_im[1:, 0])
    r_b = readout(ca_re[1:, 1][::-1], ca_im[1:, 1][::-1])
    w2 = jnp.concatenate([r_f, r_b], axis=1).astype(BF16)

    def trans(d):
        are, aim = pw_re[t_len, d], pw_im[t_len, d]
        return (jnp.concatenate([are, are], axis=-1), jnp.concatenate([-aim, aim], axis=-1))

    a1f, a2f = trans(0)
    a1b, a2b = trans(1)
    zero = jnp.zeros_like(a1f)
    coef = jnp.stack([a1f, a2f, a1b, a2b, zero, zero, zero, zero], axis=1)
    return w1, w2, coef


def _s5_kernel(x_ref, w1_ref, w2_ref, coef_ref, y_ref, z_ref, hin_ref, *, n_chunks, n_ctx_chunks):
    sw = 2 * S5_STATE
    z_ref[...] = jnp.dot(x_ref[...], w1_ref[...], preferred_element_type=F32)
    shape = (SUBLANES, sw)
    a1f = jnp.broadcast_to(coef_ref[0:1, :], shape)
    a2f = jnp.broadcast_to(coef_ref[1:2, :], shape)
    a1b = jnp.broadcast_to(coef_ref[2:3, :], shape)
    a2b = jnp.broadcast_to(coef_ref[3:4, :], shape)
    base = S5_CW

    def body(t, carry):
        hf, hfs, hb, hbs = carry
        rf = pl.multiple_of(t * SUBLANES, SUBLANES)
        hin_ref[pl.ds(rf, SUBLANES), 0:sw] = hf
        lf = z_ref[pl.ds(rf, SUBLANES), base:base + sw]
        lfs = z_ref[pl.ds(rf, SUBLANES), base + sw:base + 2 * sw]
        hf_new = a1f * hf + a2f * hfs + lf
        hfs_new = a1f * hfs - a2f * hf + lfs
        cb = jnp.where(t < n_ctx_chunks, n_ctx_chunks - 1 - t, n_chunks + n_ctx_chunks - 1 - t)
        rb = pl.multiple_of(cb * SUBLANES, SUBLANES)
        hin_ref[pl.ds(rb, SUBLANES), sw:2 * sw] = hb
        lb = z_ref[pl.ds(rb, SUBLANES), base + 2 * sw:base + 3 * sw]
        lbs = z_ref[pl.ds(rb, SUBLANES), base + 3 * sw:base + 4 * sw]
        hb_new = a1b * hb + a2b * hbs + lb
        hbs_new = a1b * hbs - a2b * hb + lbs
        return hf_new, hfs_new, hb_new, hbs_new

    zero = jnp.zeros(shape, F32)
    lax.fori_loop(0, n_chunks, body, (zero, zero, zero, zero))
    y = z_ref[:, 0:S5_CW] + jnp.dot(hin_ref[...].astype(BF16), w2_ref[...], preferred_element_type=F32)
    y_ref[...] = y.astype(y_ref.dtype)


def _s5_scan(u_ctx, u_lat, w1, w2, coef):
    bsz, n_ctx, _ = u_ctx.shape
    n_tok = n_ctx + u_lat.shape[1]
    n_chunks = n_tok // S5_CHUNK
    rows = n_chunks * bsz
    u = jnp.concatenate([u_ctx, u_lat], axis=1)
    x = u.reshape(bsz, n_chunks, S5_CHUNK, S5_GROUPS, S5_GROUP)
    x = jnp.transpose(x, (3, 1, 0, 2, 4)).reshape(S5_GROUPS, rows, S5_CW)
    y = pl.pallas_call(
        functools.partial(_s5_kernel, n_chunks=n_chunks, n_ctx_chunks=n_ctx // S5_CHUNK),
        grid=(S5_GROUPS,),
        in_specs=[
            pl.BlockSpec((None, rows, S5_CW), lambda g: (g, 0, 0)),
            pl.BlockSpec((None, S5_CW, 3 * S5_CW), lambda g: (g, 0, 0)),
            pl.BlockSpec((None, 4 * S5_STATE, S5_CW), lambda g: (g, 0, 0)),
            pl.BlockSpec((None, SUBLANES, 2 * S5_STATE), lambda g: (g, 0, 0)),
        ],
        out_specs=pl.BlockSpec((None, rows, S5_CW), lambda g: (g, 0, 0)),
        out_shape=jax.ShapeDtypeStruct((S5_GROUPS, rows, S5_CW), BF16),
        scratch_shapes=[pltpu.VMEM((rows, 3 * S5_CW), F32), pltpu.VMEM((rows, 4 * S5_STATE), F32)],
        compiler_params=pltpu.CompilerParams(
            dimension_semantics=("parallel",), vmem_limit_bytes=VMEM_LIMIT),
        name="s5_scan",
    )(x, w1, w2, coef)
    y = y.reshape(S5_GROUPS, n_chunks, bsz, S5_CHUNK, S5_GROUP)
    return jnp.transpose(y, (2, 1, 3, 0, 4)).reshape(bsz, n_tok, B_WIDTH)


def _gelu_tanh(x):
    return 0.5 * x * (1.0 + jnp.tanh(math.sqrt(2.0 / math.pi) * (x + 0.044715 * (x * x * x))))


def _silu(x):
    return x * jax.nn.sigmoid(x)


def _merge_kernel(h_ref, mod_ref, oa_ref, ga_ref, yb_ref, gb_ref, oc_ref, gc_ref, mg_ref,
                  wglu_ref, bglu_ref, woa_ref, wob_ref, woc_ref, wout_ref, npost_ref, o_ref):
    yb = _gelu_tanh(yb_ref[...].astype(F32))
    glu = jnp.dot(yb.astype(BF16), wglu_ref[...], preferred_element_type=F32) + bglu_ref[...]
    ob = yb * jax.nn.sigmoid(glu)

    def branch(o, g_ref, w_ref, m_lo):
        t = (o * _silu(g_ref[...].astype(F32))).astype(BF16)
        t = jnp.dot(t, w_ref[...], preferred_element_type=F32)
        gate = jax.nn.sigmoid(mg_ref[:, m_lo:m_lo + D_MODEL].astype(F32))
        return gate * t

    y = (branch(oa_ref[...].astype(F32), ga_ref, woa_ref, 0)
         + branch(ob, gb_ref, wob_ref, D_MODEL)
         + branch(oc_ref[...].astype(F32), gc_ref, woc_ref, 2 * D_MODEL))
    y = jnp.dot(y.astype(BF16), wout_ref[...], preferred_element_type=F32)
    ms = jnp.mean(y * y, axis=-1, keepdims=True)
    yn = y * lax.rsqrt(ms + EPS) * npost_ref[...]
    gate = mod_ref[:, 2 * D_MODEL:3 * D_MODEL]
    o_ref[...] = h_ref[...] + gate * yn


def _merge(h, mod, mod_row, o_a, g_a, y_b, yb_tile_off, g_b, o_c, g_c, mg, wts):
    bsz, rows, _ = h.shape
    if mod_row is None:
        mod_map = lambda b, i: (b, 0, 0)
    else:
        mod_map = lambda b, i: (mod_row, 0, 0)
    tok = lambda w: pl.BlockSpec((None, ROW_TILE, w), lambda b, i: (b, i, 0))
    in_specs = [
        tok(D_MODEL),
        pl.BlockSpec((None, 1, 3 * D_MODEL), mod_map),
        tok(A_WIDTH), tok(A_WIDTH),
        pl.BlockSpec((None, ROW_TILE, B_WIDTH), lambda b, i: (b, i + yb_tile_off, 0)),
        tok(B_WIDTH), tok(C_WIDTH), tok(C_WIDTH), tok(3 * D_MODEL),
        _const_spec((B_WIDTH, B_WIDTH)), _const_spec((1, B_WIDTH)),
        _const_spec((A_WIDTH, D_MODEL)), _const_spec((B_WIDTH, D_MODEL)), _const_spec((C_WIDTH, D_MODEL)),
        _const_spec((D_MODEL, D_MODEL)), _const_spec((1, D_MODEL)),
    ]
    return pl.pallas_call(
        _merge_kernel,
        grid=(bsz, rows // ROW_TILE),
        in_specs=in_specs,
        out_specs=tok(D_MODEL),
        out_shape=jax.ShapeDtypeStruct((bsz, rows, D_MODEL), F32),
        compiler_params=pltpu.CompilerParams(
            dimension_semantics=("parallel", "parallel"), vmem_limit_bytes=VMEM_LIMIT),
        name="merge_out",
    )(h, mod, o_a, g_a, y_b, g_b, o_c, g_c, mg, *wts)


def _rope_tables(n_tokens):
    rows = n_tokens // GRID_W
    pos_r = jnp.repeat(jnp.arange(rows, dtype=F32), GRID_W)
    pos_c = jnp.tile(jnp.arange(GRID_W, dtype=F32), rows)
    inv = ROPE_BASE ** (-jnp.arange(AX_FREQS, dtype=F32) / AX_FREQS)
    ang_r = pos_r[:, None] * inv[None]
    ang_c = pos_c[:, None] * inv[None]
    ang = jnp.concatenate([ang_r, ang_r, ang_c, ang_c], axis=-1)
    cos, sin = jnp.cos(ang), jnp.sin(ang)
    first = (jnp.arange(HEAD_DIM) % (2 * AX_FREQS)) < AX_FREQS
    sa = jnp.where(first[None], -sin, 0.0)
    sb = jnp.where(first[None], 0.0, sin)
    tile2 = lambda t: jnp.concatenate([t, t], axis=-1)
    return tile2(cos), tile2(sa), tile2(sb)


def kernel(x, c, ctx, c_ctx, w_mod, b_mod, norm_pre, norm_post, w_in, swa_sink, s5_a_re, s5_a_im, s5_log_dt, s5_b_re, s5_b_im, s5_c_re, s5_c_im, s5_d, s5_w_glu, s5_b_glu, diff_lq1, diff_lk1, diff_lq2, diff_lk2, diff_subln, w_o_a, w_o_b, w_o_c, w_out):
    bsz, n_lat, _ = x.shape
    n_ctx = ctx.shape[1]
    ctx_row = bsz
    cc = jnp.zeros((16, D_MODEL), F32).at[:bsz].set(c).at[ctx_row].set(c_ctx)
    mod_all = _modulation(cc, w_mod, b_mod)
    rope_tabs = _rope_tables(n_lat)
    ctx_tiles = n_ctx // ROW_TILE

    h_lat, h_ctx = x, ctx
    for l in range(DEPTH):
        last = l == DEPTH - 1
        lam_init = 0.8 - 0.6 * math.exp(-0.3 * l)
        mod = mod_all[l].reshape(16, 1, 3 * D_MODEL)
        w_in_l = w_in[l].astype(BF16)
        zl = _in_projection(h_lat, mod, None, norm_pre[l], w_in_l, _SEG_NAMES, rope_tabs)
        ctx_names = ("ak", "av", "bu", "ck", "cv") if last else _SEG_NAMES
        zc = _in_projection(h_ctx, mod, ctx_row, norm_pre[l], w_in_l, ctx_names, None)

        o_a_l = _swa(swa_sink[l], zl["aq"], zc["ak"], zc["av"], zl["ak"], zl["av"])

        w1, w2, coef = _s5_operators(s5_a_re[l], s5_a_im[l], s5_log_dt[l], s5_b_re[l], s5_b_im[l],
                                     s5_c_re[l], s5_c_im[l], s5_d[l])
        y_b = _s5_scan(zc["bu"], zl["bu"], w1, w2, coef)

        lam_params = jnp.stack([diff_lq1[l], diff_lk1[l], diff_lq2[l], diff_lk2[l]]).astype(F32)
        o_c_l = _diff_attention(lam_params, diff_subln[l], lam_init, zl["cq"], zc["ck"], zc["cv"],
                                zl["ck"], zl["cv"])

        wts = (s5_w_glu[l].astype(BF16), s5_b_glu[l].reshape(1, B_WIDTH),
               w_o_a[l].astype(BF16), w_o_b[l].astype(BF16), w_o_c[l].astype(BF16),
               w_out[l].astype(BF16), norm_post[l].reshape(1, D_MODEL))
        h_lat_new = _merge(h_lat, mod, None, o_a_l, zl["ag"], y_b, ctx_tiles, zl["bg"], o_c_l, zl["cg"],
                           zl["mg"], wts)
        if not last:
            o_a_c = _swa(swa_sink[l], zc["aq"], zc["ak"], zc["av"])
            o_c_c = _diff_attention(lam_params, diff_subln[l], lam_init, zc["cq"], zc["ck"], zc["cv"])
            h_ctx = _merge(h_ctx, mod, ctx_row, o_a_c, zc["ag"], y_b, 0, zc["bg"], o_c_c, zc["cg"],
                           zc["mg"], wts)
        h_lat = h_lat_new
    return h_lat
```

```python
import functools
import math

import jax
import jax.numpy as jnp
from jax import lax
from jax.experimental import pallas as pl
from jax.experimental.pallas import tpu as pltpu

F32 = jnp.float32
BF16 = jnp.bfloat16

D_MODEL = 1024
DEPTH = 2
GRID_W = 64
HEAD_DIM = 64
WINDOW = 128
ROPE_BASE = 10000.0
AX_FREQS = HEAD_DIM // 4
NEG_INF = -1e30
A_HEADS = 8
A_KV_HEADS = 2
A_GROUP = A_HEADS // A_KV_HEADS
A_WIDTH = A_HEADS * HEAD_DIM
A_KV_WIDTH = A_KV_HEADS * HEAD_DIM
B_WIDTH = 512
S5_GROUP = 16
S5_GROUPS = B_WIDTH // S5_GROUP
S5_STATE = 64
C_HEADS = 4
C_WIDTH = C_HEADS * 2 * HEAD_DIM
EPS = 1e-6

LANES = 128
SUBLANES = 8

ROW_TILE = 256
PROJ_COLS = 512
Q_BLOCK = 128
S5_CHUNK = 16
S5_CW = S5_CHUNK * S5_GROUP
STAGE_PITCH = S5_CHUNK + SUBLANES
VMEM_LIMIT = 56 * 1024 * 1024

_SEG_NAMES = ("aq", "ak", "av", "ag", "bu", "bg", "cq", "ck", "cv", "cg", "mg")
_SEG_WIDTHS = (A_WIDTH, A_KV_WIDTH, A_KV_WIDTH, A_WIDTH, B_WIDTH, B_WIDTH,
               C_WIDTH, C_WIDTH, C_WIDTH, C_WIDTH, 3 * D_MODEL)
SEGS = {}
_off = 0
for _n, _w in zip(_SEG_NAMES, _SEG_WIDTHS):
    SEGS[_n] = (_off, _w)
    _off += _w
IN_WIDTH = _off
ROPE_SEGS = {"aq": HEAD_DIM ** -0.5, "ak": 1.0, "cq": HEAD_DIM ** -0.5, "ck": 1.0}


def _const_spec(shape):
    nd = len(shape)
    return pl.BlockSpec(shape, lambda *_: (0,) * nd, pipeline_mode=pl.Buffered(1))


def _mod_kernel(c_ref, w_ref, b_ref, o_ref):
    c = c_ref[...]
    a = c * jax.nn.sigmoid(c)
    o_ref[...] = jnp.dot(a, w_ref[...], preferred_element_type=F32) + b_ref[...]


def _modulation(cc, w_mod, b_mod):
    n_tiles = 3
    return pl.pallas_call(
        _mod_kernel,
        grid=(DEPTH, n_tiles),
        in_specs=[
            pl.BlockSpec((16, D_MODEL), lambda l, j: (0, 0)),
            pl.BlockSpec((None, D_MODEL, D_MODEL), lambda l, j: (l, 0, j)),
            pl.BlockSpec((None, 1, D_MODEL), lambda l, j: (l, 0, j)),
        ],
        out_specs=pl.BlockSpec((None, 16, D_MODEL), lambda l, j: (l, 0, j)),
        out_shape=jax.ShapeDtypeStruct((DEPTH, 16, 3 * D_MODEL), F32),
        name="modulation",
    )(cc, w_mod, b_mod.reshape(DEPTH, 1, 3 * D_MODEL))


def _block_transpose8(xs):
    lane_blk = lax.broadcasted_iota(jnp.int32, xs[0].shape, 1) // S5_GROUP
    for s in (4, 2, 1):
        keep = (lane_blk & s) == 0
        out = list(xs)
        for p in range(8):
            if p & s:
                continue
            a, b = xs[p], xs[p + s]
            out[p] = jnp.where(keep, a, pltpu.roll(b, S5_GROUP * s, 1))
            out[p + s] = jnp.where(keep, pltpu.roll(a, LANES - S5_GROUP * s, 1), b)
        xs = out
    return xs


def _tokens_to_groups(z, stage_ref, o_ref):
    n_chunks = ROW_TILE // S5_CHUNK
    for v in range(B_WIDTH // LANES):
        for c in range(n_chunks):
            stage_ref[v, c * STAGE_PITCH:c * STAGE_PITCH + S5_CHUNK, :] = \
                z[c * S5_CHUNK:(c + 1) * S5_CHUNK, v * LANES:(v + 1) * LANES]
    for w in range(S5_CW // LANES):
        for v in range(B_WIDTH // LANES):
            src = [stage_ref[v, pl.ds(8 * w + jj, n_chunks, stride=STAGE_PITCH), :] for jj in range(8)]
            dst = _block_transpose8(src)
            for gg in range(8):
                o_ref[8 * v + gg, :, w * LANES:(w + 1) * LANES] = dst[gg].astype(o_ref.dtype)


def _groups_to_tokens(y_ref, stage_ref):
    n_chunks = ROW_TILE // S5_CHUNK
    for w in range(S5_CW // LANES):
        for v in range(B_WIDTH // LANES):
            src = [y_ref[8 * v + gg, :, w * LANES:(w + 1) * LANES].astype(F32) for gg in range(8)]
            dst = _block_transpose8(src)
            for jj in range(8):
                stage_ref[v, pl.ds(8 * w + jj, n_chunks, stride=STAGE_PITCH), :] = dst[jj]
    rows = []
    for c in range(n_chunks):
        rows.append(jnp.concatenate(
            [stage_ref[v, c * STAGE_PITCH:c * STAGE_PITCH + S5_CHUNK, :] for v in range(B_WIDTH // LANES)], axis=1))
    return jnp.concatenate(rows, axis=0)


def _inproj_kernel(*refs, names, use_rope):
    h_ref, mod_ref, nw_ref, w_ref = refs[:4]
    stage_ref = refs[-1]
    refs = refs[:-1]
    if use_rope:
        cos_ref, sa_ref, sb_ref = refs[4:7]
        outs = refs[7:]
    else:
        outs = refs[4:]
    x = h_ref[...]
    ms = jnp.mean(x * x, axis=-1, keepdims=True)
    xn = x * lax.rsqrt(ms + EPS) * nw_ref[...]
    shift = mod_ref[:, 0:D_MODEL]
    scale = mod_ref[:, D_MODEL:2 * D_MODEL]
    hn = (xn * (1.0 + scale) + shift).astype(BF16)
    for name, o_ref in zip(names, outs):
        off, width = SEGS[name]
        if use_rope and name in ROPE_SEGS:
            mul = ROPE_SEGS[name]
            cos = cos_ref[...]
            sa = sa_ref[...]
            sb = sb_ref[...]
            for c in range(width // LANES):
                t = jnp.dot(hn, w_ref[:, off + c * LANES: off + (c + 1) * LANES],
                            preferred_element_type=F32)
                r = t * cos + pltpu.roll(t, LANES - AX_FREQS, 1) * sa + pltpu.roll(t, AX_FREQS, 1) * sb
                if mul != 1.0:
                    r = r * mul
                o_ref[:, c * LANES:(c + 1) * LANES] = r.astype(o_ref.dtype)
        elif name == "bu":
            z = jnp.dot(hn, w_ref[:, off:off + width], preferred_element_type=F32)
            _tokens_to_groups(z, stage_ref, o_ref)
        else:
            mul = ROPE_SEGS.get(name, 1.0)
            for c in range(0, width, PROJ_COLS):
                cw = min(PROJ_COLS, width - c)
                z = jnp.dot(hn, w_ref[:, off + c: off + c + cw], preferred_element_type=F32)
                if mul != 1.0:
                    z = z * mul
                o_ref[:, c:c + cw] = z.astype(o_ref.dtype)


def _in_projection(h, mod, mod_row, norm_w, w_bf16, names, rope_tabs):
    bsz, rows, _ = h.shape
    use_rope = rope_tabs is not None
    if mod_row is None:
        mod_map = lambda b, i: (b, 0, 0)
    else:
        mod_map = lambda b, i: (mod_row, 0, 0)
    in_specs = [
        pl.BlockSpec((None, ROW_TILE, D_MODEL), lambda b, i: (b, i, 0)),
        pl.BlockSpec((None, 1, 3 * D_MODEL), mod_map),
        _const_spec((1, D_MODEL)),
        _const_spec((D_MODEL, IN_WIDTH)),
    ]
    args = [h, mod, norm_w.reshape(1, D_MODEL), w_bf16]
    if use_rope:
        for t in rope_tabs:
            in_specs.append(pl.BlockSpec((ROW_TILE, LANES), lambda b, i: (i, 0)))
            args.append(t)
    tiles = rows // ROW_TILE
    chunk_rows = ROW_TILE // S5_CHUNK
    out_specs, out_shape = [], []
    for n in names:
        if n == "bu":
            out_specs.append(pl.BlockSpec((S5_GROUPS, chunk_rows, S5_CW), lambda b, i: (0, b * tiles + i, 0)))
            out_shape.append(jax.ShapeDtypeStruct((S5_GROUPS, bsz * rows // S5_CHUNK, S5_CW), BF16))
        else:
            out_specs.append(pl.BlockSpec((None, ROW_TILE, SEGS[n][1]), lambda b, i: (b, i, 0)))
            out_shape.append(jax.ShapeDtypeStruct((bsz, rows, SEGS[n][1]), BF16))
    outs = pl.pallas_call(
        functools.partial(_inproj_kernel, names=tuple(names), use_rope=use_rope),
        grid=(bsz, tiles),
        in_specs=in_specs,
        out_specs=out_specs,
        out_shape=out_shape,
        scratch_shapes=[pltpu.VMEM((B_WIDTH // LANES, (ROW_TILE // S5_CHUNK) * STAGE_PITCH, LANES), F32)],
        compiler_params=pltpu.CompilerParams(
            dimension_semantics=("parallel", "parallel"), vmem_limit_bytes=VMEM_LIMIT),
        name="in_projection_rope" if use_rope else "in_projection",
    )(*args)
    return dict(zip(names, outs))


def _dot_nt(a, b):
    return lax.dot_general(a, b, (((1,), (1,)), ((), ())), preferred_element_type=F32)


def _swa_kernel(*refs, has_local, n_lat):
    if has_local:
        sink_ref, q_ref, kc_ref, vc_ref, kl_ref, vl_ref, o_ref = refs
    else:
        sink_ref, q_ref, kc_ref, vc_ref, o_ref = refs
    rows = A_GROUP * Q_BLOCK
    band = 3 * Q_BLOCK
    if has_local:
        n = pl.program_id(1)
        start = pl.multiple_of(jnp.clip((n - 1) * Q_BLOCK, 0, n_lat - band), Q_BLOCK)
        kl = kl_ref[pl.ds(start, band), :]
        vl = vl_ref[pl.ds(start, band), :]
        qpos = n * Q_BLOCK + (lax.broadcasted_iota(jnp.int32, (rows, band), 0) & (Q_BLOCK - 1))
        kpos = start + lax.broadcasted_iota(jnp.int32, (rows, band), 1)
        valid = jnp.abs(qpos - kpos) <= WINDOW
    kc = kc_ref[...]
    vc = vc_ref[...]
    for kh in range(A_KV_HEADS):
        lo, hi = kh * HEAD_DIM, (kh + 1) * HEAD_DIM
        q4 = jnp.concatenate(
            [q_ref[:, (kh * A_GROUP + g) * HEAD_DIM:(kh * A_GROUP + g + 1) * HEAD_DIM] for g in range(A_GROUP)],
            axis=0)
        sink = jnp.concatenate(
            [jnp.full((Q_BLOCK, 1), sink_ref[kh * A_GROUP + g], F32) for g in range(A_GROUP)], axis=0)
        s_c = _dot_nt(q4, kc[:, lo:hi])
        m = jnp.maximum(jnp.max(s_c, axis=1, keepdims=True), sink)
        if has_local:
            s_l = jnp.where(valid, _dot_nt(q4, kl[:, lo:hi]), NEG_INF)
            m = jnp.maximum(m, jnp.max(s_l, axis=1, keepdims=True))
        p_c = jnp.exp(s_c - m)
        den = jnp.sum(p_c, axis=1, keepdims=True) + jnp.exp(sink - m)
        o = jnp.dot(p_c.astype(BF16), vc[:, lo:hi], preferred_element_type=F32)
        if has_local:
            p_l = jnp.exp(s_l - m)
            den = den + jnp.sum(p_l, axis=1, keepdims=True)
            o = o + jnp.dot(p_l.astype(BF16), vl[:, lo:hi], preferred_element_type=F32)
        o = o * (1.0 / den)
        for g in range(A_GROUP):
            h = kh * A_GROUP + g
            o_ref[:, h * HEAD_DIM:(h + 1) * HEAD_DIM] = o[g * Q_BLOCK:(g + 1) * Q_BLOCK].astype(o_ref.dtype)


def _swa(sink, q, k_ctx, v_ctx, k_lat=None, v_lat=None):
    bsz, rows, _ = q.shape
    n_ctx = k_ctx.shape[1]
    has_local = k_lat is not None
    in_specs = [
        pl.BlockSpec(memory_space=pltpu.SMEM),
        pl.BlockSpec((None, Q_BLOCK, A_WIDTH), lambda b, n: (b, n, 0)),
        pl.BlockSpec((None, n_ctx, A_KV_WIDTH), lambda b, n: (b, 0, 0)),
        pl.BlockSpec((None, n_ctx, A_KV_WIDTH), lambda b, n: (b, 0, 0)),
    ]
    args = [sink, q, k_ctx, v_ctx]
    n_lat = 0
    if has_local:
        n_lat = k_lat.shape[1]
        in_specs += [pl.BlockSpec((None, n_lat, A_KV_WIDTH), lambda b, n: (b, 0, 0))] * 2
        args += [k_lat, v_lat]
    return pl.pallas_call(
        functools.partial(_swa_kernel, has_local=has_local, n_lat=n_lat),
        grid=(bsz, rows // Q_BLOCK),
        in_specs=in_specs,
        out_specs=pl.BlockSpec((None, Q_BLOCK, A_WIDTH), lambda b, n: (b, n, 0)),
        out_shape=jax.ShapeDtypeStruct((bsz, rows, A_WIDTH), BF16),
        compiler_params=pltpu.CompilerParams(
            dimension_semantics=("parallel", "arbitrary"), vmem_limit_bytes=VMEM_LIMIT),
        name="swa_latent" if has_local else "swa_context",
    )(*args)


def _diff_kernel(*refs, has_lat, lam_init):
    if has_lat:
        lam_ref, sub_ref, q_ref, kc_ref, vc_ref, kl_ref, vl_ref, o_ref = refs
    else:
        lam_ref, sub_ref, q_ref, kc_ref, vc_ref, o_ref = refs
    lp = lam_ref[...]
    lam = (jnp.exp(jnp.sum(lp[0:1] * lp[1:2], axis=1, keepdims=True))
           - jnp.exp(jnp.sum(lp[2:3] * lp[3:4], axis=1, keepdims=True)) + lam_init)
    sub_w = sub_ref[...] * (1.0 - lam_init)
    width = 2 * HEAD_DIM
    for h in range(C_HEADS):
        a_c = None
        a_l = None
        for j in range(2):
            lo = h * width + j * HEAD_DIM
            q = q_ref[:, lo:lo + HEAD_DIM]
            s_c = _dot_nt(q, kc_ref[:, lo:lo + HEAD_DIM])
            m = jnp.max(s_c, axis=1, keepdims=True)
            if has_lat:
                s_l = _dot_nt(q, kl_ref[:, lo:lo + HEAD_DIM])
                m = jnp.maximum(m, jnp.max(s_l, axis=1, keepdims=True))
            p_c = jnp.exp(s_c - m)
            den = jnp.sum(p_c, axis=1, keepdims=True)
            if has_lat:
                p_l = jnp.exp(s_l - m)
                den = den + jnp.sum(p_l, axis=1, keepdims=True)
            coef = 1.0 / den
            if j == 1:
                coef = -lam * coef
            a_c = p_c * coef if a_c is None else a_c + p_c * coef
            if has_lat:
                a_l = p_l * coef if a_l is None else a_l + p_l * coef
        o = jnp.dot(a_c.astype(BF16), vc_ref[:, h * width:(h + 1) * width], preferred_element_type=F32)
        if has_lat:
            o = o + jnp.dot(a_l.astype(BF16), vl_ref[:, h * width:(h + 1) * width],
                            preferred_element_type=F32)
        ms = jnp.mean(o * o, axis=1, keepdims=True)
        o_ref[:, h * width:(h + 1) * width] = (o * lax.rsqrt(ms + EPS) * sub_w).astype(o_ref.dtype)


def _diff_attention(lam_params, subln, lam_init, q, k_ctx, v_ctx, k_lat=None, v_lat=None):
    bsz, rows, _ = q.shape
    n_ctx = k_ctx.shape[1]
    has_lat = k_lat is not None
    in_specs = [
        _const_spec((4, HEAD_DIM)),
        _const_spec((1, 2 * HEAD_DIM)),
        pl.BlockSpec((None, Q_BLOCK, C_WIDTH), lambda b, n: (b, n, 0)),
        pl.BlockSpec((None, n_ctx, C_WIDTH), lambda b, n: (b, 0, 0)),
        pl.BlockSpec((None, n_ctx, C_WIDTH), lambda b, n: (b, 0, 0)),
    ]
    args = [lam_params, subln.reshape(1, 2 * HEAD_DIM), q, k_ctx, v_ctx]
    if has_lat:
        n_lat = k_lat.shape[1]
        in_specs += [pl.BlockSpec((None, n_lat, C_WIDTH), lambda b, n: (b, 0, 0))] * 2
        args += [k_lat, v_lat]
    return pl.pallas_call(
        functools.partial(_diff_kernel, has_lat=has_lat, lam_init=lam_init),
        grid=(bsz, rows // Q_BLOCK),
        in_specs=in_specs,
        out_specs=pl.BlockSpec((None, Q_BLOCK, C_WIDTH), lambda b, n: (b, n, 0)),
        out_shape=jax.ShapeDtypeStruct((bsz, rows, C_WIDTH), BF16),
        compiler_params=pltpu.CompilerParams(
            dimension_semantics=("parallel", "arbitrary"), vmem_limit_bytes=VMEM_LIMIT),
        name="diff_latent" if has_lat else "diff_context",
    )(*args)


def _s5_operators(a_re, a_im, log_dt, b_re, b_im, c_re, c_im, d_skip):
    hp = lax.Precision.HIGHEST
    t_len = S5_CHUNK
    a_re = a_re.astype(F32)
    a_im = a_im.astype(F32)
    dt = jnp.exp(log_dt.astype(F32))[..., None]
    mag = jnp.exp(a_re * dt)
    abar_re = mag * jnp.cos(a_im * dt)
    abar_im = mag * jnp.sin(a_im * dt)
    den = a_re * a_re + a_im * a_im
    nr = abar_re - 1.0
    ni = abar_im
    f_re = ((nr * a_re + ni * a_im) / den)[..., None]
    f_im = ((ni * a_re - nr * a_im) / den)[..., None]
    b_re = b_re.astype(F32)
    b_im = b_im.astype(F32)
    bb_re = f_re * b_re - f_im * b_im
    bb_im = f_re * b_im + f_im * b_re
    tau = jnp.arange(t_len + 1, dtype=F32)[:, None, None, None]
    pmag = jnp.exp(tau * (a_re * dt)[None])
    pw_re = pmag * jnp.cos(tau * (a_im * dt)[None])
    pw_im = pmag * jnp.sin(tau * (a_im * dt)[None])
    c_re = c_re.astype(F32)
    c_im = c_im.astype(F32)
    ca_re = c_re[None] * pw_re[:, :, :, None, :] - c_im[None] * pw_im[:, :, :, None, :]
    ca_im = c_re[None] * pw_im[:, :, :, None, :] + c_im[None] * pw_re[:, :, :, None, :]
    k = (jnp.einsum('tdgop,dgpi->tdgoi', ca_re[:t_len], bb_re, precision=hp)
         - jnp.einsum('tdgop,dgpi->tdgoi', ca_im[:t_len], bb_im, precision=hp))
    jj = jnp.arange(t_len)[:, None]
    ii = jnp.arange(t_len)[None, :]
    lag_f = ii - jj
    kf = jnp.where((lag_f >= 0)[:, :, None, None, None], k[jnp.clip(lag_f, 0, t_len - 1), 0], 0.0)
    kb = jnp.where((lag_f <= 0)[:, :, None, None, None], k[jnp.clip(-lag_f, 0, t_len - 1), 1], 0.0)
    eye_t = (lag_f == 0)[:, :, None, None, None]
    eye_h = jnp.eye(S5_GROUP, dtype=F32)[None, None, None]
    dsk = d_skip.astype(F32).reshape(S5_GROUPS, 1, S5_GROUP)[None, None]
    m_tot = kf + kb + jnp.where(eye_t, eye_h * dsk, 0.0)
    m_tot = jnp.transpose(m_tot, (2, 0, 4, 1, 3)).reshape(S5_GROUPS, S5_CW, S5_CW)

    def local_state(pre, pim, d):
        sre = pre[..., None] * bb_re[d][None] - pim[..., None] * bb_im[d][None]
        sim = pre[..., None] * bb_im[d][None] + pim[..., None] * bb_re[d][None]
        sre = jnp.transpose(sre, (1, 0, 3, 2)).reshape(S5_GROUPS, S5_CW, S5_STATE)
        sim = jnp.transpose(sim, (1, 0, 3, 2)).reshape(S5_GROUPS, S5_CW, S5_STATE)
        return jnp.concatenate([sre, sim, sim, sre], axis=-1)

    s_f = local_state(pw_re[:t_len, 0][::-1], pw_im[:t_len, 0][::-1], 0)
    s_b = local_state(pw_re[:t_len, 1], pw_im[:t_len, 1], 1)
    w1 = jnp.concatenate([m_tot, s_f, s_b], axis=-1).astype(BF16)

    def readout(cre, cim):
        r_re = jnp.transpose(cre, (1, 3, 0, 2)).reshape(S5_GROUPS, S5_STATE, S5_CW)
        r_im = jnp.transpose(-cim, (1, 3, 0, 2)).reshape(S5_GROUPS, S5_STATE, S5_CW)
        return jnp.concatenate([r_re, r_im], axis=1)

    r_f = readout(ca_re[1:, 0], ca_im[1:, 0])
    r_b = readout(ca_re[1:, 1][::-1], ca_im[1:, 1][::-1])
    w2 = jnp.concatenate([r_f, r_b], axis=1).astype(BF16)

    def trans(d):
        are, aim = pw_re[t_len, d], pw_im[t_len, d]
        return (jnp.concatenate([are, are], axis=-1), jnp.concatenate([-aim, aim], axis=-1))

    a1f, a2f = trans(0)
    a1b, a2b = trans(1)
    zero = jnp.zeros_like(a1f)
    coef = jnp.stack([a1f, a2f, a1b, a2b, zero, zero, zero, zero], axis=1)
    return w1, w2, coef


def _s5_kernel(xc_ref, xl_ref, w1_ref, w2_ref, coef_ref, yc_ref, yl_ref,
               zic_ref, zil_ref, zsc_ref, zsl_ref, hc_ref, hl_ref, *, bsz, ctx_chunks, lat_chunks):
    sw = 2 * S5_STATE
    streams = ((xc_ref, zic_ref, zsc_ref, hc_ref, yc_ref, ctx_chunks),
               (xl_ref, zil_ref, zsl_ref, hl_ref, yl_ref, lat_chunks))
    w1 = w1_ref[...]
    for x_ref, zi_ref, zs_ref, _, _, n in streams:
        pitch = n + SUBLANES
        for b in range(bsz):
            z = jnp.dot(x_ref[b * n:(b + 1) * n, :], w1, preferred_element_type=F32)
            zi_ref[b * n:(b + 1) * n, :] = z[:, 0:S5_CW]
            for k in range(4):
                zs_ref[k, b * pitch:b * pitch + n, :] = z[:, S5_CW + k * sw:S5_CW + (k + 1) * sw]

    shape = (bsz, sw)
    a1f = jnp.broadcast_to(coef_ref[0:1, :], shape)
    a2f = jnp.broadcast_to(coef_ref[1:2, :], shape)
    a1b = jnp.broadcast_to(coef_ref[2:3, :], shape)
    a2b = jnp.broadcast_to(coef_ref[3:4, :], shape)

    def locate(c):
        if c < ctx_chunks:
            return zsc_ref, hc_ref, c, ctx_chunks + SUBLANES
        return zsl_ref, hl_ref, c - ctx_chunks, lat_chunks + SUBLANES

    n_chunks = ctx_chunks + lat_chunks
    hf = hfs = hb = hbs = jnp.zeros(shape, F32)
    for t in range(n_chunks):
        zs_ref, h_ref, c, pitch = locate(t)
        rows = pl.ds(c, bsz, stride=pitch)
        h_ref[0, rows, :] = hf
        lf = zs_ref[0, rows, :]
        lfs = zs_ref[1, rows, :]
        hf, hfs = a1f * hf + a2f * hfs + lf, a1f * hfs - a2f * hf + lfs
        cb = ctx_chunks - 1 - t if t < ctx_chunks else n_chunks + ctx_chunks - 1 - t
        zs_ref, h_ref, c, pitch = locate(cb)
        rows = pl.ds(c, bsz, stride=pitch)
        h_ref[1, rows, :] = hb
        lb = zs_ref[2, rows, :]
        lbs = zs_ref[3, rows, :]
        hb, hbs = a1b * hb + a2b * hbs + lb, a1b * hbs - a2b * hb + lbs

    w2 = w2_ref[...]
    for _, zi_ref, _, h_ref, y_ref, n in streams:
        pitch = n + SUBLANES
        for b in range(bsz):
            hin = jnp.concatenate([h_ref[0, b * pitch:b * pitch + n, :], h_ref[1, b * pitch:b * pitch + n, :]],
                                  axis=1)
            y = zi_ref[b * n:(b + 1) * n, :] + jnp.dot(hin.astype(BF16), w2, preferred_element_type=F32)
            y_ref[b * n:(b + 1) * n, :] = y.astype(y_ref.dtype)


def _s5_scan(x_ctx, x_lat, w1, w2, coef, bsz):
    ctx_chunks = x_ctx.shape[1] // bsz
    lat_chunks = x_lat.shape[1] // bsz
    sw = 2 * S5_STATE
    grp = lambda rows, cols: pl.BlockSpec((None, rows, cols), lambda g: (g, 0, 0))
    slab = lambda k, n: pltpu.VMEM((k, bsz * (n + SUBLANES), sw), F32)
    return pl.pallas_call(
        functools.partial(_s5_kernel, bsz=bsz, ctx_chunks=ctx_chunks, lat_chunks=lat_chunks),
        grid=(S5_GROUPS,),
        in_specs=[
            grp(bsz * ctx_chunks, S5_CW), grp(bsz * lat_chunks, S5_CW),
            grp(S5_CW, 3 * S5_CW), grp(4 * S5_STATE, S5_CW), grp(SUBLANES, sw),
        ],
        out_specs=[grp(bsz * ctx_chunks, S5_CW), grp(bsz * lat_chunks, S5_CW)],
        out_shape=[jax.ShapeDtypeStruct(x_ctx.shape, BF16), jax.ShapeDtypeStruct(x_lat.shape, BF16)],
        scratch_shapes=[
            pltpu.VMEM((bsz * ctx_chunks, S5_CW), F32), pltpu.VMEM((bsz * lat_chunks, S5_CW), F32),
            slab(4, ctx_chunks), slab(4, lat_chunks), slab(2, ctx_chunks), slab(2, lat_chunks),
        ],
        compiler_params=pltpu.CompilerParams(
            dimension_semantics=("parallel",), vmem_limit_bytes=VMEM_LIMIT),
        name="s5_scan",
    )(x_ctx, x_lat, w1, w2, coef)


def _gelu_tanh(x):
    return 0.5 * x * (1.0 + jnp.tanh(math.sqrt(2.0 / math.pi) * (x + 0.044715 * (x * x * x))))


def _silu(x):
    return x * jax.nn.sigmoid(x)


def _merge_kernel(h_ref, mod_ref, oa_ref, ga_ref, yb_ref, gb_ref, oc_ref, gc_ref, mg_ref,
                  wglu_ref, bglu_ref, woa_ref, wob_ref, woc_ref, wout_ref, npost_ref, o_ref, stage_ref):
    yb = _gelu_tanh(_groups_to_tokens(yb_ref, stage_ref))
    glu = jnp.dot(yb.astype(BF16), wglu_ref[...], preferred_element_type=F32) + bglu_ref[...]
    ob = yb * jax.nn.sigmoid(glu)

    def branch(o, g_ref, w_ref, m_lo):
        t = (o * _silu(g_ref[...].astype(F32))).astype(BF16)
        t = jnp.dot(t, w_ref[...], preferred_element_type=F32)
        gate = jax.nn.sigmoid(mg_ref[:, m_lo:m_lo + D_MODEL].astype(F32))
        return gate * t

    y = (branch(oa_ref[...].astype(F32), ga_ref, woa_ref, 0)
         + branch(ob, gb_ref, wob_ref, D_MODEL)
         + branch(oc_ref[...].astype(F32), gc_ref, woc_ref, 2 * D_MODEL))
    y = jnp.dot(y.astype(BF16), wout_ref[...], preferred_element_type=F32)
    ms = jnp.mean(y * y, axis=-1, keepdims=True)
    yn = y * lax.rsqrt(ms + EPS) * npost_ref[...]
    gate = mod_ref[:, 2 * D_MODEL:3 * D_MODEL]
    o_ref[...] = h_ref[...] + gate * yn


def _merge(h, mod, mod_row, o_a, g_a, y_b, g_b, o_c, g_c, mg, wts):
    bsz, rows, _ = h.shape
    tiles = rows // ROW_TILE
    if mod_row is None:
        mod_map = lambda b, i: (b, 0, 0)
    else:
        mod_map = lambda b, i: (mod_row, 0, 0)
    tok = lambda w: pl.BlockSpec((None, ROW_TILE, w), lambda b, i: (b, i, 0))
    in_specs = [
        tok(D_MODEL),
        pl.BlockSpec((None, 1, 3 * D_MODEL), mod_map),
        tok(A_WIDTH), tok(A_WIDTH),
        pl.BlockSpec((S5_GROUPS, ROW_TILE // S5_CHUNK, S5_CW), lambda b, i: (0, b * tiles + i, 0)),
        tok(B_WIDTH), tok(C_WIDTH), tok(C_WIDTH), tok(3 * D_MODEL),
        _const_spec((B_WIDTH, B_WIDTH)), _const_spec((1, B_WIDTH)),
        _const_spec((A_WIDTH, D_MODEL)), _const_spec((B_WIDTH, D_MODEL)), _const_spec((C_WIDTH, D_MODEL)),
        _const_spec((D_MODEL, D_MODEL)), _const_spec((1, D_MODEL)),
    ]
    return pl.pallas_call(
        _merge_kernel,
        grid=(bsz, rows // ROW_TILE),
        in_specs=in_specs,
        out_specs=tok(D_MODEL),
        out_shape=jax.ShapeDtypeStruct((bsz, rows, D_MODEL), F32),
        scratch_shapes=[pltpu.VMEM((B_WIDTH // LANES, (ROW_TILE // S5_CHUNK) * STAGE_PITCH, LANES), F32)],
        compiler_params=pltpu.CompilerParams(
            dimension_semantics=("parallel", "parallel"), vmem_limit_bytes=VMEM_LIMIT),
        name="merge_out",
    )(h, mod, o_a, g_a, y_b, g_b, o_c, g_c, mg, *wts)


def _rope_tables(n_tokens):
    rows = n_tokens // GRID_W
    pos_r = jnp.repeat(jnp.arange(rows, dtype=F32), GRID_W)
    pos_c = jnp.tile(jnp.arange(GRID_W, dtype=F32), rows)
    inv = ROPE_BASE ** (-jnp.arange(AX_FREQS, dtype=F32) / AX_FREQS)
    ang_r = pos_r[:, None] * inv[None]
    ang_c = pos_c[:, None] * inv[None]
    ang = jnp.concatenate([ang_r, ang_r, ang_c, ang_c], axis=-1)
    cos, sin = jnp.cos(ang), jnp.sin(ang)
    first = (jnp.arange(HEAD_DIM) % (2 * AX_FREQS)) < AX_FREQS
    sa = jnp.where(first[None], -sin, 0.0)
    sb = jnp.where(first[None], 0.0, sin)
    tile2 = lambda t: jnp.concatenate([t, t], axis=-1)
    return tile2(cos), tile2(sa), tile2(sb)


def kernel(x, c, ctx, c_ctx, w_mod, b_mod, norm_pre, norm_post, w_in, swa_sink, s5_a_re, s5_a_im, s5_log_dt, s5_b_re, s5_b_im, s5_c_re, s5_c_im, s5_d, s5_w_glu, s5_b_glu, diff_lq1, diff_lk1, diff_lq2, diff_lk2, diff_subln, w_o_a, w_o_b, w_o_c, w_out):
    bsz, n_lat, _ = x.shape
    n_ctx = ctx.shape[1]
    ctx_row = bsz
    cc = jnp.zeros((16, D_MODEL), F32).at[:bsz].set(c).at[ctx_row].set(c_ctx)
    mod_all = _modulation(cc, w_mod, b_mod)
    rope_tabs = _rope_tables(n_lat)

    h_lat, h_ctx = x, ctx
    for l in range(DEPTH):
        last = l == DEPTH - 1
        lam_init = 0.8 - 0.6 * math.exp(-0.3 * l)
        mod = mod_all[l].reshape(16, 1, 3 * D_MODEL)
        w_in_l = w_in[l].astype(BF16)
        zl = _in_projection(h_lat, mod, None, norm_pre[l], w_in_l, _SEG_NAMES, rope_tabs)
        ctx_names = ("ak", "av", "bu", "ck", "cv") if last else _SEG_NAMES
        zc = _in_projection(h_ctx, mod, ctx_row, norm_pre[l], w_in_l, ctx_names, None)

        o_a_l = _swa(swa_sink[l], zl["aq"], zc["ak"], zc["av"], zl["ak"], zl["av"])

        w1, w2, coef = _s5_operators(s5_a_re[l], s5_a_im[l], s5_log_dt[l], s5_b_re[l], s5_b_im[l],
                                     s5_c_re[l], s5_c_im[l], s5_d[l])
        y_b_c, y_b_l = _s5_scan(zc["bu"], zl["bu"], w1, w2, coef, bsz)

        lam_params = jnp.stack([diff_lq1[l], diff_lk1[l], diff_lq2[l], diff_lk2[l]]).astype(F32)
        o_c_l = _diff_attention(lam_params, diff_subln[l], lam_init, zl["cq"], zc["ck"], zc["cv"],
                                zl["ck"], zl["cv"])

        wts = (s5_w_glu[l].astype(BF16), s5_b_glu[l].reshape(1, B_WIDTH),
               w_o_a[l].astype(BF16), w_o_b[l].astype(BF16), w_o_c[l].astype(BF16),
               w_out[l].astype(BF16), norm_post[l].reshape(1, D_MODEL))
        h_lat_new = _merge(h_lat, mod, None, o_a_l, zl["ag"], y_b_l, zl["bg"], o_c_l, zl["cg"], zl["mg"], wts)
        if not last:
            o_a_c = _swa(swa_sink[l], zc["aq"], zc["ak"], zc["av"])
            o_c_c = _diff_attention(lam_params, diff_subln[l], lam_init, zc["cq"], zc["ck"], zc["cv"])
            h_ctx = _merge(h_ctx, mod, ctx_row, o_a_c, zc["ag"], y_b_c, zc["bg"], o_c_c, zc["cg"], zc["mg"], wts)
        h_lat = h_lat_new
    return h_lat
```

```python
import functools
import math

import jax
import jax.numpy as jnp
from jax import lax
from jax.experimental import pallas as pl
from jax.experimental.pallas import tpu as pltpu

F32 = jnp.float32
BF16 = jnp.bfloat16

D_MODEL = 1024
DEPTH = 2
GRID_W = 64
HEAD_DIM = 64
WINDOW = 128
ROPE_BASE = 10000.0
AX_FREQS = HEAD_DIM // 4
NEG_INF = -1e30
A_HEADS = 8
A_KV_HEADS = 2
A_GROUP = A_HEADS // A_KV_HEADS
A_WIDTH = A_HEADS * HEAD_DIM
A_KV_WIDTH = A_KV_HEADS * HEAD_DIM
B_WIDTH = 512
S5_GROUP = 16
S5_GROUPS = B_WIDTH // S5_GROUP
S5_STATE = 64
C_HEADS = 4
C_WIDTH = C_HEADS * 2 * HEAD_DIM
EPS = 1e-6

LANES = 128
SUBLANES = 8

ROW_TILE = 256
PROJ_COLS = 512
Q_BLOCK = 128
S5_CHUNK = 16
S5_CW = S5_CHUNK * S5_GROUP
STAGE_PITCH = S5_CHUNK + SUBLANES
VMEM_LIMIT = 56 * 1024 * 1024

_SEG_NAMES = ("aq", "ak", "av", "ag", "bu", "bg", "cq", "ck", "cv", "cg", "mg")
_SEG_WIDTHS = (A_WIDTH, A_KV_WIDTH, A_KV_WIDTH, A_WIDTH, B_WIDTH, B_WIDTH,
               C_WIDTH, C_WIDTH, C_WIDTH, C_WIDTH, 3 * D_MODEL)
SEGS = {}
_off = 0
for _n, _w in zip(_SEG_NAMES, _SEG_WIDTHS):
    SEGS[_n] = (_off, _w)
    _off += _w
IN_WIDTH = _off
ROPE_SEGS = {"aq": HEAD_DIM ** -0.5, "ak": 1.0, "cq": HEAD_DIM ** -0.5, "ck": 1.0}


def _const_spec(shape):
    nd = len(shape)
    return pl.BlockSpec(shape, lambda *_: (0,) * nd, pipeline_mode=pl.Buffered(1))


def _mod_kernel(c_ref, w_ref, b_ref, o_ref):
    c = c_ref[...]
    a = c * jax.nn.sigmoid(c)
    o_ref[...] = jnp.dot(a, w_ref[...], preferred_element_type=F32) + b_ref[...]


def _modulation(cc, w_mod, b_mod):
    n_tiles = 3
    return pl.pallas_call(
        _mod_kernel,
        grid=(DEPTH, n_tiles),
        in_specs=[
            pl.BlockSpec((16, D_MODEL), lambda l, j: (0, 0)),
            pl.BlockSpec((None, D_MODEL, D_MODEL), lambda l, j: (l, 0, j)),
            pl.BlockSpec((None, 1, D_MODEL), lambda l, j: (l, 0, j)),
        ],
        out_specs=pl.BlockSpec((None, 16, D_MODEL), lambda l, j: (l, 0, j)),
        out_shape=jax.ShapeDtypeStruct((DEPTH, 16, 3 * D_MODEL), F32),
        name="modulation",
    )(cc, w_mod, b_mod.reshape(DEPTH, 1, 3 * D_MODEL))


def _block_transpose8(xs):
    lane_blk = lax.broadcasted_iota(jnp.int32, xs[0].shape, 1) // S5_GROUP
    for s in (4, 2, 1):
        keep = (lane_blk & s) == 0
        out = list(xs)
        for p in range(8):
            if p & s:
                continue
            a, b = xs[p], xs[p + s]
            out[p] = jnp.where(keep, a, pltpu.roll(b, S5_GROUP * s, 1))
            out[p + s] = jnp.where(keep, pltpu.roll(a, LANES - S5_GROUP * s, 1), b)
        xs = out
    return xs


def _tokens_to_groups(z, stage_ref, o_ref):
    n_chunks = ROW_TILE // S5_CHUNK
    for v in range(B_WIDTH // LANES):
        for c in range(n_chunks):
            stage_ref[v, c * STAGE_PITCH:c * STAGE_PITCH + S5_CHUNK, :] = \
                z[c * S5_CHUNK:(c + 1) * S5_CHUNK, v * LANES:(v + 1) * LANES]
    for w in range(S5_CW // LANES):
        for v in range(B_WIDTH // LANES):
            src = [stage_ref[v, pl.ds(8 * w + jj, n_chunks, stride=STAGE_PITCH), :] for jj in range(8)]
            dst = _block_transpose8(src)
            for gg in range(8):
                o_ref[8 * v + gg, :, w * LANES:(w + 1) * LANES] = dst[gg].astype(o_ref.dtype)


def _groups_to_tokens(y_ref, stage_ref):
    n_chunks = ROW_TILE // S5_CHUNK
    for w in range(S5_CW // LANES):
        for v in range(B_WIDTH // LANES):
            src = [y_ref[8 * v + gg, :, w * LANES:(w + 1) * LANES].astype(F32) for gg in range(8)]
            dst = _block_transpose8(src)
            for jj in range(8):
                stage_ref[v, pl.ds(8 * w + jj, n_chunks, stride=STAGE_PITCH), :] = dst[jj]
    rows = []
    for c in range(n_chunks):
        rows.append(jnp.concatenate(
            [stage_ref[v, c * STAGE_PITCH:c * STAGE_PITCH + S5_CHUNK, :] for v in range(B_WIDTH // LANES)], axis=1))
    return jnp.concatenate(rows, axis=0)


def _inproj_kernel(*refs, names, use_rope):
    h_ref, mod_ref, nw_ref, w_ref = refs[:4]
    stage_ref = refs[-1]
    refs = refs[:-1]
    if use_rope:
        cos_ref, sa_ref, sb_ref = refs[4:7]
        outs = refs[7:]
    else:
        outs = refs[4:]
    x = h_ref[...]
    ms = jnp.mean(x * x, axis=-1, keepdims=True)
    xn = x * lax.rsqrt(ms + EPS) * nw_ref[...]
    shift = mod_ref[:, 0:D_MODEL]
    scale = mod_ref[:, D_MODEL:2 * D_MODEL]
    hn = (xn * (1.0 + scale) + shift).astype(BF16)
    for name, o_ref in zip(names, outs):
        off, width = SEGS[name]
        if use_rope and name in ROPE_SEGS:
            mul = ROPE_SEGS[name]
            cos = cos_ref[...]
            sa = sa_ref[...]
            sb = sb_ref[...]
            for c in range(width // LANES):
                t = jnp.dot(hn, w_ref[:, off + c * LANES: off + (c + 1) * LANES],
                            preferred_element_type=F32)
                r = t * cos + pltpu.roll(t, LANES - AX_FREQS, 1) * sa + pltpu.roll(t, AX_FREQS, 1) * sb
                if mul != 1.0:
                    r = r * mul
                o_ref[:, c * LANES:(c + 1) * LANES] = r.astype(o_ref.dtype)
        elif name == "bu":
            z = jnp.dot(hn, w_ref[:, off:off + width], preferred_element_type=F32)
            _tokens_to_groups(z, stage_ref, o_ref)
        else:
            mul = ROPE_SEGS.get(name, 1.0)
            for c in range(0, width, PROJ_COLS):
                cw = min(PROJ_COLS, width - c)
                z = jnp.dot(hn, w_ref[:, off + c: off + c + cw], preferred_element_type=F32)
                if mul != 1.0:
                    z = z * mul
                o_ref[:, c:c + cw] = z.astype(o_ref.dtype)


def _in_projection(h, mod, mod_row, norm_w, w_bf16, names, rope_tabs):
    bsz, rows, _ = h.shape
    use_rope = rope_tabs is not None
    if mod_row is None:
        mod_map = lambda b, i: (b, 0, 0)
    else:
        mod_map = lambda b, i: (mod_row, 0, 0)
    in_specs = [
        pl.BlockSpec((None, ROW_TILE, D_MODEL), lambda b, i: (b, i, 0)),
        pl.BlockSpec((None, 1, 3 * D_MODEL), mod_map),
        _const_spec((1, D_MODEL)),
        _const_spec((D_MODEL, IN_WIDTH)),
    ]
    args = [h, mod, norm_w.reshape(1, D_MODEL), w_bf16]
    if use_rope:
        for t in rope_tabs:
            in_specs.append(pl.BlockSpec((ROW_TILE, LANES), lambda b, i: (i, 0)))
            args.append(t)
    tiles = rows // ROW_TILE
    chunk_rows = ROW_TILE // S5_CHUNK
    out_specs, out_shape = [], []
    for n in names:
        if n == "bu":
            out_specs.append(pl.BlockSpec((S5_GROUPS, chunk_rows, S5_CW), lambda b, i: (0, b * tiles + i, 0)))
            out_shape.append(jax.ShapeDtypeStruct((S5_GROUPS, bsz * rows // S5_CHUNK, S5_CW), BF16))
        else:
            out_specs.append(pl.BlockSpec((None, ROW_TILE, SEGS[n][1]), lambda b, i: (b, i, 0)))
            out_shape.append(jax.ShapeDtypeStruct((bsz, rows, SEGS[n][1]), BF16))
    outs = pl.pallas_call(
        functools.partial(_inproj_kernel, names=tuple(names), use_rope=use_rope),
        grid=(bsz, tiles),
        in_specs=in_specs,
        out_specs=out_specs,
        out_shape=out_shape,
        scratch_shapes=[pltpu.VMEM((B_WIDTH // LANES, (ROW_TILE // S5_CHUNK) * STAGE_PITCH, LANES), F32)],
        compiler_params=pltpu.CompilerParams(
            dimension_semantics=("parallel", "parallel"), vmem_limit_bytes=VMEM_LIMIT),
        name="in_projection_rope" if use_rope else "in_projection",
    )(*args)
    return dict(zip(names, outs))


def _dot_nt(a, b):
    return lax.dot_general(a, b, (((1,), (1,)), ((), ())), preferred_element_type=F32)


def _swa_kernel(*refs, has_local, n_lat):
    if has_local:
        sink_ref, q_ref, kc_ref, vc_ref, kl_ref, vl_ref, o_ref = refs
    else:
        sink_ref, q_ref, kc_ref, vc_ref, o_ref = refs
    rows = A_GROUP * Q_BLOCK
    band = 3 * Q_BLOCK
    if has_local:
        n = pl.program_id(1)
        start = pl.multiple_of(jnp.clip((n - 1) * Q_BLOCK, 0, n_lat - band), Q_BLOCK)
        kl = kl_ref[pl.ds(start, band), :]
        vl = vl_ref[pl.ds(start, band), :]
        qpos = n * Q_BLOCK + (lax.broadcasted_iota(jnp.int32, (rows, band), 0) & (Q_BLOCK - 1))
        kpos = start + lax.broadcasted_iota(jnp.int32, (rows, band), 1)
        valid = jnp.abs(qpos - kpos) <= WINDOW
    kc = kc_ref[...]
    vc = vc_ref[...]
    for kh in range(A_KV_HEADS):
        lo, hi = kh * HEAD_DIM, (kh + 1) * HEAD_DIM
        q4 = jnp.concatenate(
            [q_ref[:, (kh * A_GROUP + g) * HEAD_DIM:(kh * A_GROUP + g + 1) * HEAD_DIM] for g in range(A_GROUP)],
            axis=0)
        sink = jnp.concatenate(
            [jnp.full((Q_BLOCK, 1), sink_ref[kh * A_GROUP + g], F32) for g in range(A_GROUP)], axis=0)
        s_c = _dot_nt(q4, kc[:, lo:hi])
        m = jnp.maximum(jnp.max(s_c, axis=1, keepdims=True), sink)
        if has_local:
            s_l = jnp.where(valid, _dot_nt(q4, kl[:, lo:hi]), NEG_INF)
            m = jnp.maximum(m, jnp.max(s_l, axis=1, keepdims=True))
        p_c = jnp.exp(s_c - m)
        den = jnp.sum(p_c, axis=1, keepdims=True) + jnp.exp(sink - m)
        o = jnp.dot(p_c.astype(BF16), vc[:, lo:hi], preferred_element_type=F32)
        if has_local:
            p_l = jnp.exp(s_l - m)
            den = den + jnp.sum(p_l, axis=1, keepdims=True)
            o = o + jnp.dot(p_l.astype(BF16), vl[:, lo:hi], preferred_element_type=F32)
        o = o * (1.0 / den)
        for g in range(A_GROUP):
            h = kh * A_GROUP + g
            o_ref[:, h * HEAD_DIM:(h + 1) * HEAD_DIM] = o[g * Q_BLOCK:(g + 1) * Q_BLOCK].astype(o_ref.dtype)


def _swa(sink, q, k_ctx, v_ctx, k_lat=None, v_lat=None):
    bsz, rows, _ = q.shape
    n_ctx = k_ctx.shape[1]
    has_local = k_lat is not None
    in_specs = [
        pl.BlockSpec(memory_space=pltpu.SMEM),
        pl.BlockSpec((None, Q_BLOCK, A_WIDTH), lambda b, n: (b, n, 0)),
        pl.BlockSpec((None, n_ctx, A_KV_WIDTH), lambda b, n: (b, 0, 0)),
        pl.BlockSpec((None, n_ctx, A_KV_WIDTH), lambda b, n: (b, 0, 0)),
    ]
    args = [sink, q, k_ctx, v_ctx]
    n_lat = 0
    if has_local:
        n_lat = k_lat.shape[1]
        in_specs += [pl.BlockSpec((None, n_lat, A_KV_WIDTH), lambda b, n: (b, 0, 0))] * 2
        args += [k_lat, v_lat]
    return pl.pallas_call(
        functools.partial(_swa_kernel, has_local=has_local, n_lat=n_lat),
        grid=(bsz, rows // Q_BLOCK),
        in_specs=in_specs,
        out_specs=pl.BlockSpec((None, Q_BLOCK, A_WIDTH), lambda b, n: (b, n, 0)),
        out_shape=jax.ShapeDtypeStruct((bsz, rows, A_WIDTH), BF16),
        compiler_params=pltpu.CompilerParams(
            dimension_semantics=("parallel", "arbitrary"), vmem_limit_bytes=VMEM_LIMIT),
        name="swa_latent" if has_local else "swa_context",
    )(*args)


def _diff_kernel(*refs, has_lat, lam_init):
    if has_lat:
        lam_ref, sub_ref, q_ref, kc_ref, vc_ref, kl_ref, vl_ref, o_ref = refs
    else:
        lam_ref, sub_ref, q_ref, kc_ref, vc_ref, o_ref = refs
    lp = lam_ref[...]
    lam = (jnp.exp(jnp.sum(lp[0:1] * lp[1:2], axis=1, keepdims=True))
           - jnp.exp(jnp.sum(lp[2:3] * lp[3:4], axis=1, keepdims=True)) + lam_init)
    sub_w = sub_ref[...] * (1.0 - lam_init)
    width = 2 * HEAD_DIM
    for h in range(C_HEADS):
        a_c = None
        a_l = None
        for j in range(2):
            lo = h * width + j * HEAD_DIM
            q = q_ref[:, lo:lo + HEAD_DIM]
            s_c = _dot_nt(q, kc_ref[:, lo:lo + HEAD_DIM])
            m = jnp.max(s_c, axis=1, keepdims=True)
            if has_lat:
                s_l = _dot_nt(q, kl_ref[:, lo:lo + HEAD_DIM])
                m = jnp.maximum(m, jnp.max(s_l, axis=1, keepdims=True))
            p_c = jnp.exp(s_c - m)
            den = jnp.sum(p_c, axis=1, keepdims=True)
            if has_lat:
                p_l = jnp.exp(s_l - m)
                den = den + jnp.sum(p_l, axis=1, keepdims=True)
            coef = 1.0 / den
            if j == 1:
                coef = -lam * coef
            a_c = p_c * coef if a_c is None else a_c + p_c * coef
            if has_lat:
                a_l = p_l * coef if a_l is None else a_l + p_l * coef
        o = jnp.dot(a_c.astype(BF16), vc_ref[:, h * width:(h + 1) * width], preferred_element_type=F32)
        if has_lat:
            o = o + jnp.dot(a_l.astype(BF16), vl_ref[:, h * width:(h + 1) * width],
                            preferred_element_type=F32)
        ms = jnp.mean(o * o, axis=1, keepdims=True)
        o_ref[:, h * width:(h + 1) * width] = (o * lax.rsqrt(ms + EPS) * sub_w).astype(o_ref.dtype)


def _diff_attention(lam_params, subln, lam_init, q, k_ctx, v_ctx, k_lat=None, v_lat=None):
    bsz, rows, _ = q.shape
    n_ctx = k_ctx.shape[1]
    has_lat = k_lat is not None
    in_specs = [
        _const_spec((4, HEAD_DIM)),
        _const_spec((1, 2 * HEAD_DIM)),
        pl.BlockSpec((None, Q_BLOCK, C_WIDTH), lambda b, n: (b, n, 0)),
        pl.BlockSpec((None, n_ctx, C_WIDTH), lambda b, n: (b, 0, 0)),
        pl.BlockSpec((None, n_ctx, C_WIDTH), lambda b, n: (b, 0, 0)),
    ]
    args = [lam_params, subln.reshape(1, 2 * HEAD_DIM), q, k_ctx, v_ctx]
    if has_lat:
        n_lat = k_lat.shape[1]
        in_specs += [pl.BlockSpec((None, n_lat, C_WIDTH), lambda b, n: (b, 0, 0))] * 2
        args += [k_lat, v_lat]
    return pl.pallas_call(
        functools.partial(_diff_kernel, has_lat=has_lat, lam_init=lam_init),
        grid=(bsz, rows // Q_BLOCK),
        in_specs=in_specs,
        out_specs=pl.BlockSpec((None, Q_BLOCK, C_WIDTH), lambda b, n: (b, n, 0)),
        out_shape=jax.ShapeDtypeStruct((bsz, rows, C_WIDTH), BF16),
        compiler_params=pltpu.CompilerParams(
            dimension_semantics=("parallel", "arbitrary"), vmem_limit_bytes=VMEM_LIMIT),
        name="diff_latent" if has_lat else "diff_context",
    )(*args)


def _s5_operators(a_re, a_im, log_dt, b_re, b_im, c_re, c_im, d_skip):
    f = lambda t: t.astype(F32)
    quad = lambda t: jnp.concatenate([t, t, t, t], axis=-1)
    a_re, a_im, b_re, b_im, c_re, c_im = map(f, (a_re, a_im, b_re, b_im, c_re, c_im))
    g, p, h = S5_GROUPS, S5_STATE, S5_GROUP
    dt = jnp.exp(f(log_dt))[..., None]
    mag = jnp.exp(a_re * dt)
    abar_re = mag * jnp.cos(a_im * dt)
    abar_im = mag * jnp.sin(a_im * dt)
    den = a_re * a_re + a_im * a_im
    nr = abar_re - 1.0
    ni = abar_im
    f_re = (nr * a_re + ni * a_im) / den
    f_im = (ni * a_re - nr * a_im) / den
    pw_re, pw_im = [jnp.ones_like(abar_re)], [jnp.zeros_like(abar_im)]
    for _ in range(S5_CHUNK):
        pre, pim = pw_re[-1], pw_im[-1]
        pw_re.append(pre * abar_re - pim * abar_im)
        pw_im.append(pre * abar_im + pim * abar_re)
    pad = [jnp.zeros_like(abar_re)] * (3 * SUBLANES - S5_CHUNK - 1)
    pw = jnp.stack([jnp.stack(pw_re + pad, axis=2), jnp.stack(pw_im + pad, axis=2)], axis=1)
    pw = jnp.transpose(quad(pw), (2, 0, 1, 3, 4))
    dsk = jnp.tile(f(d_skip).reshape(g, h), (1, S5_CW // h))
    sgn = jnp.broadcast_to(jnp.repeat(jnp.array([-1.0, 1.0, 1.0, -1.0], F32), p), (g, 4 * p))
    zero = jnp.zeros_like(dsk)
    rows = jnp.stack([quad(f_re[0]), quad(f_im[0]), quad(f_re[1]), quad(f_im[1]), dsk, sgn, zero, zero],
                     axis=1)
    bt_re = jnp.swapaxes(b_re, -1, -2)
    bt_im = jnp.swapaxes(b_im, -1, -2)
    bpk = jnp.stack([jnp.concatenate([bt_re, bt_im, bt_im, bt_re], axis=-1),
                     jnp.concatenate([bt_im, bt_re, bt_re, bt_im], axis=-1)], axis=1)
    cpk = jnp.stack([jnp.concatenate([c_re, -c_im], axis=-1),
                     jnp.concatenate([-c_im, -c_re], axis=-1)], axis=1)
    bpk = jnp.transpose(bpk, (2, 0, 1, 3, 4))
    cpk = jnp.transpose(cpk, (2, 0, 1, 3, 4))
    grp = lambda shape: pl.BlockSpec((None,) + shape, lambda i: (i,) + (0,) * len(shape))
    return pl.pallas_call(
        _s5_operator_kernel,
        grid=(g,),
        in_specs=[grp((8, 4 * p)), grp((2, 2, 3 * SUBLANES, 4 * p)), grp((2, 2, h, 4 * p)),
                  grp((2, 2, h, 2 * p))],
        out_specs=[grp((S5_CW, 3 * S5_CW)), grp((4 * p, S5_CW)), grp((SUBLANES, 2 * p))],
        out_shape=[jax.ShapeDtypeStruct((g, S5_CW, 3 * S5_CW), BF16),
                   jax.ShapeDtypeStruct((g, 4 * p, S5_CW), BF16),
                   jax.ShapeDtypeStruct((g, SUBLANES, 2 * p), F32)],
        compiler_params=pltpu.CompilerParams(dimension_semantics=("parallel",)),
        name="s5_operators",
    )(rows, pw, bpk, cpk)


def _shift_lanes(x, s):
    lo, hi = x[:, :LANES], x[:, LANES:]
    lane = lax.broadcasted_iota(jnp.int32, lo.shape, 1)
    zero = jnp.zeros_like(lo)
    rot = lambda t, r: pltpu.roll(t, r, 1) if r % LANES else t
    if s >= 0:
        if s < LANES:
            rl, rh = rot(lo, s), rot(hi, s)
            out = (jnp.where(lane >= s, rl, 0.0), jnp.where(lane >= s, rh, rl))
        else:
            rl = rot(lo, s - LANES)
            out = (zero, jnp.where(lane >= s - LANES, rl, 0.0))
    else:
        s = -s
        if s < LANES:
            rl, rh = rot(lo, LANES - s), rot(hi, LANES - s)
            out = (jnp.where(lane < LANES - s, rl, rh), jnp.where(lane < LANES - s, rh, 0.0))
        else:
            rh = rot(hi, 2 * LANES - s)
            out = (jnp.where(lane < 2 * LANES - s, rh, 0.0), zero)
    return jnp.concatenate(out, axis=1)


def _s5_operator_kernel(rows_ref, pw_ref, b_ref, c_ref, w1_ref, w2_ref, coef_ref):
    t_len = S5_CHUNK
    half = 2 * S5_STATE
    sgn = rows_ref[5:6, :]
    a4, bx4, pr4, pi4, g2 = [], [], [], [], []
    for d in range(2):
        f_re = rows_ref[2 * d:2 * d + 1, :]
        f_im = rows_ref[2 * d + 1:2 * d + 2, :]
        p1 = b_ref[d, 0]
        p2 = b_ref[d, 1]
        a4.append(f_re * p1 + sgn * f_im * p2)
        bx4.append(f_re * p2 - sgn * f_im * p1)
        pr4.append(pw_ref[d, 0])
        pi4.append(pw_ref[d, 1])
        cx = jnp.concatenate([c_ref[d, 0]] * t_len, axis=0)
        cy = jnp.concatenate([c_ref[d, 1]] * t_len, axis=0)

        def ca(order, d=d, cx=cx, cy=cy):
            pr = jnp.concatenate(
                [jnp.broadcast_to(pr4[d][t:t + 1, :half], (S5_GROUP, half)) for t in order], axis=0)
            pi = jnp.concatenate(
                [jnp.broadcast_to(pi4[d][t:t + 1, :half], (S5_GROUP, half)) for t in order], axis=0)
            return cx * pr + cy * pi
        g2.append(ca)

    def strip(d, order):
        return lax.dot_general(a4[d][:, :half], g2[d](order), (((1,), (1,)), ((), ())),
                               precision=lax.Precision.HIGHEST, preferred_element_type=F32)

    row = lax.broadcasted_iota(jnp.int32, (S5_GROUP, S5_CW), 0)
    col = lax.broadcasted_iota(jnp.int32, (S5_GROUP, S5_CW), 1)
    k_fwd = strip(0, range(t_len)) + jnp.where(row == col, rows_ref[4:5, :], 0.0)
    k_bwd = strip(1, [t_len - 1 - k for k in range(t_len)])
    for j in range(t_len):
        lo, hi = j * S5_GROUP, (j + 1) * S5_GROUP
        m = _shift_lanes(k_fwd, j * S5_GROUP) + _shift_lanes(k_bwd, -(t_len - 1 - j) * S5_GROUP)
        w1_ref[lo:hi, 0:S5_CW] = m.astype(w1_ref.dtype)
        tf, tb = t_len - 1 - j, j
        s_f = a4[0] * pr4[0][tf:tf + 1, :] + bx4[0] * (sgn * pi4[0][tf:tf + 1, :])
        s_b = a4[1] * pr4[1][tb:tb + 1, :] + bx4[1] * (sgn * pi4[1][tb:tb + 1, :])
        w1_ref[lo:hi, S5_CW:2 * S5_CW] = s_f.astype(w1_ref.dtype)
        w1_ref[lo:hi, 2 * S5_CW:3 * S5_CW] = s_b.astype(w1_ref.dtype)
    w2_ref[0:half, :] = g2[0]([i + 1 for i in range(t_len)]).T.astype(w2_ref.dtype)
    w2_ref[half:2 * half, :] = g2[1]([t_len - i for i in range(t_len)]).T.astype(w2_ref.dtype)
    zero = jnp.zeros((1, half), F32)
    coef_ref[...] = jnp.concatenate(
        [pr4[0][t_len:t_len + 1, :half], (sgn * pi4[0][t_len:t_len + 1, :])[:, :half],
         pr4[1][t_len:t_len + 1, :half], (sgn * pi4[1][t_len:t_len + 1, :])[:, :half],
         zero, zero, zero, zero], axis=0)


def _s5_kernel(xc_ref, xl_ref, w1_ref, w2_ref, coef_ref, yc_ref, yl_ref,
               zic_ref, zil_ref, zsc_ref, zsl_ref, hc_ref, hl_ref, *, bsz, ctx_chunks, lat_chunks):
    sw = 2 * S5_STATE
    streams = ((xc_ref, zic_ref, zsc_ref, hc_ref, yc_ref, ctx_chunks),
               (xl_ref, zil_ref, zsl_ref, hl_ref, yl_ref, lat_chunks))
    w1 = w1_ref[...]
    for x_ref, zi_ref, zs_ref, _, _, n in streams:
        pitch = n + SUBLANES
        for b in range(bsz):
            z = jnp.dot(x_ref[b * n:(b + 1) * n, :], w1, preferred_element_type=F32)
            zi_ref[b * n:(b + 1) * n, :] = z[:, 0:S5_CW]
            for k in range(4):
                zs_ref[k, b * pitch:b * pitch + n, :] = z[:, S5_CW + k * sw:S5_CW + (k + 1) * sw]

    shape = (bsz, sw)
    a1f = jnp.broadcast_to(coef_ref[0:1, :], shape)
    a2f = jnp.broadcast_to(coef_ref[1:2, :], shape)
    a1b = jnp.broadcast_to(coef_ref[2:3, :], shape)
    a2b = jnp.broadcast_to(coef_ref[3:4, :], shape)

    def locate(c):
        if c < ctx_chunks:
            return zsc_ref, hc_ref, c, ctx_chunks + SUBLANES
        return zsl_ref, hl_ref, c - ctx_chunks, lat_chunks + SUBLANES

    n_chunks = ctx_chunks + lat_chunks
    hf = hfs = hb = hbs = jnp.zeros(shape, F32)
    for t in range(n_chunks):
        zs_ref, h_ref, c, pitch = locate(t)
        rows = pl.ds(c, bsz, stride=pitch)
        h_ref[0, rows, :] = hf
        lf = zs_ref[0, rows, :]
        lfs = zs_ref[1, rows, :]
        hf, hfs = a1f * hf + a2f * hfs + lf, a1f * hfs - a2f * hf + lfs
        cb = ctx_chunks - 1 - t if t < ctx_chunks else n_chunks + ctx_chunks - 1 - t
        zs_ref, h_ref, c, pitch = locate(cb)
        rows = pl.ds(c, bsz, stride=pitch)
        h_ref[1, rows, :] = hb
        lb = zs_ref[2, rows, :]
        lbs = zs_ref[3, rows, :]
        hb, hbs = a1b * hb + a2b * hbs + lb, a1b * hbs - a2b * hb + lbs

    w2 = w2_ref[...]
    for _, zi_ref, _, h_ref, y_ref, n in streams:
        pitch = n + SUBLANES
        for b in range(bsz):
            hin = jnp.concatenate([h_ref[0, b * pitch:b * pitch + n, :], h_ref[1, b * pitch:b * pitch + n, :]],
                                  axis=1)
            y = zi_ref[b * n:(b + 1) * n, :] + jnp.dot(hin.astype(BF16), w2, preferred_element_type=F32)
            y_ref[b * n:(b + 1) * n, :] = y.astype(y_ref.dtype)


def _s5_scan(x_ctx, x_lat, w1, w2, coef, bsz):
    ctx_chunks = x_ctx.shape[1] // bsz
    lat_chunks = x_lat.shape[1] // bsz
    sw = 2 * S5_STATE
    grp = lambda rows, cols: pl.BlockSpec((None, rows, cols), lambda g: (g, 0, 0))
    slab = lambda k, n: pltpu.VMEM((k, bsz * (n + SUBLANES), sw), F32)
    return pl.pallas_call(
        functools.partial(_s5_kernel, bsz=bsz, ctx_chunks=ctx_chunks, lat_chunks=lat_chunks),
        grid=(S5_GROUPS,),
        in_specs=[
            grp(bsz * ctx_chunks, S5_CW), grp(bsz * lat_chunks, S5_CW),
            grp(S5_CW, 3 * S5_CW), grp(4 * S5_STATE, S5_CW), grp(SUBLANES, sw),
        ],
        out_specs=[grp(bsz * ctx_chunks, S5_CW), grp(bsz * lat_chunks, S5_CW)],
        out_shape=[jax.ShapeDtypeStruct(x_ctx.shape, BF16), jax.ShapeDtypeStruct(x_lat.shape, BF16)],
        scratch_shapes=[
            pltpu.VMEM((bsz * ctx_chunks, S5_CW), F32), pltpu.VMEM((bsz * lat_chunks, S5_CW), F32),
            slab(4, ctx_chunks), slab(4, lat_chunks), slab(2, ctx_chunks), slab(2, lat_chunks),
        ],
        compiler_params=pltpu.CompilerParams(
            dimension_semantics=("parallel",), vmem_limit_bytes=VMEM_LIMIT),
        name="s5_scan",
    )(x_ctx, x_lat, w1, w2, coef)


def _gelu_tanh(x):
    return 0.5 * x * (1.0 + jnp.tanh(math.sqrt(2.0 / math.pi) * (x + 0.044715 * (x * x * x))))


def _silu(x):
    return x * jax.nn.sigmoid(x)


def _merge_kernel(h_ref, mod_ref, oa_ref, ga_ref, yb_ref, gb_ref, oc_ref, gc_ref, mg_ref,
                  wglu_ref, bglu_ref, woa_ref, wob_ref, woc_ref, wout_ref, npost_ref, o_ref, stage_ref):
    yb = _gelu_tanh(_groups_to_tokens(yb_ref, stage_ref))
    glu = jnp.dot(yb.astype(BF16), wglu_ref[...], preferred_element_type=F32) + bglu_ref[...]
    ob = yb * jax.nn.sigmoid(glu)

    def branch(o, g_ref, w_ref, m_lo):
        t = (o * _silu(g_ref[...].astype(F32))).astype(BF16)
        t = jnp.dot(t, w_ref[...], preferred_element_type=F32)
        gate = jax.nn.sigmoid(mg_ref[:, m_lo:m_lo + D_MODEL].astype(F32))
        return gate * t

    y = (branch(oa_ref[...].astype(F32), ga_ref, woa_ref, 0)
         + branch(ob, gb_ref, wob_ref, D_MODEL)
         + branch(oc_ref[...].astype(F32), gc_ref, woc_ref, 2 * D_MODEL))
    y = jnp.dot(y.astype(BF16), wout_ref[...], preferred_element_type=F32)
    ms = jnp.mean(y * y, axis=-1, keepdims=True)
    yn = y * lax.rsqrt(ms + EPS) * npost_ref[...]
    gate = mod_ref[:, 2 * D_MODEL:3 * D_MODEL]
    o_ref[...] = h_ref[...] + gate * yn


def _merge(h, mod, mod_row, o_a, g_a, y_b, g_b, o_c, g_c, mg, wts):
    bsz, rows, _ = h.shape
    tiles = rows // ROW_TILE
    if mod_row is None:
        mod_map = lambda b, i: (b, 0, 0)
    else:
        mod_map = lambda b, i: (mod_row, 0, 0)
    tok = lambda w: pl.BlockSpec((None, ROW_TILE, w), lambda b, i: (b, i, 0))
    in_specs = [
        tok(D_MODEL),
        pl.BlockSpec((None, 1, 3 * D_MODEL), mod_map),
        tok(A_WIDTH), tok(A_WIDTH),
        pl.BlockSpec((S5_GROUPS, ROW_TILE // S5_CHUNK, S5_CW), lambda b, i: (0, b * tiles + i, 0)),
        tok(B_WIDTH), tok(C_WIDTH), tok(C_WIDTH), tok(3 * D_MODEL),
        _const_spec((B_WIDTH, B_WIDTH)), _const_spec((1, B_WIDTH)),
        _const_spec((A_WIDTH, D_MODEL)), _const_spec((B_WIDTH, D_MODEL)), _const_spec((C_WIDTH, D_MODEL)),
        _const_spec((D_MODEL, D_MODEL)), _const_spec((1, D_MODEL)),
    ]
    return pl.pallas_call(
        _merge_kernel,
        grid=(bsz, rows // ROW_TILE),
        in_specs=in_specs,
        out_specs=tok(D_MODEL),
        out_shape=jax.ShapeDtypeStruct((bsz, rows, D_MODEL), F32),
        scratch_shapes=[pltpu.VMEM((B_WIDTH // LANES, (ROW_TILE // S5_CHUNK) * STAGE_PITCH, LANES), F32)],
        compiler_params=pltpu.CompilerParams(
            dimension_semantics=("parallel", "parallel"), vmem_limit_bytes=VMEM_LIMIT),
        name="merge_out",
    )(h, mod, o_a, g_a, y_b, g_b, o_c, g_c, mg, *wts)


def _rope_tables(n_tokens):
    rows = n_tokens // GRID_W
    pos_r = jnp.repeat(jnp.arange(rows, dtype=F32), GRID_W)
    pos_c = jnp.tile(jnp.arange(GRID_W, dtype=F32), rows)
    inv = ROPE_BASE ** (-jnp.arange(AX_FREQS, dtype=F32) / AX_FREQS)
    ang_r = pos_r[:, None] * inv[None]
    ang_c = pos_c[:, None] * inv[None]
    ang = jnp.concatenate([ang_r, ang_r, ang_c, ang_c], axis=-1)
    cos, sin = jnp.cos(ang), jnp.sin(ang)
    first = (jnp.arange(HEAD_DIM) % (2 * AX_FREQS)) < AX_FREQS
    sa = jnp.where(first[None], -sin, 0.0)
    sb = jnp.where(first[None], 0.0, sin)
    tile2 = lambda t: jnp.concatenate([t, t], axis=-1)
    return tile2(cos), tile2(sa), tile2(sb)


def kernel(x, c, ctx, c_ctx, w_mod, b_mod, norm_pre, norm_post, w_in, swa_sink, s5_a_re, s5_a_im, s5_log_dt, s5_b_re, s5_b_im, s5_c_re, s5_c_im, s5_d, s5_w_glu, s5_b_glu, diff_lq1, diff_lk1, diff_lq2, diff_lk2, diff_subln, w_o_a, w_o_b, w_o_c, w_out):
    bsz, n_lat, _ = x.shape
    n_ctx = ctx.shape[1]
    ctx_row = bsz
    cc = jnp.zeros((16, D_MODEL), F32).at[:bsz].set(c).at[ctx_row].set(c_ctx)
    mod_all = _modulation(cc, w_mod, b_mod)
    rope_tabs = _rope_tables(n_lat)

    h_lat, h_ctx = x, ctx
    for l in range(DEPTH):
        last = l == DEPTH - 1
        lam_init = 0.8 - 0.6 * math.exp(-0.3 * l)
        mod = mod_all[l].reshape(16, 1, 3 * D_MODEL)
        w_in_l = w_in[l].astype(BF16)
        zl = _in_projection(h_lat, mod, None, norm_pre[l], w_in_l, _SEG_NAMES, rope_tabs)
        ctx_names = ("ak", "av", "bu", "ck", "cv") if last else _SEG_NAMES
        zc = _in_projection(h_ctx, mod, ctx_row, norm_pre[l], w_in_l, ctx_names, None)

        o_a_l = _swa(swa_sink[l], zl["aq"], zc["ak"], zc["av"], zl["ak"], zl["av"])

        w1, w2, coef = _s5_operators(s5_a_re[l], s5_a_im[l], s5_log_dt[l], s5_b_re[l], s5_b_im[l],
                                     s5_c_re[l], s5_c_im[l], s5_d[l])
        y_b_c, y_b_l = _s5_scan(zc["bu"], zl["bu"], w1, w2, coef, bsz)

        lam_params = jnp.stack([diff_lq1[l], diff_lk1[l], diff_lq2[l], diff_lk2[l]]).astype(F32)
        o_c_l = _diff_attention(lam_params, diff_subln[l], lam_init, zl["cq"], zc["ck"], zc["cv"],
                                zl["ck"], zl["cv"])

        wts = (s5_w_glu[l].astype(BF16), s5_b_glu[l].reshape(1, B_WIDTH),
               w_o_a[l].astype(BF16), w_o_b[l].astype(BF16), w_o_c[l].astype(BF16),
               w_out[l].astype(BF16), norm_post[l].reshape(1, D_MODEL))
        h_lat_new = _merge(h_lat, mod, None, o_a_l, zl["ag"], y_b_l, zl["bg"], o_c_l, zl["cg"], zl["mg"], wts)
        if not last:
            o_a_c = _swa(swa_sink[l], zc["aq"], zc["ak"], zc["av"])
            o_c_c = _diff_attention(lam_params, diff_subln[l], lam_init, zc["cq"], zc["ck"], zc["cv"])
            h_ctx = _merge(h_ctx, mod, ctx_row, o_a_c, zc["ag"], y_b_c, zc["bg"], o_c_c, zc["cg"], zc["mg"], wts)
        h_lat = h_lat_new
    return h_lat
```

```python
import functools
import math

import jax
import jax.numpy as jnp
from jax import lax
from jax.experimental import pallas as pl
from jax.experimental.pallas import tpu as pltpu

F32 = jnp.float32
BF16 = jnp.bfloat16

D_MODEL = 1024
DEPTH = 2
GRID_W = 64
HEAD_DIM = 64
WINDOW = 128
ROPE_BASE = 10000.0
AX_FREQS = HEAD_DIM // 4
NEG_INF = -1e30
A_HEADS = 8
A_KV_HEADS = 2
A_GROUP = A_HEADS // A_KV_HEADS
A_WIDTH = A_HEADS * HEAD_DIM
A_KV_WIDTH = A_KV_HEADS * HEAD_DIM
B_WIDTH = 512
S5_GROUP = 16
S5_GROUPS = B_WIDTH // S5_GROUP
S5_STATE = 64
C_HEADS = 4
C_WIDTH = C_HEADS * 2 * HEAD_DIM
N_BRANCH = 3
GATE_WIDTH = A_WIDTH + B_WIDTH + C_WIDTH + N_BRANCH * D_MODEL
EPS = 1e-6

LANES = 128
SUBLANES = 8

ROW_TILE = 256
PROJ_COLS = 512
Q_BLOCK = 128
A_HEAD_ORDER = tuple(h for v in range(4) for h in (v, 4 + v))
S5_CHUNK = 16
S5_CW = S5_CHUNK * S5_GROUP
STAGE_PITCH = S5_CHUNK + SUBLANES
VMEM_LIMIT = 56 * 1024 * 1024

_SEG_NAMES = ("aq", "ak", "av", "ag", "bu", "bg", "cq", "ck", "cv", "cg", "mg")
_SEG_WIDTHS = (A_WIDTH, A_KV_WIDTH, A_KV_WIDTH, A_WIDTH, B_WIDTH, B_WIDTH,
               C_WIDTH, C_WIDTH, C_WIDTH, C_WIDTH, 3 * D_MODEL)
SEGS = {}
_off = 0
for _n, _w in zip(_SEG_NAMES, _SEG_WIDTHS):
    SEGS[_n] = (_off, _w)
    _off += _w
IN_WIDTH = _off
PROJ_NAMES = ("aq", "ak", "av", "bu", "cq", "ck", "cv")
ROPE_SEGS = {"aq": HEAD_DIM ** -0.5, "ak": 1.0, "cq": HEAD_DIM ** -0.5, "ck": 1.0}


def _const_spec(shape):
    nd = len(shape)
    return pl.BlockSpec(shape, lambda *_: (0,) * nd, pipeline_mode=pl.Buffered(1))


def _mod_kernel(c_ref, w_ref, b_ref, o_ref):
    c = c_ref[...]
    a = c * jax.nn.sigmoid(c)
    o_ref[...] = jnp.dot(a, w_ref[...], preferred_element_type=F32) + b_ref[...]


def _modulation(cc, w_mod, b_mod):
    n_tiles = 3
    return pl.pallas_call(
        _mod_kernel,
        grid=(DEPTH, n_tiles),
        in_specs=[
            pl.BlockSpec((16, D_MODEL), lambda l, j: (0, 0)),
            pl.BlockSpec((None, D_MODEL, D_MODEL), lambda l, j: (l, 0, j)),
            pl.BlockSpec((None, 1, D_MODEL), lambda l, j: (l, 0, j)),
        ],
        out_specs=pl.BlockSpec((None, 16, D_MODEL), lambda l, j: (l, 0, j)),
        out_shape=jax.ShapeDtypeStruct((DEPTH, 16, 3 * D_MODEL), F32),
        name="modulation",
    )(cc, w_mod, b_mod.reshape(DEPTH, 1, 3 * D_MODEL))


def _block_transpose8(xs):
    lane_blk = lax.broadcasted_iota(jnp.int32, xs[0].shape, 1) // S5_GROUP
    for s in (4, 2, 1):
        keep = (lane_blk & s) == 0
        out = list(xs)
        for p in range(8):
            if p & s:
                continue
            a, b = xs[p], xs[p + s]
            out[p] = jnp.where(keep, a, pltpu.roll(b, S5_GROUP * s, 1))
            out[p + s] = jnp.where(keep, pltpu.roll(a, LANES - S5_GROUP * s, 1), b)
        xs = out
    return xs


def _tokens_to_groups(z, stage_ref, o_ref):
    n_chunks = ROW_TILE // S5_CHUNK
    for v in range(B_WIDTH // LANES):
        for c in range(n_chunks):
            stage_ref[v, c * STAGE_PITCH:c * STAGE_PITCH + S5_CHUNK, :] = \
                z[c * S5_CHUNK:(c + 1) * S5_CHUNK, v * LANES:(v + 1) * LANES]
    for w in range(S5_CW // LANES):
        for v in range(B_WIDTH // LANES):
            src = [stage_ref[v, pl.ds(8 * w + jj, n_chunks, stride=STAGE_PITCH), :] for jj in range(8)]
            dst = _block_transpose8(src)
            for gg in range(8):
                o_ref[8 * v + gg, :, w * LANES:(w + 1) * LANES] = dst[gg].astype(o_ref.dtype)


def _groups_to_tokens(y_ref, stage_ref):
    n_chunks = ROW_TILE // S5_CHUNK
    for w in range(S5_CW // LANES):
        for v in range(B_WIDTH // LANES):
            src = [y_ref[8 * v + gg, :, w * LANES:(w + 1) * LANES].astype(F32) for gg in range(8)]
            dst = _block_transpose8(src)
            for jj in range(8):
                stage_ref[v, pl.ds(8 * w + jj, n_chunks, stride=STAGE_PITCH), :] = dst[jj]
    rows = []
    for c in range(n_chunks):
        rows.append(jnp.concatenate(
            [stage_ref[v, c * STAGE_PITCH:c * STAGE_PITCH + S5_CHUNK, :] for v in range(B_WIDTH // LANES)], axis=1))
    return jnp.concatenate(rows, axis=0)


def _modulated_norm(x, mod_ref, nw_ref):
    ms = jnp.mean(x * x, axis=-1, keepdims=True)
    xn = x * lax.rsqrt(ms + EPS) * nw_ref[...]
    shift = mod_ref[:, 0:D_MODEL]
    scale = mod_ref[:, D_MODEL:2 * D_MODEL]
    return (xn * (1.0 + scale) + shift).astype(BF16)


def _inproj_kernel(*refs, names, use_rope):
    h_ref, mod_ref, nw_ref, w_ref = refs[:4]
    stage_ref = refs[-1]
    refs = refs[:-1]
    if use_rope:
        cos_ref, sa_ref, sb_ref = refs[4:7]
        outs = refs[7:]
    else:
        outs = refs[4:]
    hn = _modulated_norm(h_ref[...], mod_ref, nw_ref)
    for name, o_ref in zip(names, outs):
        off, width = SEGS[name]
        if use_rope and name in ROPE_SEGS:
            mul = ROPE_SEGS[name]
            cos = cos_ref[...]
            sa = sa_ref[...]
            sb = sb_ref[...]
            for c in range(width // LANES):
                t = jnp.dot(hn, w_ref[:, off + c * LANES: off + (c + 1) * LANES],
                            preferred_element_type=F32)
                r = t * cos + pltpu.roll(t, LANES - AX_FREQS, 1) * sa + pltpu.roll(t, AX_FREQS, 1) * sb
                if mul != 1.0:
                    r = r * mul
                o_ref[:, c * LANES:(c + 1) * LANES] = r.astype(o_ref.dtype)
        elif name == "bu":
            z = jnp.dot(hn, w_ref[:, off:off + width], preferred_element_type=F32)
            _tokens_to_groups(z, stage_ref, o_ref)
        elif name == "av":
            z = jnp.dot(hn, w_ref[:, off:off + width], preferred_element_type=F32)
            for t in range(ROW_TILE // Q_BLOCK):
                o_ref[t] = z[t * Q_BLOCK:(t + 1) * Q_BLOCK, :].T.astype(o_ref.dtype)
        elif name == "cv":
            z = jnp.dot(hn, w_ref[:, off:off + width], preferred_element_type=F32)
            o_ref[...] = z.T.astype(o_ref.dtype)
        else:
            mul = ROPE_SEGS.get(name, 1.0)
            for c in range(0, width, PROJ_COLS):
                cw = min(PROJ_COLS, width - c)
                z = jnp.dot(hn, w_ref[:, off + c: off + c + cw], preferred_element_type=F32)
                if mul != 1.0:
                    z = z * mul
                o_ref[:, c:c + cw] = z.astype(o_ref.dtype)


def _in_projection(h, mod, mod_row, norm_w, w_bf16, names, rope_tabs):
    bsz, rows, _ = h.shape
    use_rope = rope_tabs is not None
    if mod_row is None:
        mod_map = lambda b, i: (b, 0, 0)
    else:
        mod_map = lambda b, i: (mod_row, 0, 0)
    in_specs = [
        pl.BlockSpec((None, ROW_TILE, D_MODEL), lambda b, i: (b, i, 0)),
        pl.BlockSpec((None, 1, 3 * D_MODEL), mod_map),
        _const_spec((1, D_MODEL)),
        _const_spec((D_MODEL, IN_WIDTH)),
    ]
    args = [h, mod, norm_w.reshape(1, D_MODEL), w_bf16]
    if use_rope:
        for t in rope_tabs:
            in_specs.append(pl.BlockSpec((ROW_TILE, LANES), lambda b, i: (i, 0)))
            args.append(t)
    tiles = rows // ROW_TILE
    chunk_rows = ROW_TILE // S5_CHUNK
    out_specs, out_shape = [], []
    for n in names:
        if n == "bu":
            out_specs.append(pl.BlockSpec((S5_GROUPS, chunk_rows, S5_CW), lambda b, i: (0, b * tiles + i, 0)))
            out_shape.append(jax.ShapeDtypeStruct((S5_GROUPS, bsz * rows // S5_CHUNK, S5_CW), BF16))
        elif n == "av":
            kb = ROW_TILE // Q_BLOCK
            out_specs.append(pl.BlockSpec((None, kb, A_KV_WIDTH, Q_BLOCK), lambda b, i: (b, i, 0, 0)))
            out_shape.append(jax.ShapeDtypeStruct((bsz, rows // Q_BLOCK, A_KV_WIDTH, Q_BLOCK), BF16))
        elif n == "cv":
            out_specs.append(pl.BlockSpec((None, C_WIDTH, ROW_TILE), lambda b, i: (b, 0, i)))
            out_shape.append(jax.ShapeDtypeStruct((bsz, C_WIDTH, rows), BF16))
        else:
            out_specs.append(pl.BlockSpec((None, ROW_TILE, SEGS[n][1]), lambda b, i: (b, i, 0)))
            out_shape.append(jax.ShapeDtypeStruct((bsz, rows, SEGS[n][1]), BF16))
    outs = pl.pallas_call(
        functools.partial(_inproj_kernel, names=tuple(names), use_rope=use_rope),
        grid=(bsz, tiles),
        in_specs=in_specs,
        out_specs=out_specs,
        out_shape=out_shape,
        scratch_shapes=[pltpu.VMEM((B_WIDTH // LANES, (ROW_TILE // S5_CHUNK) * STAGE_PITCH, LANES), F32)],
        compiler_params=pltpu.CompilerParams(
            dimension_semantics=("parallel", "parallel"), vmem_limit_bytes=VMEM_LIMIT),
        name="in_projection_rope" if use_rope else "in_projection",
    )(*args)
    return dict(zip(names, outs))


def _dot_nt(a, b):
    return lax.dot_general(a, b, (((1,), (1,)), ((), ())), preferred_element_type=F32)


def _half_masked(q, lower):
    lane = lax.broadcasted_iota(jnp.int32, q.shape, 1)
    keep = lane < HEAD_DIM if lower else lane >= HEAD_DIM
    return jnp.where(keep, q.astype(F32), 0.0).astype(q.dtype)


def _swa_kernel(*refs, has_local, lat_blocks):
    if has_local:
        sink_ref, q_ref, kc_ref, vtc_ref, kl_ref, vtl_ref, o_ref = refs
    else:
        sink_ref, q_ref, kc_ref, vtc_ref, o_ref = refs
    n_ctx = kc_ref.shape[0]
    cols = A_HEADS * Q_BLOCK
    qblk = jnp.concatenate(
        [_half_masked(q_ref[:, v * LANES:(v + 1) * LANES], kh == 0)
         for kh in range(A_KV_HEADS) for v in range(A_GROUP)], axis=0)
    sink = jnp.concatenate(
        [jnp.full((1, Q_BLOCK), sink_ref[kh * A_GROUP + v], F32)
         for kh in range(A_KV_HEADS) for v in range(A_GROUP)], axis=1)
    keys = [kc_ref[...]]
    vts = [vtc_ref[t] for t in range(n_ctx // Q_BLOCK)]
    if has_local:
        n = pl.program_id(1)
        sb = jnp.clip(n - 1, 0, lat_blocks - 3)
        keys.append(kl_ref[pl.ds(pl.multiple_of(sb * Q_BLOCK, Q_BLOCK), 3 * Q_BLOCK), :])
        vts += [vtl_ref[sb + t] for t in range(3)]
    s_t = _dot_nt(jnp.concatenate(keys, axis=0), qblk)
    if has_local:
        r = lax.broadcasted_iota(jnp.int32, (3 * Q_BLOCK, Q_BLOCK), 0)
        i = lax.broadcasted_iota(jnp.int32, (3 * Q_BLOCK, Q_BLOCK), 1)
        in_band = jnp.abs((n - sb) * Q_BLOCK + i - r) <= WINDOW
        bias = jnp.where(in_band, 0.0, NEG_INF)
        s_t = jnp.concatenate([s_t[:n_ctx], s_t[n_ctx:] + jnp.concatenate([bias] * A_HEADS, axis=1)], axis=0)
    m = jnp.maximum(jnp.max(s_t, axis=0, keepdims=True), sink)
    e_t = jnp.exp(s_t - m)
    den = jnp.sum(e_t, axis=0, keepdims=True) + jnp.exp(sink - m)
    p_t = (e_t * (1.0 / den)).astype(BF16)
    o_t = jnp.dot(jnp.concatenate(vts, axis=1), p_t, preferred_element_type=F32)
    lane = lax.broadcasted_iota(jnp.int32, (Q_BLOCK, LANES), 1)
    half = A_GROUP * Q_BLOCK
    for v in range(A_GROUP):
        lower = o_t[:, v * Q_BLOCK:(v + 1) * Q_BLOCK].T
        upper = o_t[:, half + v * Q_BLOCK:half + (v + 1) * Q_BLOCK].T
        o_ref[:, v * LANES:(v + 1) * LANES] = jnp.where(lane < HEAD_DIM, lower, upper).astype(o_ref.dtype)


def _swa(sink, q, k_ctx, vt_ctx, k_lat=None, vt_lat=None):
    bsz, rows, _ = q.shape
    n_ctx = k_ctx.shape[1]
    has_local = k_lat is not None
    in_specs = [
        pl.BlockSpec(memory_space=pltpu.SMEM),
        pl.BlockSpec((None, Q_BLOCK, A_WIDTH), lambda b, n: (b, n, 0)),
        pl.BlockSpec((None, n_ctx, A_KV_WIDTH), lambda b, n: (b, 0, 0)),
        pl.BlockSpec((None, n_ctx // Q_BLOCK, A_KV_WIDTH, Q_BLOCK), lambda b, n: (b, 0, 0, 0)),
    ]
    args = [sink, q, k_ctx, vt_ctx]
    lat_blocks = 0
    if has_local:
        n_lat = k_lat.shape[1]
        lat_blocks = n_lat // Q_BLOCK
        in_specs += [pl.BlockSpec((None, n_lat, A_KV_WIDTH), lambda b, n: (b, 0, 0)),
                     pl.BlockSpec((None, lat_blocks, A_KV_WIDTH, Q_BLOCK), lambda b, n: (b, 0, 0, 0))]
        args += [k_lat, vt_lat]
    return pl.pallas_call(
        functools.partial(_swa_kernel, has_local=has_local, lat_blocks=lat_blocks),
        grid=(bsz, rows // Q_BLOCK),
        in_specs=in_specs,
        out_specs=pl.BlockSpec((None, Q_BLOCK, A_WIDTH), lambda b, n: (b, n, 0)),
        out_shape=jax.ShapeDtypeStruct((bsz, rows, A_WIDTH), F32),
        compiler_params=pltpu.CompilerParams(
            dimension_semantics=("parallel", "arbitrary"), vmem_limit_bytes=VMEM_LIMIT),
        name="swa_latent" if has_local else "swa_context",
    )(*args)


def _diff_kernel(*refs, has_lat, lam_init):
    if has_lat:
        lam_ref, sub_ref, q_ref, kc_ref, vtc_ref, kl_ref, vtl_ref, o_ref = refs
    else:
        lam_ref, sub_ref, q_ref, kc_ref, vtc_ref, o_ref = refs
    lp = lam_ref[...]
    lam = (jnp.exp(jnp.sum(lp[0:1] * lp[1:2], axis=1, keepdims=True))
           - jnp.exp(jnp.sum(lp[2:3] * lp[3:4], axis=1, keepdims=True)) + lam_init)
    sub_w = sub_ref[...] * (1.0 - lam_init)
    width = 2 * HEAD_DIM
    streams = [(kc_ref, vtc_ref)] + ([(kl_ref, vtl_ref)] if has_lat else [])
    heads = [slice(h * width, (h + 1) * width) for h in range(C_HEADS)]

    def scores(h):
        hs = heads[h]
        qblk = jnp.concatenate([_half_masked(q_ref[:, hs], True), _half_masked(q_ref[:, hs], False)], axis=0)
        return [_dot_nt(k_ref[:, hs], qblk) for k_ref, _ in streams]

    s_next = scores(0)
    for h, hs in enumerate(heads):
        s_cur = s_next
        if h + 1 < C_HEADS:
            s_next = scores(h + 1)
        m = functools.reduce(jnp.maximum, [jnp.max(s, axis=0, keepdims=True) for s in s_cur])
        e_cur = [jnp.exp(s - m) for s in s_cur]
        den = functools.reduce(jnp.add, [jnp.sum(e, axis=0, keepdims=True) for e in e_cur])
        inv = 1.0 / den
        c1 = inv[:, :Q_BLOCK]
        c2 = lam * inv[:, Q_BLOCK:]
        o_t = None
        for e, (_, vt_ref) in zip(e_cur, streams):
            a_t = (e[:, :Q_BLOCK] * c1 - e[:, Q_BLOCK:] * c2).astype(BF16)
            pv = jnp.dot(vt_ref[hs, :], a_t, preferred_element_type=F32)
            o_t = pv if o_t is None else o_t + pv
        o = o_t.T
        ms = jnp.mean(o * o, axis=1, keepdims=True)
        o_ref[:, hs] = (o * lax.rsqrt(ms + EPS) * sub_w).astype(o_ref.dtype)


def _diff_attention(lam_params, subln, lam_init, q, k_ctx, vt_ctx, k_lat=None, vt_lat=None):
    bsz, rows, _ = q.shape
    n_ctx = k_ctx.shape[1]
    has_lat = k_lat is not None
    in_specs = [
        _const_spec((4, HEAD_DIM)),
        _const_spec((1, 2 * HEAD_DIM)),
        pl.BlockSpec((None, Q_BLOCK, C_WIDTH), lambda b, n: (b, n, 0)),
        pl.BlockSpec((None, n_ctx, C_WIDTH), lambda b, n: (b, 0, 0)),
        pl.BlockSpec((None, C_WIDTH, n_ctx), lambda b, n: (b, 0, 0)),
    ]
    args = [lam_params, subln.reshape(1, 2 * HEAD_DIM), q, k_ctx, vt_ctx]
    if has_lat:
        n_lat = k_lat.shape[1]
        in_specs += [pl.BlockSpec((None, n_lat, C_WIDTH), lambda b, n: (b, 0, 0)),
                     pl.BlockSpec((None, C_WIDTH, n_lat), lambda b, n: (b, 0, 0))]
        args += [k_lat, vt_lat]
    return pl.pallas_call(
        functools.partial(_diff_kernel, has_lat=has_lat, lam_init=lam_init),
        grid=(bsz, rows // Q_BLOCK),
        in_specs=in_specs,
        out_specs=pl.BlockSpec((None, Q_BLOCK, C_WIDTH), lambda b, n: (b, n, 0)),
        out_shape=jax.ShapeDtypeStruct((bsz, rows, C_WIDTH), F32),
        compiler_params=pltpu.CompilerParams(
            dimension_semantics=("parallel", "arbitrary"), vmem_limit_bytes=VMEM_LIMIT),
        name="diff_latent" if has_lat else "diff_context",
    )(*args)


def _s5_operators(a_re, a_im, log_dt, b_re, b_im, c_re, c_im, d_skip):
    f = lambda t: t.astype(F32)
    quad = lambda t: jnp.concatenate([t, t, t, t], axis=-1)
    a_re, a_im, b_re, b_im, c_re, c_im = map(f, (a_re, a_im, b_re, b_im, c_re, c_im))
    g, p, h = S5_GROUPS, S5_STATE, S5_GROUP
    dt = jnp.exp(f(log_dt))[..., None]
    mag = jnp.exp(a_re * dt)
    abar_re = mag * jnp.cos(a_im * dt)
    abar_im = mag * jnp.sin(a_im * dt)
    den = a_re * a_re + a_im * a_im
    nr = abar_re - 1.0
    ni = abar_im
    f_re = (nr * a_re + ni * a_im) / den
    f_im = (ni * a_re - nr * a_im) / den
    pw_re, pw_im = [jnp.ones_like(abar_re)], [jnp.zeros_like(abar_im)]
    for _ in range(S5_CHUNK):
        pre, pim = pw_re[-1], pw_im[-1]
        pw_re.append(pre * abar_re - pim * abar_im)
        pw_im.append(pre * abar_im + pim * abar_re)
    pad = [jnp.zeros_like(abar_re)] * (3 * SUBLANES - S5_CHUNK - 1)
    pw = jnp.stack([jnp.stack(pw_re + pad, axis=2), jnp.stack(pw_im + pad, axis=2)], axis=1)
    pw = jnp.transpose(quad(pw), (2, 0, 1, 3, 4))
    dsk = jnp.tile(f(d_skip).reshape(g, h), (1, S5_CW // h))
    sgn = jnp.broadcast_to(jnp.repeat(jnp.array([-1.0, 1.0, 1.0, -1.0], F32), p), (g, 4 * p))
    zero = jnp.zeros_like(dsk)
    rows = jnp.stack([quad(f_re[0]), quad(f_im[0]), quad(f_re[1]), quad(f_im[1]), dsk, sgn, zero, zero],
                     axis=1)
    bt_re = jnp.swapaxes(b_re, -1, -2)
    bt_im = jnp.swapaxes(b_im, -1, -2)
    bpk = jnp.stack([jnp.concatenate([bt_re, bt_im, bt_im, bt_re], axis=-1),
                     jnp.concatenate([bt_im, bt_re, bt_re, bt_im], axis=-1)], axis=1)
    cpk = jnp.stack([jnp.concatenate([c_re, -c_im], axis=-1),
                     jnp.concatenate([-c_im, -c_re], axis=-1)], axis=1)
    bpk = jnp.transpose(bpk, (2, 0, 1, 3, 4))
    cpk = jnp.transpose(cpk, (2, 0, 1, 3, 4))
    grp = lambda shape: pl.BlockSpec((None,) + shape, lambda i: (i,) + (0,) * len(shape))
    return pl.pallas_call(
        _s5_operator_kernel,
        grid=(g,),
        in_specs=[grp((8, 4 * p)), grp((2, 2, 3 * SUBLANES, 4 * p)), grp((2, 2, h, 4 * p)),
                  grp((2, 2, h, 2 * p))],
        out_specs=[grp((S5_CW, 3 * S5_CW)), grp((4 * p, S5_CW)), grp((SUBLANES, 2 * p))],
        out_shape=[jax.ShapeDtypeStruct((g, S5_CW, 3 * S5_CW), BF16),
                   jax.ShapeDtypeStruct((g, 4 * p, S5_CW), BF16),
                   jax.ShapeDtypeStruct((g, SUBLANES, 2 * p), F32)],
        compiler_params=pltpu.CompilerParams(dimension_semantics=("parallel",)),
        name="s5_operators",
    )(rows, pw, bpk, cpk)


def _shift_lanes(x, s):
    lo, hi = x[:, :LANES], x[:, LANES:]
    lane = lax.broadcasted_iota(jnp.int32, lo.shape, 1)
    zero = jnp.zeros_like(lo)
    rot = lambda t, r: pltpu.roll(t, r, 1) if r % LANES else t
    if s >= 0:
        if s < LANES:
            rl, rh = rot(lo, s), rot(hi, s)
            out = (jnp.where(lane >= s, rl, 0.0), jnp.where(lane >= s, rh, rl))
        else:
            rl = rot(lo, s - LANES)
            out = (zero, jnp.where(lane >= s - LANES, rl, 0.0))
    else:
        s = -s
        if s < LANES:
            rl, rh = rot(lo, LANES - s), rot(hi, LANES - s)
            out = (jnp.where(lane < LANES - s, rl, rh), jnp.where(lane < LANES - s, rh, 0.0))
        else:
            rh = rot(hi, 2 * LANES - s)
            out = (jnp.where(lane < 2 * LANES - s, rh, 0.0), zero)
    return jnp.concatenate(out, axis=1)


def _s5_operator_kernel(rows_ref, pw_ref, b_ref, c_ref, w1_ref, w2_ref, coef_ref):
    t_len = S5_CHUNK
    half = 2 * S5_STATE
    sgn = rows_ref[5:6, :]
    mxu_operand = lambda t: t.astype(BF16).astype(F32)
    a4, bx4, pr4, pi4, g2 = [], [], [], [], []
    for d in range(2):
        f_re = rows_ref[2 * d:2 * d + 1, :]
        f_im = rows_ref[2 * d + 1:2 * d + 2, :]
        p1 = b_ref[d, 0]
        p2 = b_ref[d, 1]
        a4.append(mxu_operand(f_re * p1 + sgn * f_im * p2))
        bx4.append(mxu_operand(f_re * p2 - sgn * f_im * p1))
        pr4.append(pw_ref[d, 0])
        pi4.append(pw_ref[d, 1])
        cx = jnp.concatenate([mxu_operand(c_ref[d, 0])] * t_len, axis=0)
        cy = jnp.concatenate([mxu_operand(c_ref[d, 1])] * t_len, axis=0)

        def ca(order, d=d, cx=cx, cy=cy):
            pr = jnp.concatenate(
                [jnp.broadcast_to(pr4[d][t:t + 1, :half], (S5_GROUP, half)) for t in order], axis=0)
            pi = jnp.concatenate(
                [jnp.broadcast_to(pi4[d][t:t + 1, :half], (S5_GROUP, half)) for t in order], axis=0)
            return cx * pr + cy * pi
        g2.append(ca)

    def strip(d, order):
        return lax.dot_general(a4[d][:, :half], g2[d](order), (((1,), (1,)), ((), ())),
                               precision=lax.Precision.HIGHEST, preferred_element_type=F32)

    row = lax.broadcasted_iota(jnp.int32, (S5_GROUP, S5_CW), 0)
    col = lax.broadcasted_iota(jnp.int32, (S5_GROUP, S5_CW), 1)
    k_fwd = strip(0, range(t_len)) + jnp.where(row == col, rows_ref[4:5, :], 0.0)
    k_bwd = strip(1, [t_len - 1 - k for k in range(t_len)])
    for j in range(t_len):
        lo, hi = j * S5_GROUP, (j + 1) * S5_GROUP
        m = _shift_lanes(k_fwd, j * S5_GROUP) + _shift_lanes(k_bwd, -(t_len - 1 - j) * S5_GROUP)
        w1_ref[lo:hi, 0:S5_CW] = m.astype(w1_ref.dtype)
        tf, tb = t_len - 1 - j, j
        s_f = a4[0] * pr4[0][tf:tf + 1, :] + bx4[0] * (sgn * pi4[0][tf:tf + 1, :])
        s_b = a4[1] * pr4[1][tb:tb + 1, :] + bx4[1] * (sgn * pi4[1][tb:tb + 1, :])
        w1_ref[lo:hi, S5_CW:2 * S5_CW] = s_f.astype(w1_ref.dtype)
        w1_ref[lo:hi, 2 * S5_CW:3 * S5_CW] = s_b.astype(w1_ref.dtype)
    w2_ref[0:half, :] = g2[0]([i + 1 for i in range(t_len)]).T.astype(w2_ref.dtype)
    w2_ref[half:2 * half, :] = g2[1]([t_len - i for i in range(t_len)]).T.astype(w2_ref.dtype)
    zero = jnp.zeros((1, half), F32)
    coef_ref[...] = jnp.concatenate(
        [pr4[0][t_len:t_len + 1, :half], (sgn * pi4[0][t_len:t_len + 1, :])[:, :half],
         pr4[1][t_len:t_len + 1, :half], (sgn * pi4[1][t_len:t_len + 1, :])[:, :half],
         zero, zero, zero, zero], axis=0)


def _s5_kernel(xc_ref, xl_ref, w1_ref, w2_ref, coef_ref, yc_ref, yl_ref,
               zic_ref, zil_ref, zsc_ref, zsl_ref, hc_ref, hl_ref, *, bsz, ctx_chunks, lat_chunks):
    sw = 2 * S5_STATE
    streams = ((xc_ref, zic_ref, zsc_ref, hc_ref, yc_ref, ctx_chunks),
               (xl_ref, zil_ref, zsl_ref, hl_ref, yl_ref, lat_chunks))
    w1 = w1_ref[...]
    for x_ref, zi_ref, zs_ref, _, _, n in streams:
        pitch = n + SUBLANES
        for b in range(bsz):
            z = jnp.dot(x_ref[b * n:(b + 1) * n, :], w1, preferred_element_type=F32)
            zi_ref[b * n:(b + 1) * n, :] = z[:, 0:S5_CW]
            for k in range(4):
                zs_ref[k, b * pitch:b * pitch + n, :] = z[:, S5_CW + k * sw:S5_CW + (k + 1) * sw]

    shape = (bsz, sw)
    a1f = jnp.broadcast_to(coef_ref[0:1, :], shape)
    a2f = jnp.broadcast_to(coef_ref[1:2, :], shape)
    a1b = jnp.broadcast_to(coef_ref[2:3, :], shape)
    a2b = jnp.broadcast_to(coef_ref[3:4, :], shape)

    def locate(c):
        if c < ctx_chunks:
            return zsc_ref, hc_ref, c, ctx_chunks + SUBLANES
        return zsl_ref, hl_ref, c - ctx_chunks, lat_chunks + SUBLANES

    n_chunks = ctx_chunks + lat_chunks
    hf = hfs = hb = hbs = jnp.zeros(shape, F32)
    for t in range(n_chunks):
        zs_ref, h_ref, c, pitch = locate(t)
        rows = pl.ds(c, bsz, stride=pitch)
        h_ref[0, rows, :] = hf
        lf = zs_ref[0, rows, :]
        lfs = zs_ref[1, rows, :]
        hf, hfs = a1f * hf + a2f * hfs + lf, a1f * hfs - a2f * hf + lfs
        cb = ctx_chunks - 1 - t if t < ctx_chunks else n_chunks + ctx_chunks - 1 - t
        zs_ref, h_ref, c, pitch = locate(cb)
        rows = pl.ds(c, bsz, stride=pitch)
        h_ref[1, rows, :] = hb
        lb = zs_ref[2, rows, :]
        lbs = zs_ref[3, rows, :]
        hb, hbs = a1b * hb + a2b * hbs + lb, a1b * hbs - a2b * hb + lbs

    w2 = w2_ref[...]
    for _, zi_ref, _, h_ref, y_ref, n in streams:
        pitch = n + SUBLANES
        for b in range(bsz):
            hin = jnp.concatenate([h_ref[0, b * pitch:b * pitch + n, :], h_ref[1, b * pitch:b * pitch + n, :]],
                                  axis=1)
            y = zi_ref[b * n:(b + 1) * n, :] + jnp.dot(hin.astype(BF16), w2, preferred_element_type=F32)
            y_ref[b * n:(b + 1) * n, :] = y.astype(y_ref.dtype)


def _s5_scan(x_ctx, x_lat, w1, w2, coef, bsz):
    ctx_chunks = x_ctx.shape[1] // bsz
    lat_chunks = x_lat.shape[1] // bsz
    sw = 2 * S5_STATE
    grp = lambda rows, cols: pl.BlockSpec((None, rows, cols), lambda g: (g, 0, 0))
    slab = lambda k, n: pltpu.VMEM((k, bsz * (n + SUBLANES), sw), F32)
    return pl.pallas_call(
        functools.partial(_s5_kernel, bsz=bsz, ctx_chunks=ctx_chunks, lat_chunks=lat_chunks),
        grid=(S5_GROUPS,),
        in_specs=[
            grp(bsz * ctx_chunks, S5_CW), grp(bsz * lat_chunks, S5_CW),
            grp(S5_CW, 3 * S5_CW), grp(4 * S5_STATE, S5_CW), grp(SUBLANES, sw),
        ],
        out_specs=[grp(bsz * ctx_chunks, S5_CW), grp(bsz * lat_chunks, S5_CW)],
        out_shape=[jax.ShapeDtypeStruct(x_ctx.shape, F32), jax.ShapeDtypeStruct(x_lat.shape, F32)],
        scratch_shapes=[
            pltpu.VMEM((bsz * ctx_chunks, S5_CW), F32), pltpu.VMEM((bsz * lat_chunks, S5_CW), F32),
            slab(4, ctx_chunks), slab(4, lat_chunks), slab(2, ctx_chunks), slab(2, lat_chunks),
        ],
        compiler_params=pltpu.CompilerParams(
            dimension_semantics=("parallel",), vmem_limit_bytes=VMEM_LIMIT),
        name="s5_scan",
    )(x_ctx, x_lat, w1, w2, coef)


def _gelu_tanh(x):
    return 0.5 * x * (1.0 + jnp.tanh(math.sqrt(2.0 / math.pi) * (x + 0.044715 * (x * x * x))))


def _silu(x):
    return x * jax.nn.sigmoid(x)


def _merge_kernel(h_ref, mod_ref, npre_ref, wg_ref, oa_ref, yb_ref, oc_ref,
                  wglu_ref, bglu_ref, woa_ref, wob_ref, woc_ref, wout_ref, npost_ref, o_ref, stage_ref):
    hn = _modulated_norm(h_ref[...], mod_ref, npre_ref)
    yb = _gelu_tanh(_groups_to_tokens(yb_ref, stage_ref))
    glu = jnp.dot(yb.astype(BF16), wglu_ref[...], preferred_element_type=F32) + bglu_ref[...]
    ob = yb * jax.nn.sigmoid(glu)

    def branch(i, o, w_ref):
        g = jnp.dot(hn, wg_ref[:, i * A_WIDTH:(i + 1) * A_WIDTH], preferred_element_type=F32)
        t = jnp.dot((o * _silu(g)).astype(BF16), w_ref[...], preferred_element_type=F32)
        m_lo = N_BRANCH * A_WIDTH + i * D_MODEL
        gate = jax.nn.sigmoid(jnp.dot(hn, wg_ref[:, m_lo:m_lo + D_MODEL], preferred_element_type=F32))
        return gate * t

    y = branch(0, oa_ref[...], woa_ref) + branch(1, ob, wob_ref) + branch(2, oc_ref[...], woc_ref)
    y = jnp.dot(y.astype(BF16), wout_ref[...], preferred_element_type=F32)
    ms = jnp.mean(y * y, axis=-1, keepdims=True)
    yn = y * lax.rsqrt(ms + EPS) * npost_ref[...]
    gate = mod_ref[:, 2 * D_MODEL:3 * D_MODEL]
    o_ref[...] = h_ref[...] + gate * yn


def _merge(h, mod, mod_row, norm_pre, w_gate, o_a, y_b, o_c, wts):
    bsz, rows, _ = h.shape
    tiles = rows // ROW_TILE
    if mod_row is None:
        mod_map = lambda b, i: (b, 0, 0)
    else:
        mod_map = lambda b, i: (mod_row, 0, 0)
    tok = lambda w: pl.BlockSpec((None, ROW_TILE, w), lambda b, i: (b, i, 0))
    in_specs = [
        tok(D_MODEL),
        pl.BlockSpec((None, 1, 3 * D_MODEL), mod_map),
        _const_spec((1, D_MODEL)), _const_spec((D_MODEL, GATE_WIDTH)),
        tok(A_WIDTH),
        pl.BlockSpec((S5_GROUPS, ROW_TILE // S5_CHUNK, S5_CW), lambda b, i: (0, b * tiles + i, 0)),
        tok(C_WIDTH),
        _const_spec((B_WIDTH, B_WIDTH)), _const_spec((1, B_WIDTH)),
        _const_spec((A_WIDTH, D_MODEL)), _const_spec((B_WIDTH, D_MODEL)), _const_spec((C_WIDTH, D_MODEL)),
        _const_spec((D_MODEL, D_MODEL)), _const_spec((1, D_MODEL)),
    ]
    return pl.pallas_call(
        _merge_kernel,
        grid=(bsz, rows // ROW_TILE),
        in_specs=in_specs,
        out_specs=tok(D_MODEL),
        out_shape=jax.ShapeDtypeStruct((bsz, rows, D_MODEL), F32),
        scratch_shapes=[pltpu.VMEM((B_WIDTH // LANES, (ROW_TILE // S5_CHUNK) * STAGE_PITCH, LANES), F32)],
        compiler_params=pltpu.CompilerParams(
            dimension_semantics=("parallel", "parallel"), vmem_limit_bytes=VMEM_LIMIT),
        name="merge_out",
    )(h, mod, norm_pre.reshape(1, D_MODEL), w_gate, o_a, y_b, o_c, *wts)


def _rope_tables(n_tokens):
    rows = n_tokens // GRID_W
    pos_r = jnp.repeat(jnp.arange(rows, dtype=F32), GRID_W)
    pos_c = jnp.tile(jnp.arange(GRID_W, dtype=F32), rows)
    inv = ROPE_BASE ** (-jnp.arange(AX_FREQS, dtype=F32) / AX_FREQS)
    ang_r = pos_r[:, None] * inv[None]
    ang_c = pos_c[:, None] * inv[None]
    ang = jnp.concatenate([ang_r, ang_r, ang_c, ang_c], axis=-1)
    cos, sin = jnp.cos(ang), jnp.sin(ang)
    first = (jnp.arange(HEAD_DIM) % (2 * AX_FREQS)) < AX_FREQS
    sa = jnp.where(first[None], -sin, 0.0)
    sb = jnp.where(first[None], 0.0, sin)
    tile2 = lambda t: jnp.concatenate([t, t], axis=-1)
    return tile2(cos), tile2(sa), tile2(sb)


def _reorder_a_heads(w, axis):
    shape = w.shape
    split = shape[:axis] + (A_HEADS, HEAD_DIM) + shape[axis + 1:]
    return jnp.take(w.reshape(split), jnp.array(A_HEAD_ORDER), axis=axis).reshape(shape)


def kernel(x, c, ctx, c_ctx, w_mod, b_mod, norm_pre, norm_post, w_in, swa_sink, s5_a_re, s5_a_im, s5_log_dt, s5_b_re, s5_b_im, s5_c_re, s5_c_im, s5_d, s5_w_glu, s5_b_glu, diff_lq1, diff_lk1, diff_lq2, diff_lk2, diff_subln, w_o_a, w_o_b, w_o_c, w_out):
    bsz, n_lat, _ = x.shape
    n_ctx = ctx.shape[1]
    ctx_row = bsz
    cc = jnp.zeros((16, D_MODEL), F32).at[:bsz].set(c).at[ctx_row].set(c_ctx)
    mod_all = _modulation(cc, w_mod, b_mod)
    rope_tabs = _rope_tables(n_lat)

    h_lat, h_ctx = x, ctx
    for l in range(DEPTH):
        last = l == DEPTH - 1
        lam_init = 0.8 - 0.6 * math.exp(-0.3 * l)
        mod = mod_all[l].reshape(16, 1, 3 * D_MODEL)
        w_l = w_in[l]
        cols = lambda name: w_l[:, SEGS[name][0]:SEGS[name][0] + SEGS[name][1]]
        aq0, aqw = SEGS["aq"]
        w_in_l = jnp.concatenate([_reorder_a_heads(cols("aq"), 1), w_l[:, aq0 + aqw:]], axis=1).astype(BF16)
        w_gate = jnp.concatenate(
            [_reorder_a_heads(cols("ag"), 1), cols("bg"), cols("cg"), cols("mg")], axis=1).astype(BF16)
        zl = _in_projection(h_lat, mod, None, norm_pre[l], w_in_l, PROJ_NAMES, rope_tabs)
        ctx_names = ("ak", "av", "bu", "ck", "cv") if last else PROJ_NAMES
        zc = _in_projection(h_ctx, mod, ctx_row, norm_pre[l], w_in_l, ctx_names, None)

        o_a_l = _swa(swa_sink[l], zl["aq"], zc["ak"], zc["av"], zl["ak"], zl["av"])

        w1, w2, coef = _s5_operators(s5_a_re[l], s5_a_im[l], s5_log_dt[l], s5_b_re[l], s5_b_im[l],
                                     s5_c_re[l], s5_c_im[l], s5_d[l])
        y_b_c, y_b_l = _s5_scan(zc["bu"], zl["bu"], w1, w2, coef, bsz)

        lam_params = jnp.stack([diff_lq1[l], diff_lk1[l], diff_lq2[l], diff_lk2[l]]).astype(F32)
        o_c_l = _diff_attention(lam_params, diff_subln[l], lam_init, zl["cq"], zc["ck"], zc["cv"],
                                zl["ck"], zl["cv"])

        wts = (s5_w_glu[l].astype(BF16), s5_b_glu[l].reshape(1, B_WIDTH),
               _reorder_a_heads(w_o_a[l], 0).astype(BF16), w_o_b[l].astype(BF16), w_o_c[l].astype(BF16),
               w_out[l].astype(BF16), norm_post[l].reshape(1, D_MODEL))
        h_lat_new = _merge(h_lat, mod, None, norm_pre[l], w_gate, o_a_l, y_b_l, o_c_l, wts)
        if not last:
            o_a_c = _swa(swa_sink[l], zc["aq"], zc["ak"], zc["av"])
            o_c_c = _diff_attention(lam_params, diff_subln[l], lam_init, zc["cq"], zc["ck"], zc["cv"])
            h_ctx = _merge(h_ctx, mod, ctx_row, norm_pre[l], w_gate, o_a_c, y_b_c, o_c_c, wts)
        h_lat = h_lat_new
    return h_lat
```

```python
import functools
import math

import jax
import jax.numpy as jnp
from jax import lax
from jax.experimental import pallas as pl
from jax.experimental.pallas import tpu as pltpu

F32 = jnp.float32
BF16 = jnp.bfloat16

D_MODEL = 1024
DEPTH = 2
GRID_W = 64
HEAD_DIM = 64
WINDOW = 128
ROPE_BASE = 10000.0
AX_FREQS = HEAD_DIM // 4
NEG_INF = -1e30
A_HEADS = 8
A_KV_HEADS = 2
A_GROUP = A_HEADS // A_KV_HEADS
A_WIDTH = A_HEADS * HEAD_DIM
A_KV_WIDTH = A_KV_HEADS * HEAD_DIM
B_WIDTH = 512
S5_GROUP = 16
S5_GROUPS = B_WIDTH // S5_GROUP
S5_STATE = 64
C_HEADS = 4
C_WIDTH = C_HEADS * 2 * HEAD_DIM
N_BRANCH = 3
GATE_WIDTH = A_WIDTH + B_WIDTH + C_WIDTH + N_BRANCH * D_MODEL
EPS = 1e-6

LANES = 128
SUBLANES = 8

ROW_TILE = 512
Q_BLOCK = 128
DIFF_Q_ROWS = 256
A_HEAD_ORDER = tuple(h for v in range(4) for h in (v, 4 + v))
S5_CHUNK = 16
S5_CW = S5_CHUNK * S5_GROUP
STAGE_PITCH = S5_CHUNK + SUBLANES
VMEM_LIMIT = 56 * 1024 * 1024

_SEG_NAMES = ("aq", "ak", "av", "ag", "bu", "bg", "cq", "ck", "cv", "cg", "mg")
_SEG_WIDTHS = (A_WIDTH, A_KV_WIDTH, A_KV_WIDTH, A_WIDTH, B_WIDTH, B_WIDTH,
               C_WIDTH, C_WIDTH, C_WIDTH, C_WIDTH, 3 * D_MODEL)
SEGS = {}
_off = 0
for _n, _w in zip(_SEG_NAMES, _SEG_WIDTHS):
    SEGS[_n] = (_off, _w)
    _off += _w
IN_WIDTH = _off
PROJ_NAMES = ("aq", "ak", "av", "bu", "cq", "ck", "cv")
ROPE_SEGS = {"aq": HEAD_DIM ** -0.5, "ak": 1.0, "cq": HEAD_DIM ** -0.5, "ck": 1.0}


def _const_spec(shape):
    nd = len(shape)
    return pl.BlockSpec(shape, lambda *_: (0,) * nd, pipeline_mode=pl.Buffered(1))


def _mod_kernel(c_ref, w_ref, b_ref, o_ref):
    c = c_ref[...]
    a = c * jax.nn.sigmoid(c)
    o_ref[...] = jnp.dot(a, w_ref[...], preferred_element_type=F32) + b_ref[...]


def _modulation(cc, w_mod, b_mod):
    n_tiles = 3
    return pl.pallas_call(
        _mod_kernel,
        grid=(DEPTH, n_tiles),
        in_specs=[
            pl.BlockSpec((16, D_MODEL), lambda l, j: (0, 0)),
            pl.BlockSpec((None, D_MODEL, D_MODEL), lambda l, j: (l, 0, j)),
            pl.BlockSpec((None, 1, D_MODEL), lambda l, j: (l, 0, j)),
        ],
        out_specs=pl.BlockSpec((None, 16, D_MODEL), lambda l, j: (l, 0, j)),
        out_shape=jax.ShapeDtypeStruct((DEPTH, 16, 3 * D_MODEL), F32),
        name="modulation",
    )(cc, w_mod, b_mod.reshape(DEPTH, 1, 3 * D_MODEL))


def _block_transpose8(xs):
    lane_blk = lax.broadcasted_iota(jnp.int32, xs[0].shape, 1) // S5_GROUP
    for s in (4, 2, 1):
        keep = (lane_blk & s) == 0
        out = list(xs)
        for p in range(8):
            if p & s:
                continue
            a, b = xs[p], xs[p + s]
            out[p] = jnp.where(keep, a, pltpu.roll(b, S5_GROUP * s, 1))
            out[p + s] = jnp.where(keep, pltpu.roll(a, LANES - S5_GROUP * s, 1), b)
        xs = out
    return xs


def _tokens_to_groups(z, stage_ref, o_ref):
    n_chunks = z.shape[0] // S5_CHUNK
    for v in range(B_WIDTH // LANES):
        for c in range(n_chunks):
            stage_ref[v, c * STAGE_PITCH:c * STAGE_PITCH + S5_CHUNK, :] = \
                z[c * S5_CHUNK:(c + 1) * S5_CHUNK, v * LANES:(v + 1) * LANES]
    for w in range(S5_CW // LANES):
        for v in range(B_WIDTH // LANES):
            src = [stage_ref[v, pl.ds(8 * w + jj, n_chunks, stride=STAGE_PITCH), :] for jj in range(8)]
            dst = _block_transpose8(src)
            for gg in range(8):
                o_ref[8 * v + gg, :, w * LANES:(w + 1) * LANES] = dst[gg].astype(o_ref.dtype)


def _groups_to_tokens(y_ref, stage_ref):
    n_chunks = y_ref.shape[1]
    for w in range(S5_CW // LANES):
        for v in range(B_WIDTH // LANES):
            src = [y_ref[8 * v + gg, :, w * LANES:(w + 1) * LANES].astype(F32) for gg in range(8)]
            dst = _block_transpose8(src)
            for jj in range(8):
                stage_ref[v, pl.ds(8 * w + jj, n_chunks, stride=STAGE_PITCH), :] = dst[jj]
    rows = []
    for c in range(n_chunks):
        rows.append(jnp.concatenate(
            [stage_ref[v, c * STAGE_PITCH:c * STAGE_PITCH + S5_CHUNK, :] for v in range(B_WIDTH // LANES)], axis=1))
    return jnp.concatenate(rows, axis=0)


def _modulated_norm(x, mod_ref, nw_ref):
    ms = jnp.mean(x * x, axis=-1, keepdims=True)
    xn = x * lax.rsqrt(ms + EPS) * nw_ref[...]
    shift = mod_ref[:, 0:D_MODEL]
    scale = mod_ref[:, D_MODEL:2 * D_MODEL]
    return (xn * (1.0 + scale) + shift).astype(BF16)


def _inproj_kernel(*refs, names, use_rope):
    h_ref, mod_ref, nw_ref, w_ref = refs[:4]
    stage_ref = refs[-1]
    refs = refs[:-1]
    if use_rope:
        cos_ref, sa_ref, sb_ref = refs[4:7]
        outs = refs[7:]
    else:
        outs = refs[4:]
    hn = _modulated_norm(h_ref[...], mod_ref, nw_ref)
    kv_off = SEGS["ak"][0]
    assert SEGS["av"][0] == kv_off + A_KV_WIDTH
    kv = jnp.dot(hn, w_ref[:, kv_off:kv_off + 2 * A_KV_WIDTH], preferred_element_type=F32)

    def project(name):
        off, width = SEGS[name]
        if name in ("ak", "av"):
            return kv[:, off - kv_off:off - kv_off + width]
        return jnp.dot(hn, w_ref[:, off:off + width], preferred_element_type=F32)

    for name, o_ref in zip(names, outs):
        z = project(name)
        mul = ROPE_SEGS.get(name, 1.0)
        if use_rope and name in ROPE_SEGS:
            cos = cos_ref[...]
            sa = sa_ref[...]
            sb = sb_ref[...]
            for c in range(z.shape[1] // LANES):
                t = z[:, c * LANES:(c + 1) * LANES]
                r = t * cos + pltpu.roll(t, LANES - AX_FREQS, 1) * sa + pltpu.roll(t, AX_FREQS, 1) * sb
                if mul != 1.0:
                    r = r * mul
                o_ref[:, c * LANES:(c + 1) * LANES] = r.astype(o_ref.dtype)
        elif name == "bu":
            _tokens_to_groups(z, stage_ref, o_ref)
        elif name == "av":
            for t in range(z.shape[0] // Q_BLOCK):
                o_ref[t] = z[t * Q_BLOCK:(t + 1) * Q_BLOCK, :].T.astype(o_ref.dtype)
        elif name == "cv":
            o_ref[...] = z.T.astype(o_ref.dtype)
        else:
            if mul != 1.0:
                z = z * mul
            o_ref[...] = z.astype(o_ref.dtype)


def _in_projection(h, mod, mod_row, norm_w, w_bf16, names, rope_tabs):
    bsz, rows, _ = h.shape
    tile = min(ROW_TILE, rows)
    use_rope = rope_tabs is not None
    if mod_row is None:
        mod_map = lambda b, i: (b, 0, 0)
    else:
        mod_map = lambda b, i: (mod_row, 0, 0)
    in_specs = [
        pl.BlockSpec((None, tile,D_MODEL), lambda b, i: (b, i, 0)),
        pl.BlockSpec((None, 1, 3 * D_MODEL), mod_map),
        _const_spec((1, D_MODEL)),
        _const_spec((D_MODEL, IN_WIDTH)),
    ]
    args = [h, mod, norm_w.reshape(1, D_MODEL), w_bf16]
    if use_rope:
        for t in rope_tabs:
            in_specs.append(pl.BlockSpec((tile, LANES), lambda b, i: (i, 0)))
            args.append(t)
    tiles = rows // tile
    chunk_rows = tile // S5_CHUNK
    out_specs, out_shape = [], []
    for n in names:
        if n == "bu":
            out_specs.append(pl.BlockSpec((S5_GROUPS, chunk_rows, S5_CW), lambda b, i: (0, b * tiles + i, 0)))
            out_shape.append(jax.ShapeDtypeStruct((S5_GROUPS, bsz * rows // S5_CHUNK, S5_CW), BF16))
        elif n == "av":
            kb = tile // Q_BLOCK
            out_specs.append(pl.BlockSpec((None, kb, A_KV_WIDTH, Q_BLOCK), lambda b, i: (b, i, 0, 0)))
            out_shape.append(jax.ShapeDtypeStruct((bsz, rows // Q_BLOCK, A_KV_WIDTH, Q_BLOCK), BF16))
        elif n == "cv":
            out_specs.append(pl.BlockSpec((None, C_WIDTH, tile), lambda b, i: (b, 0, i)))
            out_shape.append(jax.ShapeDtypeStruct((bsz, C_WIDTH, rows), BF16))
        else:
            out_specs.append(pl.BlockSpec((None, tile,SEGS[n][1]), lambda b, i: (b, i, 0)))
            out_shape.append(jax.ShapeDtypeStruct((bsz, rows, SEGS[n][1]), BF16))
    outs = pl.pallas_call(
        functools.partial(_inproj_kernel, names=tuple(names), use_rope=use_rope),
        grid=(bsz, tiles),
        in_specs=in_specs,
        out_specs=out_specs,
        out_shape=out_shape,
        scratch_shapes=[pltpu.VMEM((B_WIDTH // LANES, (tile // S5_CHUNK) * STAGE_PITCH, LANES), F32)],
        compiler_params=pltpu.CompilerParams(
            dimension_semantics=("parallel", "parallel"), vmem_limit_bytes=VMEM_LIMIT),
        name="in_projection_rope" if use_rope else "in_projection",
    )(*args)
    return dict(zip(names, outs))


def _dot_nt(a, b):
    return lax.dot_general(a, b, (((1,), (1,)), ((), ())), preferred_element_type=F32)


def _half_masked(q, lower):
    lane = lax.broadcasted_iota(jnp.int32, q.shape, 1)
    keep = lane < HEAD_DIM if lower else lane >= HEAD_DIM
    return jnp.where(keep, q.astype(F32), 0.0).astype(q.dtype)


def _swa_kernel(*refs, has_local, lat_blocks):
    if has_local:
        sink_ref, q_ref, kc_ref, vtc_ref, kl_ref, vtl_ref, o_ref = refs
    else:
        sink_ref, q_ref, kc_ref, vtc_ref, o_ref = refs
    n_ctx = kc_ref.shape[0]
    cols = A_HEADS * Q_BLOCK
    qblk = jnp.concatenate(
        [_half_masked(q_ref[:, v * LANES:(v + 1) * LANES], kh == 0)
         for kh in range(A_KV_HEADS) for v in range(A_GROUP)], axis=0)
    sink = jnp.concatenate(
        [jnp.full((1, Q_BLOCK), sink_ref[kh * A_GROUP + v], F32)
         for kh in range(A_KV_HEADS) for v in range(A_GROUP)], axis=1)
    keys = [kc_ref[...]]
    vts = [vtc_ref[t] for t in range(n_ctx // Q_BLOCK)]
    if has_local:
        n = pl.program_id(1)
        sb = jnp.clip(n - 1, 0, lat_blocks - 3)
        keys.append(kl_ref[pl.ds(pl.multiple_of(sb * Q_BLOCK, Q_BLOCK), 3 * Q_BLOCK), :])
        vts += [vtl_ref[sb + t] for t in range(3)]
    s_t = _dot_nt(jnp.concatenate(keys, axis=0), qblk)
    if has_local:
        r = lax.broadcasted_iota(jnp.int32, (3 * Q_BLOCK, Q_BLOCK), 0)
        i = lax.broadcasted_iota(jnp.int32, (3 * Q_BLOCK, Q_BLOCK), 1)
        in_band = jnp.abs((n - sb) * Q_BLOCK + i - r) <= WINDOW
        bias = jnp.where(in_band, 0.0, NEG_INF)
        s_t = jnp.concatenate([s_t[:n_ctx], s_t[n_ctx:] + jnp.concatenate([bias] * A_HEADS, axis=1)], axis=0)
    m = jnp.maximum(jnp.max(s_t, axis=0, keepdims=True), sink)
    e_t = jnp.exp(s_t - m)
    den = jnp.sum(e_t, axis=0, keepdims=True) + jnp.exp(sink - m)
    p_t = (e_t * (1.0 / den)).astype(BF16)
    o_t = jnp.dot(jnp.concatenate(vts, axis=1), p_t, preferred_element_type=F32)
    lane = lax.broadcasted_iota(jnp.int32, (Q_BLOCK, LANES), 1)
    half = A_GROUP * Q_BLOCK
    for v in range(A_GROUP):
        lower = o_t[:, v * Q_BLOCK:(v + 1) * Q_BLOCK].T
        upper = o_t[:, half + v * Q_BLOCK:half + (v + 1) * Q_BLOCK].T
        o_ref[:, v * LANES:(v + 1) * LANES] = jnp.where(lane < HEAD_DIM, lower, upper).astype(o_ref.dtype)


def _swa(sink, q, k_ctx, vt_ctx, k_lat=None, vt_lat=None):
    bsz, rows, _ = q.shape
    n_ctx = k_ctx.shape[1]
    has_local = k_lat is not None
    in_specs = [
        pl.BlockSpec(memory_space=pltpu.SMEM),
        pl.BlockSpec((None, Q_BLOCK, A_WIDTH), lambda b, n: (b, n, 0)),
        pl.BlockSpec((None, n_ctx, A_KV_WIDTH), lambda b, n: (b, 0, 0)),
        pl.BlockSpec((None, n_ctx // Q_BLOCK, A_KV_WIDTH, Q_BLOCK), lambda b, n: (b, 0, 0, 0)),
    ]
    args = [sink, q, k_ctx, vt_ctx]
    lat_blocks = 0
    if has_local:
        n_lat = k_lat.shape[1]
        lat_blocks = n_lat // Q_BLOCK
        in_specs += [pl.BlockSpec((None, n_lat, A_KV_WIDTH), lambda b, n: (b, 0, 0)),
                     pl.BlockSpec((None, lat_blocks, A_KV_WIDTH, Q_BLOCK), lambda b, n: (b, 0, 0, 0))]
        args += [k_lat, vt_lat]
    return pl.pallas_call(
        functools.partial(_swa_kernel, has_local=has_local, lat_blocks=lat_blocks),
        grid=(bsz, rows // Q_BLOCK),
        in_specs=in_specs,
        out_specs=pl.BlockSpec((None, Q_BLOCK, A_WIDTH), lambda b, n: (b, n, 0)),
        out_shape=jax.ShapeDtypeStruct((bsz, rows, A_WIDTH), F32),
        compiler_params=pltpu.CompilerParams(
            dimension_semantics=("parallel", "arbitrary"), vmem_limit_bytes=VMEM_LIMIT),
        name="swa_latent" if has_local else "swa_context",
    )(*args)


def _diff_kernel(*refs, has_lat, lam_init):
    if has_lat:
        lam_ref, sub_ref, q_ref, kc_ref, vtc_ref, kl_ref, vtl_ref, o_ref = refs
    else:
        lam_ref, sub_ref, q_ref, kc_ref, vtc_ref, o_ref = refs
    lp = lam_ref[...]
    lam = (jnp.exp(jnp.sum(lp[0:1] * lp[1:2], axis=1, keepdims=True))
           - jnp.exp(jnp.sum(lp[2:3] * lp[3:4], axis=1, keepdims=True)) + lam_init)
    sub_w = sub_ref[...] * (1.0 - lam_init)
    width = 2 * HEAD_DIM
    streams = [(kc_ref, vtc_ref)] + ([(kl_ref, vtl_ref)] if has_lat else [])
    heads = [slice(h * width, (h + 1) * width) for h in range(C_HEADS)]

    items = [(hs, slice(qb, qb + Q_BLOCK)) for qb in range(0, q_ref.shape[0], Q_BLOCK) for hs in heads]

    def scores(item):
        hs, qs = item
        qblk = jnp.concatenate([_half_masked(q_ref[qs, hs], True), _half_masked(q_ref[qs, hs], False)], axis=0)
        return [_dot_nt(k_ref[:, hs], qblk) for k_ref, _ in streams]

    s_next = scores(items[0])
    for it, (hs, qs) in enumerate(items):
        s_cur = s_next
        if it + 1 < len(items):
            s_next = scores(items[it + 1])
        m = functools.reduce(jnp.maximum, [jnp.max(s, axis=0, keepdims=True) for s in s_cur])
        e_cur = [jnp.exp(s - m) for s in s_cur]
        den = functools.reduce(jnp.add, [jnp.sum(e, axis=0, keepdims=True) for e in e_cur])
        inv = 1.0 / den
        c1 = inv[:, :Q_BLOCK]
        c2 = lam * inv[:, Q_BLOCK:]
        o_t = None
        for e, (_, vt_ref) in zip(e_cur, streams):
            a_t = (e[:, :Q_BLOCK] * c1 - e[:, Q_BLOCK:] * c2).astype(BF16)
            pv = jnp.dot(vt_ref[hs, :], a_t, preferred_element_type=F32)
            o_t = pv if o_t is None else o_t + pv
        o = o_t.T
        ms = jnp.mean(o * o, axis=1, keepdims=True)
        o_ref[qs, hs] = (o * lax.rsqrt(ms + EPS) * sub_w).astype(o_ref.dtype)


def _diff_attention(lam_params, subln, lam_init, q, k_ctx, vt_ctx, k_lat=None, vt_lat=None):
    bsz, rows, _ = q.shape
    n_ctx = k_ctx.shape[1]
    has_lat = k_lat is not None
    in_specs = [
        _const_spec((4, HEAD_DIM)),
        _const_spec((1, 2 * HEAD_DIM)),
        pl.BlockSpec((None, DIFF_Q_ROWS, C_WIDTH), lambda b, n: (b, n, 0)),
        pl.BlockSpec((None, n_ctx, C_WIDTH), lambda b, n: (b, 0, 0)),
        pl.BlockSpec((None, C_WIDTH, n_ctx), lambda b, n: (b, 0, 0)),
    ]
    args = [lam_params, subln.reshape(1, 2 * HEAD_DIM), q, k_ctx, vt_ctx]
    if has_lat:
        n_lat = k_lat.shape[1]
        in_specs += [pl.BlockSpec((None, n_lat, C_WIDTH), lambda b, n: (b, 0, 0)),
                     pl.BlockSpec((None, C_WIDTH, n_lat), lambda b, n: (b, 0, 0))]
        args += [k_lat, vt_lat]
    return pl.pallas_call(
        functools.partial(_diff_kernel, has_lat=has_lat, lam_init=lam_init),
        grid=(bsz, rows // DIFF_Q_ROWS),
        in_specs=in_specs,
        out_specs=pl.BlockSpec((None, DIFF_Q_ROWS, C_WIDTH), lambda b, n: (b, n, 0)),
        out_shape=jax.ShapeDtypeStruct((bsz, rows, C_WIDTH), F32),
        compiler_params=pltpu.CompilerParams(
            dimension_semantics=("parallel", "arbitrary"), vmem_limit_bytes=VMEM_LIMIT),
        name="diff_latent" if has_lat else "diff_context",
    )(*args)


def _s5_operators(a_re, a_im, log_dt, b_re, b_im, c_re, c_im, d_skip):
    f = lambda t: t.astype(F32)
    quad = lambda t: jnp.concatenate([t, t, t, t], axis=-1)
    a_re, a_im, b_re, b_im, c_re, c_im = map(f, (a_re, a_im, b_re, b_im, c_re, c_im))
    g, p, h = S5_GROUPS, S5_STATE, S5_GROUP
    dt = jnp.exp(f(log_dt))[..., None]
    mag = jnp.exp(a_re * dt)
    abar_re = mag * jnp.cos(a_im * dt)
    abar_im = mag * jnp.sin(a_im * dt)
    den = a_re * a_re + a_im * a_im
    nr = abar_re - 1.0
    ni = abar_im
    f_re = (nr * a_re + ni * a_im) / den
    f_im = (ni * a_re - nr * a_im) / den
    pw_re, pw_im = [jnp.ones_like(abar_re)], [jnp.zeros_like(abar_im)]
    for _ in range(S5_CHUNK):
        pre, pim = pw_re[-1], pw_im[-1]
        pw_re.append(pre * abar_re - pim * abar_im)
        pw_im.append(pre * abar_im + pim * abar_re)
    pad = [jnp.zeros_like(abar_re)] * (3 * SUBLANES - S5_CHUNK - 1)
    pw = jnp.stack([jnp.stack(pw_re + pad, axis=2), jnp.stack(pw_im + pad, axis=2)], axis=1)
    pw = jnp.transpose(quad(pw), (2, 0, 1, 3, 4))
    dsk = jnp.tile(f(d_skip).reshape(g, h), (1, S5_CW // h))
    sgn = jnp.broadcast_to(jnp.repeat(jnp.array([-1.0, 1.0, 1.0, -1.0], F32), p), (g, 4 * p))
    zero = jnp.zeros_like(dsk)
    rows = jnp.stack([quad(f_re[0]), quad(f_im[0]), quad(f_re[1]), quad(f_im[1]), dsk, sgn, zero, zero],
                     axis=1)
    bt_re = jnp.swapaxes(b_re, -1, -2)
    bt_im = jnp.swapaxes(b_im, -1, -2)
    bpk = jnp.stack([jnp.concatenate([bt_re, bt_im, bt_im, bt_re], axis=-1),
                     jnp.concatenate([bt_im, bt_re, bt_re, bt_im], axis=-1)], axis=1)
    cpk = jnp.stack([jnp.concatenate([c_re, -c_im], axis=-1),
                     jnp.concatenate([-c_im, -c_re], axis=-1)], axis=1)
    bpk = jnp.transpose(bpk, (2, 0, 1, 3, 4))
    cpk = jnp.transpose(cpk, (2, 0, 1, 3, 4))
    grp = lambda shape: pl.BlockSpec((None,) + shape, lambda i: (i,) + (0,) * len(shape))
    return pl.pallas_call(
        _s5_operator_kernel,
        grid=(g,),
        in_specs=[grp((8, 4 * p)), grp((2, 2, 3 * SUBLANES, 4 * p)), grp((2, 2, h, 4 * p)),
                  grp((2, 2, h, 2 * p))],
        out_specs=[grp((S5_CW, 3 * S5_CW)), grp((4 * p, S5_CW)), grp((SUBLANES, 2 * p))],
        out_shape=[jax.ShapeDtypeStruct((g, S5_CW, 3 * S5_CW), BF16),
                   jax.ShapeDtypeStruct((g, 4 * p, S5_CW), BF16),
                   jax.ShapeDtypeStruct((g, SUBLANES, 2 * p), F32)],
        compiler_params=pltpu.CompilerParams(dimension_semantics=("parallel",)),
        name="s5_operators",
    )(rows, pw, bpk, cpk)


def _shift_lanes(x, s):
    lo, hi = x[:, :LANES], x[:, LANES:]
    lane = lax.broadcasted_iota(jnp.int32, lo.shape, 1)
    zero = jnp.zeros_like(lo)
    rot = lambda t, r: pltpu.roll(t, r, 1) if r % LANES else t
    if s >= 0:
        if s < LANES:
            rl, rh = rot(lo, s), rot(hi, s)
            out = (jnp.where(lane >= s, rl, 0.0), jnp.where(lane >= s, rh, rl))
        else:
            rl = rot(lo, s - LANES)
            out = (zero, jnp.where(lane >= s - LANES, rl, 0.0))
    else:
        s = -s
        if s < LANES:
            rl, rh = rot(lo, LANES - s), rot(hi, LANES - s)
            out = (jnp.where(lane < LANES - s, rl, rh), jnp.where(lane < LANES - s, rh, 0.0))
        else:
            rh = rot(hi, 2 * LANES - s)
            out = (jnp.where(lane < 2 * LANES - s, rh, 0.0), zero)
    return jnp.concatenate(out, axis=1)


def _s5_operator_kernel(rows_ref, pw_ref, b_ref, c_ref, w1_ref, w2_ref, coef_ref):
    t_len = S5_CHUNK
    half = 2 * S5_STATE
    sgn = rows_ref[5:6, :]
    mxu_operand = lambda t: t.astype(BF16).astype(F32)
    a4, bx4, pr4, pi4, g2 = [], [], [], [], []
    for d in range(2):
        f_re = rows_ref[2 * d:2 * d + 1, :]
        f_im = rows_ref[2 * d + 1:2 * d + 2, :]
        p1 = b_ref[d, 0]
        p2 = b_ref[d, 1]
        a4.append(mxu_operand(f_re * p1 + sgn * f_im * p2))
        bx4.append(mxu_operand(f_re * p2 - sgn * f_im * p1))
        pr4.append(pw_ref[d, 0])
        pi4.append(pw_ref[d, 1])
        cx = jnp.concatenate([mxu_operand(c_ref[d, 0])] * t_len, axis=0)
        cy = jnp.concatenate([mxu_operand(c_ref[d, 1])] * t_len, axis=0)

        def ca(order, d=d, cx=cx, cy=cy):
            pr = jnp.concatenate(
                [jnp.broadcast_to(pr4[d][t:t + 1, :half], (S5_GROUP, half)) for t in order], axis=0)
            pi = jnp.concatenate(
                [jnp.broadcast_to(pi4[d][t:t + 1, :half], (S5_GROUP, half)) for t in order], axis=0)
            return cx * pr + cy * pi
        g2.append(ca)

    def strip(d, order):
        return lax.dot_general(a4[d][:, :half], g2[d](order), (((1,), (1,)), ((), ())),
                               precision=lax.Precision.HIGHEST, preferred_element_type=F32)

    row = lax.broadcasted_iota(jnp.int32, (S5_GROUP, S5_CW), 0)
    col = lax.broadcasted_iota(jnp.int32, (S5_GROUP, S5_CW), 1)
    k_fwd = strip(0, range(t_len)) + jnp.where(row == col, rows_ref[4:5, :], 0.0)
    k_bwd = strip(1, [t_len - 1 - k for k in range(t_len)])
    for j in range(t_len):
        lo, hi = j * S5_GROUP, (j + 1) * S5_GROUP
        m = _shift_lanes(k_fwd, j * S5_GROUP) + _shift_lanes(k_bwd, -(t_len - 1 - j) * S5_GROUP)
        w1_ref[lo:hi, 0:S5_CW] = m.astype(w1_ref.dtype)
        tf, tb = t_len - 1 - j, j
        s_f = a4[0] * pr4[0][tf:tf + 1, :] + bx4[0] * (sgn * pi4[0][tf:tf + 1, :])
        s_b = a4[1] * pr4[1][tb:tb + 1, :] + bx4[1] * (sgn * pi4[1][tb:tb + 1, :])
        w1_ref[lo:hi, S5_CW:2 * S5_CW] = s_f.astype(w1_ref.dtype)
        w1_ref[lo:hi, 2 * S5_CW:3 * S5_CW] = s_b.astype(w1_ref.dtype)
    w2_ref[0:half, :] = g2[0]([i + 1 for i in range(t_len)]).T.astype(w2_ref.dtype)
    w2_ref[half:2 * half, :] = g2[1]([t_len - i for i in range(t_len)]).T.astype(w2_ref.dtype)
    zero = jnp.zeros((1, half), F32)
    coef_ref[...] = jnp.concatenate(
        [pr4[0][t_len:t_len + 1, :half], (sgn * pi4[0][t_len:t_len + 1, :])[:, :half],
         pr4[1][t_len:t_len + 1, :half], (sgn * pi4[1][t_len:t_len + 1, :])[:, :half],
         zero, zero, zero, zero], axis=0)


def _s5_kernel(xc_ref, xl_ref, w1_ref, w2_ref, coef_ref, yc_ref, yl_ref,
               zic_ref, zil_ref, zsc_ref, zsl_ref, hc_ref, hl_ref, *, bsz, ctx_chunks, lat_chunks):
    sw = 2 * S5_STATE
    streams = ((xc_ref, zic_ref, zsc_ref, hc_ref, yc_ref, ctx_chunks),
               (xl_ref, zil_ref, zsl_ref, hl_ref, yl_ref, lat_chunks))
    w1 = w1_ref[...]
    for x_ref, zi_ref, zs_ref, _, _, n in streams:
        pitch = n + SUBLANES
        for b in range(bsz):
            z = jnp.dot(x_ref[b * n:(b + 1) * n, :], w1, preferred_element_type=F32)
            zi_ref[b * n:(b + 1) * n, :] = z[:, 0:S5_CW]
            for k in range(4):
                zs_ref[k, b * pitch:b * pitch + n, :] = z[:, S5_CW + k * sw:S5_CW + (k + 1) * sw]

    shape = (bsz, sw)
    a1f = jnp.broadcast_to(coef_ref[0:1, :], shape)
    a2f = jnp.broadcast_to(coef_ref[1:2, :], shape)
    a1b = jnp.broadcast_to(coef_ref[2:3, :], shape)
    a2b = jnp.broadcast_to(coef_ref[3:4, :], shape)

    def locate(c):
        if c < ctx_chunks:
            return zsc_ref, hc_ref, c, ctx_chunks + SUBLANES
        return zsl_ref, hl_ref, c - ctx_chunks, lat_chunks + SUBLANES

    n_chunks = ctx_chunks + lat_chunks
    hf = hfs = hb = hbs = jnp.zeros(shape, F32)
    for t in range(n_chunks):
        zs_ref, h_ref, c, pitch = locate(t)
        rows = pl.ds(c, bsz, stride=pitch)
        h_ref[0, rows, :] = hf
        lf = zs_ref[0, rows, :]
        lfs = zs_ref[1, rows, :]
        hf, hfs = a1f * hf + a2f * hfs + lf, a1f * hfs - a2f * hf + lfs
        cb = ctx_chunks - 1 - t if t < ctx_chunks else n_chunks + ctx_chunks - 1 - t
        zs_ref, h_ref, c, pitch = locate(cb)
        rows = pl.ds(c, bsz, stride=pitch)
        h_ref[1, rows, :] = hb
        lb = zs_ref[2, rows, :]
        lbs = zs_ref[3, rows, :]
        hb, hbs = a1b * hb + a2b * hbs + lb, a1b * hbs - a2b * hb + lbs

    w2 = w2_ref[...]
    for _, zi_ref, _, h_ref, y_ref, n in streams:
        pitch = n + SUBLANES
        for b in range(bsz):
            hin = jnp.concatenate([h_ref[0, b * pitch:b * pitch + n, :], h_ref[1, b * pitch:b * pitch + n, :]],
                                  axis=1)
            y = zi_ref[b * n:(b + 1) * n, :] + jnp.dot(hin.astype(BF16), w2, preferred_element_type=F32)
            y_ref[b * n:(b + 1) * n, :] = y.astype(y_ref.dtype)


def _s5_scan(x_ctx, x_lat, w1, w2, coef, bsz):
    ctx_chunks = x_ctx.shape[1] // bsz
    lat_chunks = x_lat.shape[1] // bsz
    sw = 2 * S5_STATE
    grp = lambda rows, cols: pl.BlockSpec((None, rows, cols), lambda g: (g, 0, 0))
    slab = lambda k, n: pltpu.VMEM((k, bsz * (n + SUBLANES), sw), F32)
    return pl.pallas_call(
        functools.partial(_s5_kernel, bsz=bsz, ctx_chunks=ctx_chunks, lat_chunks=lat_chunks),
        grid=(S5_GROUPS,),
        in_specs=[
            grp(bsz * ctx_chunks, S5_CW), grp(bsz * lat_chunks, S5_CW),
            grp(S5_CW, 3 * S5_CW), grp(4 * S5_STATE, S5_CW), grp(SUBLANES, sw),
        ],
        out_specs=[grp(bsz * ctx_chunks, S5_CW), grp(bsz * lat_chunks, S5_CW)],
        out_shape=[jax.ShapeDtypeStruct(x_ctx.shape, F32), jax.ShapeDtypeStruct(x_lat.shape, F32)],
        scratch_shapes=[
            pltpu.VMEM((bsz * ctx_chunks, S5_CW), F32), pltpu.VMEM((bsz * lat_chunks, S5_CW), F32),
            slab(4, ctx_chunks), slab(4, lat_chunks), slab(2, ctx_chunks), slab(2, lat_chunks),
        ],
        compiler_params=pltpu.CompilerParams(
            dimension_semantics=("parallel",), vmem_limit_bytes=VMEM_LIMIT),
        name="s5_scan",
    )(x_ctx, x_lat, w1, w2, coef)


def _gelu_tanh(x):
    return 0.5 * x * (1.0 + jnp.tanh(math.sqrt(2.0 / math.pi) * (x + 0.044715 * (x * x * x))))


def _silu(x):
    return x * jax.nn.sigmoid(x)


def _merge_kernel(h_ref, mod_ref, npre_ref, wg_ref, oa_ref, yb_ref, oc_ref,
                  wglu_ref, bglu_ref, woa_ref, wob_ref, woc_ref, wout_ref, npost_ref, o_ref, stage_ref):
    hn = _modulated_norm(h_ref[...], mod_ref, npre_ref)
    yb = _gelu_tanh(_groups_to_tokens(yb_ref, stage_ref))
    glu = jnp.dot(yb.astype(BF16), wglu_ref[...], preferred_element_type=F32) + bglu_ref[...]
    ob = yb * jax.nn.sigmoid(glu)

    def branch(i, o, w_ref):
        g = jnp.dot(hn, wg_ref[:, i * A_WIDTH:(i + 1) * A_WIDTH], preferred_element_type=F32)
        t = jnp.dot((o * _silu(g)).astype(BF16), w_ref[...], preferred_element_type=F32)
        m_lo = N_BRANCH * A_WIDTH + i * D_MODEL
        gate = jax.nn.sigmoid(jnp.dot(hn, wg_ref[:, m_lo:m_lo + D_MODEL], preferred_element_type=F32))
        return gate * t

    y = branch(0, oa_ref[...], woa_ref) + branch(1, ob, wob_ref) + branch(2, oc_ref[...], woc_ref)
    y = jnp.dot(y.astype(BF16), wout_ref[...], preferred_element_type=F32)
    ms = jnp.mean(y * y, axis=-1, keepdims=True)
    yn = y * lax.rsqrt(ms + EPS) * npost_ref[...]
    gate = mod_ref[:, 2 * D_MODEL:3 * D_MODEL]
    o_ref[...] = h_ref[...] + gate * yn


def _merge(h, mod, mod_row, norm_pre, w_gate, o_a, y_b, o_c, wts):
    bsz, rows, _ = h.shape
    tile = min(ROW_TILE, rows)
    tiles = rows // tile
    if mod_row is None:
        mod_map = lambda b, i: (b, 0, 0)
    else:
        mod_map = lambda b, i: (mod_row, 0, 0)
    tok = lambda w: pl.BlockSpec((None, tile,w), lambda b, i: (b, i, 0))
    in_specs = [
        tok(D_MODEL),
        pl.BlockSpec((None, 1, 3 * D_MODEL), mod_map),
        _const_spec((1, D_MODEL)), _const_spec((D_MODEL, GATE_WIDTH)),
        tok(A_WIDTH),
        pl.BlockSpec((S5_GROUPS, tile // S5_CHUNK, S5_CW), lambda b, i: (0, b * tiles + i, 0)),
        tok(C_WIDTH),
        _const_spec((B_WIDTH, B_WIDTH)), _const_spec((1, B_WIDTH)),
        _const_spec((A_WIDTH, D_MODEL)), _const_spec((B_WIDTH, D_MODEL)), _const_spec((C_WIDTH, D_MODEL)),
        _const_spec((D_MODEL, D_MODEL)), _const_spec((1, D_MODEL)),
    ]
    return pl.pallas_call(
        _merge_kernel,
        grid=(bsz, rows // tile),
        in_specs=in_specs,
        out_specs=tok(D_MODEL),
        out_shape=jax.ShapeDtypeStruct((bsz, rows, D_MODEL), F32),
        scratch_shapes=[pltpu.VMEM((B_WIDTH // LANES, (tile // S5_CHUNK) * STAGE_PITCH, LANES), F32)],
        compiler_params=pltpu.CompilerParams(
            dimension_semantics=("parallel", "parallel"), vmem_limit_bytes=VMEM_LIMIT),
        name="merge_out",
    )(h, mod, norm_pre.reshape(1, D_MODEL), w_gate, o_a, y_b, o_c, *wts)


def _rope_tables(n_tokens):
    rows = n_tokens // GRID_W
    pos_r = jnp.repeat(jnp.arange(rows, dtype=F32), GRID_W)
    pos_c = jnp.tile(jnp.arange(GRID_W, dtype=F32), rows)
    inv = ROPE_BASE ** (-jnp.arange(AX_FREQS, dtype=F32) / AX_FREQS)
    ang_r = pos_r[:, None] * inv[None]
    ang_c = pos_c[:, None] * inv[None]
    ang = jnp.concatenate([ang_r, ang_r, ang_c, ang_c], axis=-1)
    cos, sin = jnp.cos(ang), jnp.sin(ang)
    first = (jnp.arange(HEAD_DIM) % (2 * AX_FREQS)) < AX_FREQS
    sa = jnp.where(first[None], -sin, 0.0)
    sb = jnp.where(first[None], 0.0, sin)
    tile2 = lambda t: jnp.concatenate([t, t], axis=-1)
    return tile2(cos), tile2(sa), tile2(sb)


def _reorder_a_heads(w, axis):
    shape = w.shape
    split = shape[:axis] + (A_HEADS, HEAD_DIM) + shape[axis + 1:]
    return jnp.take(w.reshape(split), jnp.array(A_HEAD_ORDER), axis=axis).reshape(shape)


def kernel(x, c, ctx, c_ctx, w_mod, b_mod, norm_pre, norm_post, w_in, swa_sink, s5_a_re, s5_a_im, s5_log_dt, s5_b_re, s5_b_im, s5_c_re, s5_c_im, s5_d, s5_w_glu, s5_b_glu, diff_lq1, diff_lk1, diff_lq2, diff_lk2, diff_subln, w_o_a, w_o_b, w_o_c, w_out):
    bsz, n_lat, _ = x.shape
    n_ctx = ctx.shape[1]
    ctx_row = bsz
    cc = jnp.zeros((16, D_MODEL), F32).at[:bsz].set(c).at[ctx_row].set(c_ctx)
    mod_all = _modulation(cc, w_mod, b_mod)
    rope_tabs = _rope_tables(n_lat)

    h_lat, h_ctx = x, ctx
    for l in range(DEPTH):
        last = l == DEPTH - 1
        lam_init = 0.8 - 0.6 * math.exp(-0.3 * l)
        mod = mod_all[l].reshape(16, 1, 3 * D_MODEL)
        w_l = w_in[l]
        cols = lambda name: w_l[:, SEGS[name][0]:SEGS[name][0] + SEGS[name][1]]
        aq0, aqw = SEGS["aq"]
        w_in_l = jnp.concatenate([_reorder_a_heads(cols("aq"), 1), w_l[:, aq0 + aqw:]], axis=1).astype(BF16)
        w_gate = jnp.concatenate(
            [_reorder_a_heads(cols("ag"), 1), cols("bg"), cols("cg"), cols("mg")], axis=1).astype(BF16)
        zl = _in_projection(h_lat, mod, None, norm_pre[l], w_in_l, PROJ_NAMES, rope_tabs)
        ctx_names = ("ak", "av", "bu", "ck", "cv") if last else PROJ_NAMES
        zc = _in_projection(h_ctx, mod, ctx_row, norm_pre[l], w_in_l, ctx_names, None)

        o_a_l = _swa(swa_sink[l], zl["aq"], zc["ak"], zc["av"], zl["ak"], zl["av"])

        w1, w2, coef = _s5_operators(s5_a_re[l], s5_a_im[l], s5_log_dt[l], s5_b_re[l], s5_b_im[l],
                                     s5_c_re[l], s5_c_im[l], s5_d[l])
        y_b_c, y_b_l = _s5_scan(zc["bu"], zl["bu"], w1, w2, coef, bsz)

        lam_params = jnp.stack([diff_lq1[l], diff_lk1[l], diff_lq2[l], diff_lk2[l]]).astype(F32)
        o_c_l = _diff_attention(lam_params, diff_subln[l], lam_init, zl["cq"], zc["ck"], zc["cv"],
                                zl["ck"], zl["cv"])

        wts = (s5_w_glu[l].astype(BF16), s5_b_glu[l].reshape(1, B_WIDTH),
               _reorder_a_heads(w_o_a[l], 0).astype(BF16), w_o_b[l].astype(BF16), w_o_c[l].astype(BF16),
               w_out[l].astype(BF16), norm_post[l].reshape(1, D_MODEL))
        h_lat_new = _merge(h_lat, mod, None, norm_pre[l], w_gate, o_a_l, y_b_l, o_c_l, wts)
        if not last:
            o_a_c = _swa(swa_sink[l], zc["aq"], zc["ak"], zc["av"])
            o_c_c = _diff_attention(lam_params, diff_subln[l], lam_init, zc["cq"], zc["ck"], zc["cv"])
            h_ctx = _merge(h_ctx, mod, ctx_row, norm_pre[l], w_gate, o_a_c, y_b_c, o_c_c, wts)
        h_lat = h_lat_new
    return h_lat
```

```python
import functools
import math

import jax
import jax.numpy as jnp
from jax import lax
from jax.experimental import pallas as pl
from jax.experimental.pallas import tpu as pltpu

F32 = jnp.float32
BF16 = jnp.bfloat16

D_MODEL = 1024
DEPTH = 2
GRID_W = 64
HEAD_DIM = 64
WINDOW = 128
ROPE_BASE = 10000.0
AX_FREQS = HEAD_DIM // 4
NEG_INF = -1e30
A_HEADS = 8
A_KV_HEADS = 2
A_GROUP = A_HEADS // A_KV_HEADS
A_WIDTH = A_HEADS * HEAD_DIM
A_KV_WIDTH = A_KV_HEADS * HEAD_DIM
B_WIDTH = 512
S5_GROUP = 16
S5_GROUPS = B_WIDTH // S5_GROUP
S5_STATE = 64
C_HEADS = 4
C_WIDTH = C_HEADS * 2 * HEAD_DIM
N_BRANCH = 3
GATE_WIDTH = A_WIDTH + B_WIDTH + C_WIDTH + N_BRANCH * D_MODEL
EPS = 1e-6

LANES = 128
SUBLANES = 8

ROW_TILE = 512
Q_BLOCK = 128
SWA_Q_ROWS = 256
DIFF_Q_ROWS = 256
A_HEAD_ORDER = tuple(h for v in range(4) for h in (v, 4 + v))
S5_CHUNK = 16
S5_CW = S5_CHUNK * S5_GROUP
STAGE_PITCH = S5_CHUNK + SUBLANES
VMEM_LIMIT = 56 * 1024 * 1024

_SEG_NAMES = ("aq", "ak", "av", "ag", "bu", "bg", "cq", "ck", "cv", "cg", "mg")
_SEG_WIDTHS = (A_WIDTH, A_KV_WIDTH, A_KV_WIDTH, A_WIDTH, B_WIDTH, B_WIDTH,
               C_WIDTH, C_WIDTH, C_WIDTH, C_WIDTH, 3 * D_MODEL)
SEGS = {}
_off = 0
for _n, _w in zip(_SEG_NAMES, _SEG_WIDTHS):
    SEGS[_n] = (_off, _w)
    _off += _w
IN_WIDTH = _off
PROJ_NAMES = ("aq", "ak", "av", "bu", "cq", "ck", "cv")
Q_SCALE = HEAD_DIM ** -0.5 * math.log2(math.e)
ROPE_SEGS = {"aq": Q_SCALE, "ak": 1.0, "cq": Q_SCALE, "ck": 1.0}


def _const_spec(shape):
    nd = len(shape)
    return pl.BlockSpec(shape, lambda *_: (0,) * nd, pipeline_mode=pl.Buffered(1))


def _mod_kernel(c_ref, w_ref, b_ref, o_ref):
    c = c_ref[...]
    a = c * jax.nn.sigmoid(c)
    o_ref[...] = jnp.dot(a, w_ref[...], preferred_element_type=F32) + b_ref[...]


def _modulation(cc, w_mod, b_mod):
    n_tiles = 3
    return pl.pallas_call(
        _mod_kernel,
        grid=(DEPTH, n_tiles),
        in_specs=[
            pl.BlockSpec((16, D_MODEL), lambda l, j: (0, 0)),
            pl.BlockSpec((None, D_MODEL, D_MODEL), lambda l, j: (l, 0, j)),
            pl.BlockSpec((None, 1, D_MODEL), lambda l, j: (l, 0, j)),
        ],
        out_specs=pl.BlockSpec((None, 16, D_MODEL), lambda l, j: (l, 0, j)),
        out_shape=jax.ShapeDtypeStruct((DEPTH, 16, 3 * D_MODEL), F32),
        name="modulation",
    )(cc, w_mod, b_mod.reshape(DEPTH, 1, 3 * D_MODEL))


def _block_transpose8(xs):
    lane_blk = lax.broadcasted_iota(jnp.int32, xs[0].shape, 1) // S5_GROUP
    for s in (4, 2, 1):
        keep = (lane_blk & s) == 0
        out = list(xs)
        for p in range(8):
            if p & s:
                continue
            a, b = xs[p], xs[p + s]
            out[p] = jnp.where(keep, a, pltpu.roll(b, S5_GROUP * s, 1))
            out[p + s] = jnp.where(keep, pltpu.roll(a, LANES - S5_GROUP * s, 1), b)
        xs = out
    return xs


def _tokens_to_groups(z, stage_ref, o_ref):
    n_chunks = z.shape[0] // S5_CHUNK
    for v in range(B_WIDTH // LANES):
        for c in range(n_chunks):
            stage_ref[v, c * STAGE_PITCH:c * STAGE_PITCH + S5_CHUNK, :] = \
                z[c * S5_CHUNK:(c + 1) * S5_CHUNK, v * LANES:(v + 1) * LANES]
    for w in range(S5_CW // LANES):
        for v in range(B_WIDTH // LANES):
            src = [stage_ref[v, pl.ds(8 * w + jj, n_chunks, stride=STAGE_PITCH), :] for jj in range(8)]
            dst = _block_transpose8(src)
            for gg in range(8):
                o_ref[8 * v + gg, :, w * LANES:(w + 1) * LANES] = dst[gg].astype(o_ref.dtype)


def _groups_to_tokens(y_ref, stage_ref):
    n_chunks = y_ref.shape[1]
    for w in range(S5_CW // LANES):
        for v in range(B_WIDTH // LANES):
            src = [y_ref[8 * v + gg, :, w * LANES:(w + 1) * LANES].astype(F32) for gg in range(8)]
            dst = _block_transpose8(src)
            for jj in range(8):
                stage_ref[v, pl.ds(8 * w + jj, n_chunks, stride=STAGE_PITCH), :] = dst[jj]
    rows = []
    for c in range(n_chunks):
        rows.append(jnp.concatenate(
            [stage_ref[v, c * STAGE_PITCH:c * STAGE_PITCH + S5_CHUNK, :] for v in range(B_WIDTH // LANES)], axis=1))
    return jnp.concatenate(rows, axis=0)


def _modulated_norm(x, mod_ref, nw_ref):
    ms = jnp.mean(x * x, axis=-1, keepdims=True)
    xn = x * lax.rsqrt(ms + EPS) * nw_ref[...]
    shift = mod_ref[:, 0:D_MODEL]
    scale = mod_ref[:, D_MODEL:2 * D_MODEL]
    return (xn * (1.0 + scale) + shift).astype(BF16)


def _inproj_kernel(*refs, names, use_rope):
    h_ref, mod_ref, nw_ref, w_ref = refs[:4]
    stage_ref = refs[-1]
    refs = refs[:-1]
    if use_rope:
        cos_ref, sa_ref, sb_ref = refs[4:7]
        outs = refs[7:]
    else:
        outs = refs[4:]
    hn = _modulated_norm(h_ref[...], mod_ref, nw_ref)
    kv_off = SEGS["ak"][0]
    assert SEGS["av"][0] == kv_off + A_KV_WIDTH
    kv = jnp.dot(hn, w_ref[:, kv_off:kv_off + 2 * A_KV_WIDTH], preferred_element_type=F32)

    def project(name):
        off, width = SEGS[name]
        if name in ("ak", "av"):
            return kv[:, off - kv_off:off - kv_off + width]
        return jnp.dot(hn, w_ref[:, off:off + width], preferred_element_type=F32)

    for name, o_ref in zip(names, outs):
        z = project(name)
        mul = ROPE_SEGS.get(name, 1.0)
        if use_rope and name in ROPE_SEGS:
            cos = cos_ref[...]
            sa = sa_ref[...]
            sb = sb_ref[...]
            for c in range(z.shape[1] // LANES):
                t = z[:, c * LANES:(c + 1) * LANES]
                r = t * cos + pltpu.roll(t, LANES - AX_FREQS, 1) * sa + pltpu.roll(t, AX_FREQS, 1) * sb
                if mul != 1.0:
                    r = r * mul
                o_ref[:, c * LANES:(c + 1) * LANES] = r.astype(o_ref.dtype)
        elif name == "bu":
            _tokens_to_groups(z, stage_ref, o_ref)
        elif name == "av":
            for t in range(z.shape[0] // Q_BLOCK):
                o_ref[t] = z[t * Q_BLOCK:(t + 1) * Q_BLOCK, :].T.astype(o_ref.dtype)
        elif name == "cv":
            o_ref[...] = z.T.astype(o_ref.dtype)
        else:
            if mul != 1.0:
                z = z * mul
            o_ref[...] = z.astype(o_ref.dtype)


def _in_projection(h, mod, mod_row, norm_w, w_bf16, names, rope_tabs):
    bsz, rows, _ = h.shape
    tile = min(ROW_TILE, rows)
    use_rope = rope_tabs is not None
    if mod_row is None:
        mod_map = lambda b, i: (b, 0, 0)
    else:
        mod_map = lambda b, i: (mod_row, 0, 0)
    in_specs = [
        pl.BlockSpec((None, tile,D_MODEL), lambda b, i: (b, i, 0)),
        pl.BlockSpec((None, 1, 3 * D_MODEL), mod_map),
        _const_spec((1, D_MODEL)),
        _const_spec((D_MODEL, IN_WIDTH)),
    ]
    args = [h, mod, norm_w.reshape(1, D_MODEL), w_bf16]
    if use_rope:
        for t in rope_tabs:
            in_specs.append(pl.BlockSpec((tile, LANES), lambda b, i: (i, 0)))
            args.append(t)
    tiles = rows // tile
    chunk_rows = tile // S5_CHUNK
    out_specs, out_shape = [], []
    for n in names:
        if n == "bu":
            out_specs.append(pl.BlockSpec((S5_GROUPS, chunk_rows, S5_CW), lambda b, i: (0, b * tiles + i, 0)))
            out_shape.append(jax.ShapeDtypeStruct((S5_GROUPS, bsz * rows // S5_CHUNK, S5_CW), BF16))
        elif n == "av":
            kb = tile // Q_BLOCK
            out_specs.append(pl.BlockSpec((None, kb, A_KV_WIDTH, Q_BLOCK), lambda b, i: (b, i, 0, 0)))
            out_shape.append(jax.ShapeDtypeStruct((bsz, rows // Q_BLOCK, A_KV_WIDTH, Q_BLOCK), BF16))
        elif n == "cv":
            out_specs.append(pl.BlockSpec((None, C_WIDTH, tile), lambda b, i: (b, 0, i)))
            out_shape.append(jax.ShapeDtypeStruct((bsz, C_WIDTH, rows), BF16))
        else:
            out_specs.append(pl.BlockSpec((None, tile,SEGS[n][1]), lambda b, i: (b, i, 0)))
            out_shape.append(jax.ShapeDtypeStruct((bsz, rows, SEGS[n][1]), BF16))
    outs = pl.pallas_call(
        functools.partial(_inproj_kernel, names=tuple(names), use_rope=use_rope),
        grid=(bsz, tiles),
        in_specs=in_specs,
        out_specs=out_specs,
        out_shape=out_shape,
        scratch_shapes=[pltpu.VMEM((B_WIDTH // LANES, (tile // S5_CHUNK) * STAGE_PITCH, LANES), F32)],
        compiler_params=pltpu.CompilerParams(
            dimension_semantics=("parallel", "parallel"), vmem_limit_bytes=VMEM_LIMIT),
        name="in_projection_rope" if use_rope else "in_projection",
    )(*args)
    return dict(zip(names, outs))


def _dot_nt(a, b):
    return lax.dot_general(a, b, (((1,), (1,)), ((), ())), preferred_element_type=F32)


def _half_masked(q, lower):
    lane = lax.broadcasted_iota(jnp.int32, q.shape, 1)
    keep = lane < HEAD_DIM if lower else lane >= HEAD_DIM
    return jnp.where(keep, q.astype(F32), 0.0).astype(q.dtype)


def _swa_kernel(*refs, has_local, lat_blocks):
    if has_local:
        sink_ref, q_ref, kc_ref, vtc_ref, kl_ref, vtl_ref, o_ref = refs
    else:
        sink_ref, q_ref, kc_ref, vtc_ref, o_ref = refs
    n_ctx = kc_ref.shape[0]
    n_items = q_ref.shape[0] // Q_BLOCK
    sink = jnp.concatenate(
        [jnp.full((1, Q_BLOCK), sink_ref[kh * A_GROUP + v] * math.log2(math.e), F32)
         for kh in range(A_KV_HEADS) for v in range(A_GROUP)], axis=1)

    def scores(t):
        qs = slice(t * Q_BLOCK, (t + 1) * Q_BLOCK)
        qblk = jnp.concatenate(
            [_half_masked(q_ref[qs, v * LANES:(v + 1) * LANES], kh == 0)
             for kh in range(A_KV_HEADS) for v in range(A_GROUP)], axis=0)
        keys = [kc_ref[...]]
        vts = [vtc_ref[c] for c in range(n_ctx // Q_BLOCK)]
        if has_local:
            n = pl.program_id(1) * n_items + t
            sb = jnp.clip(n - 1, 0, lat_blocks - 3)
            keys.append(kl_ref[pl.ds(pl.multiple_of(sb * Q_BLOCK, Q_BLOCK), 3 * Q_BLOCK), :])
            vts += [vtl_ref[sb + c] for c in range(3)]
        s_t = _dot_nt(jnp.concatenate(keys, axis=0), qblk)
        if has_local:
            r = lax.broadcasted_iota(jnp.int32, (3 * Q_BLOCK, Q_BLOCK), 0)
            i = lax.broadcasted_iota(jnp.int32, (3 * Q_BLOCK, Q_BLOCK), 1)
            in_band = jnp.abs((n - sb) * Q_BLOCK + i - r) <= WINDOW
            bias = jnp.where(in_band, 0.0, NEG_INF)
            s_t = jnp.concatenate([s_t[:n_ctx], s_t[n_ctx:] + jnp.concatenate([bias] * A_HEADS, axis=1)],
                                  axis=0)
        return s_t, jnp.concatenate(vts, axis=1)

    lane = lax.broadcasted_iota(jnp.int32, (Q_BLOCK, LANES), 1)
    half = A_GROUP * Q_BLOCK
    nxt = scores(0)
    for t in range(n_items):
        s_t, vt = nxt
        if t + 1 < n_items:
            nxt = scores(t + 1)
        m = jnp.maximum(jnp.max(s_t, axis=0, keepdims=True), sink)
        e_t = jnp.exp2(s_t - m)
        den = jnp.sum(e_t, axis=0, keepdims=True) + jnp.exp2(sink - m)
        p_t = (e_t * (1.0 / den)).astype(BF16)
        o_t = jnp.dot(vt, p_t, preferred_element_type=F32)
        qs = slice(t * Q_BLOCK, (t + 1) * Q_BLOCK)
        for v in range(A_GROUP):
            lower = o_t[:, v * Q_BLOCK:(v + 1) * Q_BLOCK].T
            upper = o_t[:, half + v * Q_BLOCK:half + (v + 1) * Q_BLOCK].T
            o_ref[qs, v * LANES:(v + 1) * LANES] = jnp.where(lane < HEAD_DIM, lower, upper).astype(o_ref.dtype)


def _swa(sink, q, k_ctx, vt_ctx, k_lat=None, vt_lat=None):
    bsz, rows, _ = q.shape
    n_ctx = k_ctx.shape[1]
    has_local = k_lat is not None
    in_specs = [
        pl.BlockSpec(memory_space=pltpu.SMEM),
        pl.BlockSpec((None, SWA_Q_ROWS, A_WIDTH), lambda b, n: (b, n, 0)),
        pl.BlockSpec((None, n_ctx, A_KV_WIDTH), lambda b, n: (b, 0, 0)),
        pl.BlockSpec((None, n_ctx // Q_BLOCK, A_KV_WIDTH, Q_BLOCK), lambda b, n: (b, 0, 0, 0)),
    ]
    args = [sink, q, k_ctx, vt_ctx]
    lat_blocks = 0
    if has_local:
        n_lat = k_lat.shape[1]
        lat_blocks = n_lat // Q_BLOCK
        in_specs += [pl.BlockSpec((None, n_lat, A_KV_WIDTH), lambda b, n: (b, 0, 0)),
                     pl.BlockSpec((None, lat_blocks, A_KV_WIDTH, Q_BLOCK), lambda b, n: (b, 0, 0, 0))]
        args += [k_lat, vt_lat]
    return pl.pallas_call(
        functools.partial(_swa_kernel, has_local=has_local, lat_blocks=lat_blocks),
        grid=(bsz, rows // SWA_Q_ROWS),
        in_specs=in_specs,
        out_specs=pl.BlockSpec((None, SWA_Q_ROWS, A_WIDTH), lambda b, n: (b, n, 0)),
        out_shape=jax.ShapeDtypeStruct((bsz, rows, A_WIDTH), F32),
        compiler_params=pltpu.CompilerParams(
            dimension_semantics=("parallel", "arbitrary"), vmem_limit_bytes=VMEM_LIMIT),
        name="swa_latent" if has_local else "swa_context",
    )(*args)


def _diff_kernel(*refs, has_lat, lam_init):
    if has_lat:
        lam_ref, sub_ref, q_ref, kc_ref, vtc_ref, kl_ref, vtl_ref, o_ref = refs
    else:
        lam_ref, sub_ref, q_ref, kc_ref, vtc_ref, o_ref = refs
    lp = lam_ref[...]
    lam = (jnp.exp(jnp.sum(lp[0:1] * lp[1:2], axis=1, keepdims=True))
           - jnp.exp(jnp.sum(lp[2:3] * lp[3:4], axis=1, keepdims=True)) + lam_init)
    sub_w = sub_ref[...] * (1.0 - lam_init)
    width = 2 * HEAD_DIM
    streams = [(kc_ref, vtc_ref)] + ([(kl_ref, vtl_ref)] if has_lat else [])
    heads = [slice(h * width, (h + 1) * width) for h in range(C_HEADS)]

    items = [(hs, slice(qb, qb + Q_BLOCK)) for qb in range(0, q_ref.shape[0], Q_BLOCK) for hs in heads]

    def scores(item):
        hs, qs = item
        qblk = jnp.concatenate([_half_masked(q_ref[qs, hs], True), _half_masked(q_ref[qs, hs], False)], axis=0)
        return [_dot_nt(k_ref[:, hs], qblk) for k_ref, _ in streams]

    s_next = scores(items[0])
    for it, (hs, qs) in enumerate(items):
        s_cur = s_next
        if it + 1 < len(items):
            s_next = scores(items[it + 1])
        m = functools.reduce(jnp.maximum, [jnp.max(s, axis=0, keepdims=True) for s in s_cur])
        e_cur = [jnp.exp2(s - m) for s in s_cur]
        den = functools.reduce(jnp.add, [jnp.sum(e, axis=0, keepdims=True) for e in e_cur])
        inv = 1.0 / den
        c1 = inv[:, :Q_BLOCK]
        c2 = lam * inv[:, Q_BLOCK:]
        o_t = None
        for e, (_, vt_ref) in zip(e_cur, streams):
            a_t = (e[:, :Q_BLOCK] * c1 - e[:, Q_BLOCK:] * c2).astype(BF16)
            pv = jnp.dot(vt_ref[hs, :], a_t, preferred_element_type=F32)
            o_t = pv if o_t is None else o_t + pv
        o = o_t.T
        ms = jnp.mean(o * o, axis=1, keepdims=True)
        o_ref[qs, hs] = (o * lax.rsqrt(ms + EPS) * sub_w).astype(o_ref.dtype)


def _diff_attention(lam_params, subln, lam_init, q, k_ctx, vt_ctx, k_lat=None, vt_lat=None):
    bsz, rows, _ = q.shape
    n_ctx = k_ctx.shape[1]
    has_lat = k_lat is not None
    in_specs = [
        _const_spec((4, HEAD_DIM)),
        _const_spec((1, 2 * HEAD_DIM)),
        pl.BlockSpec((None, DIFF_Q_ROWS, C_WIDTH), lambda b, n: (b, n, 0)),
        pl.BlockSpec((None, n_ctx, C_WIDTH), lambda b, n: (b, 0, 0)),
        pl.BlockSpec((None, C_WIDTH, n_ctx), lambda b, n: (b, 0, 0)),
    ]
    args = [lam_params, subln.reshape(1, 2 * HEAD_DIM), q, k_ctx, vt_ctx]
    if has_lat:
        n_lat = k_lat.shape[1]
        in_specs += [pl.BlockSpec((None, n_lat, C_WIDTH), lambda b, n: (b, 0, 0)),
                     pl.BlockSpec((None, C_WIDTH, n_lat), lambda b, n: (b, 0, 0))]
        args += [k_lat, vt_lat]
    return pl.pallas_call(
        functools.partial(_diff_kernel, has_lat=has_lat, lam_init=lam_init),
        grid=(bsz, rows // DIFF_Q_ROWS),
        in_specs=in_specs,
        out_specs=pl.BlockSpec((None, DIFF_Q_ROWS, C_WIDTH), lambda b, n: (b, n, 0)),
        out_shape=jax.ShapeDtypeStruct((bsz, rows, C_WIDTH), F32),
        compiler_params=pltpu.CompilerParams(
            dimension_semantics=("parallel", "arbitrary"), vmem_limit_bytes=VMEM_LIMIT),
        name="diff_latent" if has_lat else "diff_context",
    )(*args)


def _s5_operators(a_re, a_im, log_dt, b_re, b_im, c_re, c_im, d_skip):
    f = lambda t: t.astype(F32)
    quad = lambda t: jnp.concatenate([t, t, t, t], axis=-1)
    a_re, a_im, b_re, b_im, c_re, c_im = map(f, (a_re, a_im, b_re, b_im, c_re, c_im))
    g, p, h = S5_GROUPS, S5_STATE, S5_GROUP
    dt = jnp.exp(f(log_dt))[..., None]
    mag = jnp.exp(a_re * dt)
    abar_re = mag * jnp.cos(a_im * dt)
    abar_im = mag * jnp.sin(a_im * dt)
    den = a_re * a_re + a_im * a_im
    nr = abar_re - 1.0
    ni = abar_im
    f_re = (nr * a_re + ni * a_im) / den
    f_im = (ni * a_re - nr * a_im) / den
    pw_re, pw_im = [jnp.ones_like(abar_re)], [jnp.zeros_like(abar_im)]
    for _ in range(S5_CHUNK):
        pre, pim = pw_re[-1], pw_im[-1]
        pw_re.append(pre * abar_re - pim * abar_im)
        pw_im.append(pre * abar_im + pim * abar_re)
    pad = [jnp.zeros_like(abar_re)] * (3 * SUBLANES - S5_CHUNK - 1)
    pw = jnp.stack([jnp.stack(pw_re + pad, axis=2), jnp.stack(pw_im + pad, axis=2)], axis=1)
    pw = jnp.transpose(quad(pw), (2, 0, 1, 3, 4))
    dsk = jnp.tile(f(d_skip).reshape(g, h), (1, S5_CW // h))
    sgn = jnp.broadcast_to(jnp.repeat(jnp.array([-1.0, 1.0, 1.0, -1.0], F32), p), (g, 4 * p))
    zero = jnp.zeros_like(dsk)
    rows = jnp.stack([quad(f_re[0]), quad(f_im[0]), quad(f_re[1]), quad(f_im[1]), dsk, sgn, zero, zero],
                     axis=1)
    bt_re = jnp.swapaxes(b_re, -1, -2)
    bt_im = jnp.swapaxes(b_im, -1, -2)
    bpk = jnp.stack([jnp.concatenate([bt_re, bt_im, bt_im, bt_re], axis=-1),
                     jnp.concatenate([bt_im, bt_re, bt_re, bt_im], axis=-1)], axis=1)
    cpk = jnp.stack([jnp.concatenate([c_re, -c_im], axis=-1),
                     jnp.concatenate([-c_im, -c_re], axis=-1)], axis=1)
    bpk = jnp.transpose(bpk, (2, 0, 1, 3, 4))
    cpk = jnp.transpose(cpk, (2, 0, 1, 3, 4))
    grp = lambda shape: pl.BlockSpec((None,) + shape, lambda i: (i,) + (0,) * len(shape))
    return pl.pallas_call(
        _s5_operator_kernel,
        grid=(g,),
        in_specs=[grp((8, 4 * p)), grp((2, 2, 3 * SUBLANES, 4 * p)), grp((2, 2, h, 4 * p)),
                  grp((2, 2, h, 2 * p))],
        out_specs=[grp((S5_CW, 3 * S5_CW)), grp((4 * p, S5_CW)), grp((SUBLANES, 2 * p))],
        out_shape=[jax.ShapeDtypeStruct((g, S5_CW, 3 * S5_CW), BF16),
                   jax.ShapeDtypeStruct((g, 4 * p, S5_CW), BF16),
                   jax.ShapeDtypeStruct((g, SUBLANES, 2 * p), F32)],
        compiler_params=pltpu.CompilerParams(dimension_semantics=("parallel",)),
        name="s5_operators",
    )(rows, pw, bpk, cpk)


def _shift_lanes(x, s):
    lo, hi = x[:, :LANES], x[:, LANES:]
    lane = lax.broadcasted_iota(jnp.int32, lo.shape, 1)
    zero = jnp.zeros_like(lo)
    rot = lambda t, r: pltpu.roll(t, r, 1) if r % LANES else t
    if s >= 0:
        if s < LANES:
            rl, rh = rot(lo, s), rot(hi, s)
            out = (jnp.where(lane >= s, rl, 0.0), jnp.where(lane >= s, rh, rl))
        else:
            rl = rot(lo, s - LANES)
            out = (zero, jnp.where(lane >= s - LANES, rl, 0.0))
    else:
        s = -s
        if s < LANES:
            rl, rh = rot(lo, LANES - s), rot(hi, LANES - s)
            out = (jnp.where(lane < LANES - s, rl, rh), jnp.where(lane < LANES - s, rh, 0.0))
        else:
            rh = rot(hi, 2 * LANES - s)
            out = (jnp.where(lane < 2 * LANES - s, rh, 0.0), zero)
    return jnp.concatenate(out, axis=1)


def _s5_operator_kernel(rows_ref, pw_ref, b_ref, c_ref, w1_ref, w2_ref, coef_ref):
    t_len = S5_CHUNK
    half = 2 * S5_STATE
    sgn = rows_ref[5:6, :]
    mxu_operand = lambda t: t.astype(BF16).astype(F32)
    a4, bx4, pr4, pi4, g2 = [], [], [], [], []
    for d in range(2):
        f_re = rows_ref[2 * d:2 * d + 1, :]
        f_im = rows_ref[2 * d + 1:2 * d + 2, :]
        p1 = b_ref[d, 0]
        p2 = b_ref[d, 1]
        a4.append(mxu_operand(f_re * p1 + sgn * f_im * p2))
        bx4.append(mxu_operand(f_re * p2 - sgn * f_im * p1))
        pr4.append(pw_ref[d, 0])
        pi4.append(pw_ref[d, 1])
        cx = jnp.concatenate([mxu_operand(c_ref[d, 0])] * t_len, axis=0)
        cy = jnp.concatenate([mxu_operand(c_ref[d, 1])] * t_len, axis=0)

        def ca(order, d=d, cx=cx, cy=cy):
            pr = jnp.concatenate(
                [jnp.broadcast_to(pr4[d][t:t + 1, :half], (S5_GROUP, half)) for t in order], axis=0)
            pi = jnp.concatenate(
                [jnp.broadcast_to(pi4[d][t:t + 1, :half], (S5_GROUP, half)) for t in order], axis=0)
            return cx * pr + cy * pi
        g2.append(ca)

    def strip(d, order):
        return lax.dot_general(a4[d][:, :half], g2[d](order), (((1,), (1,)), ((), ())),
                               precision=lax.Precision.HIGHEST, preferred_element_type=F32)

    row = lax.broadcasted_iota(jnp.int32, (S5_GROUP, S5_CW), 0)
    col = lax.broadcasted_iota(jnp.int32, (S5_GROUP, S5_CW), 1)
    k_fwd = strip(0, range(t_len)) + jnp.where(row == col, rows_ref[4:5, :], 0.0)
    k_bwd = strip(1, [t_len - 1 - k for k in range(t_len)])
    for j in range(t_len):
        lo, hi = j * S5_GROUP, (j + 1) * S5_GROUP
        m = _shift_lanes(k_fwd, j * S5_GROUP) + _shift_lanes(k_bwd, -(t_len - 1 - j) * S5_GROUP)
        w1_ref[lo:hi, 0:S5_CW] = m.astype(w1_ref.dtype)
        tf, tb = t_len - 1 - j, j
        s_f = a4[0] * pr4[0][tf:tf + 1, :] + bx4[0] * (sgn * pi4[0][tf:tf + 1, :])
        s_b = a4[1] * pr4[1][tb:tb + 1, :] + bx4[1] * (sgn * pi4[1][tb:tb + 1, :])
        w1_ref[lo:hi, S5_CW:2 * S5_CW] = s_f.astype(w1_ref.dtype)
        w1_ref[lo:hi, 2 * S5_CW:3 * S5_CW] = s_b.astype(w1_ref.dtype)
    w2_ref[0:half, :] = g2[0]([i + 1 for i in range(t_len)]).T.astype(w2_ref.dtype)
    w2_ref[half:2 * half, :] = g2[1]([t_len - i for i in range(t_len)]).T.astype(w2_ref.dtype)
    zero = jnp.zeros((1, half), F32)
    coef_ref[...] = jnp.concatenate(
        [pr4[0][t_len:t_len + 1, :half], (sgn * pi4[0][t_len:t_len + 1, :])[:, :half],
         pr4[1][t_len:t_len + 1, :half], (sgn * pi4[1][t_len:t_len + 1, :])[:, :half],
         zero, zero, zero, zero], axis=0)


def _s5_kernel(xc_ref, xl_ref, w1_ref, w2_ref, coef_ref, yc_ref, yl_ref,
               zic_ref, zil_ref, zsc_ref, zsl_ref, hc_ref, hl_ref, *, bsz, ctx_chunks, lat_chunks):
    sw = 2 * S5_STATE
    streams = ((xc_ref, zic_ref, zsc_ref, hc_ref, yc_ref, ctx_chunks),
               (xl_ref, zil_ref, zsl_ref, hl_ref, yl_ref, lat_chunks))
    w1 = w1_ref[...]
    for x_ref, zi_ref, zs_ref, _, _, n in streams:
        pitch = n + SUBLANES
        for b in range(bsz):
            z = jnp.dot(x_ref[b * n:(b + 1) * n, :], w1, preferred_element_type=F32)
            zi_ref[b * n:(b + 1) * n, :] = z[:, 0:S5_CW]
            for k in range(4):
                zs_ref[k, b * pitch:b * pitch + n, :] = z[:, S5_CW + k * sw:S5_CW + (k + 1) * sw]

    shape = (bsz, sw)
    a1f = jnp.broadcast_to(coef_ref[0:1, :], shape)
    a2f = jnp.broadcast_to(coef_ref[1:2, :], shape)
    a1b = jnp.broadcast_to(coef_ref[2:3, :], shape)
    a2b = jnp.broadcast_to(coef_ref[3:4, :], shape)

    def locate(c):
        if c < ctx_chunks:
            return zsc_ref, hc_ref, c, ctx_chunks + SUBLANES
        return zsl_ref, hl_ref, c - ctx_chunks, lat_chunks + SUBLANES

    n_chunks = ctx_chunks + lat_chunks
    hf = hfs = hb = hbs = jnp.zeros(shape, F32)
    for t in range(n_chunks):
        zs_ref, h_ref, c, pitch = locate(t)
        rows = pl.ds(c, bsz, stride=pitch)
        h_ref[0, rows, :] = hf
        lf = zs_ref[0, rows, :]
        lfs = zs_ref[1, rows, :]
        hf, hfs = a1f * hf + a2f * hfs + lf, a1f * hfs - a2f * hf + lfs
        cb = ctx_chunks - 1 - t if t < ctx_chunks else n_chunks + ctx_chunks - 1 - t
        zs_ref, h_ref, c, pitch = locate(cb)
        rows = pl.ds(c, bsz, stride=pitch)
        h_ref[1, rows, :] = hb
        lb = zs_ref[2, rows, :]
        lbs = zs_ref[3, rows, :]
        hb, hbs = a1b * hb + a2b * hbs + lb, a1b * hbs - a2b * hb + lbs

    w2 = w2_ref[...]
    for _, zi_ref, _, h_ref, y_ref, n in streams:
        pitch = n + SUBLANES
        for b in range(bsz):
            hin = jnp.concatenate([h_ref[0, b * pitch:b * pitch + n, :], h_ref[1, b * pitch:b * pitch + n, :]],
                                  axis=1)
            y = zi_ref[b * n:(b + 1) * n, :] + jnp.dot(hin.astype(BF16), w2, preferred_element_type=F32)
            y_ref[b * n:(b + 1) * n, :] = y.astype(y_ref.dtype)


def _s5_scan(x_ctx, x_lat, w1, w2, coef, bsz):
    ctx_chunks = x_ctx.shape[1] // bsz
    lat_chunks = x_lat.shape[1] // bsz
    sw = 2 * S5_STATE
    grp = lambda rows, cols: pl.BlockSpec((None, rows, cols), lambda g: (g, 0, 0))
    slab = lambda k, n: pltpu.VMEM((k, bsz * (n + SUBLANES), sw), F32)
    return pl.pallas_call(
        functools.partial(_s5_kernel, bsz=bsz, ctx_chunks=ctx_chunks, lat_chunks=lat_chunks),
        grid=(S5_GROUPS,),
        in_specs=[
            grp(bsz * ctx_chunks, S5_CW), grp(bsz * lat_chunks, S5_CW),
            grp(S5_CW, 3 * S5_CW), grp(4 * S5_STATE, S5_CW), grp(SUBLANES, sw),
        ],
        out_specs=[grp(bsz * ctx_chunks, S5_CW), grp(bsz * lat_chunks, S5_CW)],
        out_shape=[jax.ShapeDtypeStruct(x_ctx.shape, F32), jax.ShapeDtypeStruct(x_lat.shape, F32)],
        scratch_shapes=[
            pltpu.VMEM((bsz * ctx_chunks, S5_CW), F32), pltpu.VMEM((bsz * lat_chunks, S5_CW), F32),
            slab(4, ctx_chunks), slab(4, lat_chunks), slab(2, ctx_chunks), slab(2, lat_chunks),
        ],
        compiler_params=pltpu.CompilerParams(
            dimension_semantics=("parallel",), vmem_limit_bytes=VMEM_LIMIT),
        name="s5_scan",
    )(x_ctx, x_lat, w1, w2, coef)


def _gelu_tanh(x):
    return 0.5 * x * (1.0 + jnp.tanh(math.sqrt(2.0 / math.pi) * (x + 0.044715 * (x * x * x))))


def _silu(x):
    return x * jax.nn.sigmoid(x)


def _merge_kernel(h_ref, mod_ref, npre_ref, wg_ref, oa_ref, yb_ref, oc_ref,
                  wglu_ref, bglu_ref, woa_ref, wob_ref, woc_ref, wout_ref, npost_ref, o_ref, stage_ref):
    hn = _modulated_norm(h_ref[...], mod_ref, npre_ref)
    yb = _gelu_tanh(_groups_to_tokens(yb_ref, stage_ref))
    glu = jnp.dot(yb.astype(BF16), wglu_ref[...], preferred_element_type=F32) + bglu_ref[...]
    ob = yb * jax.nn.sigmoid(glu)

    def branch(i, o, w_ref):
        g = jnp.dot(hn, wg_ref[:, i * A_WIDTH:(i + 1) * A_WIDTH], preferred_element_type=F32)
        t = jnp.dot((o * _silu(g)).astype(BF16), w_ref[...], preferred_element_type=F32)
        m_lo = N_BRANCH * A_WIDTH + i * D_MODEL
        gate = jax.nn.sigmoid(jnp.dot(hn, wg_ref[:, m_lo:m_lo + D_MODEL], preferred_element_type=F32))
        return gate * t

    y = branch(0, oa_ref[...], woa_ref) + branch(1, ob, wob_ref) + branch(2, oc_ref[...], woc_ref)
    y = jnp.dot(y.astype(BF16), wout_ref[...], preferred_element_type=F32)
    ms = jnp.mean(y * y, axis=-1, keepdims=True)
    yn = y * lax.rsqrt(ms + EPS) * npost_ref[...]
    gate = mod_ref[:, 2 * D_MODEL:3 * D_MODEL]
    o_ref[...] = h_ref[...] + gate * yn


def _merge(h, mod, mod_row, norm_pre, w_gate, o_a, y_b, o_c, wts):
    bsz, rows, _ = h.shape
    tile = min(ROW_TILE, rows)
    tiles = rows // tile
    if mod_row is None:
        mod_map = lambda b, i: (b, 0, 0)
    else:
        mod_map = lambda b, i: (mod_row, 0, 0)
    tok = lambda w: pl.BlockSpec((None, tile,w), lambda b, i: (b, i, 0))
    in_specs = [
        tok(D_MODEL),
        pl.BlockSpec((None, 1, 3 * D_MODEL), mod_map),
        _const_spec((1, D_MODEL)), _const_spec((D_MODEL, GATE_WIDTH)),
        tok(A_WIDTH),
        pl.BlockSpec((S5_GROUPS, tile // S5_CHUNK, S5_CW), lambda b, i: (0, b * tiles + i, 0)),
        tok(C_WIDTH),
        _const_spec((B_WIDTH, B_WIDTH)), _const_spec((1, B_WIDTH)),
        _const_spec((A_WIDTH, D_MODEL)), _const_spec((B_WIDTH, D_MODEL)), _const_spec((C_WIDTH, D_MODEL)),
        _const_spec((D_MODEL, D_MODEL)), _const_spec((1, D_MODEL)),
    ]
    return pl.pallas_call(
        _merge_kernel,
        grid=(bsz, rows // tile),
        in_specs=in_specs,
        out_specs=tok(D_MODEL),
        out_shape=jax.ShapeDtypeStruct((bsz, rows, D_MODEL), F32),
        scratch_shapes=[pltpu.VMEM((B_WIDTH // LANES, (tile // S5_CHUNK) * STAGE_PITCH, LANES), F32)],
        compiler_params=pltpu.CompilerParams(
            dimension_semantics=("parallel", "parallel"), vmem_limit_bytes=VMEM_LIMIT),
        name="merge_out",
    )(h, mod, norm_pre.reshape(1, D_MODEL), w_gate, o_a, y_b, o_c, *wts)


def _rope_tables(n_tokens):
    rows = n_tokens // GRID_W
    pos_r = jnp.repeat(jnp.arange(rows, dtype=F32), GRID_W)
    pos_c = jnp.tile(jnp.arange(GRID_W, dtype=F32), rows)
    inv = ROPE_BASE ** (-jnp.arange(AX_FREQS, dtype=F32) / AX_FREQS)
    ang_r = pos_r[:, None] * inv[None]
    ang_c = pos_c[:, None] * inv[None]
    ang = jnp.concatenate([ang_r, ang_r, ang_c, ang_c], axis=-1)
    cos, sin = jnp.cos(ang), jnp.sin(ang)
    first = (jnp.arange(HEAD_DIM) % (2 * AX_FREQS)) < AX_FREQS
    sa = jnp.where(first[None], -sin, 0.0)
    sb = jnp.where(first[None], 0.0, sin)
    tile2 = lambda t: jnp.concatenate([t, t], axis=-1)
    return tile2(cos), tile2(sa), tile2(sb)


def _reorder_a_heads(w, axis):
    shape = w.shape
    split = shape[:axis] + (A_HEADS, HEAD_DIM) + shape[axis + 1:]
    return jnp.take(w.reshape(split), jnp.array(A_HEAD_ORDER), axis=axis).reshape(shape)


def kernel(x, c, ctx, c_ctx, w_mod, b_mod, norm_pre, norm_post, w_in, swa_sink, s5_a_re, s5_a_im, s5_log_dt, s5_b_re, s5_b_im, s5_c_re, s5_c_im, s5_d, s5_w_glu, s5_b_glu, diff_lq1, diff_lk1, diff_lq2, diff_lk2, diff_subln, w_o_a, w_o_b, w_o_c, w_out):
    bsz, n_lat, _ = x.shape
    n_ctx = ctx.shape[1]
    ctx_row = bsz
    cc = jnp.zeros((16, D_MODEL), F32).at[:bsz].set(c).at[ctx_row].set(c_ctx)
    mod_all = _modulation(cc, w_mod, b_mod)
    rope_tabs = _rope_tables(n_lat)

    h_lat, h_ctx = x, ctx
    for l in range(DEPTH):
        last = l == DEPTH - 1
        lam_init = 0.8 - 0.6 * math.exp(-0.3 * l)
        mod = mod_all[l].reshape(16, 1, 3 * D_MODEL)
        w_l = w_in[l]
        cols = lambda name: w_l[:, SEGS[name][0]:SEGS[name][0] + SEGS[name][1]]
        aq0, aqw = SEGS["aq"]
        w_in_l = jnp.concatenate([_reorder_a_heads(cols("aq"), 1), w_l[:, aq0 + aqw:]], axis=1).astype(BF16)
        w_gate = jnp.concatenate(
            [_reorder_a_heads(cols("ag"), 1), cols("bg"), cols("cg"), cols("mg")], axis=1).astype(BF16)
        zl = _in_projection(h_lat, mod, None, norm_pre[l], w_in_l, PROJ_NAMES, rope_tabs)
        ctx_names = ("ak", "av", "bu", "ck", "cv") if last else PROJ_NAMES
        zc = _in_projection(h_ctx, mod, ctx_row, norm_pre[l], w_in_l, ctx_names, None)

        o_a_l = _swa(swa_sink[l], zl["aq"], zc["ak"], zc["av"], zl["ak"], zl["av"])

        w1, w2, coef = _s5_operators(s5_a_re[l], s5_a_im[l], s5_log_dt[l], s5_b_re[l], s5_b_im[l],
                                     s5_c_re[l], s5_c_im[l], s5_d[l])
        y_b_c, y_b_l = _s5_scan(zc["bu"], zl["bu"], w1, w2, coef, bsz)

        lam_params = jnp.stack([diff_lq1[l], diff_lk1[l], diff_lq2[l], diff_lk2[l]]).astype(F32)
        o_c_l = _diff_attention(lam_params, diff_subln[l], lam_init, zl["cq"], zc["ck"], zc["cv"],
                                zl["ck"], zl["cv"])

        wts = (s5_w_glu[l].astype(BF16), s5_b_glu[l].reshape(1, B_WIDTH),
               _reorder_a_heads(w_o_a[l], 0).astype(BF16), w_o_b[l].astype(BF16), w_o_c[l].astype(BF16),
               w_out[l].astype(BF16), norm_post[l].reshape(1, D_MODEL))
        h_lat_new = _merge(h_lat, mod, None, norm_pre[l], w_gate, o_a_l, y_b_l, o_c_l, wts)
        if not last:
            o_a_c = _swa(swa_sink[l], zc["aq"], zc["ak"], zc["av"])
            o_c_c = _diff_attention(lam_params, diff_subln[l], lam_init, zc["cq"], zc["ck"], zc["cv"])
            h_ctx = _merge(h_ctx, mod, ctx_row, norm_pre[l], w_gate, o_a_c, y_b_c, o_c_c, wts)
        h_lat = h_lat_new
    return h_lat
```

```python
import functools
import math

import jax
import jax.numpy as jnp
from jax import lax
from jax.experimental import pallas as pl
from jax.experimental.pallas import tpu as pltpu

F32 = jnp.float32
BF16 = jnp.bfloat16

D_MODEL = 1024
DEPTH = 2
GRID_W = 64
HEAD_DIM = 64
WINDOW = 128
ROPE_BASE = 10000.0
AX_FREQS = HEAD_DIM // 4
NEG_INF = -1e30
A_HEADS = 8
A_KV_HEADS = 2
A_GROUP = A_HEADS // A_KV_HEADS
A_WIDTH = A_HEADS * HEAD_DIM
A_KV_WIDTH = A_KV_HEADS * HEAD_DIM
B_WIDTH = 512
S5_GROUP = 16
S5_GROUPS = B_WIDTH // S5_GROUP
S5_STATE = 64
C_HEADS = 4
C_WIDTH = C_HEADS * 2 * HEAD_DIM
N_BRANCH = 3
GATE_WIDTH = A_WIDTH + B_WIDTH + C_WIDTH + N_BRANCH * D_MODEL
EPS = 1e-6

LANES = 128
SUBLANES = 8

ROW_TILE = 512
Q_BLOCK = 128
DIFF_KEY_CHUNK = 512
DIFF_LOOKAHEAD = 14
SWA_Q_ROWS = 256
DIFF_Q_ROWS = 256
A_HEAD_ORDER = tuple(h for v in range(4) for h in (v, 4 + v))
S5_CHUNK = 16
S5_CW = S5_CHUNK * S5_GROUP
S5_DOT_ROWS = 256
STAGE_PITCH = S5_CHUNK + SUBLANES
VMEM_LIMIT = 56 * 1024 * 1024

_SEG_NAMES = ("aq", "ak", "av", "ag", "bu", "bg", "cq", "ck", "cv", "cg", "mg")
_SEG_WIDTHS = (A_WIDTH, A_KV_WIDTH, A_KV_WIDTH, A_WIDTH, B_WIDTH, B_WIDTH,
               C_WIDTH, C_WIDTH, C_WIDTH, C_WIDTH, 3 * D_MODEL)
SEGS = {}
_off = 0
for _n, _w in zip(_SEG_NAMES, _SEG_WIDTHS):
    SEGS[_n] = (_off, _w)
    _off += _w
IN_WIDTH = _off
PROJ_NAMES = ("aq", "ak", "av", "bu", "cq", "ck", "cv")
Q_SCALE = HEAD_DIM ** -0.5 * math.log2(math.e)
ROPE_SEGS = {"aq": Q_SCALE, "ak": 1.0, "cq": Q_SCALE, "ck": 1.0}


def _const_spec(shape):
    nd = len(shape)
    return pl.BlockSpec(shape, lambda *_: (0,) * nd, pipeline_mode=pl.Buffered(1))


def _mod_kernel(c_ref, w_ref, b_ref, o_ref):
    c = c_ref[...]
    a = c * jax.nn.sigmoid(c)
    o_ref[...] = jnp.dot(a, w_ref[...], preferred_element_type=F32) + b_ref[...]


def _modulation(cc, w_mod, b_mod):
    n_tiles = 3
    return pl.pallas_call(
        _mod_kernel,
        grid=(DEPTH, n_tiles),
        in_specs=[
            pl.BlockSpec((16, D_MODEL), lambda l, j: (0, 0)),
            pl.BlockSpec((None, D_MODEL, D_MODEL), lambda l, j: (l, 0, j)),
            pl.BlockSpec((None, 1, D_MODEL), lambda l, j: (l, 0, j)),
        ],
        out_specs=pl.BlockSpec((None, 16, D_MODEL), lambda l, j: (l, 0, j)),
        out_shape=jax.ShapeDtypeStruct((DEPTH, 16, 3 * D_MODEL), F32),
        name="modulation",
    )(cc, w_mod, b_mod.reshape(DEPTH, 1, 3 * D_MODEL))


def _block_transpose8(xs):
    lane_blk = lax.broadcasted_iota(jnp.int32, xs[0].shape, 1) // S5_GROUP
    for s in (4, 2, 1):
        keep = (lane_blk & s) == 0
        out = list(xs)
        for p in range(8):
            if p & s:
                continue
            a, b = xs[p], xs[p + s]
            out[p] = jnp.where(keep, a, pltpu.roll(b, S5_GROUP * s, 1))
            out[p + s] = jnp.where(keep, pltpu.roll(a, LANES - S5_GROUP * s, 1), b)
        xs = out
    return xs


def _tokens_to_groups(z, stage_ref, o_ref):
    n_chunks = z.shape[0] // S5_CHUNK
    for v in range(B_WIDTH // LANES):
        for c in range(n_chunks):
            stage_ref[v, c * STAGE_PITCH:c * STAGE_PITCH + S5_CHUNK, :] = \
                z[c * S5_CHUNK:(c + 1) * S5_CHUNK, v * LANES:(v + 1) * LANES]
    for w in range(S5_CW // LANES):
        for v in range(B_WIDTH // LANES):
            src = [stage_ref[v, pl.ds(8 * w + jj, n_chunks, stride=STAGE_PITCH), :] for jj in range(8)]
            dst = _block_transpose8(src)
            for gg in range(8):
                o_ref[8 * v + gg, :, w * LANES:(w + 1) * LANES] = dst[gg].astype(o_ref.dtype)


def _groups_to_tokens(y_ref, stage_ref):
    n_chunks = y_ref.shape[1]
    for w in range(S5_CW // LANES):
        for v in range(B_WIDTH // LANES):
            src = [y_ref[8 * v + gg, :, w * LANES:(w + 1) * LANES].astype(F32) for gg in range(8)]
            dst = _block_transpose8(src)
            for jj in range(8):
                stage_ref[v, pl.ds(8 * w + jj, n_chunks, stride=STAGE_PITCH), :] = dst[jj]
    rows = []
    for c in range(n_chunks):
        rows.append(jnp.concatenate(
            [stage_ref[v, c * STAGE_PITCH:c * STAGE_PITCH + S5_CHUNK, :] for v in range(B_WIDTH // LANES)], axis=1))
    return jnp.concatenate(rows, axis=0)


def _modulated_norm(x, mod_ref, nw_ref):
    ms = jnp.mean(x * x, axis=-1, keepdims=True)
    xn = x * lax.rsqrt(ms + EPS) * nw_ref[...]
    shift = mod_ref[:, 0:D_MODEL]
    scale = mod_ref[:, D_MODEL:2 * D_MODEL]
    return (xn * (1.0 + scale) + shift).astype(BF16)


def _inproj_kernel(*refs, names, use_rope):
    h_ref, mod_ref, nw_ref, w_ref = refs[:4]
    stage_ref = refs[-1]
    refs = refs[:-1]
    if use_rope:
        cos_ref, sa_ref, sb_ref = refs[4:7]
        outs = refs[7:]
    else:
        outs = refs[4:]
    hn = _modulated_norm(h_ref[...], mod_ref, nw_ref)
    kv_off = SEGS["ak"][0]
    assert SEGS["av"][0] == kv_off + A_KV_WIDTH
    kv = jnp.dot(hn, w_ref[:, kv_off:kv_off + 2 * A_KV_WIDTH], preferred_element_type=F32)

    def project(name):
        off, width = SEGS[name]
        if name in ("ak", "av"):
            return kv[:, off - kv_off:off - kv_off + width]
        return jnp.dot(hn, w_ref[:, off:off + width], preferred_element_type=F32)

    for name, o_ref in zip(names, outs):
        z = project(name)
        mul = ROPE_SEGS.get(name, 1.0)
        if use_rope and name in ROPE_SEGS:
            cos = cos_ref[...]
            sa = sa_ref[...]
            sb = sb_ref[...]
            for c in range(z.shape[1] // LANES):
                t = z[:, c * LANES:(c + 1) * LANES]
                r = t * cos + pltpu.roll(t, LANES - AX_FREQS, 1) * sa + pltpu.roll(t, AX_FREQS, 1) * sb
                if mul != 1.0:
                    r = r * mul
                o_ref[:, c * LANES:(c + 1) * LANES] = r.astype(o_ref.dtype)
        elif name == "bu":
            _tokens_to_groups(z, stage_ref, o_ref)
        elif name == "av":
            for t in range(z.shape[0] // Q_BLOCK):
                o_ref[t] = z[t * Q_BLOCK:(t + 1) * Q_BLOCK, :].T.astype(o_ref.dtype)
        elif name == "cv":
            o_ref[...] = z.T.astype(o_ref.dtype)
        else:
            if mul != 1.0:
                z = z * mul
            o_ref[...] = z.astype(o_ref.dtype)


def _in_projection(h, mod, mod_row, norm_w, w_bf16, names, rope_tabs):
    bsz, rows, _ = h.shape
    tile = min(ROW_TILE, rows)
    use_rope = rope_tabs is not None
    if mod_row is None:
        mod_map = lambda b, i: (b, 0, 0)
    else:
        mod_map = lambda b, i: (mod_row, 0, 0)
    in_specs = [
        pl.BlockSpec((None, tile,D_MODEL), lambda b, i: (b, i, 0)),
        pl.BlockSpec((None, 1, 3 * D_MODEL), mod_map),
        _const_spec((1, D_MODEL)),
        _const_spec((D_MODEL, IN_WIDTH)),
    ]
    args = [h, mod, norm_w.reshape(1, D_MODEL), w_bf16]
    if use_rope:
        for t in rope_tabs:
            in_specs.append(pl.BlockSpec((tile, LANES), lambda b, i: (i, 0)))
            args.append(t)
    tiles = rows // tile
    chunk_rows = tile // S5_CHUNK
    out_specs, out_shape = [], []
    for n in names:
        if n == "bu":
            out_specs.append(pl.BlockSpec((S5_GROUPS, chunk_rows, S5_CW), lambda b, i: (0, b * tiles + i, 0)))
            out_shape.append(jax.ShapeDtypeStruct((S5_GROUPS, bsz * rows // S5_CHUNK, S5_CW), BF16))
        elif n == "av":
            kb = tile // Q_BLOCK
            out_specs.append(pl.BlockSpec((None, kb, A_KV_WIDTH, Q_BLOCK), lambda b, i: (b, i, 0, 0)))
            out_shape.append(jax.ShapeDtypeStruct((bsz, rows // Q_BLOCK, A_KV_WIDTH, Q_BLOCK), BF16))
        elif n == "cv":
            out_specs.append(pl.BlockSpec((None, C_WIDTH, tile), lambda b, i: (b, 0, i)))
            out_shape.append(jax.ShapeDtypeStruct((bsz, C_WIDTH, rows), BF16))
        else:
            out_specs.append(pl.BlockSpec((None, tile,SEGS[n][1]), lambda b, i: (b, i, 0)))
            out_shape.append(jax.ShapeDtypeStruct((bsz, rows, SEGS[n][1]), BF16))
    outs = pl.pallas_call(
        functools.partial(_inproj_kernel, names=tuple(names), use_rope=use_rope),
        grid=(bsz, tiles),
        in_specs=in_specs,
        out_specs=out_specs,
        out_shape=out_shape,
        scratch_shapes=[pltpu.VMEM((B_WIDTH // LANES, (tile // S5_CHUNK) * STAGE_PITCH, LANES), F32)],
        compiler_params=pltpu.CompilerParams(
            dimension_semantics=("parallel", "parallel"), vmem_limit_bytes=VMEM_LIMIT),
        name="in_projection_rope" if use_rope else "in_projection",
    )(*args)
    return dict(zip(names, outs))


def _dot_nt(a, b):
    return lax.dot_general(a, b, (((1,), (1,)), ((), ())), preferred_element_type=F32)


def _half_masked(q, lower):
    lane = lax.broadcasted_iota(jnp.int32, q.shape, 1)
    keep = lane < HEAD_DIM if lower else lane >= HEAD_DIM
    return jnp.where(keep, q.astype(F32), 0.0).astype(q.dtype)


def _swa_kernel(*refs, has_local, lat_blocks):
    if has_local:
        sink_ref, q_ref, kc_ref, vtc_ref, kl_ref, vtl_ref, o_ref = refs
    else:
        sink_ref, q_ref, kc_ref, vtc_ref, o_ref = refs
    n_ctx = kc_ref.shape[0]
    n_items = q_ref.shape[0] // Q_BLOCK
    sink = jnp.concatenate(
        [jnp.full((1, Q_BLOCK), sink_ref[kh * A_GROUP + v] * math.log2(math.e), F32)
         for kh in range(A_KV_HEADS) for v in range(A_GROUP)], axis=1)

    def scores(t):
        qs = slice(t * Q_BLOCK, (t + 1) * Q_BLOCK)
        qblk = jnp.concatenate(
            [_half_masked(q_ref[qs, v * LANES:(v + 1) * LANES], kh == 0)
             for kh in range(A_KV_HEADS) for v in range(A_GROUP)], axis=0)
        keys = [kc_ref[...]]
        vts = [vtc_ref[c] for c in range(n_ctx // Q_BLOCK)]
        if has_local:
            n = pl.program_id(1) * n_items + t
            sb = jnp.clip(n - 1, 0, lat_blocks - 3)
            keys.append(kl_ref[pl.ds(pl.multiple_of(sb * Q_BLOCK, Q_BLOCK), 3 * Q_BLOCK), :])
            vts += [vtl_ref[sb + c] for c in range(3)]
        s_t = _dot_nt(jnp.concatenate(keys, axis=0), qblk)
        if has_local:
            r = lax.broadcasted_iota(jnp.int32, (3 * Q_BLOCK, Q_BLOCK), 0)
            i = lax.broadcasted_iota(jnp.int32, (3 * Q_BLOCK, Q_BLOCK), 1)
            in_band = jnp.abs((n - sb) * Q_BLOCK + i - r) <= WINDOW
            bias = jnp.where(in_band, 0.0, NEG_INF)
            s_t = jnp.concatenate([s_t[:n_ctx], s_t[n_ctx:] + jnp.concatenate([bias] * A_HEADS, axis=1)],
                                  axis=0)
        return s_t, jnp.concatenate(vts, axis=1)

    lane = lax.broadcasted_iota(jnp.int32, (Q_BLOCK, LANES), 1)
    half = A_GROUP * Q_BLOCK
    nxt = scores(0)
    for t in range(n_items):
        s_t, vt = nxt
        if t + 1 < n_items:
            nxt = scores(t + 1)
        m = jnp.maximum(jnp.max(s_t, axis=0, keepdims=True), sink)
        e_t = jnp.exp2(s_t - m)
        den = jnp.sum(e_t, axis=0, keepdims=True) + jnp.exp2(sink - m)
        p_t = (e_t * (1.0 / den)).astype(BF16)
        o_t = jnp.dot(vt, p_t, preferred_element_type=F32)
        qs = slice(t * Q_BLOCK, (t + 1) * Q_BLOCK)
        for v in range(A_GROUP):
            lower = o_t[:, v * Q_BLOCK:(v + 1) * Q_BLOCK].T
            upper = o_t[:, half + v * Q_BLOCK:half + (v + 1) * Q_BLOCK].T
            o_ref[qs, v * LANES:(v + 1) * LANES] = jnp.where(lane < HEAD_DIM, lower, upper).astype(o_ref.dtype)


def _swa(sink, q, k_ctx, vt_ctx, k_lat=None, vt_lat=None):
    bsz, rows, _ = q.shape
    n_ctx = k_ctx.shape[1]
    has_local = k_lat is not None
    in_specs = [
        pl.BlockSpec(memory_space=pltpu.SMEM),
        pl.BlockSpec((None, SWA_Q_ROWS, A_WIDTH), lambda b, n: (b, n, 0)),
        pl.BlockSpec((None, n_ctx, A_KV_WIDTH), lambda b, n: (b, 0, 0)),
        pl.BlockSpec((None, n_ctx // Q_BLOCK, A_KV_WIDTH, Q_BLOCK), lambda b, n: (b, 0, 0, 0)),
    ]
    args = [sink, q, k_ctx, vt_ctx]
    lat_blocks = 0
    if has_local:
        n_lat = k_lat.shape[1]
        lat_blocks = n_lat // Q_BLOCK
        in_specs += [pl.BlockSpec((None, n_lat, A_KV_WIDTH), lambda b, n: (b, 0, 0)),
                     pl.BlockSpec((None, lat_blocks, A_KV_WIDTH, Q_BLOCK), lambda b, n: (b, 0, 0, 0))]
        args += [k_lat, vt_lat]
    return pl.pallas_call(
        functools.partial(_swa_kernel, has_local=has_local, lat_blocks=lat_blocks),
        grid=(bsz, rows // SWA_Q_ROWS),
        in_specs=in_specs,
        out_specs=pl.BlockSpec((None, SWA_Q_ROWS, A_WIDTH), lambda b, n: (b, n, 0)),
        out_shape=jax.ShapeDtypeStruct((bsz, rows, A_WIDTH), F32),
        compiler_params=pltpu.CompilerParams(
            dimension_semantics=("parallel", "arbitrary"), vmem_limit_bytes=VMEM_LIMIT),
        name="swa_latent" if has_local else "swa_context",
    )(*args)


def _diff_kernel(*refs, has_lat, lam_init):
    if has_lat:
        lam_ref, sub_ref, q_ref, kc_ref, vtc_ref, kl_ref, vtl_ref, o_ref = refs
    else:
        lam_ref, sub_ref, q_ref, kc_ref, vtc_ref, o_ref = refs
    lp = lam_ref[...]
    lam = (jnp.exp(jnp.sum(lp[0:1] * lp[1:2], axis=1, keepdims=True))
           - jnp.exp(jnp.sum(lp[2:3] * lp[3:4], axis=1, keepdims=True)) + lam_init)
    sub_w = sub_ref[...] * (1.0 - lam_init)
    width = 2 * HEAD_DIM
    chunks = [(kc_ref, vtc_ref, 0, kc_ref.shape[0])]
    if has_lat:
        chunks += [(kl_ref, vtl_ref, c, DIFF_KEY_CHUNK) for c in range(0, kl_ref.shape[0], DIFF_KEY_CHUNK)]
    heads = [slice(h * width, (h + 1) * width) for h in range(C_HEADS)]
    items = [(hs, slice(qb, qb + Q_BLOCK)) for qb in range(0, q_ref.shape[0], Q_BLOCK) for hs in heads]
    units = [(it, ci) for it in range(len(items)) for ci in range(len(chunks))]
    qblks = {}

    def scores(unit):
        it, ci = unit
        hs, qs = items[it]
        if it not in qblks:
            qblks[it] = jnp.concatenate(
                [_half_masked(q_ref[qs, hs], True), _half_masked(q_ref[qs, hs], False)], axis=0)
        k_ref, _, start, size = chunks[ci]
        return _dot_nt(k_ref[start:start + size, hs], qblks[it])

    issued = 0
    queue = []

    def next_scores():
        nonlocal issued
        while issued < len(units) and len(queue) < DIFF_LOOKAHEAD:
            queue.append(scores(units[issued]))
            issued += 1
        s = queue.pop(0)
        if issued < len(units):
            queue.append(scores(units[issued]))
            issued += 1
        return s

    for hs, qs in items:
        s_all = [next_scores() for _ in chunks]
        m = functools.reduce(jnp.maximum, [jnp.max(s, axis=0, keepdims=True) for s in s_all])
        e_all = [jnp.exp2(s - m) for s in s_all]
        den = functools.reduce(jnp.add, [jnp.sum(e, axis=0, keepdims=True) for e in e_all])
        inv = 1.0 / den
        c1 = inv[:, :Q_BLOCK]
        c2 = lam * inv[:, Q_BLOCK:]
        o_t = None
        for e, (_, vt_ref, start, size) in zip(e_all, chunks):
            a_t = (e[:, :Q_BLOCK] * c1 - e[:, Q_BLOCK:] * c2).astype(BF16)
            pv = jnp.dot(vt_ref[hs, start:start + size], a_t, preferred_element_type=F32)
            o_t = pv if o_t is None else o_t + pv
        o = o_t.T
        ms = jnp.mean(o * o, axis=1, keepdims=True)
        o_ref[qs, hs] = (o * lax.rsqrt(ms + EPS) * sub_w).astype(o_ref.dtype)


def _diff_attention(lam_params, subln, lam_init, q, k_ctx, vt_ctx, k_lat=None, vt_lat=None):
    bsz, rows, _ = q.shape
    n_ctx = k_ctx.shape[1]
    has_lat = k_lat is not None
    in_specs = [
        _const_spec((4, HEAD_DIM)),
        _const_spec((1, 2 * HEAD_DIM)),
        pl.BlockSpec((None, DIFF_Q_ROWS, C_WIDTH), lambda b, n: (b, n, 0)),
        pl.BlockSpec((None, n_ctx, C_WIDTH), lambda b, n: (b, 0, 0)),
        pl.BlockSpec((None, C_WIDTH, n_ctx), lambda b, n: (b, 0, 0)),
    ]
    args = [lam_params, subln.reshape(1, 2 * HEAD_DIM), q, k_ctx, vt_ctx]
    if has_lat:
        n_lat = k_lat.shape[1]
        in_specs += [pl.BlockSpec((None, n_lat, C_WIDTH), lambda b, n: (b, 0, 0)),
                     pl.BlockSpec((None, C_WIDTH, n_lat), lambda b, n: (b, 0, 0))]
        args += [k_lat, vt_lat]
    return pl.pallas_call(
        functools.partial(_diff_kernel, has_lat=has_lat, lam_init=lam_init),
        grid=(bsz, rows // DIFF_Q_ROWS),
        in_specs=in_specs,
        out_specs=pl.BlockSpec((None, DIFF_Q_ROWS, C_WIDTH), lambda b, n: (b, n, 0)),
        out_shape=jax.ShapeDtypeStruct((bsz, rows, C_WIDTH), F32),
        compiler_params=pltpu.CompilerParams(
            dimension_semantics=("parallel", "arbitrary"), vmem_limit_bytes=VMEM_LIMIT),
        name="diff_latent" if has_lat else "diff_context",
    )(*args)


def _s5_operators(a_re, a_im, log_dt, b_re, b_im, c_re, c_im, d_skip):
    f = lambda t: t.astype(F32)
    quad = lambda t: jnp.concatenate([t, t, t, t], axis=-1)
    a_re, a_im, b_re, b_im, c_re, c_im = map(f, (a_re, a_im, b_re, b_im, c_re, c_im))
    g, p, h = S5_GROUPS, S5_STATE, S5_GROUP
    dt = jnp.exp(f(log_dt))[..., None]
    mag = jnp.exp(a_re * dt)
    abar_re = mag * jnp.cos(a_im * dt)
    abar_im = mag * jnp.sin(a_im * dt)
    den = a_re * a_re + a_im * a_im
    nr = abar_re - 1.0
    ni = abar_im
    f_re = (nr * a_re + ni * a_im) / den
    f_im = (ni * a_re - nr * a_im) / den
    pw_re, pw_im = [jnp.ones_like(abar_re)], [jnp.zeros_like(abar_im)]
    for _ in range(S5_CHUNK):
        pre, pim = pw_re[-1], pw_im[-1]
        pw_re.append(pre * abar_re - pim * abar_im)
        pw_im.append(pre * abar_im + pim * abar_re)
    pad = [jnp.zeros_like(abar_re)] * (3 * SUBLANES - S5_CHUNK - 1)
    pw = jnp.stack([jnp.stack(pw_re + pad, axis=2), jnp.stack(pw_im + pad, axis=2)], axis=1)
    pw = jnp.transpose(quad(pw), (2, 0, 1, 3, 4))
    dsk = jnp.tile(f(d_skip).reshape(g, h), (1, S5_CW // h))
    sgn = jnp.broadcast_to(jnp.repeat(jnp.array([-1.0, 1.0, 1.0, -1.0], F32), p), (g, 4 * p))
    zero = jnp.zeros_like(dsk)
    rows = jnp.stack([quad(f_re[0]), quad(f_im[0]), quad(f_re[1]), quad(f_im[1]), dsk, sgn, zero, zero],
                     axis=1)
    bt_re = jnp.swapaxes(b_re, -1, -2)
    bt_im = jnp.swapaxes(b_im, -1, -2)
    bpk = jnp.stack([jnp.concatenate([bt_re, bt_im, bt_im, bt_re], axis=-1),
                     jnp.concatenate([bt_im, bt_re, bt_re, bt_im], axis=-1)], axis=1)
    cpk = jnp.stack([jnp.concatenate([c_re, -c_im], axis=-1),
                     jnp.concatenate([-c_im, -c_re], axis=-1)], axis=1)
    bpk = jnp.transpose(bpk, (2, 0, 1, 3, 4))
    cpk = jnp.transpose(cpk, (2, 0, 1, 3, 4))
    grp = lambda shape: pl.BlockSpec((None,) + shape, lambda i: (i,) + (0,) * len(shape))
    return pl.pallas_call(
        _s5_operator_kernel,
        grid=(g,),
        in_specs=[grp((8, 4 * p)), grp((2, 2, 3 * SUBLANES, 4 * p)), grp((2, 2, h, 4 * p)),
                  grp((2, 2, h, 2 * p))],
        out_specs=[grp((S5_CW, 3 * S5_CW)), grp((4 * p, S5_CW)), grp((SUBLANES, 2 * p))],
        out_shape=[jax.ShapeDtypeStruct((g, S5_CW, 3 * S5_CW), BF16),
                   jax.ShapeDtypeStruct((g, 4 * p, S5_CW), BF16),
                   jax.ShapeDtypeStruct((g, SUBLANES, 2 * p), F32)],
        compiler_params=pltpu.CompilerParams(dimension_semantics=("parallel",)),
        name="s5_operators",
    )(rows, pw, bpk, cpk)


def _shift_lanes(x, s):
    lo, hi = x[:, :LANES], x[:, LANES:]
    lane = lax.broadcasted_iota(jnp.int32, lo.shape, 1)
    zero = jnp.zeros_like(lo)
    rot = lambda t, r: pltpu.roll(t, r, 1) if r % LANES else t
    if s >= 0:
        if s < LANES:
            rl, rh = rot(lo, s), rot(hi, s)
            out = (jnp.where(lane >= s, rl, 0.0), jnp.where(lane >= s, rh, rl))
        else:
            rl = rot(lo, s - LANES)
            out = (zero, jnp.where(lane >= s - LANES, rl, 0.0))
    else:
        s = -s
        if s < LANES:
            rl, rh = rot(lo, LANES - s), rot(hi, LANES - s)
            out = (jnp.where(lane < LANES - s, rl, rh), jnp.where(lane < LANES - s, rh, 0.0))
        else:
            rh = rot(hi, 2 * LANES - s)
            out = (jnp.where(lane < 2 * LANES - s, rh, 0.0), zero)
    return jnp.concatenate(out, axis=1)


def _s5_operator_kernel(rows_ref, pw_ref, b_ref, c_ref, w1_ref, w2_ref, coef_ref):
    t_len = S5_CHUNK
    half = 2 * S5_STATE
    sgn = rows_ref[5:6, :]
    mxu_operand = lambda t: t.astype(BF16).astype(F32)
    a4, bx4, pr4, pi4, g2 = [], [], [], [], []
    for d in range(2):
        f_re = rows_ref[2 * d:2 * d + 1, :]
        f_im = rows_ref[2 * d + 1:2 * d + 2, :]
        p1 = b_ref[d, 0]
        p2 = b_ref[d, 1]
        a4.append(mxu_operand(f_re * p1 + sgn * f_im * p2))
        bx4.append(mxu_operand(f_re * p2 - sgn * f_im * p1))
        pr4.append(pw_ref[d, 0])
        pi4.append(pw_ref[d, 1])
        cx = jnp.concatenate([mxu_operand(c_ref[d, 0])] * t_len, axis=0)
        cy = jnp.concatenate([mxu_operand(c_ref[d, 1])] * t_len, axis=0)

        def ca(order, d=d, cx=cx, cy=cy):
            pr = jnp.concatenate(
                [jnp.broadcast_to(pr4[d][t:t + 1, :half], (S5_GROUP, half)) for t in order], axis=0)
            pi = jnp.concatenate(
                [jnp.broadcast_to(pi4[d][t:t + 1, :half], (S5_GROUP, half)) for t in order], axis=0)
            return cx * pr + cy * pi
        g2.append(ca)

    def strip(d, order):
        return lax.dot_general(a4[d][:, :half], g2[d](order), (((1,), (1,)), ((), ())),
                               precision=lax.Precision.HIGHEST, preferred_element_type=F32)

    row = lax.broadcasted_iota(jnp.int32, (S5_GROUP, S5_CW), 0)
    col = lax.broadcasted_iota(jnp.int32, (S5_GROUP, S5_CW), 1)
    k_fwd = strip(0, range(t_len)) + jnp.where(row == col, rows_ref[4:5, :], 0.0)
    k_bwd = strip(1, [t_len - 1 - k for k in range(t_len)])
    for j in range(t_len):
        lo, hi = j * S5_GROUP, (j + 1) * S5_GROUP
        m = _shift_lanes(k_fwd, j * S5_GROUP) + _shift_lanes(k_bwd, -(t_len - 1 - j) * S5_GROUP)
        w1_ref[lo:hi, 0:S5_CW] = m.astype(w1_ref.dtype)
        tf, tb = t_len - 1 - j, j
        s_f = a4[0] * pr4[0][tf:tf + 1, :] + bx4[0] * (sgn * pi4[0][tf:tf + 1, :])
        s_b = a4[1] * pr4[1][tb:tb + 1, :] + bx4[1] * (sgn * pi4[1][tb:tb + 1, :])
        w1_ref[lo:hi, S5_CW:2 * S5_CW] = s_f.astype(w1_ref.dtype)
        w1_ref[lo:hi, 2 * S5_CW:3 * S5_CW] = s_b.astype(w1_ref.dtype)
    w2_ref[0:half, :] = g2[0]([i + 1 for i in range(t_len)]).T.astype(w2_ref.dtype)
    w2_ref[half:2 * half, :] = g2[1]([t_len - i for i in range(t_len)]).T.astype(w2_ref.dtype)
    zero = jnp.zeros((1, half), F32)
    coef_ref[...] = jnp.concatenate(
        [pr4[0][t_len:t_len + 1, :half], (sgn * pi4[0][t_len:t_len + 1, :])[:, :half],
         pr4[1][t_len:t_len + 1, :half], (sgn * pi4[1][t_len:t_len + 1, :])[:, :half],
         zero, zero, zero, zero], axis=0)


def _s5_kernel(xc_ref, xl_ref, w1_ref, w2_ref, coef_ref, yc_ref, yl_ref,
               zic_ref, zil_ref, zsc_ref, zsl_ref, hc_ref, hl_ref, *, bsz, ctx_chunks, lat_chunks):
    sw = 2 * S5_STATE
    streams = ((xc_ref, zic_ref, zsc_ref, hc_ref, yc_ref, ctx_chunks),
               (xl_ref, zil_ref, zsl_ref, hl_ref, yl_ref, lat_chunks))
    w1 = w1_ref[...]
    for x_ref, zi_ref, zs_ref, _, _, n in streams:
        pitch = n + SUBLANES
        per_dot = min(bsz, max(1, S5_DOT_ROWS // n))
        for b0 in range(0, bsz, per_dot):
            zz = jnp.dot(x_ref[b0 * n:(b0 + per_dot) * n, :], w1, preferred_element_type=F32)
            for b in range(b0, b0 + per_dot):
                z = zz[(b - b0) * n:(b - b0 + 1) * n]
                zi_ref[b * n:(b + 1) * n, :] = z[:, 0:S5_CW]
                for k in range(4):
                    zs_ref[k, b * pitch:b * pitch + n, :] = z[:, S5_CW + k * sw:S5_CW + (k + 1) * sw]

    shape = (bsz, sw)
    a1f = jnp.broadcast_to(coef_ref[0:1, :], shape)
    a2f = jnp.broadcast_to(coef_ref[1:2, :], shape)
    a1b = jnp.broadcast_to(coef_ref[2:3, :], shape)
    a2b = jnp.broadcast_to(coef_ref[3:4, :], shape)

    def locate(c):
        if c < ctx_chunks:
            return zsc_ref, hc_ref, c, ctx_chunks + SUBLANES
        return zsl_ref, hl_ref, c - ctx_chunks, lat_chunks + SUBLANES

    n_chunks = ctx_chunks + lat_chunks
    hf = hfs = hb = hbs = jnp.zeros(shape, F32)
    for t in range(n_chunks):
        zs_ref, h_ref, c, pitch = locate(t)
        rows = pl.ds(c, bsz, stride=pitch)
        h_ref[0, rows, :] = hf
        lf = zs_ref[0, rows, :]
        lfs = zs_ref[1, rows, :]
        hf, hfs = a1f * hf + a2f * hfs + lf, a1f * hfs - a2f * hf + lfs
        cb = ctx_chunks - 1 - t if t < ctx_chunks else n_chunks + ctx_chunks - 1 - t
        zs_ref, h_ref, c, pitch = locate(cb)
        rows = pl.ds(c, bsz, stride=pitch)
        h_ref[1, rows, :] = hb
        lb = zs_ref[2, rows, :]
        lbs = zs_ref[3, rows, :]
        hb, hbs = a1b * hb + a2b * hbs + lb, a1b * hbs - a2b * hb + lbs

    w2 = w2_ref[...]
    for _, zi_ref, _, h_ref, y_ref, n in streams:
        pitch = n + SUBLANES
        per_dot = min(bsz, max(1, S5_DOT_ROWS // n))
        for b0 in range(0, bsz, per_dot):
            hin = jnp.concatenate(
                [jnp.concatenate([h_ref[0, b * pitch:b * pitch + n, :], h_ref[1, b * pitch:b * pitch + n, :]], axis=1)
                 for b in range(b0, b0 + per_dot)], axis=0)
            rows = slice(b0 * n, (b0 + per_dot) * n)
            y = zi_ref[rows, :] + jnp.dot(hin.astype(BF16), w2, preferred_element_type=F32)
            y_ref[rows, :] = y.astype(y_ref.dtype)


def _s5_scan(x_ctx, x_lat, w1, w2, coef, bsz):
    ctx_chunks = x_ctx.shape[1] // bsz
    lat_chunks = x_lat.shape[1] // bsz
    sw = 2 * S5_STATE
    grp = lambda rows, cols: pl.BlockSpec((None, rows, cols), lambda g: (g, 0, 0))
    slab = lambda k, n: pltpu.VMEM((k, bsz * (n + SUBLANES), sw), F32)
    return pl.pallas_call(
        functools.partial(_s5_kernel, bsz=bsz, ctx_chunks=ctx_chunks, lat_chunks=lat_chunks),
        grid=(S5_GROUPS,),
        in_specs=[
            grp(bsz * ctx_chunks, S5_CW), grp(bsz * lat_chunks, S5_CW),
            grp(S5_CW, 3 * S5_CW), grp(4 * S5_STATE, S5_CW), grp(SUBLANES, sw),
        ],
        out_specs=[grp(bsz * ctx_chunks, S5_CW), grp(bsz * lat_chunks, S5_CW)],
        out_shape=[jax.ShapeDtypeStruct(x_ctx.shape, F32), jax.ShapeDtypeStruct(x_lat.shape, F32)],
        scratch_shapes=[
            pltpu.VMEM((bsz * ctx_chunks, S5_CW), F32), pltpu.VMEM((bsz * lat_chunks, S5_CW), F32),
            slab(4, ctx_chunks), slab(4, lat_chunks), slab(2, ctx_chunks), slab(2, lat_chunks),
        ],
        compiler_params=pltpu.CompilerParams(
            dimension_semantics=("parallel",), vmem_limit_bytes=VMEM_LIMIT),
        name="s5_scan",
    )(x_ctx, x_lat, w1, w2, coef)


def _gelu_tanh(x):
    return 0.5 * x * (1.0 + jnp.tanh(math.sqrt(2.0 / math.pi) * (x + 0.044715 * (x * x * x))))


def _silu(x):
    return x * jax.nn.sigmoid(x)


def _merge_kernel(h_ref, mod_ref, npre_ref, wg_ref, oa_ref, yb_ref, oc_ref,
                  wglu_ref, bglu_ref, woa_ref, wob_ref, woc_ref, wout_ref, npost_ref, o_ref, stage_ref):
    hn = _modulated_norm(h_ref[...], mod_ref, npre_ref)

    def branch(i, o, w_ref):
        g = jnp.dot(hn, wg_ref[:, i * A_WIDTH:(i + 1) * A_WIDTH], preferred_element_type=F32)
        t = jnp.dot((o * _silu(g)).astype(BF16), w_ref[...], preferred_element_type=F32)
        m_lo = N_BRANCH * A_WIDTH + i * D_MODEL
        gate = jax.nn.sigmoid(jnp.dot(hn, wg_ref[:, m_lo:m_lo + D_MODEL], preferred_element_type=F32))
        return gate * t

    y = branch(0, oa_ref[...], woa_ref) + branch(2, oc_ref[...], woc_ref)
    yb = _gelu_tanh(_groups_to_tokens(yb_ref, stage_ref))
    glu = jnp.dot(yb.astype(BF16), wglu_ref[...], preferred_element_type=F32) + bglu_ref[...]
    y = y + branch(1, yb * jax.nn.sigmoid(glu), wob_ref)
    y = jnp.dot(y.astype(BF16), wout_ref[...], preferred_element_type=F32)
    ms = jnp.mean(y * y, axis=-1, keepdims=True)
    yn = y * lax.rsqrt(ms + EPS) * npost_ref[...]
    gate = mod_ref[:, 2 * D_MODEL:3 * D_MODEL]
    o_ref[...] = h_ref[...] + gate * yn


def _merge(h, mod, mod_row, norm_pre, w_gate, o_a, y_b, o_c, wts):
    bsz, rows, _ = h.shape
    tile = min(ROW_TILE, rows)
    tiles = rows // tile
    if mod_row is None:
        mod_map = lambda b, i: (b, 0, 0)
    else:
        mod_map = lambda b, i: (mod_row, 0, 0)
    tok = lambda w: pl.BlockSpec((None, tile,w), lambda b, i: (b, i, 0))
    in_specs = [
        tok(D_MODEL),
        pl.BlockSpec((None, 1, 3 * D_MODEL), mod_map),
        _const_spec((1, D_MODEL)), _const_spec((D_MODEL, GATE_WIDTH)),
        tok(A_WIDTH),
        pl.BlockSpec((S5_GROUPS, tile // S5_CHUNK, S5_CW), lambda b, i: (0, b * tiles + i, 0)),
        tok(C_WIDTH),
        _const_spec((B_WIDTH, B_WIDTH)), _const_spec((1, B_WIDTH)),
        _const_spec((A_WIDTH, D_MODEL)), _const_spec((B_WIDTH, D_MODEL)), _const_spec((C_WIDTH, D_MODEL)),
        _const_spec((D_MODEL, D_MODEL)), _const_spec((1, D_MODEL)),
    ]
    return pl.pallas_call(
        _merge_kernel,
        grid=(bsz, rows // tile),
        in_specs=in_specs,
        out_specs=tok(D_MODEL),
        out_shape=jax.ShapeDtypeStruct((bsz, rows, D_MODEL), F32),
        scratch_shapes=[pltpu.VMEM((B_WIDTH // LANES, (tile // S5_CHUNK) * STAGE_PITCH, LANES), F32)],
        compiler_params=pltpu.CompilerParams(
            dimension_semantics=("parallel", "parallel"), vmem_limit_bytes=VMEM_LIMIT),
        name="merge_out",
    )(h, mod, norm_pre.reshape(1, D_MODEL), w_gate, o_a, y_b, o_c, *wts)


def _rope_tables(n_tokens):
    rows = n_tokens // GRID_W
    pos_r = jnp.repeat(jnp.arange(rows, dtype=F32), GRID_W)
    pos_c = jnp.tile(jnp.arange(GRID_W, dtype=F32), rows)
    inv = ROPE_BASE ** (-jnp.arange(AX_FREQS, dtype=F32) / AX_FREQS)
    ang_r = pos_r[:, None] * inv[None]
    ang_c = pos_c[:, None] * inv[None]
    ang = jnp.concatenate([ang_r, ang_r, ang_c, ang_c], axis=-1)
    cos, sin = jnp.cos(ang), jnp.sin(ang)
    first = (jnp.arange(HEAD_DIM) % (2 * AX_FREQS)) < AX_FREQS
    sa = jnp.where(first[None], -sin, 0.0)
    sb = jnp.where(first[None], 0.0, sin)
    tile2 = lambda t: jnp.concatenate([t, t], axis=-1)
    return tile2(cos), tile2(sa), tile2(sb)


def _reorder_a_heads(w, axis):
    shape = w.shape
    split = shape[:axis] + (A_HEADS, HEAD_DIM) + shape[axis + 1:]
    return jnp.take(w.reshape(split), jnp.array(A_HEAD_ORDER), axis=axis).reshape(shape)


def kernel(x, c, ctx, c_ctx, w_mod, b_mod, norm_pre, norm_post, w_in, swa_sink, s5_a_re, s5_a_im, s5_log_dt, s5_b_re, s5_b_im, s5_c_re, s5_c_im, s5_d, s5_w_glu, s5_b_glu, diff_lq1, diff_lk1, diff_lq2, diff_lk2, diff_subln, w_o_a, w_o_b, w_o_c, w_out):
    bsz, n_lat, _ = x.shape
    n_ctx = ctx.shape[1]
    ctx_row = bsz
    cc = jnp.zeros((16, D_MODEL), F32).at[:bsz].set(c).at[ctx_row].set(c_ctx)
    mod_all = _modulation(cc, w_mod, b_mod)
    rope_tabs = _rope_tables(n_lat)

    h_lat, h_ctx = x, ctx
    for l in range(DEPTH):
        last = l == DEPTH - 1
        lam_init = 0.8 - 0.6 * math.exp(-0.3 * l)
        mod = mod_all[l].reshape(16, 1, 3 * D_MODEL)
        w_l = w_in[l]
        cols = lambda name: w_l[:, SEGS[name][0]:SEGS[name][0] + SEGS[name][1]]
        aq0, aqw = SEGS["aq"]
        w_in_l = jnp.concatenate([_reorder_a_heads(cols("aq"), 1), w_l[:, aq0 + aqw:]], axis=1).astype(BF16)
        w_gate = jnp.concatenate(
            [_reorder_a_heads(cols("ag"), 1), cols("bg"), cols("cg"), cols("mg")], axis=1).astype(BF16)
        zl = _in_projection(h_lat, mod, None, norm_pre[l], w_in_l, PROJ_NAMES, rope_tabs)
        ctx_names = ("ak", "av", "bu", "ck", "cv") if last else PROJ_NAMES
        zc = _in_projection(h_ctx, mod, ctx_row, norm_pre[l], w_in_l, ctx_names, None)

        o_a_l = _swa(swa_sink[l], zl["aq"], zc["ak"], zc["av"], zl["ak"], zl["av"])

        w1, w2, coef = _s5_operators(s5_a_re[l], s5_a_im[l], s5_log_dt[l], s5_b_re[l], s5_b_im[l],
                                     s5_c_re[l], s5_c_im[l], s5_d[l])
        y_b_c, y_b_l = _s5_scan(zc["bu"], zl["bu"], w1, w2, coef, bsz)

        lam_params = jnp.stack([diff_lq1[l], diff_lk1[l], diff_lq2[l], diff_lk2[l]]).astype(F32)
        o_c_l = _diff_attention(lam_params, diff_subln[l], lam_init, zl["cq"], zc["ck"], zc["cv"],
                                zl["ck"], zl["cv"])

        wts = (s5_w_glu[l].astype(BF16), s5_b_glu[l].reshape(1, B_WIDTH),
               _reorder_a_heads(w_o_a[l], 0).astype(BF16), w_o_b[l].astype(BF16), w_o_c[l].astype(BF16),
               w_out[l].astype(BF16), norm_post[l].reshape(1, D_MODEL))
        h_lat_new = _merge(h_lat, mod, None, norm_pre[l], w_gate, o_a_l, y_b_l, o_c_l, wts)
        if not last:
            o_a_c = _swa(swa_sink[l], zc["aq"], zc["ak"], zc["av"])
            o_c_c = _diff_attention(lam_params, diff_subln[l], lam_init, zc["cq"], zc["ck"], zc["cv"])
            h_ctx = _merge(h_ctx, mod, ctx_row, norm_pre[l], w_gate, o_a_c, y_b_c, o_c_c, wts)
        h_lat = h_lat_new
    return h_lat
```

```python
import functools
import math

import jax
import jax.numpy as jnp
from jax import lax
from jax.experimental import pallas as pl
from jax.experimental.pallas import tpu as pltpu

F32 = jnp.float32
BF16 = jnp.bfloat16

D_MODEL = 1024
DEPTH = 2
GRID_W = 64
HEAD_DIM = 64
WINDOW = 128
ROPE_BASE = 10000.0
AX_FREQS = HEAD_DIM // 4
NEG_INF = -1e30
A_HEADS = 8
A_KV_HEADS = 2
A_GROUP = A_HEADS // A_KV_HEADS
A_WIDTH = A_HEADS * HEAD_DIM
A_KV_WIDTH = A_KV_HEADS * HEAD_DIM
B_WIDTH = 512
S5_GROUP = 16
S5_GROUPS = B_WIDTH // S5_GROUP
S5_STATE = 64
C_HEADS = 4
C_WIDTH = C_HEADS * 2 * HEAD_DIM
N_BRANCH = 3
GATE_WIDTH = A_WIDTH + B_WIDTH + C_WIDTH + N_BRANCH * D_MODEL
EPS = 1e-6

LANES = 128
SUBLANES = 8

ROW_TILE = 512
Q_BLOCK = 128
DIFF_KEY_CHUNK = 512
DIFF_LOOKAHEAD = 14
SWA_LOOKAHEAD = 1
SWA_Q_ROWS = 512
DIFF_Q_ROWS = 512
A_HEAD_ORDER = tuple(h for v in range(4) for h in (v, 4 + v))
S5_CHUNK = 16
S5_CW = S5_CHUNK * S5_GROUP
S5_DOT_ROWS = 256
STAGE_PITCH = S5_CHUNK + SUBLANES
VMEM_LIMIT = 56 * 1024 * 1024

_SEG_NAMES = ("aq", "ak", "av", "ag", "bu", "bg", "cq", "ck", "cv", "cg", "mg")
_SEG_WIDTHS = (A_WIDTH, A_KV_WIDTH, A_KV_WIDTH, A_WIDTH, B_WIDTH, B_WIDTH,
               C_WIDTH, C_WIDTH, C_WIDTH, C_WIDTH, 3 * D_MODEL)
SEGS = {}
_off = 0
for _n, _w in zip(_SEG_NAMES, _SEG_WIDTHS):
    SEGS[_n] = (_off, _w)
    _off += _w
IN_WIDTH = _off
PROJ_NAMES = ("aq", "ak", "av", "bu", "cq", "ck", "cv")
Q_SCALE = HEAD_DIM ** -0.5 * math.log2(math.e)
ROPE_SEGS = {"aq": Q_SCALE, "ak": 1.0, "cq": Q_SCALE, "ck": 1.0}


def _const_spec(shape):
    nd = len(shape)
    return pl.BlockSpec(shape, lambda *_: (0,) * nd, pipeline_mode=pl.Buffered(1))


def _mod_kernel(c_ref, w_ref, b_ref, o_ref):
    c = c_ref[...]
    a = c * jax.nn.sigmoid(c)
    o_ref[...] = jnp.dot(a, w_ref[...], preferred_element_type=F32) + b_ref[...]


def _modulation(cc, w_mod, b_mod):
    n_tiles = 3
    return pl.pallas_call(
        _mod_kernel,
        grid=(DEPTH, n_tiles),
        in_specs=[
            pl.BlockSpec((16, D_MODEL), lambda l, j: (0, 0)),
            pl.BlockSpec((None, D_MODEL, D_MODEL), lambda l, j: (l, 0, j)),
            pl.BlockSpec((None, 1, D_MODEL), lambda l, j: (l, 0, j)),
        ],
        out_specs=pl.BlockSpec((None, 16, D_MODEL), lambda l, j: (l, 0, j)),
        out_shape=jax.ShapeDtypeStruct((DEPTH, 16, 3 * D_MODEL), F32),
        name="modulation",
    )(cc, w_mod, b_mod.reshape(DEPTH, 1, 3 * D_MODEL))


def _block_transpose8(xs):
    lane_blk = lax.broadcasted_iota(jnp.int32, xs[0].shape, 1) // S5_GROUP
    for s in (4, 2, 1):
        keep = (lane_blk & s) == 0
        out = list(xs)
        for p in range(8):
            if p & s:
                continue
            a, b = xs[p], xs[p + s]
            out[p] = jnp.where(keep, a, pltpu.roll(b, S5_GROUP * s, 1))
            out[p + s] = jnp.where(keep, pltpu.roll(a, LANES - S5_GROUP * s, 1), b)
        xs = out
    return xs


def _tokens_to_groups(z, stage_ref, o_ref):
    n_chunks = z.shape[0] // S5_CHUNK
    for v in range(B_WIDTH // LANES):
        for c in range(n_chunks):
            stage_ref[v, c * STAGE_PITCH:c * STAGE_PITCH + S5_CHUNK, :] = \
                z[c * S5_CHUNK:(c + 1) * S5_CHUNK, v * LANES:(v + 1) * LANES]
    for w in range(S5_CW // LANES):
        for v in range(B_WIDTH // LANES):
            src = [stage_ref[v, pl.ds(8 * w + jj, n_chunks, stride=STAGE_PITCH), :] for jj in range(8)]
            dst = _block_transpose8(src)
            for gg in range(8):
                o_ref[8 * v + gg, :, w * LANES:(w + 1) * LANES] = dst[gg].astype(o_ref.dtype)


def _groups_to_tokens(y_ref, stage_ref):
    n_chunks = y_ref.shape[1]
    for w in range(S5_CW // LANES):
        for v in range(B_WIDTH // LANES):
            src = [y_ref[8 * v + gg, :, w * LANES:(w + 1) * LANES].astype(F32) for gg in range(8)]
            dst = _block_transpose8(src)
            for jj in range(8):
                stage_ref[v, pl.ds(8 * w + jj, n_chunks, stride=STAGE_PITCH), :] = dst[jj]
    rows = []
    for c in range(n_chunks):
        rows.append(jnp.concatenate(
            [stage_ref[v, c * STAGE_PITCH:c * STAGE_PITCH + S5_CHUNK, :] for v in range(B_WIDTH // LANES)], axis=1))
    return jnp.concatenate(rows, axis=0)


def _modulated_norm(x, mod_ref, nw_ref):
    ms = jnp.mean(x * x, axis=-1, keepdims=True)
    xn = x * lax.rsqrt(ms + EPS) * nw_ref[...]
    shift = mod_ref[:, 0:D_MODEL]
    scale = mod_ref[:, D_MODEL:2 * D_MODEL]
    return (xn * (1.0 + scale) + shift).astype(BF16)


def _inproj_kernel(*refs, names, use_rope):
    h_ref, mod_ref, nw_ref, w_ref = refs[:4]
    stage_ref = refs[-1]
    refs = refs[:-1]
    if use_rope:
        cos_ref, sa_ref, sb_ref = refs[4:7]
        outs = refs[7:]
    else:
        outs = refs[4:]
    hn = _modulated_norm(h_ref[...], mod_ref, nw_ref)
    kv_off = SEGS["ak"][0]
    assert SEGS["av"][0] == kv_off + A_KV_WIDTH
    kv = jnp.dot(hn, w_ref[:, kv_off:kv_off + 2 * A_KV_WIDTH], preferred_element_type=F32)

    def project(name):
        off, width = SEGS[name]
        if name in ("ak", "av"):
            return kv[:, off - kv_off:off - kv_off + width]
        return jnp.dot(hn, w_ref[:, off:off + width], preferred_element_type=F32)

    for name, o_ref in zip(names, outs):
        z = project(name)
        mul = ROPE_SEGS.get(name, 1.0)
        if use_rope and name in ROPE_SEGS:
            cos = cos_ref[...]
            sa = sa_ref[...]
            sb = sb_ref[...]
            for c in range(z.shape[1] // LANES):
                t = z[:, c * LANES:(c + 1) * LANES]
                r = t * cos + pltpu.roll(t, LANES - AX_FREQS, 1) * sa + pltpu.roll(t, AX_FREQS, 1) * sb
                if mul != 1.0:
                    r = r * mul
                o_ref[:, c * LANES:(c + 1) * LANES] = r.astype(o_ref.dtype)
        elif name == "bu":
            _tokens_to_groups(z, stage_ref, o_ref)
        elif name == "av":
            for t in range(z.shape[0] // Q_BLOCK):
                o_ref[t] = z[t * Q_BLOCK:(t + 1) * Q_BLOCK, :].T.astype(o_ref.dtype)
        elif name == "cv":
            o_ref[...] = z.T.astype(o_ref.dtype)
        else:
            if mul != 1.0:
                z = z * mul
            o_ref[...] = z.astype(o_ref.dtype)


def _in_projection(h, mod, mod_row, norm_w, w_bf16, names, rope_tabs):
    bsz, rows, _ = h.shape
    tile = min(ROW_TILE, rows)
    use_rope = rope_tabs is not None
    if mod_row is None:
        mod_map = lambda b, i: (b, 0, 0)
    else:
        mod_map = lambda b, i: (mod_row, 0, 0)
    in_specs = [
        pl.BlockSpec((None, tile,D_MODEL), lambda b, i: (b, i, 0)),
        pl.BlockSpec((None, 1, 3 * D_MODEL), mod_map),
        _const_spec((1, D_MODEL)),
        _const_spec((D_MODEL, IN_WIDTH)),
    ]
    args = [h, mod, norm_w.reshape(1, D_MODEL), w_bf16]
    if use_rope:
        for t in rope_tabs:
            in_specs.append(pl.BlockSpec((tile, LANES), lambda b, i: (i, 0)))
            args.append(t)
    tiles = rows // tile
    chunk_rows = tile // S5_CHUNK
    out_specs, out_shape = [], []
    for n in names:
        if n == "bu":
            out_specs.append(pl.BlockSpec((S5_GROUPS, chunk_rows, S5_CW), lambda b, i: (0, b * tiles + i, 0)))
            out_shape.append(jax.ShapeDtypeStruct((S5_GROUPS, bsz * rows // S5_CHUNK, S5_CW), BF16))
        elif n == "av":
            kb = tile // Q_BLOCK
            out_specs.append(pl.BlockSpec((None, kb, A_KV_WIDTH, Q_BLOCK), lambda b, i: (b, i, 0, 0)))
            out_shape.append(jax.ShapeDtypeStruct((bsz, rows // Q_BLOCK, A_KV_WIDTH, Q_BLOCK), BF16))
        elif n == "cv":
            out_specs.append(pl.BlockSpec((None, C_WIDTH, tile), lambda b, i: (b, 0, i)))
            out_shape.append(jax.ShapeDtypeStruct((bsz, C_WIDTH, rows), BF16))
        else:
            out_specs.append(pl.BlockSpec((None, tile,SEGS[n][1]), lambda b, i: (b, i, 0)))
            out_shape.append(jax.ShapeDtypeStruct((bsz, rows, SEGS[n][1]), BF16))
    outs = pl.pallas_call(
        functools.partial(_inproj_kernel, names=tuple(names), use_rope=use_rope),
        grid=(bsz, tiles),
        in_specs=in_specs,
        out_specs=out_specs,
        out_shape=out_shape,
        scratch_shapes=[pltpu.VMEM((B_WIDTH // LANES, (tile // S5_CHUNK) * STAGE_PITCH, LANES), F32)],
        compiler_params=pltpu.CompilerParams(
            dimension_semantics=("parallel", "parallel"), vmem_limit_bytes=VMEM_LIMIT),
        name="in_projection_rope" if use_rope else "in_projection",
    )(*args)
    return dict(zip(names, outs))


def _dot_nt(a, b):
    return lax.dot_general(a, b, (((1,), (1,)), ((), ())), preferred_element_type=F32)


def _half_masked(q, lower):
    lane = lax.broadcasted_iota(jnp.int32, q.shape, 1)
    keep = lane < HEAD_DIM if lower else lane >= HEAD_DIM
    return jnp.where(keep, q.astype(F32), 0.0).astype(q.dtype)


def _swa_kernel(*refs, has_local, lat_blocks):
    if has_local:
        sink_ref, q_ref, kc_ref, vtc_ref, kl_ref, vtl_ref, o_ref = refs
    else:
        sink_ref, q_ref, kc_ref, vtc_ref, o_ref = refs
    n_ctx = kc_ref.shape[0]
    n_items = q_ref.shape[0] // Q_BLOCK
    sink = jnp.concatenate(
        [jnp.full((1, Q_BLOCK), sink_ref[kh * A_GROUP + v] * math.log2(math.e), F32)
         for kh in range(A_KV_HEADS) for v in range(A_GROUP)], axis=1)

    def scores(t):
        qs = slice(t * Q_BLOCK, (t + 1) * Q_BLOCK)
        qblk = jnp.concatenate(
            [_half_masked(q_ref[qs, v * LANES:(v + 1) * LANES], kh == 0)
             for kh in range(A_KV_HEADS) for v in range(A_GROUP)], axis=0)
        keys = [kc_ref[...]]
        vts = [vtc_ref[c] for c in range(n_ctx // Q_BLOCK)]
        if has_local:
            n = pl.program_id(1) * n_items + t
            sb = jnp.clip(n - 1, 0, lat_blocks - 3)
            keys.append(kl_ref[pl.ds(pl.multiple_of(sb * Q_BLOCK, Q_BLOCK), 3 * Q_BLOCK), :])
            vts += [vtl_ref[sb + c] for c in range(3)]
        s_t = _dot_nt(jnp.concatenate(keys, axis=0), qblk)
        if has_local:
            r = lax.broadcasted_iota(jnp.int32, (3 * Q_BLOCK, Q_BLOCK), 0)
            i = lax.broadcasted_iota(jnp.int32, (3 * Q_BLOCK, Q_BLOCK), 1)
            in_band = jnp.abs((n - sb) * Q_BLOCK + i - r) <= WINDOW
            bias = jnp.where(in_band, 0.0, NEG_INF)
            s_t = jnp.concatenate([s_t[:n_ctx], s_t[n_ctx:] + jnp.concatenate([bias] * A_HEADS, axis=1)],
                                  axis=0)
        return s_t, jnp.concatenate(vts, axis=1)

    lane = lax.broadcasted_iota(jnp.int32, (Q_BLOCK, LANES), 1)
    half = A_GROUP * Q_BLOCK
    queue = [scores(t) for t in range(min(SWA_LOOKAHEAD, n_items))]
    for t in range(n_items):
        s_t, vt = queue.pop(0)
        if t + SWA_LOOKAHEAD < n_items:
            queue.append(scores(t + SWA_LOOKAHEAD))
        m = jnp.maximum(jnp.max(s_t, axis=0, keepdims=True), sink)
        e_t = jnp.exp2(s_t - m)
        den = jnp.sum(e_t, axis=0, keepdims=True) + jnp.exp2(sink - m)
        p_t = (e_t * (1.0 / den)).astype(BF16)
        o_t = jnp.dot(vt, p_t, preferred_element_type=F32)
        qs = slice(t * Q_BLOCK, (t + 1) * Q_BLOCK)
        for v in range(A_GROUP):
            lower = o_t[:, v * Q_BLOCK:(v + 1) * Q_BLOCK].T
            upper = o_t[:, half + v * Q_BLOCK:half + (v + 1) * Q_BLOCK].T
            o_ref[qs, v * LANES:(v + 1) * LANES] = jnp.where(lane < HEAD_DIM, lower, upper).astype(o_ref.dtype)


def _swa(sink, q, k_ctx, vt_ctx, k_lat=None, vt_lat=None):
    bsz, rows, _ = q.shape
    qrows = min(SWA_Q_ROWS, rows)
    n_ctx = k_ctx.shape[1]
    has_local = k_lat is not None
    in_specs = [
        pl.BlockSpec(memory_space=pltpu.SMEM),
        pl.BlockSpec((None, qrows, A_WIDTH), lambda b, n: (b, n, 0)),
        pl.BlockSpec((None, n_ctx, A_KV_WIDTH), lambda b, n: (b, 0, 0)),
        pl.BlockSpec((None, n_ctx // Q_BLOCK, A_KV_WIDTH, Q_BLOCK), lambda b, n: (b, 0, 0, 0)),
    ]
    args = [sink, q, k_ctx, vt_ctx]
    lat_blocks = 0
    if has_local:
        n_lat = k_lat.shape[1]
        lat_blocks = n_lat // Q_BLOCK
        in_specs += [pl.BlockSpec((None, n_lat, A_KV_WIDTH), lambda b, n: (b, 0, 0)),
                     pl.BlockSpec((None, lat_blocks, A_KV_WIDTH, Q_BLOCK), lambda b, n: (b, 0, 0, 0))]
        args += [k_lat, vt_lat]
    return pl.pallas_call(
        functools.partial(_swa_kernel, has_local=has_local, lat_blocks=lat_blocks),
        grid=(bsz, rows // qrows),
        in_specs=in_specs,
        out_specs=pl.BlockSpec((None, qrows, A_WIDTH), lambda b, n: (b, n, 0)),
        out_shape=jax.ShapeDtypeStruct((bsz, rows, A_WIDTH), F32),
        compiler_params=pltpu.CompilerParams(
            dimension_semantics=("parallel", "arbitrary"), vmem_limit_bytes=VMEM_LIMIT),
        name="swa_latent" if has_local else "swa_context",
    )(*args)


def _diff_kernel(*refs, has_lat, lam_init):
    if has_lat:
        lam_ref, sub_ref, q_ref, kc_ref, vtc_ref, kl_ref, vtl_ref, o_ref = refs
    else:
        lam_ref, sub_ref, q_ref, kc_ref, vtc_ref, o_ref = refs
    lp = lam_ref[...]
    lam = (jnp.exp(jnp.sum(lp[0:1] * lp[1:2], axis=1, keepdims=True))
           - jnp.exp(jnp.sum(lp[2:3] * lp[3:4], axis=1, keepdims=True)) + lam_init)
    sub_w = sub_ref[...] * (1.0 - lam_init)
    width = 2 * HEAD_DIM
    chunks = [(kc_ref, vtc_ref, 0, kc_ref.shape[0])]
    if has_lat:
        chunks += [(kl_ref, vtl_ref, c, DIFF_KEY_CHUNK) for c in range(0, kl_ref.shape[0], DIFF_KEY_CHUNK)]
    heads = [slice(h * width, (h + 1) * width) for h in range(C_HEADS)]
    items = [(hs, slice(qb, qb + Q_BLOCK)) for qb in range(0, q_ref.shape[0], Q_BLOCK) for hs in heads]
    units = [(it, ci) for it in range(len(items)) for ci in range(len(chunks))]
    qblks = {}

    def scores(unit):
        it, ci = unit
        hs, qs = items[it]
        if it not in qblks:
            qblks[it] = jnp.concatenate(
                [_half_masked(q_ref[qs, hs], True), _half_masked(q_ref[qs, hs], False)], axis=0)
        k_ref, _, start, size = chunks[ci]
        return _dot_nt(k_ref[start:start + size, hs], qblks[it])

    issued = 0
    queue = []

    def next_scores():
        nonlocal issued
        while issued < len(units) and len(queue) < DIFF_LOOKAHEAD:
            queue.append(scores(units[issued]))
            issued += 1
        s = queue.pop(0)
        if issued < len(units):
            queue.append(scores(units[issued]))
            issued += 1
        return s

    for hs, qs in items:
        s_all = [next_scores() for _ in chunks]
        m = functools.reduce(jnp.maximum, [jnp.max(s, axis=0, keepdims=True) for s in s_all])
        e_all = [jnp.exp2(s - m) for s in s_all]
        den = functools.reduce(jnp.add, [jnp.sum(e, axis=0, keepdims=True) for e in e_all])
        inv = 1.0 / den
        c1 = inv[:, :Q_BLOCK]
        c2 = lam * inv[:, Q_BLOCK:]
        o_t = None
        for e, (_, vt_ref, start, size) in zip(e_all, chunks):
            a_t = (e[:, :Q_BLOCK] * c1 - e[:, Q_BLOCK:] * c2).astype(BF16)
            pv = jnp.dot(vt_ref[hs, start:start + size], a_t, preferred_element_type=F32)
            o_t = pv if o_t is None else o_t + pv
        o = o_t.T
        ms = jnp.mean(o * o, axis=1, keepdims=True)
        o_ref[qs, hs] = (o * lax.rsqrt(ms + EPS) * sub_w).astype(o_ref.dtype)


def _diff_attention(lam_params, subln, lam_init, q, k_ctx, vt_ctx, k_lat=None, vt_lat=None):
    bsz, rows, _ = q.shape
    qrows = min(DIFF_Q_ROWS, rows)
    n_ctx = k_ctx.shape[1]
    has_lat = k_lat is not None
    in_specs = [
        _const_spec((4, HEAD_DIM)),
        _const_spec((1, 2 * HEAD_DIM)),
        pl.BlockSpec((None, qrows, C_WIDTH), lambda b, n: (b, n, 0)),
        pl.BlockSpec((None, n_ctx, C_WIDTH), lambda b, n: (b, 0, 0)),
        pl.BlockSpec((None, C_WIDTH, n_ctx), lambda b, n: (b, 0, 0)),
    ]
    args = [lam_params, subln.reshape(1, 2 * HEAD_DIM), q, k_ctx, vt_ctx]
    if has_lat:
        n_lat = k_lat.shape[1]
        in_specs += [pl.BlockSpec((None, n_lat, C_WIDTH), lambda b, n: (b, 0, 0)),
                     pl.BlockSpec((None, C_WIDTH, n_lat), lambda b, n: (b, 0, 0))]
        args += [k_lat, vt_lat]
    return pl.pallas_call(
        functools.partial(_diff_kernel, has_lat=has_lat, lam_init=lam_init),
        grid=(bsz, rows // qrows),
        in_specs=in_specs,
        out_specs=pl.BlockSpec((None, qrows, C_WIDTH), lambda b, n: (b, n, 0)),
        out_shape=jax.ShapeDtypeStruct((bsz, rows, C_WIDTH), F32),
        compiler_params=pltpu.CompilerParams(
            dimension_semantics=("parallel", "arbitrary"), vmem_limit_bytes=VMEM_LIMIT),
        name="diff_latent" if has_lat else "diff_context",
    )(*args)


def _s5_operators(a_re, a_im, log_dt, b_re, b_im, c_re, c_im, d_skip):
    f = lambda t: t.astype(F32)
    quad = lambda t: jnp.concatenate([t, t, t, t], axis=-1)
    a_re, a_im, b_re, b_im, c_re, c_im = map(f, (a_re, a_im, b_re, b_im, c_re, c_im))
    g, p, h = S5_GROUPS, S5_STATE, S5_GROUP
    dt = jnp.exp(f(log_dt))[..., None]
    mag = jnp.exp(a_re * dt)
    abar_re = mag * jnp.cos(a_im * dt)
    abar_im = mag * jnp.sin(a_im * dt)
    den = a_re * a_re + a_im * a_im
    nr = abar_re - 1.0
    ni = abar_im
    f_re = (nr * a_re + ni * a_im) / den
    f_im = (ni * a_re - nr * a_im) / den
    pw_re, pw_im = [jnp.ones_like(abar_re)], [jnp.zeros_like(abar_im)]
    for _ in range(S5_CHUNK):
        pre, pim = pw_re[-1], pw_im[-1]
        pw_re.append(pre * abar_re - pim * abar_im)
        pw_im.append(pre * abar_im + pim * abar_re)
    pad = [jnp.zeros_like(abar_re)] * (3 * SUBLANES - S5_CHUNK - 1)
    pw = jnp.stack([jnp.stack(pw_re + pad, axis=2), jnp.stack(pw_im + pad, axis=2)], axis=1)
    pw = jnp.transpose(quad(pw), (2, 0, 1, 3, 4))
    dsk = jnp.tile(f(d_skip).reshape(g, h), (1, S5_CW // h))
    sgn = jnp.broadcast_to(jnp.repeat(jnp.array([-1.0, 1.0, 1.0, -1.0], F32), p), (g, 4 * p))
    zero = jnp.zeros_like(dsk)
    rows = jnp.stack([quad(f_re[0]), quad(f_im[0]), quad(f_re[1]), quad(f_im[1]), dsk, sgn, zero, zero],
                     axis=1)
    bt_re = jnp.swapaxes(b_re, -1, -2)
    bt_im = jnp.swapaxes(b_im, -1, -2)
    bpk = jnp.stack([jnp.concatenate([bt_re, bt_im, bt_im, bt_re], axis=-1),
                     jnp.concatenate([bt_im, bt_re, bt_re, bt_im], axis=-1)], axis=1)
    cpk = jnp.stack([jnp.concatenate([c_re, -c_im], axis=-1),
                     jnp.concatenate([-c_im, -c_re], axis=-1)], axis=1)
    bpk = jnp.transpose(bpk, (2, 0, 1, 3, 4))
    cpk = jnp.transpose(cpk, (2, 0, 1, 3, 4))
    grp = lambda shape: pl.BlockSpec((None,) + shape, lambda i: (i,) + (0,) * len(shape))
    return pl.pallas_call(
        _s5_operator_kernel,
        grid=(g,),
        in_specs=[grp((8, 4 * p)), grp((2, 2, 3 * SUBLANES, 4 * p)), grp((2, 2, h, 4 * p)),
                  grp((2, 2, h, 2 * p))],
        out_specs=[grp((S5_CW, 3 * S5_CW)), grp((4 * p, S5_CW)), grp((SUBLANES, 2 * p))],
        out_shape=[jax.ShapeDtypeStruct((g, S5_CW, 3 * S5_CW), BF16),
                   jax.ShapeDtypeStruct((g, 4 * p, S5_CW), BF16),
                   jax.ShapeDtypeStruct((g, SUBLANES, 2 * p), F32)],
        compiler_params=pltpu.CompilerParams(dimension_semantics=("parallel",)),
        name="s5_operators",
    )(rows, pw, bpk, cpk)


def _shift_lanes(x, s):
    lo, hi = x[:, :LANES], x[:, LANES:]
    lane = lax.broadcasted_iota(jnp.int32, lo.shape, 1)
    zero = jnp.zeros_like(lo)
    rot = lambda t, r: pltpu.roll(t, r, 1) if r % LANES else t
    if s >= 0:
        if s < LANES:
            rl, rh = rot(lo, s), rot(hi, s)
            out = (jnp.where(lane >= s, rl, 0.0), jnp.where(lane >= s, rh, rl))
        else:
            rl = rot(lo, s - LANES)
            out = (zero, jnp.where(lane >= s - LANES, rl, 0.0))
    else:
        s = -s
        if s < LANES:
            rl, rh = rot(lo, LANES - s), rot(hi, LANES - s)
            out = (jnp.where(lane < LANES - s, rl, rh), jnp.where(lane < LANES - s, rh, 0.0))
        else:
            rh = rot(hi, 2 * LANES - s)
            out = (jnp.where(lane < 2 * LANES - s, rh, 0.0), zero)
    return jnp.concatenate(out, axis=1)


def _s5_operator_kernel(rows_ref, pw_ref, b_ref, c_ref, w1_ref, w2_ref, coef_ref):
    t_len = S5_CHUNK
    half = 2 * S5_STATE
    sgn = rows_ref[5:6, :]
    mxu_operand = lambda t: t.astype(BF16).astype(F32)
    a4, bx4, pr4, pi4, g2 = [], [], [], [], []
    for d in range(2):
        f_re = rows_ref[2 * d:2 * d + 1, :]
        f_im = rows_ref[2 * d + 1:2 * d + 2, :]
        p1 = b_ref[d, 0]
        p2 = b_ref[d, 1]
        a4.append(mxu_operand(f_re * p1 + sgn * f_im * p2))
        bx4.append(mxu_operand(f_re * p2 - sgn * f_im * p1))
        pr4.append(pw_ref[d, 0])
        pi4.append(pw_ref[d, 1])
        cx = jnp.concatenate([mxu_operand(c_ref[d, 0])] * t_len, axis=0)
        cy = jnp.concatenate([mxu_operand(c_ref[d, 1])] * t_len, axis=0)

        def ca(order, d=d, cx=cx, cy=cy):
            pr = jnp.concatenate(
                [jnp.broadcast_to(pr4[d][t:t + 1, :half], (S5_GROUP, half)) for t in order], axis=0)
            pi = jnp.concatenate(
                [jnp.broadcast_to(pi4[d][t:t + 1, :half], (S5_GROUP, half)) for t in order], axis=0)
            return cx * pr + cy * pi
        g2.append(ca)

    def strip(d, order):
        return lax.dot_general(a4[d][:, :half], g2[d](order), (((1,), (1,)), ((), ())),
                               precision=lax.Precision.HIGHEST, preferred_element_type=F32)

    row = lax.broadcasted_iota(jnp.int32, (S5_GROUP, S5_CW), 0)
    col = lax.broadcasted_iota(jnp.int32, (S5_GROUP, S5_CW), 1)
    k_fwd = strip(0, range(t_len)) + jnp.where(row == col, rows_ref[4:5, :], 0.0)
    k_bwd = strip(1, [t_len - 1 - k for k in range(t_len)])
    for j in range(t_len):
        lo, hi = j * S5_GROUP, (j + 1) * S5_GROUP
        m = _shift_lanes(k_fwd, j * S5_GROUP) + _shift_lanes(k_bwd, -(t_len - 1 - j) * S5_GROUP)
        w1_ref[lo:hi, 0:S5_CW] = m.astype(w1_ref.dtype)
        tf, tb = t_len - 1 - j, j
        s_f = a4[0] * pr4[0][tf:tf + 1, :] + bx4[0] * (sgn * pi4[0][tf:tf + 1, :])
        s_b = a4[1] * pr4[1][tb:tb + 1, :] + bx4[1] * (sgn * pi4[1][tb:tb + 1, :])
        w1_ref[lo:hi, S5_CW:2 * S5_CW] = s_f.astype(w1_ref.dtype)
        w1_ref[lo:hi, 2 * S5_CW:3 * S5_CW] = s_b.astype(w1_ref.dtype)
    w2_ref[0:half, :] = g2[0]([i + 1 for i in range(t_len)]).T.astype(w2_ref.dtype)
    w2_ref[half:2 * half, :] = g2[1]([t_len - i for i in range(t_len)]).T.astype(w2_ref.dtype)
    zero = jnp.zeros((1, half), F32)
    coef_ref[...] = jnp.concatenate(
        [pr4[0][t_len:t_len + 1, :half], (sgn * pi4[0][t_len:t_len + 1, :])[:, :half],
         pr4[1][t_len:t_len + 1, :half], (sgn * pi4[1][t_len:t_len + 1, :])[:, :half],
         zero, zero, zero, zero], axis=0)


def _s5_kernel(xc_ref, xl_ref, w1_ref, w2_ref, coef_ref, yc_ref, yl_ref,
               zic_ref, zil_ref, zsc_ref, zsl_ref, hc_ref, hl_ref, *, bsz, ctx_chunks, lat_chunks):
    sw = 2 * S5_STATE
    streams = ((xc_ref, zic_ref, zsc_ref, hc_ref, yc_ref, ctx_chunks),
               (xl_ref, zil_ref, zsl_ref, hl_ref, yl_ref, lat_chunks))
    w1 = w1_ref[...]
    for x_ref, zi_ref, zs_ref, _, _, n in streams:
        pitch = n + SUBLANES
        per_dot = min(bsz, max(1, S5_DOT_ROWS // n))
        for b0 in range(0, bsz, per_dot):
            zz = jnp.dot(x_ref[b0 * n:(b0 + per_dot) * n, :], w1, preferred_element_type=F32)
            for b in range(b0, b0 + per_dot):
                z = zz[(b - b0) * n:(b - b0 + 1) * n]
                zi_ref[b * n:(b + 1) * n, :] = z[:, 0:S5_CW]
                for k in range(4):
                    zs_ref[k, b * pitch:b * pitch + n, :] = z[:, S5_CW + k * sw:S5_CW + (k + 1) * sw]

    shape = (bsz, sw)
    a1f = jnp.broadcast_to(coef_ref[0:1, :], shape)
    a2f = jnp.broadcast_to(coef_ref[1:2, :], shape)
    a1b = jnp.broadcast_to(coef_ref[2:3, :], shape)
    a2b = jnp.broadcast_to(coef_ref[3:4, :], shape)

    def locate(c):
        if c < ctx_chunks:
            return zsc_ref, hc_ref, c, ctx_chunks + SUBLANES
        return zsl_ref, hl_ref, c - ctx_chunks, lat_chunks + SUBLANES

    n_chunks = ctx_chunks + lat_chunks
    hf = hfs = hb = hbs = jnp.zeros(shape, F32)
    for t in range(n_chunks):
        zs_ref, h_ref, c, pitch = locate(t)
        rows = pl.ds(c, bsz, stride=pitch)
        h_ref[0, rows, :] = hf
        lf = zs_ref[0, rows, :]
        lfs = zs_ref[1, rows, :]
        hf, hfs = a1f * hf + a2f * hfs + lf, a1f * hfs - a2f * hf + lfs
        cb = ctx_chunks - 1 - t if t < ctx_chunks else n_chunks + ctx_chunks - 1 - t
        zs_ref, h_ref, c, pitch = locate(cb)
        rows = pl.ds(c, bsz, stride=pitch)
        h_ref[1, rows, :] = hb
        lb = zs_ref[2, rows, :]
        lbs = zs_ref[3, rows, :]
        hb, hbs = a1b * hb + a2b * hbs + lb, a1b * hbs - a2b * hb + lbs

    w2 = w2_ref[...]
    for _, zi_ref, _, h_ref, y_ref, n in streams:
        pitch = n + SUBLANES
        per_dot = min(bsz, max(1, S5_DOT_ROWS // n))
        for b0 in range(0, bsz, per_dot):
            hin = jnp.concatenate(
                [jnp.concatenate([h_ref[0, b * pitch:b * pitch + n, :], h_ref[1, b * pitch:b * pitch + n, :]], axis=1)
                 for b in range(b0, b0 + per_dot)], axis=0)
            rows = slice(b0 * n, (b0 + per_dot) * n)
            y = zi_ref[rows, :] + jnp.dot(hin.astype(BF16), w2, preferred_element_type=F32)
            y_ref[rows, :] = y.astype(y_ref.dtype)


def _s5_scan(x_ctx, x_lat, w1, w2, coef, bsz):
    ctx_chunks = x_ctx.shape[1] // bsz
    lat_chunks = x_lat.shape[1] // bsz
    sw = 2 * S5_STATE
    grp = lambda rows, cols: pl.BlockSpec((None, rows, cols), lambda g: (g, 0, 0))
    slab = lambda k, n: pltpu.VMEM((k, bsz * (n + SUBLANES), sw), F32)
    return pl.pallas_call(
        functools.partial(_s5_kernel, bsz=bsz, ctx_chunks=ctx_chunks, lat_chunks=lat_chunks),
        grid=(S5_GROUPS,),
        in_specs=[
            grp(bsz * ctx_chunks, S5_CW), grp(bsz * lat_chunks, S5_CW),
            grp(S5_CW, 3 * S5_CW), grp(4 * S5_STATE, S5_CW), grp(SUBLANES, sw),
        ],
        out_specs=[grp(bsz * ctx_chunks, S5_CW), grp(bsz * lat_chunks, S5_CW)],
        out_shape=[jax.ShapeDtypeStruct(x_ctx.shape, F32), jax.ShapeDtypeStruct(x_lat.shape, F32)],
        scratch_shapes=[
            pltpu.VMEM((bsz * ctx_chunks, S5_CW), F32), pltpu.VMEM((bsz * lat_chunks, S5_CW), F32),
            slab(4, ctx_chunks), slab(4, lat_chunks), slab(2, ctx_chunks), slab(2, lat_chunks),
        ],
        compiler_params=pltpu.CompilerParams(
            dimension_semantics=("parallel",), vmem_limit_bytes=VMEM_LIMIT),
        name="s5_scan",
    )(x_ctx, x_lat, w1, w2, coef)


def _gelu_tanh(x):
    return 0.5 * x * (1.0 + jnp.tanh(math.sqrt(2.0 / math.pi) * (x + 0.044715 * (x * x * x))))


def _silu(x):
    return x * jax.nn.sigmoid(x)


def _merge_kernel(h_ref, mod_ref, npre_ref, wg_ref, oa_ref, yb_ref, oc_ref,
                  wglu_ref, bglu_ref, woa_ref, wob_ref, woc_ref, wout_ref, npost_ref, o_ref, stage_ref):
    hn = _modulated_norm(h_ref[...], mod_ref, npre_ref)

    def branch(i, o, w_ref):
        g = jnp.dot(hn, wg_ref[:, i * A_WIDTH:(i + 1) * A_WIDTH], preferred_element_type=F32)
        t = jnp.dot((o * _silu(g)).astype(BF16), w_ref[...], preferred_element_type=F32)
        m_lo = N_BRANCH * A_WIDTH + i * D_MODEL
        gate = jax.nn.sigmoid(jnp.dot(hn, wg_ref[:, m_lo:m_lo + D_MODEL], preferred_element_type=F32))
        return gate * t

    y = branch(0, oa_ref[...], woa_ref) + branch(2, oc_ref[...], woc_ref)
    yb = _gelu_tanh(_groups_to_tokens(yb_ref, stage_ref))
    glu = jnp.dot(yb.astype(BF16), wglu_ref[...], preferred_element_type=F32) + bglu_ref[...]
    y = y + branch(1, yb * jax.nn.sigmoid(glu), wob_ref)
    y = jnp.dot(y.astype(BF16), wout_ref[...], preferred_element_type=F32)
    ms = jnp.mean(y * y, axis=-1, keepdims=True)
    yn = y * lax.rsqrt(ms + EPS) * npost_ref[...]
    gate = mod_ref[:, 2 * D_MODEL:3 * D_MODEL]
    o_ref[...] = h_ref[...] + gate * yn


def _merge(h, mod, mod_row, norm_pre, w_gate, o_a, y_b, o_c, wts):
    bsz, rows, _ = h.shape
    tile = min(ROW_TILE, rows)
    tiles = rows // tile
    if mod_row is None:
        mod_map = lambda b, i: (b, 0, 0)
    else:
        mod_map = lambda b, i: (mod_row, 0, 0)
    tok = lambda w: pl.BlockSpec((None, tile,w), lambda b, i: (b, i, 0))
    in_specs = [
        tok(D_MODEL),
        pl.BlockSpec((None, 1, 3 * D_MODEL), mod_map),
        _const_spec((1, D_MODEL)), _const_spec((D_MODEL, GATE_WIDTH)),
        tok(A_WIDTH),
        pl.BlockSpec((S5_GROUPS, tile // S5_CHUNK, S5_CW), lambda b, i: (0, b * tiles + i, 0)),
        tok(C_WIDTH),
        _const_spec((B_WIDTH, B_WIDTH)), _const_spec((1, B_WIDTH)),
        _const_spec((A_WIDTH, D_MODEL)), _const_spec((B_WIDTH, D_MODEL)), _const_spec((C_WIDTH, D_MODEL)),
        _const_spec((D_MODEL, D_MODEL)), _const_spec((1, D_MODEL)),
    ]
    return pl.pallas_call(
        _merge_kernel,
        grid=(bsz, rows // tile),
        in_specs=in_specs,
        out_specs=tok(D_MODEL),
        out_shape=jax.ShapeDtypeStruct((bsz, rows, D_MODEL), F32),
        scratch_shapes=[pltpu.VMEM((B_WIDTH // LANES, (tile // S5_CHUNK) * STAGE_PITCH, LANES), F32)],
        compiler_params=pltpu.CompilerParams(
            dimension_semantics=("parallel", "parallel"), vmem_limit_bytes=VMEM_LIMIT),
        name="merge_out",
    )(h, mod, norm_pre.reshape(1, D_MODEL), w_gate, o_a, y_b, o_c, *wts)


def _rope_tables(n_tokens):
    rows = n_tokens // GRID_W
    pos_r = jnp.repeat(jnp.arange(rows, dtype=F32), GRID_W)
    pos_c = jnp.tile(jnp.arange(GRID_W, dtype=F32), rows)
    inv = ROPE_BASE ** (-jnp.arange(AX_FREQS, dtype=F32) / AX_FREQS)
    ang_r = pos_r[:, None] * inv[None]
    ang_c = pos_c[:, None] * inv[None]
    ang = jnp.concatenate([ang_r, ang_r, ang_c, ang_c], axis=-1)
    cos, sin = jnp.cos(ang), jnp.sin(ang)
    first = (jnp.arange(HEAD_DIM) % (2 * AX_FREQS)) < AX_FREQS
    sa = jnp.where(first[None], -sin, 0.0)
    sb = jnp.where(first[None], 0.0, sin)
    tile2 = lambda t: jnp.concatenate([t, t], axis=-1)
    return tile2(cos), tile2(sa), tile2(sb)


def _reorder_a_heads(w, axis):
    shape = w.shape
    split = shape[:axis] + (A_HEADS, HEAD_DIM) + shape[axis + 1:]
    return jnp.take(w.reshape(split), jnp.array(A_HEAD_ORDER), axis=axis).reshape(shape)


def kernel(x, c, ctx, c_ctx, w_mod, b_mod, norm_pre, norm_post, w_in, swa_sink, s5_a_re, s5_a_im, s5_log_dt, s5_b_re, s5_b_im, s5_c_re, s5_c_im, s5_d, s5_w_glu, s5_b_glu, diff_lq1, diff_lk1, diff_lq2, diff_lk2, diff_subln, w_o_a, w_o_b, w_o_c, w_out):
    bsz, n_lat, _ = x.shape
    n_ctx = ctx.shape[1]
    ctx_row = bsz
    cc = jnp.zeros((16, D_MODEL), F32).at[:bsz].set(c).at[ctx_row].set(c_ctx)
    mod_all = _modulation(cc, w_mod, b_mod)
    rope_tabs = _rope_tables(n_lat)

    h_lat, h_ctx = x, ctx
    for l in range(DEPTH):
        last = l == DEPTH - 1
        lam_init = 0.8 - 0.6 * math.exp(-0.3 * l)
        mod = mod_all[l].reshape(16, 1, 3 * D_MODEL)
        w_l = w_in[l]
        cols = lambda name: w_l[:, SEGS[name][0]:SEGS[name][0] + SEGS[name][1]]
        aq0, aqw = SEGS["aq"]
        w_in_l = jnp.concatenate([_reorder_a_heads(cols("aq"), 1), w_l[:, aq0 + aqw:]], axis=1).astype(BF16)
        w_gate = jnp.concatenate(
            [_reorder_a_heads(cols("ag"), 1), cols("bg"), cols("cg"), cols("mg")], axis=1).astype(BF16)
        zl = _in_projection(h_lat, mod, None, norm_pre[l], w_in_l, PROJ_NAMES, rope_tabs)
        ctx_names = ("ak", "av", "bu", "ck", "cv") if last else PROJ_NAMES
        zc = _in_projection(h_ctx, mod, ctx_row, norm_pre[l], w_in_l, ctx_names, None)

        o_a_l = _swa(swa_sink[l], zl["aq"], zc["ak"], zc["av"], zl["ak"], zl["av"])

        w1, w2, coef = _s5_operators(s5_a_re[l], s5_a_im[l], s5_log_dt[l], s5_b_re[l], s5_b_im[l],
                                     s5_c_re[l], s5_c_im[l], s5_d[l])
        y_b_c, y_b_l = _s5_scan(zc["bu"], zl["bu"], w1, w2, coef, bsz)

        lam_params = jnp.stack([diff_lq1[l], diff_lk1[l], diff_lq2[l], diff_lk2[l]]).astype(F32)
        o_c_l = _diff_attention(lam_params, diff_subln[l], lam_init, zl["cq"], zc["ck"], zc["cv"],
                                zl["ck"], zl["cv"])

        wts = (s5_w_glu[l].astype(BF16), s5_b_glu[l].reshape(1, B_WIDTH),
               _reorder_a_heads(w_o_a[l], 0).astype(BF16), w_o_b[l].astype(BF16), w_o_c[l].astype(BF16),
               w_out[l].astype(BF16), norm_post[l].reshape(1, D_MODEL))
        h_lat_new = _merge(h_lat, mod, None, norm_pre[l], w_gate, o_a_l, y_b_l, o_c_l, wts)
        if not last:
            o_a_c = _swa(swa_sink[l], zc["aq"], zc["ak"], zc["av"])
            o_c_c = _diff_attention(lam_params, diff_subln[l], lam_init, zc["cq"], zc["ck"], zc["cv"])
            h_ctx = _merge(h_ctx, mod, ctx_row, norm_pre[l], w_gate, o_a_c, y_b_c, o_c_c, wts)
        h_lat = h_lat_new
    return h_lat
```

```python
import functools
import math

import jax
import jax.numpy as jnp
from jax import lax
from jax.experimental import pallas as pl
from jax.experimental.pallas import tpu as pltpu

F32 = jnp.float32
BF16 = jnp.bfloat16

D_MODEL = 1024
DEPTH = 2
GRID_W = 64
HEAD_DIM = 64
WINDOW = 128
ROPE_BASE = 10000.0
AX_FREQS = HEAD_DIM // 4
NEG_INF = -1e30
A_HEADS = 8
A_KV_HEADS = 2
A_GROUP = A_HEADS // A_KV_HEADS
A_WIDTH = A_HEADS * HEAD_DIM
A_KV_WIDTH = A_KV_HEADS * HEAD_DIM
B_WIDTH = 512
S5_GROUP = 16
S5_GROUPS = B_WIDTH // S5_GROUP
S5_STATE = 64
C_HEADS = 4
C_WIDTH = C_HEADS * 2 * HEAD_DIM
N_BRANCH = 3
GATE_WIDTH = A_WIDTH + B_WIDTH + C_WIDTH + N_BRANCH * D_MODEL
EPS = 1e-6

LANES = 128
SUBLANES = 8

ROW_TILE = 512
Q_BLOCK = 128
DIFF_KEY_CHUNK = 512
DIFF_LOOKAHEAD = 8
SWA_LOOKAHEAD = 1
SWA_Q_ROWS = 512
DIFF_Q_ROWS = 512
A_HEAD_ORDER = tuple(h for v in range(4) for h in (v, 4 + v))
S5_CHUNK = 16
S5_CW = S5_CHUNK * S5_GROUP
S5_DOT_ROWS = 256
STAGE_PITCH = S5_CHUNK + SUBLANES
VMEM_LIMIT = 56 * 1024 * 1024

_SEG_NAMES = ("aq", "ak", "av", "ag", "bu", "bg", "cq", "ck", "cv", "cg", "mg")
_SEG_WIDTHS = (A_WIDTH, A_KV_WIDTH, A_KV_WIDTH, A_WIDTH, B_WIDTH, B_WIDTH,
               C_WIDTH, C_WIDTH, C_WIDTH, C_WIDTH, 3 * D_MODEL)
SEGS = {}
_off = 0
for _n, _w in zip(_SEG_NAMES, _SEG_WIDTHS):
    SEGS[_n] = (_off, _w)
    _off += _w
IN_WIDTH = _off
PROJ_NAMES = ("aq", "ak", "av", "bu", "cq", "ck", "cv")
Q_SCALE = HEAD_DIM ** -0.5 * math.log2(math.e)
ROPE_SEGS = {"aq": Q_SCALE, "ak": 1.0, "cq": Q_SCALE, "ck": 1.0}


def _const_spec(shape):
    nd = len(shape)
    return pl.BlockSpec(shape, lambda *_: (0,) * nd, pipeline_mode=pl.Buffered(1))


def _mod_kernel(c_ref, w_ref, b_ref, o_ref):
    c = c_ref[...]
    a = c * jax.nn.sigmoid(c)
    o_ref[...] = jnp.dot(a, w_ref[...], preferred_element_type=F32) + b_ref[...]


def _modulation(cc, w_mod, b_mod):
    n_tiles = 3
    return pl.pallas_call(
        _mod_kernel,
        grid=(DEPTH, n_tiles),
        in_specs=[
            pl.BlockSpec((16, D_MODEL), lambda l, j: (0, 0)),
            pl.BlockSpec((None, D_MODEL, D_MODEL), lambda l, j: (l, 0, j)),
            pl.BlockSpec((None, 1, D_MODEL), lambda l, j: (l, 0, j)),
        ],
        out_specs=pl.BlockSpec((None, 16, D_MODEL), lambda l, j: (l, 0, j)),
        out_shape=jax.ShapeDtypeStruct((DEPTH, 16, 3 * D_MODEL), F32),
        name="modulation",
    )(cc, w_mod, b_mod.reshape(DEPTH, 1, 3 * D_MODEL))


def _block_transpose8(xs):
    lane_blk = lax.broadcasted_iota(jnp.int32, xs[0].shape, 1) // S5_GROUP
    for s in (4, 2, 1):
        keep = (lane_blk & s) == 0
        out = list(xs)
        for p in range(8):
            if p & s:
                continue
            a, b = xs[p], xs[p + s]
            out[p] = jnp.where(keep, a, pltpu.roll(b, S5_GROUP * s, 1))
            out[p + s] = jnp.where(keep, pltpu.roll(a, LANES - S5_GROUP * s, 1), b)
        xs = out
    return xs


def _tokens_to_groups(z, stage_ref, o_ref):
    n_chunks = z.shape[0] // S5_CHUNK
    for v in range(B_WIDTH // LANES):
        for c in range(n_chunks):
            stage_ref[v, c * STAGE_PITCH:c * STAGE_PITCH + S5_CHUNK, :] = \
                z[c * S5_CHUNK:(c + 1) * S5_CHUNK, v * LANES:(v + 1) * LANES]
    for w in range(S5_CW // LANES):
        for v in range(B_WIDTH // LANES):
            src = [stage_ref[v, pl.ds(8 * w + jj, n_chunks, stride=STAGE_PITCH), :] for jj in range(8)]
            dst = _block_transpose8(src)
            for gg in range(8):
                o_ref[8 * v + gg, :, w * LANES:(w + 1) * LANES] = dst[gg].astype(o_ref.dtype)


def _groups_to_tokens(y_ref, stage_ref):
    n_chunks = y_ref.shape[1]
    for w in range(S5_CW // LANES):
        for v in range(B_WIDTH // LANES):
            src = [y_ref[8 * v + gg, :, w * LANES:(w + 1) * LANES].astype(F32) for gg in range(8)]
            dst = _block_transpose8(src)
            for jj in range(8):
                stage_ref[v, pl.ds(8 * w + jj, n_chunks, stride=STAGE_PITCH), :] = dst[jj]
    rows = []
    for c in range(n_chunks):
        rows.append(jnp.concatenate(
            [stage_ref[v, c * STAGE_PITCH:c * STAGE_PITCH + S5_CHUNK, :] for v in range(B_WIDTH // LANES)], axis=1))
    return jnp.concatenate(rows, axis=0)


def _modulated_norm(x, mod_ref, nw_ref):
    ms = jnp.mean(x * x, axis=-1, keepdims=True)
    xn = x * lax.rsqrt(ms + EPS) * nw_ref[...]
    shift = mod_ref[:, 0:D_MODEL]
    scale = mod_ref[:, D_MODEL:2 * D_MODEL]
    return (xn * (1.0 + scale) + shift).astype(BF16)


def _inproj_kernel(*refs, names, use_rope):
    h_ref, mod_ref, nw_ref, w_ref = refs[:4]
    stage_ref = refs[-1]
    refs = refs[:-1]
    if use_rope:
        cos_ref, sa_ref, sb_ref = refs[4:7]
        outs = refs[7:]
    else:
        outs = refs[4:]
    hn = _modulated_norm(h_ref[...], mod_ref, nw_ref)
    kv_off = SEGS["ak"][0]
    assert SEGS["av"][0] == kv_off + A_KV_WIDTH
    kv = jnp.dot(hn, w_ref[:, kv_off:kv_off + 2 * A_KV_WIDTH], preferred_element_type=F32)

    def project(name):
        off, width = SEGS[name]
        if name in ("ak", "av"):
            return kv[:, off - kv_off:off - kv_off + width]
        return jnp.dot(hn, w_ref[:, off:off + width], preferred_element_type=F32)

    for name, o_ref in zip(names, outs):
        z = project(name)
        mul = ROPE_SEGS.get(name, 1.0)
        if use_rope and name in ROPE_SEGS:
            cos = cos_ref[...]
            sa = sa_ref[...]
            sb = sb_ref[...]
            for c in range(z.shape[1] // LANES):
                t = z[:, c * LANES:(c + 1) * LANES]
                r = t * cos + pltpu.roll(t, LANES - AX_FREQS, 1) * sa + pltpu.roll(t, AX_FREQS, 1) * sb
                if mul != 1.0:
                    r = r * mul
                o_ref[:, c * LANES:(c + 1) * LANES] = r.astype(o_ref.dtype)
        elif name == "bu":
            _tokens_to_groups(z, stage_ref, o_ref)
        elif name == "av":
            for t in range(z.shape[0] // Q_BLOCK):
                o_ref[t] = z[t * Q_BLOCK:(t + 1) * Q_BLOCK, :].T.astype(o_ref.dtype)
        elif name == "cv":
            o_ref[...] = z.T.astype(o_ref.dtype)
        else:
            if mul != 1.0:
                z = z * mul
            o_ref[...] = z.astype(o_ref.dtype)


def _in_projection(h, mod, mod_row, norm_w, w_bf16, names, rope_tabs):
    bsz, rows, _ = h.shape
    tile = min(ROW_TILE, rows)
    use_rope = rope_tabs is not None
    if mod_row is None:
        mod_map = lambda b, i: (b, 0, 0)
    else:
        mod_map = lambda b, i: (mod_row, 0, 0)
    in_specs = [
        pl.BlockSpec((None, tile,D_MODEL), lambda b, i: (b, i, 0)),
        pl.BlockSpec((None, 1, 3 * D_MODEL), mod_map),
        _const_spec((1, D_MODEL)),
        _const_spec((D_MODEL, IN_WIDTH)),
    ]
    args = [h, mod, norm_w.reshape(1, D_MODEL), w_bf16]
    if use_rope:
        for t in rope_tabs:
            in_specs.append(pl.BlockSpec((tile, LANES), lambda b, i: (i, 0)))
            args.append(t)
    tiles = rows // tile
    chunk_rows = tile // S5_CHUNK
    out_specs, out_shape = [], []
    for n in names:
        if n == "bu":
            out_specs.append(pl.BlockSpec((S5_GROUPS, chunk_rows, S5_CW), lambda b, i: (0, b * tiles + i, 0)))
            out_shape.append(jax.ShapeDtypeStruct((S5_GROUPS, bsz * rows // S5_CHUNK, S5_CW), BF16))
        elif n == "av":
            kb = tile // Q_BLOCK
            out_specs.append(pl.BlockSpec((None, kb, A_KV_WIDTH, Q_BLOCK), lambda b, i: (b, i, 0, 0)))
            out_shape.append(jax.ShapeDtypeStruct((bsz, rows // Q_BLOCK, A_KV_WIDTH, Q_BLOCK), BF16))
        elif n == "cv":
            out_specs.append(pl.BlockSpec((None, C_WIDTH, tile), lambda b, i: (b, 0, i)))
            out_shape.append(jax.ShapeDtypeStruct((bsz, C_WIDTH, rows), BF16))
        else:
            out_specs.append(pl.BlockSpec((None, tile,SEGS[n][1]), lambda b, i: (b, i, 0)))
            out_shape.append(jax.ShapeDtypeStruct((bsz, rows, SEGS[n][1]), BF16))
    outs = pl.pallas_call(
        functools.partial(_inproj_kernel, names=tuple(names), use_rope=use_rope),
        grid=(bsz, tiles),
        in_specs=in_specs,
        out_specs=out_specs,
        out_shape=out_shape,
        scratch_shapes=[pltpu.VMEM((B_WIDTH // LANES, (tile // S5_CHUNK) * STAGE_PITCH, LANES), F32)],
        compiler_params=pltpu.CompilerParams(
            dimension_semantics=("parallel", "parallel"), vmem_limit_bytes=VMEM_LIMIT),
        name="in_projection_rope" if use_rope else "in_projection",
    )(*args)
    return dict(zip(names, outs))


def _dot_nt(a, b):
    return lax.dot_general(a, b, (((1,), (1,)), ((), ())), preferred_element_type=F32)


def _half_masked(q, lower):
    lane = lax.broadcasted_iota(jnp.int32, q.shape, 1)
    keep = lane < HEAD_DIM if lower else lane >= HEAD_DIM
    return jnp.where(keep, q.astype(F32), 0.0).astype(q.dtype)


def _swa_kernel(*refs, has_local, lat_blocks):
    if has_local:
        sink_ref, q_ref, kc_ref, vtc_ref, kl_ref, vtl_ref, o_ref = refs
    else:
        sink_ref, q_ref, kc_ref, vtc_ref, o_ref = refs
    n_ctx = kc_ref.shape[0]
    n_items = q_ref.shape[0] // Q_BLOCK
    sink = jnp.concatenate(
        [jnp.full((1, Q_BLOCK), sink_ref[kh * A_GROUP + v] * math.log2(math.e), F32)
         for kh in range(A_KV_HEADS) for v in range(A_GROUP)], axis=1)

    def scores(t):
        qs = slice(t * Q_BLOCK, (t + 1) * Q_BLOCK)
        qblk = jnp.concatenate(
            [_half_masked(q_ref[qs, v * LANES:(v + 1) * LANES], kh == 0)
             for kh in range(A_KV_HEADS) for v in range(A_GROUP)], axis=0)
        keys = [kc_ref[...]]
        vts = [vtc_ref[c] for c in range(n_ctx // Q_BLOCK)]
        if has_local:
            n = pl.program_id(1) * n_items + t
            sb = jnp.clip(n - 1, 0, lat_blocks - 3)
            keys.append(kl_ref[pl.ds(pl.multiple_of(sb * Q_BLOCK, Q_BLOCK), 3 * Q_BLOCK), :])
            vts += [vtl_ref[sb + c] for c in range(3)]
        s_t = _dot_nt(jnp.concatenate(keys, axis=0), qblk)
        if has_local:
            r = lax.broadcasted_iota(jnp.int32, (3 * Q_BLOCK, Q_BLOCK), 0)
            i = lax.broadcasted_iota(jnp.int32, (3 * Q_BLOCK, Q_BLOCK), 1)
            in_band = jnp.abs((n - sb) * Q_BLOCK + i - r) <= WINDOW
            bias = jnp.where(in_band, 0.0, NEG_INF)
            s_t = jnp.concatenate([s_t[:n_ctx], s_t[n_ctx:] + jnp.concatenate([bias] * A_HEADS, axis=1)],
                                  axis=0)
        return s_t, jnp.concatenate(vts, axis=1)

    lane = lax.broadcasted_iota(jnp.int32, (Q_BLOCK, LANES), 1)
    half = A_GROUP * Q_BLOCK
    queue = [scores(t) for t in range(min(SWA_LOOKAHEAD, n_items))]
    for t in range(n_items):
        s_t, vt = queue.pop(0)
        if t + SWA_LOOKAHEAD < n_items:
            queue.append(scores(t + SWA_LOOKAHEAD))
        m = jnp.maximum(jnp.max(s_t, axis=0, keepdims=True), sink)
        e_t = jnp.exp2(s_t - m)
        den = jnp.sum(e_t, axis=0, keepdims=True) + jnp.exp2(sink - m)
        p_t = (e_t * (1.0 / den)).astype(BF16)
        o_t = jnp.dot(vt, p_t, preferred_element_type=F32)
        qs = slice(t * Q_BLOCK, (t + 1) * Q_BLOCK)
        for v in range(A_GROUP):
            lower = o_t[:, v * Q_BLOCK:(v + 1) * Q_BLOCK].T
            upper = o_t[:, half + v * Q_BLOCK:half + (v + 1) * Q_BLOCK].T
            o_ref[qs, v * LANES:(v + 1) * LANES] = jnp.where(lane < HEAD_DIM, lower, upper).astype(o_ref.dtype)


def _swa(sink, q, k_ctx, vt_ctx, k_lat=None, vt_lat=None):
    bsz, rows, _ = q.shape
    qrows = min(SWA_Q_ROWS, rows)
    n_ctx = k_ctx.shape[1]
    has_local = k_lat is not None
    in_specs = [
        pl.BlockSpec(memory_space=pltpu.SMEM),
        pl.BlockSpec((None, qrows, A_WIDTH), lambda b, n: (b, n, 0)),
        pl.BlockSpec((None, n_ctx, A_KV_WIDTH), lambda b, n: (b, 0, 0)),
        pl.BlockSpec((None, n_ctx // Q_BLOCK, A_KV_WIDTH, Q_BLOCK), lambda b, n: (b, 0, 0, 0)),
    ]
    args = [sink, q, k_ctx, vt_ctx]
    lat_blocks = 0
    if has_local:
        n_lat = k_lat.shape[1]
        lat_blocks = n_lat // Q_BLOCK
        in_specs += [pl.BlockSpec((None, n_lat, A_KV_WIDTH), lambda b, n: (b, 0, 0)),
                     pl.BlockSpec((None, lat_blocks, A_KV_WIDTH, Q_BLOCK), lambda b, n: (b, 0, 0, 0))]
        args += [k_lat, vt_lat]
    return pl.pallas_call(
        functools.partial(_swa_kernel, has_local=has_local, lat_blocks=lat_blocks),
        grid=(bsz, rows // qrows),
        in_specs=in_specs,
        out_specs=pl.BlockSpec((None, qrows, A_WIDTH), lambda b, n: (b, n, 0)),
        out_shape=jax.ShapeDtypeStruct((bsz, rows, A_WIDTH), F32),
        compiler_params=pltpu.CompilerParams(
            dimension_semantics=("parallel", "arbitrary"), vmem_limit_bytes=VMEM_LIMIT),
        name="swa_latent" if has_local else "swa_context",
    )(*args)


def _diff_kernel(*refs, has_lat, lam_init):
    if has_lat:
        lam_ref, sub_ref, q_ref, kc_ref, vtc_ref, kl_ref, vtl_ref, o_ref = refs
    else:
        lam_ref, sub_ref, q_ref, kc_ref, vtc_ref, o_ref = refs
    lp = lam_ref[...]
    lam = (jnp.exp(jnp.sum(lp[0:1] * lp[1:2], axis=1, keepdims=True))
           - jnp.exp(jnp.sum(lp[2:3] * lp[3:4], axis=1, keepdims=True)) + lam_init)
    sub_w = sub_ref[...] * (1.0 - lam_init)
    width = 2 * HEAD_DIM
    chunks = [(kc_ref, vtc_ref, 0, kc_ref.shape[0])]
    if has_lat:
        chunks += [(kl_ref, vtl_ref, c, DIFF_KEY_CHUNK) for c in range(0, kl_ref.shape[0], DIFF_KEY_CHUNK)]
    heads = [slice(h * width, (h + 1) * width) for h in range(C_HEADS)]
    items = [(hs, slice(qb, qb + Q_BLOCK)) for qb in range(0, q_ref.shape[0], Q_BLOCK) for hs in heads]
    units = [(it, ci) for it in range(len(items)) for ci in range(len(chunks))]
    qblks = {}

    def scores(unit):
        it, ci = unit
        hs, qs = items[it]
        if it not in qblks:
            qblks[it] = jnp.concatenate(
                [_half_masked(q_ref[qs, hs], True), _half_masked(q_ref[qs, hs], False)], axis=0)
        k_ref, _, start, size = chunks[ci]
        return _dot_nt(k_ref[start:start + size, hs], qblks[it])

    queue = [scores(unit) for unit in units[:DIFF_LOOKAHEAD]]
    m = den = acc = None
    for u, (it, ci) in enumerate(units):
        hs, qs = items[it]
        _, vt_ref, start, size = chunks[ci]
        s_t = queue.pop(0)
        if u + DIFF_LOOKAHEAD < len(units):
            queue.append(scores(units[u + DIFF_LOOKAHEAD]))
        m_c = jnp.max(s_t, axis=0, keepdims=True)
        m_new = m_c if ci == 0 else jnp.maximum(m, m_c)
        p_t = jnp.exp2(s_t - m_new)
        den_c = jnp.sum(p_t, axis=0, keepdims=True)
        pv = jnp.dot(vt_ref[hs, start:start + size], p_t.astype(BF16), preferred_element_type=F32)
        if ci == 0:
            den, acc = den_c, pv
        else:
            alpha = jnp.exp2(m - m_new)
            den, acc = den * alpha + den_c, acc * alpha + pv
        m = m_new
        if ci == len(chunks) - 1:
            inv = 1.0 / den
            o_t = acc[:, :Q_BLOCK] * inv[:, :Q_BLOCK] - acc[:, Q_BLOCK:] * (lam * inv[:, Q_BLOCK:])
            o = o_t.T
            ms = jnp.mean(o * o, axis=1, keepdims=True)
            o_ref[qs, hs] = (o * lax.rsqrt(ms + EPS) * sub_w).astype(o_ref.dtype)


def _diff_attention(lam_params, subln, lam_init, q, k_ctx, vt_ctx, k_lat=None, vt_lat=None):
    bsz, rows, _ = q.shape
    qrows = min(DIFF_Q_ROWS, rows)
    n_ctx = k_ctx.shape[1]
    has_lat = k_lat is not None
    in_specs = [
        _const_spec((4, HEAD_DIM)),
        _const_spec((1, 2 * HEAD_DIM)),
        pl.BlockSpec((None, qrows, C_WIDTH), lambda b, n: (b, n, 0)),
        pl.BlockSpec((None, n_ctx, C_WIDTH), lambda b, n: (b, 0, 0)),
        pl.BlockSpec((None, C_WIDTH, n_ctx), lambda b, n: (b, 0, 0)),
    ]
    args = [lam_params, subln.reshape(1, 2 * HEAD_DIM), q, k_ctx, vt_ctx]
    if has_lat:
        n_lat = k_lat.shape[1]
        in_specs += [pl.BlockSpec((None, n_lat, C_WIDTH), lambda b, n: (b, 0, 0)),
                     pl.BlockSpec((None, C_WIDTH, n_lat), lambda b, n: (b, 0, 0))]
        args += [k_lat, vt_lat]
    return pl.pallas_call(
        functools.partial(_diff_kernel, has_lat=has_lat, lam_init=lam_init),
        grid=(bsz, rows // qrows),
        in_specs=in_specs,
        out_specs=pl.BlockSpec((None, qrows, C_WIDTH), lambda b, n: (b, n, 0)),
        out_shape=jax.ShapeDtypeStruct((bsz, rows, C_WIDTH), F32),
        compiler_params=pltpu.CompilerParams(
            dimension_semantics=("parallel", "arbitrary"), vmem_limit_bytes=VMEM_LIMIT),
        name="diff_latent" if has_lat else "diff_context",
    )(*args)


def _s5_operators(a_re, a_im, log_dt, b_re, b_im, c_re, c_im, d_skip):
    f = lambda t: t.astype(F32)
    quad = lambda t: jnp.concatenate([t, t, t, t], axis=-1)
    a_re, a_im, b_re, b_im, c_re, c_im = map(f, (a_re, a_im, b_re, b_im, c_re, c_im))
    g, p, h = S5_GROUPS, S5_STATE, S5_GROUP
    dt = jnp.exp(f(log_dt))[..., None]
    mag = jnp.exp(a_re * dt)
    abar_re = mag * jnp.cos(a_im * dt)
    abar_im = mag * jnp.sin(a_im * dt)
    den = a_re * a_re + a_im * a_im
    nr = abar_re - 1.0
    ni = abar_im
    f_re = (nr * a_re + ni * a_im) / den
    f_im = (ni * a_re - nr * a_im) / den
    pw_re, pw_im = [jnp.ones_like(abar_re)], [jnp.zeros_like(abar_im)]
    for _ in range(S5_CHUNK):
        pre, pim = pw_re[-1], pw_im[-1]
        pw_re.append(pre * abar_re - pim * abar_im)
        pw_im.append(pre * abar_im + pim * abar_re)
    pad = [jnp.zeros_like(abar_re)] * (3 * SUBLANES - S5_CHUNK - 1)
    pw = jnp.stack([jnp.stack(pw_re + pad, axis=2), jnp.stack(pw_im + pad, axis=2)], axis=1)
    pw = jnp.transpose(quad(pw), (2, 0, 1, 3, 4))
    dsk = jnp.tile(f(d_skip).reshape(g, h), (1, S5_CW // h))
    sgn = jnp.broadcast_to(jnp.repeat(jnp.array([-1.0, 1.0, 1.0, -1.0], F32), p), (g, 4 * p))
    zero = jnp.zeros_like(dsk)
    rows = jnp.stack([quad(f_re[0]), quad(f_im[0]), quad(f_re[1]), quad(f_im[1]), dsk, sgn, zero, zero],
                     axis=1)
    bt_re = jnp.swapaxes(b_re, -1, -2)
    bt_im = jnp.swapaxes(b_im, -1, -2)
    bpk = jnp.stack([jnp.concatenate([bt_re, bt_im, bt_im, bt_re], axis=-1),
                     jnp.concatenate([bt_im, bt_re, bt_re, bt_im], axis=-1)], axis=1)
    cpk = jnp.stack([jnp.concatenate([c_re, -c_im], axis=-1),
                     jnp.concatenate([-c_im, -c_re], axis=-1)], axis=1)
    bpk = jnp.transpose(bpk, (2, 0, 1, 3, 4))
    cpk = jnp.transpose(cpk, (2, 0, 1, 3, 4))
    grp = lambda shape: pl.BlockSpec((None,) + shape, lambda i: (i,) + (0,) * len(shape))
    return pl.pallas_call(
        _s5_operator_kernel,
        grid=(g,),
        in_specs=[grp((8, 4 * p)), grp((2, 2, 3 * SUBLANES, 4 * p)), grp((2, 2, h, 4 * p)),
                  grp((2, 2, h, 2 * p))],
        out_specs=[grp((S5_CW, 3 * S5_CW)), grp((4 * p, S5_CW)), grp((SUBLANES, 2 * p))],
        out_shape=[jax.ShapeDtypeStruct((g, S5_CW, 3 * S5_CW), BF16),
                   jax.ShapeDtypeStruct((g, 4 * p, S5_CW), BF16),
                   jax.ShapeDtypeStruct((g, SUBLANES, 2 * p), F32)],
        compiler_params=pltpu.CompilerParams(dimension_semantics=("parallel",)),
        name="s5_operators",
    )(rows, pw, bpk, cpk)


def _shift_lanes(x, s):
    lo, hi = x[:, :LANES], x[:, LANES:]
    lane = lax.broadcasted_iota(jnp.int32, lo.shape, 1)
    zero = jnp.zeros_like(lo)
    rot = lambda t, r: pltpu.roll(t, r, 1) if r % LANES else t
    if s >= 0:
        if s < LANES:
            rl, rh = rot(lo, s), rot(hi, s)
            out = (jnp.where(lane >= s, rl, 0.0), jnp.where(lane >= s, rh, rl))
        else:
            rl = rot(lo, s - LANES)
            out = (zero, jnp.where(lane >= s - LANES, rl, 0.0))
    else:
        s = -s
        if s < LANES:
            rl, rh = rot(lo, LANES - s), rot(hi, LANES - s)
            out = (jnp.where(lane < LANES - s, rl, rh), jnp.where(lane < LANES - s, rh, 0.0))
        else:
            rh = rot(hi, 2 * LANES - s)
            out = (jnp.where(lane < 2 * LANES - s, rh, 0.0), zero)
    return jnp.concatenate(out, axis=1)


def _s5_operator_kernel(rows_ref, pw_ref, b_ref, c_ref, w1_ref, w2_ref, coef_ref):
    t_len = S5_CHUNK
    half = 2 * S5_STATE
    sgn = rows_ref[5:6, :]
    mxu_operand = lambda t: t.astype(BF16).astype(F32)
    a4, bx4, pr4, pi4, g2 = [], [], [], [], []
    for d in range(2):
        f_re = rows_ref[2 * d:2 * d + 1, :]
        f_im = rows_ref[2 * d + 1:2 * d + 2, :]
        p1 = b_ref[d, 0]
        p2 = b_ref[d, 1]
        a4.append(mxu_operand(f_re * p1 + sgn * f_im * p2))
        bx4.append(mxu_operand(f_re * p2 - sgn * f_im * p1))
        pr4.append(pw_ref[d, 0])
        pi4.append(pw_ref[d, 1])
        cx = jnp.concatenate([mxu_operand(c_ref[d, 0])] * t_len, axis=0)
        cy = jnp.concatenate([mxu_operand(c_ref[d, 1])] * t_len, axis=0)

        def ca(order, d=d, cx=cx, cy=cy):
            pr = jnp.concatenate(
                [jnp.broadcast_to(pr4[d][t:t + 1, :half], (S5_GROUP, half)) for t in order], axis=0)
            pi = jnp.concatenate(
                [jnp.broadcast_to(pi4[d][t:t + 1, :half], (S5_GROUP, half)) for t in order], axis=0)
            return cx * pr + cy * pi
        g2.append(ca)

    def strip(d, order):
        return lax.dot_general(a4[d][:, :half], g2[d](order), (((1,), (1,)), ((), ())),
                               precision=lax.Precision.HIGHEST, preferred_element_type=F32)

    row = lax.broadcasted_iota(jnp.int32, (S5_GROUP, S5_CW), 0)
    col = lax.broadcasted_iota(jnp.int32, (S5_GROUP, S5_CW), 1)
    k_fwd = strip(0, range(t_len)) + jnp.where(row == col, rows_ref[4:5, :], 0.0)
    k_bwd = strip(1, [t_len - 1 - k for k in range(t_len)])
    for j in range(t_len):
        lo, hi = j * S5_GROUP, (j + 1) * S5_GROUP
        m = _shift_lanes(k_fwd, j * S5_GROUP) + _shift_lanes(k_bwd, -(t_len - 1 - j) * S5_GROUP)
        w1_ref[lo:hi, 0:S5_CW] = m.astype(w1_ref.dtype)
        tf, tb = t_len - 1 - j, j
        s_f = a4[0] * pr4[0][tf:tf + 1, :] + bx4[0] * (sgn * pi4[0][tf:tf + 1, :])
        s_b = a4[1] * pr4[1][tb:tb + 1, :] + bx4[1] * (sgn * pi4[1][tb:tb + 1, :])
        w1_ref[lo:hi, S5_CW:2 * S5_CW] = s_f.astype(w1_ref.dtype)
        w1_ref[lo:hi, 2 * S5_CW:3 * S5_CW] = s_b.astype(w1_ref.dtype)
    w2_ref[0:half, :] = g2[0]([i + 1 for i in range(t_len)]).T.astype(w2_ref.dtype)
    w2_ref[half:2 * half, :] = g2[1]([t_len - i for i in range(t_len)]).T.astype(w2_ref.dtype)
    zero = jnp.zeros((1, half), F32)
    coef_ref[...] = jnp.concatenate(
        [pr4[0][t_len:t_len + 1, :half], (sgn * pi4[0][t_len:t_len + 1, :])[:, :half],
         pr4[1][t_len:t_len + 1, :half], (sgn * pi4[1][t_len:t_len + 1, :])[:, :half],
         zero, zero, zero, zero], axis=0)


def _s5_kernel(xc_ref, xl_ref, w1_ref, w2_ref, coef_ref, yc_ref, yl_ref,
               zic_ref, zil_ref, zsc_ref, zsl_ref, hc_ref, hl_ref, *, bsz, ctx_chunks, lat_chunks):
    sw = 2 * S5_STATE
    streams = ((xc_ref, zic_ref, zsc_ref, hc_ref, yc_ref, ctx_chunks),
               (xl_ref, zil_ref, zsl_ref, hl_ref, yl_ref, lat_chunks))
    w1 = w1_ref[...]
    for x_ref, zi_ref, zs_ref, _, _, n in streams:
        pitch = n + SUBLANES
        per_dot = min(bsz, max(1, S5_DOT_ROWS // n))
        for b0 in range(0, bsz, per_dot):
            zz = jnp.dot(x_ref[b0 * n:(b0 + per_dot) * n, :], w1, preferred_element_type=F32)
            for b in range(b0, b0 + per_dot):
                z = zz[(b - b0) * n:(b - b0 + 1) * n]
                zi_ref[b * n:(b + 1) * n, :] = z[:, 0:S5_CW]
                for k in range(4):
                    zs_ref[k, b * pitch:b * pitch + n, :] = z[:, S5_CW + k * sw:S5_CW + (k + 1) * sw]

    shape = (bsz, sw)
    a1f = jnp.broadcast_to(coef_ref[0:1, :], shape)
    a2f = jnp.broadcast_to(coef_ref[1:2, :], shape)
    a1b = jnp.broadcast_to(coef_ref[2:3, :], shape)
    a2b = jnp.broadcast_to(coef_ref[3:4, :], shape)

    def locate(c):
        if c < ctx_chunks:
            return zsc_ref, hc_ref, c, ctx_chunks + SUBLANES
        return zsl_ref, hl_ref, c - ctx_chunks, lat_chunks + SUBLANES

    n_chunks = ctx_chunks + lat_chunks
    hf = hfs = hb = hbs = jnp.zeros(shape, F32)
    for t in range(n_chunks):
        zs_ref, h_ref, c, pitch = locate(t)
        rows = pl.ds(c, bsz, stride=pitch)
        h_ref[0, rows, :] = hf
        lf = zs_ref[0, rows, :]
        lfs = zs_ref[1, rows, :]
        hf, hfs = a1f * hf + a2f * hfs + lf, a1f * hfs - a2f * hf + lfs
        cb = ctx_chunks - 1 - t if t < ctx_chunks else n_chunks + ctx_chunks - 1 - t
        zs_ref, h_ref, c, pitch = locate(cb)
        rows = pl.ds(c, bsz, stride=pitch)
        h_ref[1, rows, :] = hb
        lb = zs_ref[2, rows, :]
        lbs = zs_ref[3, rows, :]
        hb, hbs = a1b * hb + a2b * hbs + lb, a1b * hbs - a2b * hb + lbs

    w2 = w2_ref[...]
    for _, zi_ref, _, h_ref, y_ref, n in streams:
        pitch = n + SUBLANES
        per_dot = min(bsz, max(1, S5_DOT_ROWS // n))
        for b0 in range(0, bsz, per_dot):
            hin = jnp.concatenate(
                [jnp.concatenate([h_ref[0, b * pitch:b * pitch + n, :], h_ref[1, b * pitch:b * pitch + n, :]], axis=1)
                 for b in range(b0, b0 + per_dot)], axis=0)
            rows = slice(b0 * n, (b0 + per_dot) * n)
            y = zi_ref[rows, :] + jnp.dot(hin.astype(BF16), w2, preferred_element_type=F32)
            y_ref[rows, :] = y.astype(y_ref.dtype)


def _s5_scan(x_ctx, x_lat, w1, w2, coef, bsz):
    ctx_chunks = x_ctx.shape[1] // bsz
    lat_chunks = x_lat.shape[1] // bsz
    sw = 2 * S5_STATE
    grp = lambda rows, cols: pl.BlockSpec((None, rows, cols), lambda g: (g, 0, 0))
    slab = lambda k, n: pltpu.VMEM((k, bsz * (n + SUBLANES), sw), F32)
    return pl.pallas_call(
        functools.partial(_s5_kernel, bsz=bsz, ctx_chunks=ctx_chunks, lat_chunks=lat_chunks),
        grid=(S5_GROUPS,),
        in_specs=[
            grp(bsz * ctx_chunks, S5_CW), grp(bsz * lat_chunks, S5_CW),
            grp(S5_CW, 3 * S5_CW), grp(4 * S5_STATE, S5_CW), grp(SUBLANES, sw),
        ],
        out_specs=[grp(bsz * ctx_chunks, S5_CW), grp(bsz * lat_chunks, S5_CW)],
        out_shape=[jax.ShapeDtypeStruct(x_ctx.shape, F32), jax.ShapeDtypeStruct(x_lat.shape, F32)],
        scratch_shapes=[
            pltpu.VMEM((bsz * ctx_chunks, S5_CW), F32), pltpu.VMEM((bsz * lat_chunks, S5_CW), F32),
            slab(4, ctx_chunks), slab(4, lat_chunks), slab(2, ctx_chunks), slab(2, lat_chunks),
        ],
        compiler_params=pltpu.CompilerParams(
            dimension_semantics=("parallel",), vmem_limit_bytes=VMEM_LIMIT),
        name="s5_scan",
    )(x_ctx, x_lat, w1, w2, coef)


def _gelu_tanh(x):
    return 0.5 * x * (1.0 + jnp.tanh(math.sqrt(2.0 / math.pi) * (x + 0.044715 * (x * x * x))))


def _silu(x):
    return x * jax.nn.sigmoid(x)


def _merge_kernel(h_ref, mod_ref, npre_ref, wg_ref, oa_ref, yb_ref, oc_ref,
                  wglu_ref, bglu_ref, woa_ref, wob_ref, woc_ref, wout_ref, npost_ref, o_ref, stage_ref):
    hn = _modulated_norm(h_ref[...], mod_ref, npre_ref)

    def branch(i, o, w_ref):
        g = jnp.dot(hn, wg_ref[:, i * A_WIDTH:(i + 1) * A_WIDTH], preferred_element_type=F32)
        t = jnp.dot((o * _silu(g)).astype(BF16), w_ref[...], preferred_element_type=F32)
        m_lo = N_BRANCH * A_WIDTH + i * D_MODEL
        gate = jax.nn.sigmoid(jnp.dot(hn, wg_ref[:, m_lo:m_lo + D_MODEL], preferred_element_type=F32))
        return gate * t

    y = branch(0, oa_ref[...], woa_ref) + branch(2, oc_ref[...], woc_ref)
    yb = _gelu_tanh(_groups_to_tokens(yb_ref, stage_ref))
    glu = jnp.dot(yb.astype(BF16), wglu_ref[...], preferred_element_type=F32) + bglu_ref[...]
    y = y + branch(1, yb * jax.nn.sigmoid(glu), wob_ref)
    y = jnp.dot(y.astype(BF16), wout_ref[...], preferred_element_type=F32)
    ms = jnp.mean(y * y, axis=-1, keepdims=True)
    yn = y * lax.rsqrt(ms + EPS) * npost_ref[...]
    gate = mod_ref[:, 2 * D_MODEL:3 * D_MODEL]
    o_ref[...] = h_ref[...] + gate * yn


def _merge(h, mod, mod_row, norm_pre, w_gate, o_a, y_b, o_c, wts):
    bsz, rows, _ = h.shape
    tile = min(ROW_TILE, rows)
    tiles = rows // tile
    if mod_row is None:
        mod_map = lambda b, i: (b, 0, 0)
    else:
        mod_map = lambda b, i: (mod_row, 0, 0)
    tok = lambda w: pl.BlockSpec((None, tile,w), lambda b, i: (b, i, 0))
    in_specs = [
        tok(D_MODEL),
        pl.BlockSpec((None, 1, 3 * D_MODEL), mod_map),
        _const_spec((1, D_MODEL)), _const_spec((D_MODEL, GATE_WIDTH)),
        tok(A_WIDTH),
        pl.BlockSpec((S5_GROUPS, tile // S5_CHUNK, S5_CW), lambda b, i: (0, b * tiles + i, 0)),
        tok(C_WIDTH),
        _const_spec((B_WIDTH, B_WIDTH)), _const_spec((1, B_WIDTH)),
        _const_spec((A_WIDTH, D_MODEL)), _const_spec((B_WIDTH, D_MODEL)), _const_spec((C_WIDTH, D_MODEL)),
        _const_spec((D_MODEL, D_MODEL)), _const_spec((1, D_MODEL)),
    ]
    return pl.pallas_call(
        _merge_kernel,
        grid=(bsz, rows // tile),
        in_specs=in_specs,
        out_specs=tok(D_MODEL),
        out_shape=jax.ShapeDtypeStruct((bsz, rows, D_MODEL), F32),
        scratch_shapes=[pltpu.VMEM((B_WIDTH // LANES, (tile // S5_CHUNK) * STAGE_PITCH, LANES), F32)],
        compiler_params=pltpu.CompilerParams(
            dimension_semantics=("parallel", "parallel"), vmem_limit_bytes=VMEM_LIMIT),
        name="merge_out",
    )(h, mod, norm_pre.reshape(1, D_MODEL), w_gate, o_a, y_b, o_c, *wts)


def _rope_tables(n_tokens):
    rows = n_tokens // GRID_W
    pos_r = jnp.repeat(jnp.arange(rows, dtype=F32), GRID_W)
    pos_c = jnp.tile(jnp.arange(GRID_W, dtype=F32), rows)
    inv = ROPE_BASE ** (-jnp.arange(AX_FREQS, dtype=F32) / AX_FREQS)
    ang_r = pos_r[:, None] * inv[None]
    ang_c = pos_c[:, None] * inv[None]
    ang = jnp.concatenate([ang_r, ang_r, ang_c, ang_c], axis=-1)
    cos, sin = jnp.cos(ang), jnp.sin(ang)
    first = (jnp.arange(HEAD_DIM) % (2 * AX_FREQS)) < AX_FREQS
    sa = jnp.where(first[None], -sin, 0.0)
    sb = jnp.where(first[None], 0.0, sin)
    tile2 = lambda t: jnp.concatenate([t, t], axis=-1)
    return tile2(cos), tile2(sa), tile2(sb)


def _reorder_a_heads(w, axis):
    shape = w.shape
    split = shape[:axis] + (A_HEADS, HEAD_DIM) + shape[axis + 1:]
    return jnp.take(w.reshape(split), jnp.array(A_HEAD_ORDER), axis=axis).reshape(shape)


def kernel(x, c, ctx, c_ctx, w_mod, b_mod, norm_pre, norm_post, w_in, swa_sink, s5_a_re, s5_a_im, s5_log_dt, s5_b_re, s5_b_im, s5_c_re, s5_c_im, s5_d, s5_w_glu, s5_b_glu, diff_lq1, diff_lk1, diff_lq2, diff_lk2, diff_subln, w_o_a, w_o_b, w_o_c, w_out):
    bsz, n_lat, _ = x.shape
    n_ctx = ctx.shape[1]
    ctx_row = bsz
    cc = jnp.zeros((16, D_MODEL), F32).at[:bsz].set(c).at[ctx_row].set(c_ctx)
    mod_all = _modulation(cc, w_mod, b_mod)
    rope_tabs = _rope_tables(n_lat)

    h_lat, h_ctx = x, ctx
    for l in range(DEPTH):
        last = l == DEPTH - 1
        lam_init = 0.8 - 0.6 * math.exp(-0.3 * l)
        mod = mod_all[l].reshape(16, 1, 3 * D_MODEL)
        w_l = w_in[l]
        cols = lambda name: w_l[:, SEGS[name][0]:SEGS[name][0] + SEGS[name][1]]
        aq0, aqw = SEGS["aq"]
        w_in_l = jnp.concatenate([_reorder_a_heads(cols("aq"), 1), w_l[:, aq0 + aqw:]], axis=1).astype(BF16)
        w_gate = jnp.concatenate(
            [_reorder_a_heads(cols("ag"), 1), cols("bg"), cols("cg"), cols("mg")], axis=1).astype(BF16)
        zl = _in_projection(h_lat, mod, None, norm_pre[l], w_in_l, PROJ_NAMES, rope_tabs)
        ctx_names = ("ak", "av", "bu", "ck", "cv") if last else PROJ_NAMES
        zc = _in_projection(h_ctx, mod, ctx_row, norm_pre[l], w_in_l, ctx_names, None)

        o_a_l = _swa(swa_sink[l], zl["aq"], zc["ak"], zc["av"], zl["ak"], zl["av"])

        w1, w2, coef = _s5_operators(s5_a_re[l], s5_a_im[l], s5_log_dt[l], s5_b_re[l], s5_b_im[l],
                                     s5_c_re[l], s5_c_im[l], s5_d[l])
        y_b_c, y_b_l = _s5_scan(zc["bu"], zl["bu"], w1, w2, coef, bsz)

        lam_params = jnp.stack([diff_lq1[l], diff_lk1[l], diff_lq2[l], diff_lk2[l]]).astype(F32)
        o_c_l = _diff_attention(lam_params, diff_subln[l], lam_init, zl["cq"], zc["ck"], zc["cv"],
                                zl["ck"], zl["cv"])

        wts = (s5_w_glu[l].astype(BF16), s5_b_glu[l].reshape(1, B_WIDTH),
               _reorder_a_heads(w_o_a[l], 0).astype(BF16), w_o_b[l].astype(BF16), w_o_c[l].astype(BF16),
               w_out[l].astype(BF16), norm_post[l].reshape(1, D_MODEL))
        h_lat_new = _merge(h_lat, mod, None, norm_pre[l], w_gate, o_a_l, y_b_l, o_c_l, wts)
        if not last:
            o_a_c = _swa(swa_sink[l], zc["aq"], zc["ak"], zc["av"])
            o_c_c = _diff_attention(lam_params, diff_subln[l], lam_init, zc["cq"], zc["ck"], zc["cv"])
            h_ctx = _merge(h_ctx, mod, ctx_row, norm_pre[l], w_gate, o_a_c, y_b_c, o_c_c, wts)
        h_lat = h_lat_new
    return h_lat
```

```python
import functools
import math

import jax
import jax.numpy as jnp
from jax import lax
from jax.experimental import pallas as pl
from jax.experimental.pallas import tpu as pltpu

F32 = jnp.float32
BF16 = jnp.bfloat16

D_MODEL = 1024
DEPTH = 2
GRID_W = 64
HEAD_DIM = 64
WINDOW = 128
ROPE_BASE = 10000.0
AX_FREQS = HEAD_DIM // 4
NEG_INF = -1e30
A_HEADS = 8
A_KV_HEADS = 2
A_GROUP = A_HEADS // A_KV_HEADS
A_WIDTH = A_HEADS * HEAD_DIM
A_KV_WIDTH = A_KV_HEADS * HEAD_DIM
B_WIDTH = 512
S5_GROUP = 16
S5_GROUPS = B_WIDTH // S5_GROUP
S5_STATE = 64
C_HEADS = 4
C_WIDTH = C_HEADS * 2 * HEAD_DIM
N_BRANCH = 3
GATE_WIDTH = A_WIDTH + B_WIDTH + C_WIDTH + N_BRANCH * D_MODEL
EPS = 1e-6

LANES = 128
SUBLANES = 8

ROW_TILE = 512
Q_BLOCK = 128
DIFF_KEY_CHUNK = 512
DIFF_LOOKAHEAD = 8
SWA_LOOKAHEAD = 1
SWA_Q_ROWS = 512
DIFF_Q_ROWS = 512
A_HEAD_ORDER = tuple(h for v in range(4) for h in (v, 4 + v))
S5_CHUNK = 16
S5_CW = S5_CHUNK * S5_GROUP
S5_DOT_ROWS = 256
STAGE_PITCH = S5_CHUNK + SUBLANES
VMEM_LIMIT = 56 * 1024 * 1024

_SEG_NAMES = ("aq", "ak", "av", "ag", "bu", "bg", "cq", "ck", "cv", "cg", "mg")
_SEG_WIDTHS = (A_WIDTH, A_KV_WIDTH, A_KV_WIDTH, A_WIDTH, B_WIDTH, B_WIDTH,
               C_WIDTH, C_WIDTH, C_WIDTH, C_WIDTH, 3 * D_MODEL)
SEGS = {}
_off = 0
for _n, _w in zip(_SEG_NAMES, _SEG_WIDTHS):
    SEGS[_n] = (_off, _w)
    _off += _w
IN_WIDTH = _off
PROJ_NAMES = ("aq", "ak", "av", "bu", "cq", "ck", "cv")
Q_SCALE = HEAD_DIM ** -0.5 * math.log2(math.e)
ROPE_SEGS = {"aq": Q_SCALE, "ak": 1.0, "cq": Q_SCALE, "ck": 1.0}


def _const_spec(shape, layer=None):
    nd = len(shape)
    if layer is None:
        return pl.BlockSpec(shape, lambda *_: (0,) * nd, pipeline_mode=pl.Buffered(1))
    return pl.BlockSpec((None,) + shape, lambda *_: (layer,) + (0,) * nd, pipeline_mode=pl.Buffered(1))


def _mod_kernel(c_ref, w_ref, b_ref, o_ref):
    c = c_ref[...]
    a = c * jax.nn.sigmoid(c)
    o_ref[...] = jnp.dot(a, w_ref[...], preferred_element_type=F32) + b_ref[...]


def _modulation(cc, w_mod, b_mod):
    n_tiles = 3
    return pl.pallas_call(
        _mod_kernel,
        grid=(DEPTH, n_tiles),
        in_specs=[
            pl.BlockSpec((16, D_MODEL), lambda l, j: (0, 0)),
            pl.BlockSpec((None, D_MODEL, D_MODEL), lambda l, j: (l, 0, j)),
            pl.BlockSpec((None, 1, D_MODEL), lambda l, j: (l, 0, j)),
        ],
        out_specs=pl.BlockSpec((None, 16, D_MODEL), lambda l, j: (l, 0, j)),
        out_shape=jax.ShapeDtypeStruct((DEPTH, 16, 3 * D_MODEL), F32),
        name="modulation",
    )(cc, w_mod, b_mod.reshape(DEPTH, 1, 3 * D_MODEL))


def _block_transpose8(xs):
    lane_blk = lax.broadcasted_iota(jnp.int32, xs[0].shape, 1) // S5_GROUP
    for s in (4, 2, 1):
        keep = (lane_blk & s) == 0
        out = list(xs)
        for p in range(8):
            if p & s:
                continue
            a, b = xs[p], xs[p + s]
            out[p] = jnp.where(keep, a, pltpu.roll(b, S5_GROUP * s, 1))
            out[p + s] = jnp.where(keep, pltpu.roll(a, LANES - S5_GROUP * s, 1), b)
        xs = out
    return xs


def _tokens_to_groups(z, stage_ref, o_ref):
    n_chunks = z.shape[0] // S5_CHUNK
    for v in range(B_WIDTH // LANES):
        for c in range(n_chunks):
            stage_ref[v, c * STAGE_PITCH:c * STAGE_PITCH + S5_CHUNK, :] = \
                z[c * S5_CHUNK:(c + 1) * S5_CHUNK, v * LANES:(v + 1) * LANES]
    for w in range(S5_CW // LANES):
        for v in range(B_WIDTH // LANES):
            src = [stage_ref[v, pl.ds(8 * w + jj, n_chunks, stride=STAGE_PITCH), :] for jj in range(8)]
            dst = _block_transpose8(src)
            for gg in range(8):
                o_ref[8 * v + gg, :, w * LANES:(w + 1) * LANES] = dst[gg].astype(o_ref.dtype)


def _groups_to_tokens(y_ref, stage_ref):
    n_chunks = y_ref.shape[1]
    for w in range(S5_CW // LANES):
        for v in range(B_WIDTH // LANES):
            src = [y_ref[8 * v + gg, :, w * LANES:(w + 1) * LANES].astype(F32) for gg in range(8)]
            dst = _block_transpose8(src)
            for jj in range(8):
                stage_ref[v, pl.ds(8 * w + jj, n_chunks, stride=STAGE_PITCH), :] = dst[jj]
    rows = []
    for c in range(n_chunks):
        rows.append(jnp.concatenate(
            [stage_ref[v, c * STAGE_PITCH:c * STAGE_PITCH + S5_CHUNK, :] for v in range(B_WIDTH // LANES)], axis=1))
    return jnp.concatenate(rows, axis=0)


def _modulated_norm(x, mod_ref, nw_ref):
    ms = jnp.mean(x * x, axis=-1, keepdims=True)
    xn = x * lax.rsqrt(ms + EPS) * nw_ref[...]
    shift = mod_ref[:, 0:D_MODEL]
    scale = mod_ref[:, D_MODEL:2 * D_MODEL]
    return (xn * (1.0 + scale) + shift).astype(BF16)


def _inproj_kernel(*refs, names, use_rope):
    h_ref, mod_ref, nw_ref, w_ref = refs[:4]
    stage_ref = refs[-1]
    refs = refs[:-1]
    if use_rope:
        cos_ref, sa_ref, sb_ref = refs[4:7]
        outs = refs[7:]
    else:
        outs = refs[4:]
    hn = _modulated_norm(h_ref[...], mod_ref, nw_ref)
    kv_off = SEGS["ak"][0]
    assert SEGS["av"][0] == kv_off + A_KV_WIDTH
    kv = jnp.dot(hn, w_ref[:, kv_off:kv_off + 2 * A_KV_WIDTH], preferred_element_type=F32)

    def project(name):
        off, width = SEGS[name]
        if name in ("ak", "av"):
            return kv[:, off - kv_off:off - kv_off + width]
        return jnp.dot(hn, w_ref[:, off:off + width], preferred_element_type=F32)

    for name, o_ref in zip(names, outs):
        z = project(name)
        mul = ROPE_SEGS.get(name, 1.0)
        if use_rope and name in ROPE_SEGS:
            cos = cos_ref[...]
            sa = sa_ref[...]
            sb = sb_ref[...]
            for c in range(z.shape[1] // LANES):
                t = z[:, c * LANES:(c + 1) * LANES]
                r = t * cos + pltpu.roll(t, LANES - AX_FREQS, 1) * sa + pltpu.roll(t, AX_FREQS, 1) * sb
                if mul != 1.0:
                    r = r * mul
                o_ref[:, c * LANES:(c + 1) * LANES] = r.astype(o_ref.dtype)
        elif name == "bu":
            _tokens_to_groups(z, stage_ref, o_ref)
        elif name == "av":
            for t in range(z.shape[0] // Q_BLOCK):
                o_ref[t] = z[t * Q_BLOCK:(t + 1) * Q_BLOCK, :].T.astype(o_ref.dtype)
        elif name == "cv":
            o_ref[...] = z.T.astype(o_ref.dtype)
        else:
            if mul != 1.0:
                z = z * mul
            o_ref[...] = z.astype(o_ref.dtype)


def _in_projection(h, mod, mod_row, norm_w, w_bf16, layer, names, rope_tabs):
    bsz, rows, _ = h.shape
    tile = min(ROW_TILE, rows)
    use_rope = rope_tabs is not None
    if mod_row is None:
        mod_map = lambda b, i: (b, 0, 0)
    else:
        mod_map = lambda b, i: (mod_row, 0, 0)
    in_specs = [
        pl.BlockSpec((None, tile,D_MODEL), lambda b, i: (b, i, 0)),
        pl.BlockSpec((None, 1, 3 * D_MODEL), mod_map),
        _const_spec((1, D_MODEL)),
        _const_spec((D_MODEL, IN_WIDTH), layer),
    ]
    args = [h, mod, norm_w.reshape(1, D_MODEL), w_bf16]
    if use_rope:
        for t in rope_tabs:
            in_specs.append(pl.BlockSpec((tile, LANES), lambda b, i: (i, 0)))
            args.append(t)
    tiles = rows // tile
    chunk_rows = tile // S5_CHUNK
    out_specs, out_shape = [], []
    for n in names:
        if n == "bu":
            out_specs.append(pl.BlockSpec((S5_GROUPS, chunk_rows, S5_CW), lambda b, i: (0, b * tiles + i, 0)))
            out_shape.append(jax.ShapeDtypeStruct((S5_GROUPS, bsz * rows // S5_CHUNK, S5_CW), BF16))
        elif n == "av":
            kb = tile // Q_BLOCK
            out_specs.append(pl.BlockSpec((None, kb, A_KV_WIDTH, Q_BLOCK), lambda b, i: (b, i, 0, 0)))
            out_shape.append(jax.ShapeDtypeStruct((bsz, rows // Q_BLOCK, A_KV_WIDTH, Q_BLOCK), BF16))
        elif n == "cv":
            out_specs.append(pl.BlockSpec((None, C_WIDTH, tile), lambda b, i: (b, 0, i)))
            out_shape.append(jax.ShapeDtypeStruct((bsz, C_WIDTH, rows), BF16))
        else:
            out_specs.append(pl.BlockSpec((None, tile,SEGS[n][1]), lambda b, i: (b, i, 0)))
            out_shape.append(jax.ShapeDtypeStruct((bsz, rows, SEGS[n][1]), BF16))
    outs = pl.pallas_call(
        functools.partial(_inproj_kernel, names=tuple(names), use_rope=use_rope),
        grid=(bsz, tiles),
        in_specs=in_specs,
        out_specs=out_specs,
        out_shape=out_shape,
        scratch_shapes=[pltpu.VMEM((B_WIDTH // LANES, (tile // S5_CHUNK) * STAGE_PITCH, LANES), F32)],
        compiler_params=pltpu.CompilerParams(
            dimension_semantics=("parallel", "parallel"), vmem_limit_bytes=VMEM_LIMIT),
        name="in_projection_rope" if use_rope else "in_projection",
    )(*args)
    return dict(zip(names, outs))


def _dot_nt(a, b):
    return lax.dot_general(a, b, (((1,), (1,)), ((), ())), preferred_element_type=F32)


def _half_masked(q, lower):
    lane = lax.broadcasted_iota(jnp.int32, q.shape, 1)
    keep = lane < HEAD_DIM if lower else lane >= HEAD_DIM
    return jnp.where(keep, q.astype(F32), 0.0).astype(q.dtype)


def _swa_kernel(*refs, has_local, lat_blocks):
    if has_local:
        sink_ref, q_ref, kc_ref, vtc_ref, kl_ref, vtl_ref, o_ref = refs
    else:
        sink_ref, q_ref, kc_ref, vtc_ref, o_ref = refs
    n_ctx = kc_ref.shape[0]
    n_items = q_ref.shape[0] // Q_BLOCK
    sink = jnp.concatenate(
        [jnp.full((1, Q_BLOCK), sink_ref[kh * A_GROUP + v] * math.log2(math.e), F32)
         for kh in range(A_KV_HEADS) for v in range(A_GROUP)], axis=1)

    def scores(t):
        qs = slice(t * Q_BLOCK, (t + 1) * Q_BLOCK)
        qblk = jnp.concatenate(
            [_half_masked(q_ref[qs, v * LANES:(v + 1) * LANES], kh == 0)
             for kh in range(A_KV_HEADS) for v in range(A_GROUP)], axis=0)
        keys = [kc_ref[...]]
        vts = [vtc_ref[c] for c in range(n_ctx // Q_BLOCK)]
        if has_local:
            n = pl.program_id(1) * n_items + t
            sb = jnp.clip(n - 1, 0, lat_blocks - 3)
            keys.append(kl_ref[pl.ds(pl.multiple_of(sb * Q_BLOCK, Q_BLOCK), 3 * Q_BLOCK), :])
            vts += [vtl_ref[sb + c] for c in range(3)]
        s_t = _dot_nt(jnp.concatenate(keys, axis=0), qblk)
        if has_local:
            r = lax.broadcasted_iota(jnp.int32, (3 * Q_BLOCK, Q_BLOCK), 0)
            i = lax.broadcasted_iota(jnp.int32, (3 * Q_BLOCK, Q_BLOCK), 1)
            in_band = jnp.abs((n - sb) * Q_BLOCK + i - r) <= WINDOW
            bias = jnp.where(in_band, 0.0, NEG_INF)
            s_t = jnp.concatenate([s_t[:n_ctx], s_t[n_ctx:] + jnp.concatenate([bias] * A_HEADS, axis=1)],
                                  axis=0)
        return s_t, jnp.concatenate(vts, axis=1)

    lane = lax.broadcasted_iota(jnp.int32, (Q_BLOCK, LANES), 1)
    half = A_GROUP * Q_BLOCK
    queue = [scores(t) for t in range(min(SWA_LOOKAHEAD, n_items))]
    for t in range(n_items):
        s_t, vt = queue.pop(0)
        if t + SWA_LOOKAHEAD < n_items:
            queue.append(scores(t + SWA_LOOKAHEAD))
        m = jnp.maximum(jnp.max(s_t, axis=0, keepdims=True), sink)
        e_t = jnp.exp2(s_t - m)
        den = jnp.sum(e_t, axis=0, keepdims=True) + jnp.exp2(sink - m)
        p_t = (e_t * (1.0 / den)).astype(BF16)
        o_t = jnp.dot(vt, p_t, preferred_element_type=F32)
        qs = slice(t * Q_BLOCK, (t + 1) * Q_BLOCK)
        for v in range(A_GROUP):
            lower = o_t[:, v * Q_BLOCK:(v + 1) * Q_BLOCK].T
            upper = o_t[:, half + v * Q_BLOCK:half + (v + 1) * Q_BLOCK].T
            o_ref[qs, v * LANES:(v + 1) * LANES] = jnp.where(lane < HEAD_DIM, lower, upper).astype(o_ref.dtype)


def _swa(sink, q, k_ctx, vt_ctx, k_lat=None, vt_lat=None):
    bsz, rows, _ = q.shape
    qrows = min(SWA_Q_ROWS, rows)
    n_ctx = k_ctx.shape[1]
    has_local = k_lat is not None
    in_specs = [
        pl.BlockSpec(memory_space=pltpu.SMEM),
        pl.BlockSpec((None, qrows, A_WIDTH), lambda b, n: (b, n, 0)),
        pl.BlockSpec((None, n_ctx, A_KV_WIDTH), lambda b, n: (b, 0, 0)),
        pl.BlockSpec((None, n_ctx // Q_BLOCK, A_KV_WIDTH, Q_BLOCK), lambda b, n: (b, 0, 0, 0)),
    ]
    args = [sink, q, k_ctx, vt_ctx]
    lat_blocks = 0
    if has_local:
        n_lat = k_lat.shape[1]
        lat_blocks = n_lat // Q_BLOCK
        in_specs += [pl.BlockSpec((None, n_lat, A_KV_WIDTH), lambda b, n: (b, 0, 0)),
                     pl.BlockSpec((None, lat_blocks, A_KV_WIDTH, Q_BLOCK), lambda b, n: (b, 0, 0, 0))]
        args += [k_lat, vt_lat]
    return pl.pallas_call(
        functools.partial(_swa_kernel, has_local=has_local, lat_blocks=lat_blocks),
        grid=(bsz, rows // qrows),
        in_specs=in_specs,
        out_specs=pl.BlockSpec((None, qrows, A_WIDTH), lambda b, n: (b, n, 0)),
        out_shape=jax.ShapeDtypeStruct((bsz, rows, A_WIDTH), F32),
        compiler_params=pltpu.CompilerParams(
            dimension_semantics=("parallel", "arbitrary"), vmem_limit_bytes=VMEM_LIMIT),
        name="swa_latent" if has_local else "swa_context",
    )(*args)


def _diff_kernel(*refs, has_lat, lam_init):
    if has_lat:
        lam_ref, sub_ref, q_ref, kc_ref, vtc_ref, kl_ref, vtl_ref, o_ref = refs
    else:
        lam_ref, sub_ref, q_ref, kc_ref, vtc_ref, o_ref = refs
    lp = lam_ref[...]
    lam = (jnp.exp(jnp.sum(lp[0:1] * lp[1:2], axis=1, keepdims=True))
           - jnp.exp(jnp.sum(lp[2:3] * lp[3:4], axis=1, keepdims=True)) + lam_init)
    sub_w = sub_ref[...] * (1.0 - lam_init)
    width = 2 * HEAD_DIM
    chunks = [(kc_ref, vtc_ref, 0, kc_ref.shape[0])]
    if has_lat:
        chunks += [(kl_ref, vtl_ref, c, DIFF_KEY_CHUNK) for c in range(0, kl_ref.shape[0], DIFF_KEY_CHUNK)]
    heads = [slice(h * width, (h + 1) * width) for h in range(C_HEADS)]
    items = [(hs, slice(qb, qb + Q_BLOCK)) for qb in range(0, q_ref.shape[0], Q_BLOCK) for hs in heads]
    units = [(it, ci) for it in range(len(items)) for ci in range(len(chunks))]
    qblks = {}

    def scores(unit):
        it, ci = unit
        hs, qs = items[it]
        if it not in qblks:
            qblks[it] = jnp.concatenate(
                [_half_masked(q_ref[qs, hs], True), _half_masked(q_ref[qs, hs], False)], axis=0)
        k_ref, _, start, size = chunks[ci]
        return _dot_nt(k_ref[start:start + size, hs], qblks[it])

    queue = [scores(unit) for unit in units[:DIFF_LOOKAHEAD]]
    m = den = acc = None
    for u, (it, ci) in enumerate(units):
        hs, qs = items[it]
        _, vt_ref, start, size = chunks[ci]
        s_t = queue.pop(0)
        if u + DIFF_LOOKAHEAD < len(units):
            queue.append(scores(units[u + DIFF_LOOKAHEAD]))
        m_c = jnp.max(s_t, axis=0, keepdims=True)
        m_new = m_c if ci == 0 else jnp.maximum(m, m_c)
        p_t = jnp.exp2(s_t - m_new)
        den_c = jnp.sum(p_t, axis=0, keepdims=True)
        pv = jnp.dot(vt_ref[hs, start:start + size], p_t.astype(BF16), preferred_element_type=F32)
        if ci == 0:
            den, acc = den_c, pv
        else:
            alpha = jnp.exp2(m - m_new)
            den, acc = den * alpha + den_c, acc * alpha + pv
        m = m_new
        if ci == len(chunks) - 1:
            inv = 1.0 / den
            o_t = acc[:, :Q_BLOCK] * inv[:, :Q_BLOCK] - acc[:, Q_BLOCK:] * (lam * inv[:, Q_BLOCK:])
            o = o_t.T
            ms = jnp.mean(o * o, axis=1, keepdims=True)
            o_ref[qs, hs] = (o * lax.rsqrt(ms + EPS) * sub_w).astype(o_ref.dtype)


def _diff_attention(lam_params, subln, lam_init, q, k_ctx, vt_ctx, k_lat=None, vt_lat=None):
    bsz, rows, _ = q.shape
    qrows = min(DIFF_Q_ROWS, rows)
    n_ctx = k_ctx.shape[1]
    has_lat = k_lat is not None
    in_specs = [
        _const_spec((4, HEAD_DIM)),
        _const_spec((1, 2 * HEAD_DIM)),
        pl.BlockSpec((None, qrows, C_WIDTH), lambda b, n: (b, n, 0)),
        pl.BlockSpec((None, n_ctx, C_WIDTH), lambda b, n: (b, 0, 0)),
        pl.BlockSpec((None, C_WIDTH, n_ctx), lambda b, n: (b, 0, 0)),
    ]
    args = [lam_params, subln.reshape(1, 2 * HEAD_DIM), q, k_ctx, vt_ctx]
    if has_lat:
        n_lat = k_lat.shape[1]
        in_specs += [pl.BlockSpec((None, n_lat, C_WIDTH), lambda b, n: (b, 0, 0)),
                     pl.BlockSpec((None, C_WIDTH, n_lat), lambda b, n: (b, 0, 0))]
        args += [k_lat, vt_lat]
    return pl.pallas_call(
        functools.partial(_diff_kernel, has_lat=has_lat, lam_init=lam_init),
        grid=(bsz, rows // qrows),
        in_specs=in_specs,
        out_specs=pl.BlockSpec((None, qrows, C_WIDTH), lambda b, n: (b, n, 0)),
        out_shape=jax.ShapeDtypeStruct((bsz, rows, C_WIDTH), F32),
        compiler_params=pltpu.CompilerParams(
            dimension_semantics=("parallel", "arbitrary"), vmem_limit_bytes=VMEM_LIMIT),
        name="diff_latent" if has_lat else "diff_context",
    )(*args)


def _s5_operators(a_re, a_im, log_dt, b_re, b_im, c_re, c_im, d_skip):
    f = lambda t: t.astype(F32)
    quad = lambda t: jnp.concatenate([t, t, t, t], axis=-1)
    a_re, a_im, b_re, b_im, c_re, c_im = map(f, (a_re, a_im, b_re, b_im, c_re, c_im))
    g, p, h = a_re.shape[1], S5_STATE, S5_GROUP
    dt = jnp.exp(f(log_dt))[..., None]
    mag = jnp.exp(a_re * dt)
    abar_re = mag * jnp.cos(a_im * dt)
    abar_im = mag * jnp.sin(a_im * dt)
    den = a_re * a_re + a_im * a_im
    nr = abar_re - 1.0
    ni = abar_im
    f_re = (nr * a_re + ni * a_im) / den
    f_im = (ni * a_re - nr * a_im) / den
    pw_re, pw_im = [jnp.ones_like(abar_re)], [jnp.zeros_like(abar_im)]
    for _ in range(S5_CHUNK):
        pre, pim = pw_re[-1], pw_im[-1]
        pw_re.append(pre * abar_re - pim * abar_im)
        pw_im.append(pre * abar_im + pim * abar_re)
    pad = [jnp.zeros_like(abar_re)] * (3 * SUBLANES - S5_CHUNK - 1)
    pw = jnp.stack([jnp.stack(pw_re + pad, axis=2), jnp.stack(pw_im + pad, axis=2)], axis=1)
    pw = jnp.transpose(quad(pw), (2, 0, 1, 3, 4))
    dsk = jnp.tile(f(d_skip).reshape(g, h), (1, S5_CW // h))
    sgn = jnp.broadcast_to(jnp.repeat(jnp.array([-1.0, 1.0, 1.0, -1.0], F32), p), (g, 4 * p))
    zero = jnp.zeros_like(dsk)
    rows = jnp.stack([quad(f_re[0]), quad(f_im[0]), quad(f_re[1]), quad(f_im[1]), dsk, sgn, zero, zero],
                     axis=1)
    bt_re = jnp.swapaxes(b_re, -1, -2)
    bt_im = jnp.swapaxes(b_im, -1, -2)
    bpk = jnp.stack([jnp.concatenate([bt_re, bt_im, bt_im, bt_re], axis=-1),
                     jnp.concatenate([bt_im, bt_re, bt_re, bt_im], axis=-1)], axis=1)
    cpk = jnp.stack([jnp.concatenate([c_re, -c_im], axis=-1),
                     jnp.concatenate([-c_im, -c_re], axis=-1)], axis=1)
    bpk = jnp.transpose(bpk, (2, 0, 1, 3, 4))
    cpk = jnp.transpose(cpk, (2, 0, 1, 3, 4))
    grp = lambda shape: pl.BlockSpec((None,) + shape, lambda i: (i,) + (0,) * len(shape))
    return pl.pallas_call(
        _s5_operator_kernel,
        grid=(g,),
        in_specs=[grp((8, 4 * p)), grp((2, 2, 3 * SUBLANES, 4 * p)), grp((2, 2, h, 4 * p)),
                  grp((2, 2, h, 2 * p))],
        out_specs=[grp((S5_CW, 3 * S5_CW)), grp((4 * p, S5_CW)), grp((SUBLANES, 2 * p))],
        out_shape=[jax.ShapeDtypeStruct((g, S5_CW, 3 * S5_CW), BF16),
                   jax.ShapeDtypeStruct((g, 4 * p, S5_CW), BF16),
                   jax.ShapeDtypeStruct((g, SUBLANES, 2 * p), F32)],
        compiler_params=pltpu.CompilerParams(dimension_semantics=("parallel",)),
        name="s5_operators",
    )(rows, pw, bpk, cpk)


def _shift_lanes(x, s):
    lo, hi = x[:, :LANES], x[:, LANES:]
    lane = lax.broadcasted_iota(jnp.int32, lo.shape, 1)
    zero = jnp.zeros_like(lo)
    rot = lambda t, r: pltpu.roll(t, r, 1) if r % LANES else t
    if s >= 0:
        if s < LANES:
            rl, rh = rot(lo, s), rot(hi, s)
            out = (jnp.where(lane >= s, rl, 0.0), jnp.where(lane >= s, rh, rl))
        else:
            rl = rot(lo, s - LANES)
            out = (zero, jnp.where(lane >= s - LANES, rl, 0.0))
    else:
        s = -s
        if s < LANES:
            rl, rh = rot(lo, LANES - s), rot(hi, LANES - s)
            out = (jnp.where(lane < LANES - s, rl, rh), jnp.where(lane < LANES - s, rh, 0.0))
        else:
            rh = rot(hi, 2 * LANES - s)
            out = (jnp.where(lane < 2 * LANES - s, rh, 0.0), zero)
    return jnp.concatenate(out, axis=1)


def _s5_operator_kernel(rows_ref, pw_ref, b_ref, c_ref, w1_ref, w2_ref, coef_ref):
    t_len = S5_CHUNK
    half = 2 * S5_STATE
    sgn = rows_ref[5:6, :]
    mxu_operand = lambda t: t.astype(BF16).astype(F32)
    a4, bx4, pr4, pi4, g2 = [], [], [], [], []
    for d in range(2):
        f_re = rows_ref[2 * d:2 * d + 1, :]
        f_im = rows_ref[2 * d + 1:2 * d + 2, :]
        p1 = b_ref[d, 0]
        p2 = b_ref[d, 1]
        a4.append(mxu_operand(f_re * p1 + sgn * f_im * p2))
        bx4.append(mxu_operand(f_re * p2 - sgn * f_im * p1))
        pr4.append(pw_ref[d, 0])
        pi4.append(pw_ref[d, 1])
        cx = jnp.concatenate([mxu_operand(c_ref[d, 0])] * t_len, axis=0)
        cy = jnp.concatenate([mxu_operand(c_ref[d, 1])] * t_len, axis=0)

        def ca(order, d=d, cx=cx, cy=cy):
            pr = jnp.concatenate(
                [jnp.broadcast_to(pr4[d][t:t + 1, :half], (S5_GROUP, half)) for t in order], axis=0)
            pi = jnp.concatenate(
                [jnp.broadcast_to(pi4[d][t:t + 1, :half], (S5_GROUP, half)) for t in order], axis=0)
            return cx * pr + cy * pi
        g2.append(ca)

    def strip(d, order):
        return lax.dot_general(a4[d][:, :half], g2[d](order), (((1,), (1,)), ((), ())),
                               precision=lax.Precision.HIGHEST, preferred_element_type=F32)

    row = lax.broadcasted_iota(jnp.int32, (S5_GROUP, S5_CW), 0)
    col = lax.broadcasted_iota(jnp.int32, (S5_GROUP, S5_CW), 1)
    k_fwd = strip(0, range(t_len)) + jnp.where(row == col, rows_ref[4:5, :], 0.0)
    k_bwd = strip(1, [t_len - 1 - k for k in range(t_len)])
    for j in range(t_len):
        lo, hi = j * S5_GROUP, (j + 1) * S5_GROUP
        m = _shift_lanes(k_fwd, j * S5_GROUP) + _shift_lanes(k_bwd, -(t_len - 1 - j) * S5_GROUP)
        w1_ref[lo:hi, 0:S5_CW] = m.astype(w1_ref.dtype)
        tf, tb = t_len - 1 - j, j
        s_f = a4[0] * pr4[0][tf:tf + 1, :] + bx4[0] * (sgn * pi4[0][tf:tf + 1, :])
        s_b = a4[1] * pr4[1][tb:tb + 1, :] + bx4[1] * (sgn * pi4[1][tb:tb + 1, :])
        w1_ref[lo:hi, S5_CW:2 * S5_CW] = s_f.astype(w1_ref.dtype)
        w1_ref[lo:hi, 2 * S5_CW:3 * S5_CW] = s_b.astype(w1_ref.dtype)
    w2_ref[0:half, :] = g2[0]([i + 1 for i in range(t_len)]).T.astype(w2_ref.dtype)
    w2_ref[half:2 * half, :] = g2[1]([t_len - i for i in range(t_len)]).T.astype(w2_ref.dtype)
    zero = jnp.zeros((1, half), F32)
    coef_ref[...] = jnp.concatenate(
        [pr4[0][t_len:t_len + 1, :half], (sgn * pi4[0][t_len:t_len + 1, :])[:, :half],
         pr4[1][t_len:t_len + 1, :half], (sgn * pi4[1][t_len:t_len + 1, :])[:, :half],
         zero, zero, zero, zero], axis=0)


def _s5_kernel(xc_ref, xl_ref, w1_ref, w2_ref, coef_ref, yc_ref, yl_ref,
               zic_ref, zil_ref, zsc_ref, zsl_ref, hc_ref, hl_ref, *, bsz, ctx_chunks, lat_chunks):
    sw = 2 * S5_STATE
    streams = ((xc_ref, zic_ref, zsc_ref, hc_ref, yc_ref, ctx_chunks),
               (xl_ref, zil_ref, zsl_ref, hl_ref, yl_ref, lat_chunks))
    w1 = w1_ref[...]
    for x_ref, zi_ref, zs_ref, _, _, n in streams:
        pitch = n + SUBLANES
        per_dot = min(bsz, max(1, S5_DOT_ROWS // n))
        for b0 in range(0, bsz, per_dot):
            zz = jnp.dot(x_ref[b0 * n:(b0 + per_dot) * n, :], w1, preferred_element_type=F32)
            for b in range(b0, b0 + per_dot):
                z = zz[(b - b0) * n:(b - b0 + 1) * n]
                zi_ref[b * n:(b + 1) * n, :] = z[:, 0:S5_CW]
                for k in range(4):
                    zs_ref[k, b * pitch:b * pitch + n, :] = z[:, S5_CW + k * sw:S5_CW + (k + 1) * sw]

    shape = (bsz, sw)
    a1f = jnp.broadcast_to(coef_ref[0:1, :], shape)
    a2f = jnp.broadcast_to(coef_ref[1:2, :], shape)
    a1b = jnp.broadcast_to(coef_ref[2:3, :], shape)
    a2b = jnp.broadcast_to(coef_ref[3:4, :], shape)

    def locate(c):
        if c < ctx_chunks:
            return zsc_ref, hc_ref, c, ctx_chunks + SUBLANES
        return zsl_ref, hl_ref, c - ctx_chunks, lat_chunks + SUBLANES

    n_chunks = ctx_chunks + lat_chunks
    hf = hfs = hb = hbs = jnp.zeros(shape, F32)
    for t in range(n_chunks):
        zs_ref, h_ref, c, pitch = locate(t)
        rows = pl.ds(c, bsz, stride=pitch)
        h_ref[0, rows, :] = hf
        lf = zs_ref[0, rows, :]
        lfs = zs_ref[1, rows, :]
        hf, hfs = a1f * hf + a2f * hfs + lf, a1f * hfs - a2f * hf + lfs
        cb = ctx_chunks - 1 - t if t < ctx_chunks else n_chunks + ctx_chunks - 1 - t
        zs_ref, h_ref, c, pitch = locate(cb)
        rows = pl.ds(c, bsz, stride=pitch)
        h_ref[1, rows, :] = hb
        lb = zs_ref[2, rows, :]
        lbs = zs_ref[3, rows, :]
        hb, hbs = a1b * hb + a2b * hbs + lb, a1b * hbs - a2b * hb + lbs

    w2 = w2_ref[...]
    for _, zi_ref, _, h_ref, y_ref, n in streams:
        pitch = n + SUBLANES
        per_dot = min(bsz, max(1, S5_DOT_ROWS // n))
        for b0 in range(0, bsz, per_dot):
            hin = jnp.concatenate(
                [jnp.concatenate([h_ref[0, b * pitch:b * pitch + n, :], h_ref[1, b * pitch:b * pitch + n, :]], axis=1)
                 for b in range(b0, b0 + per_dot)], axis=0)
            rows = slice(b0 * n, (b0 + per_dot) * n)
            y = zi_ref[rows, :] + jnp.dot(hin.astype(BF16), w2, preferred_element_type=F32)
            y_ref[rows, :] = y.astype(y_ref.dtype)


def _s5_scan(x_ctx, x_lat, w1, w2, coef, layer, bsz):
    ctx_chunks = x_ctx.shape[1] // bsz
    lat_chunks = x_lat.shape[1] // bsz
    sw = 2 * S5_STATE
    grp = lambda rows, cols: pl.BlockSpec((None, rows, cols), lambda g: (g, 0, 0))
    opr = lambda rows, cols: pl.BlockSpec((None, None, rows, cols), lambda g: (layer, g, 0, 0))
    slab = lambda k, n: pltpu.VMEM((k, bsz * (n + SUBLANES), sw), F32)
    return pl.pallas_call(
        functools.partial(_s5_kernel, bsz=bsz, ctx_chunks=ctx_chunks, lat_chunks=lat_chunks),
        grid=(S5_GROUPS,),
        in_specs=[
            grp(bsz * ctx_chunks, S5_CW), grp(bsz * lat_chunks, S5_CW),
            opr(S5_CW, 3 * S5_CW), opr(4 * S5_STATE, S5_CW), opr(SUBLANES, sw),
        ],
        out_specs=[grp(bsz * ctx_chunks, S5_CW), grp(bsz * lat_chunks, S5_CW)],
        out_shape=[jax.ShapeDtypeStruct(x_ctx.shape, F32), jax.ShapeDtypeStruct(x_lat.shape, F32)],
        scratch_shapes=[
            pltpu.VMEM((bsz * ctx_chunks, S5_CW), F32), pltpu.VMEM((bsz * lat_chunks, S5_CW), F32),
            slab(4, ctx_chunks), slab(4, lat_chunks), slab(2, ctx_chunks), slab(2, lat_chunks),
        ],
        compiler_params=pltpu.CompilerParams(
            dimension_semantics=("parallel",), vmem_limit_bytes=VMEM_LIMIT),
        name="s5_scan",
    )(x_ctx, x_lat, w1, w2, coef)


def _gelu_tanh(x):
    return 0.5 * x * (1.0 + jnp.tanh(math.sqrt(2.0 / math.pi) * (x + 0.044715 * (x * x * x))))


def _silu(x):
    return x * jax.nn.sigmoid(x)


def _merge_kernel(h_ref, mod_ref, npre_ref, wg_ref, oa_ref, yb_ref, oc_ref,
                  wglu_ref, bglu_ref, woa_ref, wob_ref, woc_ref, wout_ref, npost_ref, o_ref, stage_ref):
    hn = _modulated_norm(h_ref[...], mod_ref, npre_ref)

    def branch(i, o, w_ref):
        g = jnp.dot(hn, wg_ref[:, i * A_WIDTH:(i + 1) * A_WIDTH], preferred_element_type=F32)
        t = jnp.dot((o * _silu(g)).astype(BF16), w_ref[...], preferred_element_type=F32)
        m_lo = N_BRANCH * A_WIDTH + i * D_MODEL
        gate = jax.nn.sigmoid(jnp.dot(hn, wg_ref[:, m_lo:m_lo + D_MODEL], preferred_element_type=F32))
        return gate * t

    y = branch(0, oa_ref[...], woa_ref) + branch(2, oc_ref[...], woc_ref)
    yb = _gelu_tanh(_groups_to_tokens(yb_ref, stage_ref))
    glu = jnp.dot(yb.astype(BF16), wglu_ref[...], preferred_element_type=F32) + bglu_ref[...]
    y = y + branch(1, yb * jax.nn.sigmoid(glu), wob_ref)
    y = jnp.dot(y.astype(BF16), wout_ref[...], preferred_element_type=F32)
    ms = jnp.mean(y * y, axis=-1, keepdims=True)
    yn = y * lax.rsqrt(ms + EPS) * npost_ref[...]
    gate = mod_ref[:, 2 * D_MODEL:3 * D_MODEL]
    o_ref[...] = h_ref[...] + gate * yn


def _merge(h, mod, mod_row, norm_pre, layer, w_gate, o_a, y_b, o_c, wts):
    bsz, rows, _ = h.shape
    tile = min(ROW_TILE, rows)
    tiles = rows // tile
    if mod_row is None:
        mod_map = lambda b, i: (b, 0, 0)
    else:
        mod_map = lambda b, i: (mod_row, 0, 0)
    tok = lambda w: pl.BlockSpec((None, tile,w), lambda b, i: (b, i, 0))
    in_specs = [
        tok(D_MODEL),
        pl.BlockSpec((None, 1, 3 * D_MODEL), mod_map),
        _const_spec((1, D_MODEL)), _const_spec((D_MODEL, GATE_WIDTH), layer),
        tok(A_WIDTH),
        pl.BlockSpec((S5_GROUPS, tile // S5_CHUNK, S5_CW), lambda b, i: (0, b * tiles + i, 0)),
        tok(C_WIDTH),
        _const_spec((B_WIDTH, B_WIDTH), layer), _const_spec((1, B_WIDTH)),
        _const_spec((A_WIDTH, D_MODEL), layer), _const_spec((B_WIDTH, D_MODEL), layer),
        _const_spec((C_WIDTH, D_MODEL), layer),
        _const_spec((D_MODEL, D_MODEL), layer), _const_spec((1, D_MODEL)),
    ]
    return pl.pallas_call(
        _merge_kernel,
        grid=(bsz, rows // tile),
        in_specs=in_specs,
        out_specs=tok(D_MODEL),
        out_shape=jax.ShapeDtypeStruct((bsz, rows, D_MODEL), F32),
        scratch_shapes=[pltpu.VMEM((B_WIDTH // LANES, (tile // S5_CHUNK) * STAGE_PITCH, LANES), F32)],
        compiler_params=pltpu.CompilerParams(
            dimension_semantics=("parallel", "parallel"), vmem_limit_bytes=VMEM_LIMIT),
        name="merge_out",
    )(h, mod, norm_pre.reshape(1, D_MODEL), w_gate, o_a, y_b, o_c, *wts)


def _rope_tables(n_tokens):
    rows = n_tokens // GRID_W
    pos_r = jnp.repeat(jnp.arange(rows, dtype=F32), GRID_W)
    pos_c = jnp.tile(jnp.arange(GRID_W, dtype=F32), rows)
    inv = ROPE_BASE ** (-jnp.arange(AX_FREQS, dtype=F32) / AX_FREQS)
    ang_r = pos_r[:, None] * inv[None]
    ang_c = pos_c[:, None] * inv[None]
    ang = jnp.concatenate([ang_r, ang_r, ang_c, ang_c], axis=-1)
    cos, sin = jnp.cos(ang), jnp.sin(ang)
    first = (jnp.arange(HEAD_DIM) % (2 * AX_FREQS)) < AX_FREQS
    sa = jnp.where(first[None], -sin, 0.0)
    sb = jnp.where(first[None], 0.0, sin)
    tile2 = lambda t: jnp.concatenate([t, t], axis=-1)
    return tile2(cos), tile2(sa), tile2(sb)


def _reorder_a_heads(w, axis):
    axis = axis % w.ndim
    shape = w.shape
    split = shape[:axis] + (A_HEADS, HEAD_DIM) + shape[axis + 1:]
    return jnp.take(w.reshape(split), jnp.array(A_HEAD_ORDER), axis=axis).reshape(shape)


def _layers_as_groups(t):
    t = jnp.swapaxes(t, 0, 1)
    return t.reshape((2, DEPTH * S5_GROUPS) + t.shape[3:])


def kernel(x, c, ctx, c_ctx, w_mod, b_mod, norm_pre, norm_post, w_in, swa_sink, s5_a_re, s5_a_im, s5_log_dt, s5_b_re, s5_b_im, s5_c_re, s5_c_im, s5_d, s5_w_glu, s5_b_glu, diff_lq1, diff_lk1, diff_lq2, diff_lk2, diff_subln, w_o_a, w_o_b, w_o_c, w_out):
    bsz, n_lat, _ = x.shape
    n_ctx = ctx.shape[1]
    ctx_row = bsz
    cc = jnp.zeros((16, D_MODEL), F32).at[:bsz].set(c).at[ctx_row].set(c_ctx)
    mod_all = _modulation(cc, w_mod, b_mod)
    rope_tabs = _rope_tables(n_lat)

    cols = lambda name: w_in[..., SEGS[name][0]:SEGS[name][0] + SEGS[name][1]]
    aq0, aqw = SEGS["aq"]
    w_proj_all = jnp.concatenate([_reorder_a_heads(cols("aq"), -1), w_in[..., aq0 + aqw:]], axis=-1).astype(BF16)
    w_gate_all = jnp.concatenate(
        [_reorder_a_heads(cols("ag"), -1), cols("bg"), cols("cg"), cols("mg")], axis=-1).astype(BF16)
    w_glu_all, w_ob_all, w_oc_all, w_out_all = (t.astype(BF16) for t in (s5_w_glu, w_o_b, w_o_c, w_out))
    w_oa_all = _reorder_a_heads(w_o_a, -2).astype(BF16)
    lam_all = jnp.stack([diff_lq1, diff_lk1, diff_lq2, diff_lk2], axis=1).astype(F32)
    s5_ops = _s5_operators(*(_layers_as_groups(t) for t in (s5_a_re, s5_a_im, s5_log_dt, s5_b_re, s5_b_im,
                                                            s5_c_re, s5_c_im)), s5_d.reshape(-1))
    w1_all, w2_all, coef_all = (t.reshape((DEPTH, S5_GROUPS) + t.shape[1:]) for t in s5_ops)

    h_lat, h_ctx = x, ctx
    for l in range(DEPTH):
        last = l == DEPTH - 1
        lam_init = 0.8 - 0.6 * math.exp(-0.3 * l)
        mod = mod_all[l].reshape(16, 1, 3 * D_MODEL)
        zl = _in_projection(h_lat, mod, None, norm_pre[l], w_proj_all, l, PROJ_NAMES, rope_tabs)
        ctx_names = ("ak", "av", "bu", "ck", "cv") if last else PROJ_NAMES
        zc = _in_projection(h_ctx, mod, ctx_row, norm_pre[l], w_proj_all, l, ctx_names, None)

        o_a_l = _swa(swa_sink[l], zl["aq"], zc["ak"], zc["av"], zl["ak"], zl["av"])

        y_b_c, y_b_l = _s5_scan(zc["bu"], zl["bu"], w1_all, w2_all, coef_all, l, bsz)

        lam_params = lam_all[l]
        o_c_l = _diff_attention(lam_params, diff_subln[l], lam_init, zl["cq"], zc["ck"], zc["cv"],
                                zl["ck"], zl["cv"])

        wts = (w_glu_all, s5_b_glu[l].reshape(1, B_WIDTH), w_oa_all, w_ob_all, w_oc_all,
               w_out_all, norm_post[l].reshape(1, D_MODEL))
        h_lat_new = _merge(h_lat, mod, None, norm_pre[l], l, w_gate_all, o_a_l, y_b_l, o_c_l, wts)
        if not last:
            o_a_c = _swa(swa_sink[l], zc["aq"], zc["ak"], zc["av"])
            o_c_c = _diff_attention(lam_params, diff_subln[l], lam_init, zc["cq"], zc["ck"], zc["cv"])
            h_ctx = _merge(h_ctx, mod, ctx_row, norm_pre[l], l, w_gate_all, o_a_c, y_b_c, o_c_c, wts)
        h_lat = h_lat_new
    return h_lat
```

```python
import functools
import math

import jax
import jax.numpy as jnp
from jax import lax
from jax.experimental import pallas as pl
from jax.experimental.pallas import tpu as pltpu

F32 = jnp.float32
BF16 = jnp.bfloat16

D_MODEL = 1024
DEPTH = 2
GRID_W = 64
HEAD_DIM = 64
WINDOW = 128
ROPE_BASE = 10000.0
AX_FREQS = HEAD_DIM // 4
NEG_INF = -1e30
A_HEADS = 8
A_KV_HEADS = 2
A_GROUP = A_HEADS // A_KV_HEADS
A_WIDTH = A_HEADS * HEAD_DIM
A_KV_WIDTH = A_KV_HEADS * HEAD_DIM
B_WIDTH = 512
S5_GROUP = 16
S5_GROUPS = B_WIDTH // S5_GROUP
S5_STATE = 64
C_HEADS = 4
C_WIDTH = C_HEADS * 2 * HEAD_DIM
EPS = 1e-6

LANES = 128
SUBLANES = 8

ROW_TILE = 512
Q_BLOCK = 128
DIFF_KEY_CHUNK = 512
DIFF_LOOKAHEAD = 8
SWA_LOOKAHEAD = 1
SWA_Q_ROWS = 512
DIFF_Q_ROWS = 512
A_HEAD_ORDER = tuple(h for v in range(4) for h in (v, 4 + v))
S5_CHUNK = 16
S5_CW = S5_CHUNK * S5_GROUP
S5_DOT_ROWS = 256
STAGE_PITCH = S5_CHUNK + SUBLANES
VMEM_LIMIT = 56 * 1024 * 1024

_SEG_NAMES = ("aq", "ak", "av", "ag", "bu", "bg", "cq", "ck", "cv", "cg", "mg")
_SEG_WIDTHS = (A_WIDTH, A_KV_WIDTH, A_KV_WIDTH, A_WIDTH, B_WIDTH, B_WIDTH,
               C_WIDTH, C_WIDTH, C_WIDTH, C_WIDTH, 3 * D_MODEL)
SEGS = {}
_off = 0
for _n, _w in zip(_SEG_NAMES, _SEG_WIDTHS):
    SEGS[_n] = (_off, _w)
    _off += _w
IN_WIDTH = _off
PROJ_NAMES = ("aq", "ak", "av", "bu", "cq", "ck", "cv")
Q_SCALE = HEAD_DIM ** -0.5 * math.log2(math.e)
ROPE_SEGS = {"aq": Q_SCALE, "ak": 1.0, "cq": Q_SCALE, "ck": 1.0}


def _const_spec(shape, layer=None):
    nd = len(shape)
    if layer is None:
        return pl.BlockSpec(shape, lambda *_: (0,) * nd, pipeline_mode=pl.Buffered(1))
    return pl.BlockSpec((None,) + shape, lambda *_: (layer,) + (0,) * nd, pipeline_mode=pl.Buffered(1))


def _mod_kernel(c_ref, w_ref, b_ref, o_ref):
    c = c_ref[...]
    a = c * jax.nn.sigmoid(c)
    o_ref[...] = jnp.dot(a, w_ref[...], preferred_element_type=F32) + b_ref[...]


def _modulation(cc, w_mod, b_mod):
    n_tiles = 3
    return pl.pallas_call(
        _mod_kernel,
        grid=(DEPTH, n_tiles),
        in_specs=[
            pl.BlockSpec((16, D_MODEL), lambda l, j: (0, 0)),
            pl.BlockSpec((None, D_MODEL, D_MODEL), lambda l, j: (l, 0, j)),
            pl.BlockSpec((None, 1, D_MODEL), lambda l, j: (l, 0, j)),
        ],
        out_specs=pl.BlockSpec((None, 16, D_MODEL), lambda l, j: (l, 0, j)),
        out_shape=jax.ShapeDtypeStruct((DEPTH, 16, 3 * D_MODEL), F32),
        name="modulation",
    )(cc, w_mod, b_mod.reshape(DEPTH, 1, 3 * D_MODEL))


def _block_transpose8(xs):
    lane_blk = lax.broadcasted_iota(jnp.int32, xs[0].shape, 1) // S5_GROUP
    for s in (4, 2, 1):
        keep = (lane_blk & s) == 0
        out = list(xs)
        for p in range(8):
            if p & s:
                continue
            a, b = xs[p], xs[p + s]
            out[p] = jnp.where(keep, a, pltpu.roll(b, S5_GROUP * s, 1))
            out[p + s] = jnp.where(keep, pltpu.roll(a, LANES - S5_GROUP * s, 1), b)
        xs = out
    return xs


def _tokens_to_groups(z, stage_ref, o_ref):
    n_chunks = z.shape[0] // S5_CHUNK
    for v in range(B_WIDTH // LANES):
        for c in range(n_chunks):
            stage_ref[v, c * STAGE_PITCH:c * STAGE_PITCH + S5_CHUNK, :] = \
                z[c * S5_CHUNK:(c + 1) * S5_CHUNK, v * LANES:(v + 1) * LANES]
    for w in range(S5_CW // LANES):
        for v in range(B_WIDTH // LANES):
            src = [stage_ref[v, pl.ds(8 * w + jj, n_chunks, stride=STAGE_PITCH), :] for jj in range(8)]
            dst = _block_transpose8(src)
            for gg in range(8):
                o_ref[8 * v + gg, :, w * LANES:(w + 1) * LANES] = dst[gg].astype(o_ref.dtype)


def _groups_to_tokens(y_ref, stage_ref):
    n_chunks = y_ref.shape[1]
    for w in range(S5_CW // LANES):
        for v in range(B_WIDTH // LANES):
            src = [y_ref[8 * v + gg, :, w * LANES:(w + 1) * LANES].astype(F32) for gg in range(8)]
            dst = _block_transpose8(src)
            for jj in range(8):
                stage_ref[v, pl.ds(8 * w + jj, n_chunks, stride=STAGE_PITCH), :] = dst[jj]
    rows = []
    for c in range(n_chunks):
        rows.append(jnp.concatenate(
            [stage_ref[v, c * STAGE_PITCH:c * STAGE_PITCH + S5_CHUNK, :] for v in range(B_WIDTH // LANES)], axis=1))
    return jnp.concatenate(rows, axis=0)


def _modulated_norm(x, mod_ref, nw_ref):
    ms = jnp.mean(x * x, axis=-1, keepdims=True)
    xn = x * lax.rsqrt(ms + EPS) * nw_ref[...]
    shift = mod_ref[:, 0:D_MODEL]
    scale = mod_ref[:, D_MODEL:2 * D_MODEL]
    return (xn * (1.0 + scale) + shift).astype(BF16)


def _inproj_kernel(*refs, names, use_rope):
    h_ref, mod_ref, nw_ref, w_ref, wa_ref = refs[:5]
    stage_ref = refs[-1]
    refs = refs[:-1]
    if use_rope:
        cos_ref, sa_ref, sb_ref = refs[5:8]
        outs = refs[8:]
    else:
        outs = refs[5:]
    hn = _modulated_norm(h_ref[...], mod_ref, nw_ref)
    kv_off = SEGS["ak"][0]
    assert SEGS["av"][0] == kv_off + A_KV_WIDTH
    kv = jnp.dot(hn, w_ref[:, kv_off:kv_off + 2 * A_KV_WIDTH], preferred_element_type=F32)

    def project(name):
        off, width = SEGS[name]
        if name in ("ak", "av"):
            return kv[:, off - kv_off:off - kv_off + width]
        if name == "aq":
            return jnp.dot(hn, wa_ref[:, 0:width], preferred_element_type=F32)
        return jnp.dot(hn, w_ref[:, off:off + width], preferred_element_type=F32)

    for name, o_ref in zip(names, outs):
        z = project(name)
        mul = ROPE_SEGS.get(name, 1.0)
        if use_rope and name in ROPE_SEGS:
            cos = cos_ref[...]
            sa = sa_ref[...]
            sb = sb_ref[...]
            for c in range(z.shape[1] // LANES):
                t = z[:, c * LANES:(c + 1) * LANES]
                r = t * cos + pltpu.roll(t, LANES - AX_FREQS, 1) * sa + pltpu.roll(t, AX_FREQS, 1) * sb
                if mul != 1.0:
                    r = r * mul
                o_ref[:, c * LANES:(c + 1) * LANES] = r.astype(o_ref.dtype)
        elif name == "bu":
            _tokens_to_groups(z, stage_ref, o_ref)
        elif name == "av":
            for t in range(z.shape[0] // Q_BLOCK):
                o_ref[t] = z[t * Q_BLOCK:(t + 1) * Q_BLOCK, :].T.astype(o_ref.dtype)
        elif name == "cv":
            o_ref[...] = z.T.astype(o_ref.dtype)
        else:
            if mul != 1.0:
                z = z * mul
            o_ref[...] = z.astype(o_ref.dtype)


def _in_projection(h, mod, mod_row, norm_w, w_bf16, w_a, layer, names, rope_tabs):
    bsz, rows, _ = h.shape
    tile = min(ROW_TILE, rows)
    use_rope = rope_tabs is not None
    if mod_row is None:
        mod_map = lambda b, i: (b, 0, 0)
    else:
        mod_map = lambda b, i: (mod_row, 0, 0)
    in_specs = [
        pl.BlockSpec((None, tile,D_MODEL), lambda b, i: (b, i, 0)),
        pl.BlockSpec((None, 1, 3 * D_MODEL), mod_map),
        _const_spec((1, D_MODEL)),
        _const_spec((D_MODEL, IN_WIDTH), layer),
        _const_spec((D_MODEL, 2 * A_WIDTH), layer),
    ]
    args = [h, mod, norm_w.reshape(1, D_MODEL), w_bf16, w_a]
    if use_rope:
        for t in rope_tabs:
            in_specs.append(pl.BlockSpec((tile, LANES), lambda b, i: (i, 0)))
            args.append(t)
    tiles = rows // tile
    chunk_rows = tile // S5_CHUNK
    out_specs, out_shape = [], []
    for n in names:
        if n == "bu":
            out_specs.append(pl.BlockSpec((S5_GROUPS, chunk_rows, S5_CW), lambda b, i: (0, b * tiles + i, 0)))
            out_shape.append(jax.ShapeDtypeStruct((S5_GROUPS, bsz * rows // S5_CHUNK, S5_CW), BF16))
        elif n == "av":
            kb = tile // Q_BLOCK
            out_specs.append(pl.BlockSpec((None, kb, A_KV_WIDTH, Q_BLOCK), lambda b, i: (b, i, 0, 0)))
            out_shape.append(jax.ShapeDtypeStruct((bsz, rows // Q_BLOCK, A_KV_WIDTH, Q_BLOCK), BF16))
        elif n == "cv":
            out_specs.append(pl.BlockSpec((None, C_WIDTH, tile), lambda b, i: (b, 0, i)))
            out_shape.append(jax.ShapeDtypeStruct((bsz, C_WIDTH, rows), BF16))
        else:
            out_specs.append(pl.BlockSpec((None, tile,SEGS[n][1]), lambda b, i: (b, i, 0)))
            out_shape.append(jax.ShapeDtypeStruct((bsz, rows, SEGS[n][1]), BF16))
    outs = pl.pallas_call(
        functools.partial(_inproj_kernel, names=tuple(names), use_rope=use_rope),
        grid=(bsz, tiles),
        in_specs=in_specs,
        out_specs=out_specs,
        out_shape=out_shape,
        scratch_shapes=[pltpu.VMEM((B_WIDTH // LANES, (tile // S5_CHUNK) * STAGE_PITCH, LANES), F32)],
        compiler_params=pltpu.CompilerParams(
            dimension_semantics=("parallel", "parallel"), vmem_limit_bytes=VMEM_LIMIT),
        name="in_projection_rope" if use_rope else "in_projection",
    )(*args)
    return dict(zip(names, outs))


def _dot_nt(a, b):
    return lax.dot_general(a, b, (((1,), (1,)), ((), ())), preferred_element_type=F32)


def _half_masked(q, lower):
    lane = lax.broadcasted_iota(jnp.int32, q.shape, 1)
    keep = lane < HEAD_DIM if lower else lane >= HEAD_DIM
    return jnp.where(keep, q.astype(F32), 0.0).astype(q.dtype)


def _swa_kernel(*refs, has_local, lat_blocks):
    if has_local:
        sink_ref, q_ref, kc_ref, vtc_ref, kl_ref, vtl_ref, o_ref = refs
    else:
        sink_ref, q_ref, kc_ref, vtc_ref, o_ref = refs
    n_ctx = kc_ref.shape[0]
    n_items = q_ref.shape[0] // Q_BLOCK
    sink = jnp.concatenate(
        [jnp.full((1, Q_BLOCK), sink_ref[kh * A_GROUP + v] * math.log2(math.e), F32)
         for kh in range(A_KV_HEADS) for v in range(A_GROUP)], axis=1)

    def scores(t):
        qs = slice(t * Q_BLOCK, (t + 1) * Q_BLOCK)
        qblk = jnp.concatenate(
            [_half_masked(q_ref[qs, v * LANES:(v + 1) * LANES], kh == 0)
             for kh in range(A_KV_HEADS) for v in range(A_GROUP)], axis=0)
        keys = [kc_ref[...]]
        vts = [vtc_ref[c] for c in range(n_ctx // Q_BLOCK)]
        if has_local:
            n = pl.program_id(1) * n_items + t
            sb = jnp.clip(n - 1, 0, lat_blocks - 3)
            keys.append(kl_ref[pl.ds(pl.multiple_of(sb * Q_BLOCK, Q_BLOCK), 3 * Q_BLOCK), :])
            vts += [vtl_ref[sb + c] for c in range(3)]
        s_t = _dot_nt(jnp.concatenate(keys, axis=0), qblk)
        if has_local:
            r = lax.broadcasted_iota(jnp.int32, (3 * Q_BLOCK, Q_BLOCK), 0)
            i = lax.broadcasted_iota(jnp.int32, (3 * Q_BLOCK, Q_BLOCK), 1)
            in_band = jnp.abs((n - sb) * Q_BLOCK + i - r) <= WINDOW
            bias = jnp.where(in_band, 0.0, NEG_INF)
            s_t = jnp.concatenate([s_t[:n_ctx], s_t[n_ctx:] + jnp.concatenate([bias] * A_HEADS, axis=1)],
                                  axis=0)
        return s_t, jnp.concatenate(vts, axis=1)

    lane = lax.broadcasted_iota(jnp.int32, (Q_BLOCK, LANES), 1)
    half = A_GROUP * Q_BLOCK
    queue = [scores(t) for t in range(min(SWA_LOOKAHEAD, n_items))]
    for t in range(n_items):
        s_t, vt = queue.pop(0)
        if t + SWA_LOOKAHEAD < n_items:
            queue.append(scores(t + SWA_LOOKAHEAD))
        m = jnp.maximum(jnp.max(s_t, axis=0, keepdims=True), sink)
        e_t = jnp.exp2(s_t - m)
        den = jnp.sum(e_t, axis=0, keepdims=True) + jnp.exp2(sink - m)
        p_t = (e_t * (1.0 / den)).astype(BF16)
        o_t = jnp.dot(vt, p_t, preferred_element_type=F32)
        qs = slice(t * Q_BLOCK, (t + 1) * Q_BLOCK)
        for v in range(A_GROUP):
            lower = o_t[:, v * Q_BLOCK:(v + 1) * Q_BLOCK].T
            upper = o_t[:, half + v * Q_BLOCK:half + (v + 1) * Q_BLOCK].T
            o_ref[qs, v * LANES:(v + 1) * LANES] = jnp.where(lane < HEAD_DIM, lower, upper).astype(o_ref.dtype)


def _swa(sink, q, k_ctx, vt_ctx, k_lat=None, vt_lat=None):
    bsz, rows, _ = q.shape
    qrows = min(SWA_Q_ROWS, rows)
    n_ctx = k_ctx.shape[1]
    has_local = k_lat is not None
    in_specs = [
        pl.BlockSpec(memory_space=pltpu.SMEM),
        pl.BlockSpec((None, qrows, A_WIDTH), lambda b, n: (b, n, 0)),
        pl.BlockSpec((None, n_ctx, A_KV_WIDTH), lambda b, n: (b, 0, 0)),
        pl.BlockSpec((None, n_ctx // Q_BLOCK, A_KV_WIDTH, Q_BLOCK), lambda b, n: (b, 0, 0, 0)),
    ]
    args = [sink, q, k_ctx, vt_ctx]
    lat_blocks = 0
    if has_local:
        n_lat = k_lat.shape[1]
        lat_blocks = n_lat // Q_BLOCK
        in_specs += [pl.BlockSpec((None, n_lat, A_KV_WIDTH), lambda b, n: (b, 0, 0)),
                     pl.BlockSpec((None, lat_blocks, A_KV_WIDTH, Q_BLOCK), lambda b, n: (b, 0, 0, 0))]
        args += [k_lat, vt_lat]
    return pl.pallas_call(
        functools.partial(_swa_kernel, has_local=has_local, lat_blocks=lat_blocks),
        grid=(bsz, rows // qrows),
        in_specs=in_specs,
        out_specs=pl.BlockSpec((None, qrows, A_WIDTH), lambda b, n: (b, n, 0)),
        out_shape=jax.ShapeDtypeStruct((bsz, rows, A_WIDTH), F32),
        compiler_params=pltpu.CompilerParams(
            dimension_semantics=("parallel", "arbitrary"), vmem_limit_bytes=VMEM_LIMIT),
        name="swa_latent" if has_local else "swa_context",
    )(*args)


def _diff_kernel(*refs, has_lat, lam_init):
    if has_lat:
        lam_ref, sub_ref, q_ref, kc_ref, vtc_ref, kl_ref, vtl_ref, o_ref = refs
    else:
        lam_ref, sub_ref, q_ref, kc_ref, vtc_ref, o_ref = refs
    lp = lam_ref[...]
    lam = (jnp.exp(jnp.sum(lp[0:1] * lp[1:2], axis=1, keepdims=True))
           - jnp.exp(jnp.sum(lp[2:3] * lp[3:4], axis=1, keepdims=True)) + lam_init)
    sub_w = sub_ref[...] * (1.0 - lam_init)
    width = 2 * HEAD_DIM
    chunks = [(kc_ref, vtc_ref, 0, kc_ref.shape[0])]
    if has_lat:
        chunks += [(kl_ref, vtl_ref, c, DIFF_KEY_CHUNK) for c in range(0, kl_ref.shape[0], DIFF_KEY_CHUNK)]
    heads = [slice(h * width, (h + 1) * width) for h in range(C_HEADS)]
    items = [(hs, slice(qb, qb + Q_BLOCK)) for qb in range(0, q_ref.shape[0], Q_BLOCK) for hs in heads]
    units = [(it, ci) for it in range(len(items)) for ci in range(len(chunks))]
    qblks = {}

    def scores(unit):
        it, ci = unit
        hs, qs = items[it]
        if it not in qblks:
            qblks[it] = jnp.concatenate(
                [_half_masked(q_ref[qs, hs], True), _half_masked(q_ref[qs, hs], False)], axis=0)
        k_ref, _, start, size = chunks[ci]
        return _dot_nt(k_ref[start:start + size, hs], qblks[it])

    queue = [scores(unit) for unit in units[:DIFF_LOOKAHEAD]]
    m = den = acc = None
    for u, (it, ci) in enumerate(units):
        hs, qs = items[it]
        _, vt_ref, start, size = chunks[ci]
        s_t = queue.pop(0)
        if u + DIFF_LOOKAHEAD < len(units):
            queue.append(scores(units[u + DIFF_LOOKAHEAD]))
        m_c = jnp.max(s_t, axis=0, keepdims=True)
        m_new = m_c if ci == 0 else jnp.maximum(m, m_c)
        p_t = jnp.exp2(s_t - m_new)
        den_c = jnp.sum(p_t, axis=0, keepdims=True)
        pv = jnp.dot(vt_ref[hs, start:start + size], p_t.astype(BF16), preferred_element_type=F32)
        if ci == 0:
            den, acc = den_c, pv
        else:
            alpha = jnp.exp2(m - m_new)
            den, acc = den * alpha + den_c, acc * alpha + pv
        m = m_new
        if ci == len(chunks) - 1:
            inv = 1.0 / den
            o_t = acc[:, :Q_BLOCK] * inv[:, :Q_BLOCK] - acc[:, Q_BLOCK:] * (lam * inv[:, Q_BLOCK:])
            o = o_t.T
            ms = jnp.mean(o * o, axis=1, keepdims=True)
            o_ref[qs, hs] = (o * lax.rsqrt(ms + EPS) * sub_w).astype(o_ref.dtype)


def _diff_attention(lam_params, subln, lam_init, q, k_ctx, vt_ctx, k_lat=None, vt_lat=None):
    bsz, rows, _ = q.shape
    qrows = min(DIFF_Q_ROWS, rows)
    n_ctx = k_ctx.shape[1]
    has_lat = k_lat is not None
    in_specs = [
        _const_spec((4, HEAD_DIM)),
        _const_spec((1, 2 * HEAD_DIM)),
        pl.BlockSpec((None, qrows, C_WIDTH), lambda b, n: (b, n, 0)),
        pl.BlockSpec((None, n_ctx, C_WIDTH), lambda b, n: (b, 0, 0)),
        pl.BlockSpec((None, C_WIDTH, n_ctx), lambda b, n: (b, 0, 0)),
    ]
    args = [lam_params, subln.reshape(1, 2 * HEAD_DIM), q, k_ctx, vt_ctx]
    if has_lat:
        n_lat = k_lat.shape[1]
        in_specs += [pl.BlockSpec((None, n_lat, C_WIDTH), lambda b, n: (b, 0, 0)),
                     pl.BlockSpec((None, C_WIDTH, n_lat), lambda b, n: (b, 0, 0))]
        args += [k_lat, vt_lat]
    return pl.pallas_call(
        functools.partial(_diff_kernel, has_lat=has_lat, lam_init=lam_init),
        grid=(bsz, rows // qrows),
        in_specs=in_specs,
        out_specs=pl.BlockSpec((None, qrows, C_WIDTH), lambda b, n: (b, n, 0)),
        out_shape=jax.ShapeDtypeStruct((bsz, rows, C_WIDTH), F32),
        compiler_params=pltpu.CompilerParams(
            dimension_semantics=("parallel", "arbitrary"), vmem_limit_bytes=VMEM_LIMIT),
        name="diff_latent" if has_lat else "diff_context",
    )(*args)


def _s5_operators(a_re, a_im, log_dt, b_re, b_im, c_re, c_im, d_skip):
    f = lambda t: t.astype(F32)
    quad = lambda t: jnp.concatenate([t, t, t, t], axis=-1)
    a_re, a_im, b_re, b_im, c_re, c_im = map(f, (a_re, a_im, b_re, b_im, c_re, c_im))
    g, p, h = a_re.shape[1], S5_STATE, S5_GROUP
    dt = jnp.exp(f(log_dt))[..., None]
    mag = jnp.exp(a_re * dt)
    abar_re = mag * jnp.cos(a_im * dt)
    abar_im = mag * jnp.sin(a_im * dt)
    den = a_re * a_re + a_im * a_im
    nr = abar_re - 1.0
    ni = abar_im
    f_re = (nr * a_re + ni * a_im) / den
    f_im = (ni * a_re - nr * a_im) / den
    pw_re, pw_im = [jnp.ones_like(abar_re)], [jnp.zeros_like(abar_im)]
    for _ in range(S5_CHUNK):
        pre, pim = pw_re[-1], pw_im[-1]
        pw_re.append(pre * abar_re - pim * abar_im)
        pw_im.append(pre * abar_im + pim * abar_re)
    pad = [jnp.zeros_like(abar_re)] * (3 * SUBLANES - S5_CHUNK - 1)
    pw = jnp.stack([jnp.stack(pw_re + pad, axis=2), jnp.stack(pw_im + pad, axis=2)], axis=1)
    pw = jnp.transpose(quad(pw), (2, 0, 1, 3, 4))
    dsk = jnp.tile(f(d_skip).reshape(g, h), (1, S5_CW // h))
    sgn = jnp.broadcast_to(jnp.repeat(jnp.array([-1.0, 1.0, 1.0, -1.0], F32), p), (g, 4 * p))
    zero = jnp.zeros_like(dsk)
    rows = jnp.stack([quad(f_re[0]), quad(f_im[0]), quad(f_re[1]), quad(f_im[1]), dsk, sgn, zero, zero],
                     axis=1)
    bt_re = jnp.swapaxes(b_re, -1, -2)
    bt_im = jnp.swapaxes(b_im, -1, -2)
    bpk = jnp.stack([jnp.concatenate([bt_re, bt_im, bt_im, bt_re], axis=-1),
                     jnp.concatenate([bt_im, bt_re, bt_re, bt_im], axis=-1)], axis=1)
    cpk = jnp.stack([jnp.concatenate([c_re, -c_im], axis=-1),
                     jnp.concatenate([-c_im, -c_re], axis=-1)], axis=1)
    bpk = jnp.transpose(bpk, (2, 0, 1, 3, 4))
    cpk = jnp.transpose(cpk, (2, 0, 1, 3, 4))
    grp = lambda shape: pl.BlockSpec((None,) + shape, lambda i: (i,) + (0,) * len(shape))
    return pl.pallas_call(
        _s5_operator_kernel,
        grid=(g,),
        in_specs=[grp((8, 4 * p)), grp((2, 2, 3 * SUBLANES, 4 * p)), grp((2, 2, h, 4 * p)),
                  grp((2, 2, h, 2 * p))],
        out_specs=[grp((S5_CW, 3 * S5_CW)), grp((4 * p, S5_CW)), grp((SUBLANES, 2 * p))],
        out_shape=[jax.ShapeDtypeStruct((g, S5_CW, 3 * S5_CW), BF16),
                   jax.ShapeDtypeStruct((g, 4 * p, S5_CW), BF16),
                   jax.ShapeDtypeStruct((g, SUBLANES, 2 * p), F32)],
        compiler_params=pltpu.CompilerParams(dimension_semantics=("parallel",)),
        name="s5_operators",
    )(rows, pw, bpk, cpk)


def _shift_lanes(x, s):
    lo, hi = x[:, :LANES], x[:, LANES:]
    lane = lax.broadcasted_iota(jnp.int32, lo.shape, 1)
    zero = jnp.zeros_like(lo)
    rot = lambda t, r: pltpu.roll(t, r, 1) if r % LANES else t
    if s >= 0:
        if s < LANES:
            rl, rh = rot(lo, s), rot(hi, s)
            out = (jnp.where(lane >= s, rl, 0.0), jnp.where(lane >= s, rh, rl))
        else:
            rl = rot(lo, s - LANES)
            out = (zero, jnp.where(lane >= s - LANES, rl, 0.0))
    else:
        s = -s
        if s < LANES:
            rl, rh = rot(lo, LANES - s), rot(hi, LANES - s)
            out = (jnp.where(lane < LANES - s, rl, rh), jnp.where(lane < LANES - s, rh, 0.0))
        else:
            rh = rot(hi, 2 * LANES - s)
            out = (jnp.where(lane < 2 * LANES - s, rh, 0.0), zero)
    return jnp.concatenate(out, axis=1)


def _s5_operator_kernel(rows_ref, pw_ref, b_ref, c_ref, w1_ref, w2_ref, coef_ref):
    t_len = S5_CHUNK
    half = 2 * S5_STATE
    sgn = rows_ref[5:6, :]
    mxu_operand = lambda t: t.astype(BF16).astype(F32)
    a4, bx4, pr4, pi4, g2 = [], [], [], [], []
    for d in range(2):
        f_re = rows_ref[2 * d:2 * d + 1, :]
        f_im = rows_ref[2 * d + 1:2 * d + 2, :]
        p1 = b_ref[d, 0]
        p2 = b_ref[d, 1]
        a4.append(mxu_operand(f_re * p1 + sgn * f_im * p2))
        bx4.append(mxu_operand(f_re * p2 - sgn * f_im * p1))
        pr4.append(pw_ref[d, 0])
        pi4.append(pw_ref[d, 1])
        cx = jnp.concatenate([mxu_operand(c_ref[d, 0])] * t_len, axis=0)
        cy = jnp.concatenate([mxu_operand(c_ref[d, 1])] * t_len, axis=0)

        def ca(order, d=d, cx=cx, cy=cy):
            pr = jnp.concatenate(
                [jnp.broadcast_to(pr4[d][t:t + 1, :half], (S5_GROUP, half)) for t in order], axis=0)
            pi = jnp.concatenate(
                [jnp.broadcast_to(pi4[d][t:t + 1, :half], (S5_GROUP, half)) for t in order], axis=0)
            return cx * pr + cy * pi
        g2.append(ca)

    def strip(d, order):
        return lax.dot_general(a4[d][:, :half], g2[d](order), (((1,), (1,)), ((), ())),
                               precision=lax.Precision.HIGHEST, preferred_element_type=F32)

    row = lax.broadcasted_iota(jnp.int32, (S5_GROUP, S5_CW), 0)
    col = lax.broadcasted_iota(jnp.int32, (S5_GROUP, S5_CW), 1)
    k_fwd = strip(0, range(t_len)) + jnp.where(row == col, rows_ref[4:5, :], 0.0)
    k_bwd = strip(1, [t_len - 1 - k for k in range(t_len)])
    for j in range(t_len):
        lo, hi = j * S5_GROUP, (j + 1) * S5_GROUP
        m = _shift_lanes(k_fwd, j * S5_GROUP) + _shift_lanes(k_bwd, -(t_len - 1 - j) * S5_GROUP)
        w1_ref[lo:hi, 0:S5_CW] = m.astype(w1_ref.dtype)
        tf, tb = t_len - 1 - j, j
        s_f = a4[0] * pr4[0][tf:tf + 1, :] + bx4[0] * (sgn * pi4[0][tf:tf + 1, :])
        s_b = a4[1] * pr4[1][tb:tb + 1, :] + bx4[1] * (sgn * pi4[1][tb:tb + 1, :])
        w1_ref[lo:hi, S5_CW:2 * S5_CW] = s_f.astype(w1_ref.dtype)
        w1_ref[lo:hi, 2 * S5_CW:3 * S5_CW] = s_b.astype(w1_ref.dtype)
    w2_ref[0:half, :] = g2[0]([i + 1 for i in range(t_len)]).T.astype(w2_ref.dtype)
    w2_ref[half:2 * half, :] = g2[1]([t_len - i for i in range(t_len)]).T.astype(w2_ref.dtype)
    zero = jnp.zeros((1, half), F32)
    coef_ref[...] = jnp.concatenate(
        [pr4[0][t_len:t_len + 1, :half], (sgn * pi4[0][t_len:t_len + 1, :])[:, :half],
         pr4[1][t_len:t_len + 1, :half], (sgn * pi4[1][t_len:t_len + 1, :])[:, :half],
         zero, zero, zero, zero], axis=0)


def _s5_kernel(xc_ref, xl_ref, w1_ref, w2_ref, coef_ref, yc_ref, yl_ref,
               zic_ref, zil_ref, zsc_ref, zsl_ref, hc_ref, hl_ref, *, bsz, ctx_chunks, lat_chunks):
    sw = 2 * S5_STATE
    streams = ((xc_ref, zic_ref, zsc_ref, hc_ref, yc_ref, ctx_chunks),
               (xl_ref, zil_ref, zsl_ref, hl_ref, yl_ref, lat_chunks))
    w1 = w1_ref[...]
    for x_ref, zi_ref, zs_ref, _, _, n in streams:
        pitch = n + SUBLANES
        per_dot = min(bsz, max(1, S5_DOT_ROWS // n))
        for b0 in range(0, bsz, per_dot):
            zz = jnp.dot(x_ref[b0 * n:(b0 + per_dot) * n, :], w1, preferred_element_type=F32)
            for b in range(b0, b0 + per_dot):
                z = zz[(b - b0) * n:(b - b0 + 1) * n]
                zi_ref[b * n:(b + 1) * n, :] = z[:, 0:S5_CW]
                for k in range(4):
                    zs_ref[k, b * pitch:b * pitch + n, :] = z[:, S5_CW + k * sw:S5_CW + (k + 1) * sw]

    shape = (bsz, sw)
    a1f = jnp.broadcast_to(coef_ref[0:1, :], shape)
    a2f = jnp.broadcast_to(coef_ref[1:2, :], shape)
    a1b = jnp.broadcast_to(coef_ref[2:3, :], shape)
    a2b = jnp.broadcast_to(coef_ref[3:4, :], shape)

    def locate(c):
        if c < ctx_chunks:
            return zsc_ref, hc_ref, c, ctx_chunks + SUBLANES
        return zsl_ref, hl_ref, c - ctx_chunks, lat_chunks + SUBLANES

    n_chunks = ctx_chunks + lat_chunks
    hf = hfs = hb = hbs = jnp.zeros(shape, F32)
    for t in range(n_chunks):
        zs_ref, h_ref, c, pitch = locate(t)
        rows = pl.ds(c, bsz, stride=pitch)
        h_ref[0, rows, :] = hf
        lf = zs_ref[0, rows, :]
        lfs = zs_ref[1, rows, :]
        hf, hfs = a1f * hf + a2f * hfs + lf, a1f * hfs - a2f * hf + lfs
        cb = ctx_chunks - 1 - t if t < ctx_chunks else n_chunks + ctx_chunks - 1 - t
        zs_ref, h_ref, c, pitch = locate(cb)
        rows = pl.ds(c, bsz, stride=pitch)
        h_ref[1, rows, :] = hb
        lb = zs_ref[2, rows, :]
        lbs = zs_ref[3, rows, :]
        hb, hbs = a1b * hb + a2b * hbs + lb, a1b * hbs - a2b * hb + lbs

    w2 = w2_ref[...]
    for _, zi_ref, _, h_ref, y_ref, n in streams:
        pitch = n + SUBLANES
        per_dot = min(bsz, max(1, S5_DOT_ROWS // n))
        for b0 in range(0, bsz, per_dot):
            hin = jnp.concatenate(
                [jnp.concatenate([h_ref[0, b * pitch:b * pitch + n, :], h_ref[1, b * pitch:b * pitch + n, :]], axis=1)
                 for b in range(b0, b0 + per_dot)], axis=0)
            rows = slice(b0 * n, (b0 + per_dot) * n)
            y = zi_ref[rows, :] + jnp.dot(hin.astype(BF16), w2, preferred_element_type=F32)
            y_ref[rows, :] = y.astype(y_ref.dtype)


def _s5_scan(x_ctx, x_lat, w1, w2, coef, layer, bsz):
    ctx_chunks = x_ctx.shape[1] // bsz
    lat_chunks = x_lat.shape[1] // bsz
    sw = 2 * S5_STATE
    grp = lambda rows, cols: pl.BlockSpec((None, rows, cols), lambda g: (g, 0, 0))
    opr = lambda rows, cols: pl.BlockSpec((None, None, rows, cols), lambda g: (layer, g, 0, 0))
    slab = lambda k, n: pltpu.VMEM((k, bsz * (n + SUBLANES), sw), F32)
    return pl.pallas_call(
        functools.partial(_s5_kernel, bsz=bsz, ctx_chunks=ctx_chunks, lat_chunks=lat_chunks),
        grid=(S5_GROUPS,),
        in_specs=[
            grp(bsz * ctx_chunks, S5_CW), grp(bsz * lat_chunks, S5_CW),
            opr(S5_CW, 3 * S5_CW), opr(4 * S5_STATE, S5_CW), opr(SUBLANES, sw),
        ],
        out_specs=[grp(bsz * ctx_chunks, S5_CW), grp(bsz * lat_chunks, S5_CW)],
        out_shape=[jax.ShapeDtypeStruct(x_ctx.shape, F32), jax.ShapeDtypeStruct(x_lat.shape, F32)],
        scratch_shapes=[
            pltpu.VMEM((bsz * ctx_chunks, S5_CW), F32), pltpu.VMEM((bsz * lat_chunks, S5_CW), F32),
            slab(4, ctx_chunks), slab(4, lat_chunks), slab(2, ctx_chunks), slab(2, lat_chunks),
        ],
        compiler_params=pltpu.CompilerParams(
            dimension_semantics=("parallel",), vmem_limit_bytes=VMEM_LIMIT),
        name="s5_scan",
    )(x_ctx, x_lat, w1, w2, coef)


def _gelu_tanh(x):
    return 0.5 * x * (1.0 + jnp.tanh(math.sqrt(2.0 / math.pi) * (x + 0.044715 * (x * x * x))))


def _silu(x):
    return x * jax.nn.sigmoid(x)


def _merge_kernel(h_ref, mod_ref, npre_ref, win_ref, wa_ref, oa_ref, yb_ref, oc_ref,
                  wglu_ref, bglu_ref, woa_ref, wob_ref, woc_ref, wout_ref, npost_ref, o_ref, stage_ref):
    hn = _modulated_norm(h_ref[...], mod_ref, npre_ref)
    gate_cols = (None, SEGS["bg"], SEGS["cg"])

    def branch(i, o, w_ref):
        if i == 0:
            g = jnp.dot(hn, wa_ref[:, A_WIDTH:2 * A_WIDTH], preferred_element_type=F32)
        else:
            off, width = gate_cols[i]
            g = jnp.dot(hn, win_ref[:, off:off + width], preferred_element_type=F32)
        t = jnp.dot((o * _silu(g)).astype(BF16), w_ref[...], preferred_element_type=F32)
        m_lo = SEGS["mg"][0] + i * D_MODEL
        gate = jax.nn.sigmoid(jnp.dot(hn, win_ref[:, m_lo:m_lo + D_MODEL], preferred_element_type=F32))
        return gate * t

    y = branch(0, oa_ref[...], woa_ref) + branch(2, oc_ref[...], woc_ref)
    yb = _gelu_tanh(_groups_to_tokens(yb_ref, stage_ref))
    glu = jnp.dot(yb.astype(BF16), wglu_ref[...], preferred_element_type=F32) + bglu_ref[...]
    y = y + branch(1, yb * jax.nn.sigmoid(glu), wob_ref)
    y = jnp.dot(y.astype(BF16), wout_ref[...], preferred_element_type=F32)
    ms = jnp.mean(y * y, axis=-1, keepdims=True)
    yn = y * lax.rsqrt(ms + EPS) * npost_ref[...]
    gate = mod_ref[:, 2 * D_MODEL:3 * D_MODEL]
    o_ref[...] = h_ref[...] + gate * yn


def _merge(h, mod, mod_row, norm_pre, layer, w_bf16, w_a, o_a, y_b, o_c, wts):
    bsz, rows, _ = h.shape
    tile = min(ROW_TILE, rows)
    tiles = rows // tile
    if mod_row is None:
        mod_map = lambda b, i: (b, 0, 0)
    else:
        mod_map = lambda b, i: (mod_row, 0, 0)
    tok = lambda w: pl.BlockSpec((None, tile,w), lambda b, i: (b, i, 0))
    in_specs = [
        tok(D_MODEL),
        pl.BlockSpec((None, 1, 3 * D_MODEL), mod_map),
        _const_spec((1, D_MODEL)), _const_spec((D_MODEL, IN_WIDTH), layer),
        _const_spec((D_MODEL, 2 * A_WIDTH), layer),
        tok(A_WIDTH),
        pl.BlockSpec((S5_GROUPS, tile // S5_CHUNK, S5_CW), lambda b, i: (0, b * tiles + i, 0)),
        tok(C_WIDTH),
        _const_spec((B_WIDTH, B_WIDTH), layer), _const_spec((1, B_WIDTH)),
        _const_spec((A_WIDTH, D_MODEL), layer), _const_spec((B_WIDTH, D_MODEL), layer),
        _const_spec((C_WIDTH, D_MODEL), layer),
        _const_spec((D_MODEL, D_MODEL), layer), _const_spec((1, D_MODEL)),
    ]
    return pl.pallas_call(
        _merge_kernel,
        grid=(bsz, rows // tile),
        in_specs=in_specs,
        out_specs=tok(D_MODEL),
        out_shape=jax.ShapeDtypeStruct((bsz, rows, D_MODEL), F32),
        scratch_shapes=[pltpu.VMEM((B_WIDTH // LANES, (tile // S5_CHUNK) * STAGE_PITCH, LANES), F32)],
        compiler_params=pltpu.CompilerParams(
            dimension_semantics=("parallel", "parallel"), vmem_limit_bytes=VMEM_LIMIT),
        name="merge_out",
    )(h, mod, norm_pre.reshape(1, D_MODEL), w_bf16, w_a, o_a, y_b, o_c, *wts)


def _rope_tables(n_tokens):
    rows = n_tokens // GRID_W
    pos_r = jnp.repeat(jnp.arange(rows, dtype=F32), GRID_W)
    pos_c = jnp.tile(jnp.arange(GRID_W, dtype=F32), rows)
    inv = ROPE_BASE ** (-jnp.arange(AX_FREQS, dtype=F32) / AX_FREQS)
    ang_r = pos_r[:, None] * inv[None]
    ang_c = pos_c[:, None] * inv[None]
    ang = jnp.concatenate([ang_r, ang_r, ang_c, ang_c], axis=-1)
    cos, sin = jnp.cos(ang), jnp.sin(ang)
    first = (jnp.arange(HEAD_DIM) % (2 * AX_FREQS)) < AX_FREQS
    sa = jnp.where(first[None], -sin, 0.0)
    sb = jnp.where(first[None], 0.0, sin)
    tile2 = lambda t: jnp.concatenate([t, t], axis=-1)
    return tile2(cos), tile2(sa), tile2(sb)


def _reorder_a_heads(w, axis):
    axis = axis % w.ndim
    shape = w.shape
    split = shape[:axis] + (A_HEADS, HEAD_DIM) + shape[axis + 1:]
    return jnp.take(w.reshape(split), jnp.array(A_HEAD_ORDER), axis=axis).reshape(shape)


def _layers_as_groups(t):
    t = jnp.swapaxes(t, 0, 1)
    return t.reshape((2, DEPTH * S5_GROUPS) + t.shape[3:])


def kernel(x, c, ctx, c_ctx, w_mod, b_mod, norm_pre, norm_post, w_in, swa_sink, s5_a_re, s5_a_im, s5_log_dt, s5_b_re, s5_b_im, s5_c_re, s5_c_im, s5_d, s5_w_glu, s5_b_glu, diff_lq1, diff_lk1, diff_lq2, diff_lk2, diff_subln, w_o_a, w_o_b, w_o_c, w_out):
    bsz, n_lat, _ = x.shape
    n_ctx = ctx.shape[1]
    ctx_row = bsz
    cc = jnp.zeros((16, D_MODEL), F32).at[:bsz].set(c).at[ctx_row].set(c_ctx)
    mod_all = _modulation(cc, w_mod, b_mod)
    rope_tabs = _rope_tables(n_lat)

    cols = lambda name: w_in[..., SEGS[name][0]:SEGS[name][0] + SEGS[name][1]]
    w_in_all = w_in.astype(BF16)
    w_a_all = jnp.concatenate(
        [_reorder_a_heads(cols("aq"), -1), _reorder_a_heads(cols("ag"), -1)], axis=-1).astype(BF16)
    w_glu_all, w_ob_all, w_oc_all, w_out_all = (t.astype(BF16) for t in (s5_w_glu, w_o_b, w_o_c, w_out))
    w_oa_all = _reorder_a_heads(w_o_a, -2).astype(BF16)
    lam_all = jnp.stack([diff_lq1, diff_lk1, diff_lq2, diff_lk2], axis=1).astype(F32)
    s5_ops = _s5_operators(*(_layers_as_groups(t) for t in (s5_a_re, s5_a_im, s5_log_dt, s5_b_re, s5_b_im,
                                                            s5_c_re, s5_c_im)), s5_d.reshape(-1))
    w1_all, w2_all, coef_all = (t.reshape((DEPTH, S5_GROUPS) + t.shape[1:]) for t in s5_ops)

    h_lat, h_ctx = x, ctx
    for l in range(DEPTH):
        last = l == DEPTH - 1
        lam_init = 0.8 - 0.6 * math.exp(-0.3 * l)
        mod = mod_all[l].reshape(16, 1, 3 * D_MODEL)
        zl = _in_projection(h_lat, mod, None, norm_pre[l], w_in_all, w_a_all, l, PROJ_NAMES, rope_tabs)
        ctx_names = ("ak", "av", "bu", "ck", "cv") if last else PROJ_NAMES
        zc = _in_projection(h_ctx, mod, ctx_row, norm_pre[l], w_in_all, w_a_all, l, ctx_names, None)

        o_a_l = _swa(swa_sink[l], zl["aq"], zc["ak"], zc["av"], zl["ak"], zl["av"])

        y_b_c, y_b_l = _s5_scan(zc["bu"], zl["bu"], w1_all, w2_all, coef_all, l, bsz)

        lam_params = lam_all[l]
        o_c_l = _diff_attention(lam_params, diff_subln[l], lam_init, zl["cq"], zc["ck"], zc["cv"],
                                zl["ck"], zl["cv"])

        wts = (w_glu_all, s5_b_glu[l].reshape(1, B_WIDTH), w_oa_all, w_ob_all, w_oc_all,
               w_out_all, norm_post[l].reshape(1, D_MODEL))
        h_lat_new = _merge(h_lat, mod, None, norm_pre[l], l, w_in_all, w_a_all, o_a_l, y_b_l, o_c_l, wts)
        if not last:
            o_a_c = _swa(swa_sink[l], zc["aq"], zc["ak"], zc["av"])
            o_c_c = _diff_attention(lam_params, diff_subln[l], lam_init, zc["cq"], zc["ck"], zc["cv"])
            h_ctx = _merge(h_ctx, mod, ctx_row, norm_pre[l], l, w_in_all, w_a_all, o_a_c, y_b_c, o_c_c, wts)
        h_lat = h_lat_new
    return h_lat
```

```python
import functools
import math

import jax
import jax.numpy as jnp
from jax import lax
from jax.experimental import pallas as pl
from jax.experimental.pallas import tpu as pltpu

F32 = jnp.float32
BF16 = jnp.bfloat16

D_MODEL = 1024
DEPTH = 2
GRID_W = 64
HEAD_DIM = 64
WINDOW = 128
ROPE_BASE = 10000.0
AX_FREQS = HEAD_DIM // 4
NEG_INF = -1e30
A_HEADS = 8
A_KV_HEADS = 2
A_GROUP = A_HEADS // A_KV_HEADS
A_WIDTH = A_HEADS * HEAD_DIM
A_KV_WIDTH = A_KV_HEADS * HEAD_DIM
B_WIDTH = 512
S5_GROUP = 16
S5_GROUPS = B_WIDTH // S5_GROUP
S5_STATE = 64
C_HEADS = 4
C_WIDTH = C_HEADS * 2 * HEAD_DIM
EPS = 1e-6

LANES = 128
SUBLANES = 8

PROJ_ROW_TILE = 1024
ROW_TILE = 512
MERGE_PARTS = 1
Q_BLOCK = 128
DIFF_KEY_CHUNK = 512
DIFF_LOOKAHEAD = 8
SWA_LOOKAHEAD = 1
SWA_Q_ROWS = 512
DIFF_Q_ROWS = 512
A_HEAD_ORDER = tuple(h for v in range(4) for h in (v, 4 + v))
S5_CHUNK = 16
S5_CW = S5_CHUNK * S5_GROUP
S5_DOT_ROWS = 256
S5_SCAN_GROUPS_PER_STEP = 4
S5_OP_GROUPS_PER_STEP = 4
STAGE_PITCH = S5_CHUNK + SUBLANES
VMEM_LIMIT = 56 * 1024 * 1024

_SEG_NAMES = ("aq", "ak", "av", "ag", "bu", "bg", "cq", "ck", "cv", "cg", "mg")
_SEG_WIDTHS = (A_WIDTH, A_KV_WIDTH, A_KV_WIDTH, A_WIDTH, B_WIDTH, B_WIDTH,
               C_WIDTH, C_WIDTH, C_WIDTH, C_WIDTH, 3 * D_MODEL)
SEGS = {}
_off = 0
for _n, _w in zip(_SEG_NAMES, _SEG_WIDTHS):
    SEGS[_n] = (_off, _w)
    _off += _w
IN_WIDTH = _off
PROJ_NAMES = ("aq", "ak", "av", "bu", "cq", "ck", "cv")
Q_SCALE = HEAD_DIM ** -0.5 * math.log2(math.e)
ROPE_SEGS = {"aq": Q_SCALE, "ak": 1.0, "cq": Q_SCALE, "ck": 1.0}


def _const_spec(shape, layer=None):
    nd = len(shape)
    if layer is None:
        return pl.BlockSpec(shape, lambda *_: (0,) * nd, pipeline_mode=pl.Buffered(1))
    return pl.BlockSpec((None,) + shape, lambda *_: (layer,) + (0,) * nd, pipeline_mode=pl.Buffered(1))


def _mod_kernel(c_ref, w_ref, b_ref, o_ref):
    c = c_ref[...]
    a = c * jax.nn.sigmoid(c)
    o_ref[...] = jnp.dot(a, w_ref[...], preferred_element_type=F32) + b_ref[...]


def _modulation(cc, w_mod, b_mod):
    n_tiles = 3
    return pl.pallas_call(
        _mod_kernel,
        grid=(DEPTH, n_tiles),
        in_specs=[
            pl.BlockSpec((16, D_MODEL), lambda l, j: (0, 0)),
            pl.BlockSpec((None, D_MODEL, D_MODEL), lambda l, j: (l, 0, j)),
            pl.BlockSpec((None, 1, D_MODEL), lambda l, j: (l, 0, j)),
        ],
        out_specs=pl.BlockSpec((None, 16, D_MODEL), lambda l, j: (l, 0, j)),
        out_shape=jax.ShapeDtypeStruct((DEPTH, 16, 3 * D_MODEL), F32),
        name="modulation",
    )(cc, w_mod, b_mod.reshape(DEPTH, 1, 3 * D_MODEL))


def _block_transpose8(xs):
    lane_blk = lax.broadcasted_iota(jnp.int32, xs[0].shape, 1) // S5_GROUP
    for s in (4, 2, 1):
        keep = (lane_blk & s) == 0
        out = list(xs)
        for p in range(8):
            if p & s:
                continue
            a, b = xs[p], xs[p + s]
            out[p] = jnp.where(keep, a, pltpu.roll(b, S5_GROUP * s, 1))
            out[p + s] = jnp.where(keep, pltpu.roll(a, LANES - S5_GROUP * s, 1), b)
        xs = out
    return xs


def _tokens_to_groups(z, stage_ref, o_ref):
    n_chunks = z.shape[0] // S5_CHUNK
    for v in range(B_WIDTH // LANES):
        for c in range(n_chunks):
            stage_ref[v, c * STAGE_PITCH:c * STAGE_PITCH + S5_CHUNK, :] = \
                z[c * S5_CHUNK:(c + 1) * S5_CHUNK, v * LANES:(v + 1) * LANES]
    for w in range(S5_CW // LANES):
        for v in range(B_WIDTH // LANES):
            src = [stage_ref[v, pl.ds(8 * w + jj, n_chunks, stride=STAGE_PITCH), :] for jj in range(8)]
            dst = _block_transpose8(src)
            for gg in range(8):
                o_ref[8 * v + gg, :, w * LANES:(w + 1) * LANES] = dst[gg].astype(o_ref.dtype)


def _groups_to_tokens(y_ref, stage_ref):
    n_chunks = y_ref.shape[1]
    for w in range(S5_CW // LANES):
        for v in range(B_WIDTH // LANES):
            src = [y_ref[8 * v + gg, :, w * LANES:(w + 1) * LANES].astype(F32) for gg in range(8)]
            dst = _block_transpose8(src)
            for jj in range(8):
                stage_ref[v, pl.ds(8 * w + jj, n_chunks, stride=STAGE_PITCH), :] = dst[jj]
    rows = []
    for c in range(n_chunks):
        rows.append(jnp.concatenate(
            [stage_ref[v, c * STAGE_PITCH:c * STAGE_PITCH + S5_CHUNK, :] for v in range(B_WIDTH // LANES)], axis=1))
    return jnp.concatenate(rows, axis=0)


def _modulated_norm(x, mod_ref, nw_ref):
    ms = jnp.mean(x * x, axis=-1, keepdims=True)
    xn = x * lax.rsqrt(ms + EPS) * nw_ref[...]
    shift = mod_ref[:, 0:D_MODEL]
    scale = mod_ref[:, D_MODEL:2 * D_MODEL]
    return (xn * (1.0 + scale) + shift).astype(BF16)


def _inproj_kernel(*refs, names, use_rope):
    h_ref, mod_ref, nw_ref, w_ref, wa_ref = refs[:5]
    stage_ref = refs[-1]
    refs = refs[:-1]
    if use_rope:
        cos_ref, sa_ref, sb_ref = refs[5:8]
        outs = refs[8:]
    else:
        outs = refs[5:]
    hn = _modulated_norm(h_ref[...], mod_ref, nw_ref)
    kv_off = SEGS["ak"][0]
    assert SEGS["av"][0] == kv_off + A_KV_WIDTH
    kv = jnp.dot(hn, w_ref[:, kv_off:kv_off + 2 * A_KV_WIDTH], preferred_element_type=F32)

    def project(name):
        off, width = SEGS[name]
        if name in ("ak", "av"):
            return kv[:, off - kv_off:off - kv_off + width]
        if name == "aq":
            return jnp.dot(hn, wa_ref[:, 0:width], preferred_element_type=F32)
        return jnp.dot(hn, w_ref[:, off:off + width], preferred_element_type=F32)

    for name, o_ref in zip(names, outs):
        z = project(name)
        mul = ROPE_SEGS.get(name, 1.0)
        if use_rope and name in ROPE_SEGS:
            cos = cos_ref[...]
            sa = sa_ref[...]
            sb = sb_ref[...]
            for c in range(z.shape[1] // LANES):
                t = z[:, c * LANES:(c + 1) * LANES]
                r = t * cos + pltpu.roll(t, LANES - AX_FREQS, 1) * sa + pltpu.roll(t, AX_FREQS, 1) * sb
                if mul != 1.0:
                    r = r * mul
                o_ref[:, c * LANES:(c + 1) * LANES] = r.astype(o_ref.dtype)
        elif name == "bu":
            _tokens_to_groups(z, stage_ref, o_ref)
        elif name == "av":
            for t in range(z.shape[0] // Q_BLOCK):
                o_ref[t] = z[t * Q_BLOCK:(t + 1) * Q_BLOCK, :].T.astype(o_ref.dtype)
        elif name == "cv":
            o_ref[...] = z.T.astype(o_ref.dtype)
        else:
            if mul != 1.0:
                z = z * mul
            o_ref[...] = z.astype(o_ref.dtype)


def _in_projection(h, mod, mod_row, norm_w, w_bf16, w_a, layer, names, rope_tabs):
    bsz, rows, _ = h.shape
    tile = min(PROJ_ROW_TILE, rows)
    use_rope = rope_tabs is not None
    if mod_row is None:
        mod_map = lambda b, i: (b, 0, 0)
    else:
        mod_map = lambda b, i: (mod_row, 0, 0)
    in_specs = [
        pl.BlockSpec((None, tile,D_MODEL), lambda b, i: (b, i, 0)),
        pl.BlockSpec((None, 1, 3 * D_MODEL), mod_map),
        _const_spec((1, D_MODEL)),
        _const_spec((D_MODEL, IN_WIDTH), layer),
        _const_spec((D_MODEL, 2 * A_WIDTH), layer),
    ]
    args = [h, mod, norm_w.reshape(1, D_MODEL), w_bf16, w_a]
    if use_rope:
        for t in rope_tabs:
            in_specs.append(pl.BlockSpec((tile, LANES), lambda b, i: (i, 0)))
            args.append(t)
    tiles = rows // tile
    chunk_rows = tile // S5_CHUNK
    out_specs, out_shape = [], []
    for n in names:
        if n == "bu":
            out_specs.append(pl.BlockSpec((S5_GROUPS, chunk_rows, S5_CW), lambda b, i: (0, b * tiles + i, 0)))
            out_shape.append(jax.ShapeDtypeStruct((S5_GROUPS, bsz * rows // S5_CHUNK, S5_CW), BF16))
        elif n == "av":
            kb = tile // Q_BLOCK
            out_specs.append(pl.BlockSpec((None, kb, A_KV_WIDTH, Q_BLOCK), lambda b, i: (b, i, 0, 0)))
            out_shape.append(jax.ShapeDtypeStruct((bsz, rows // Q_BLOCK, A_KV_WIDTH, Q_BLOCK), BF16))
        elif n == "cv":
            out_specs.append(pl.BlockSpec((None, C_WIDTH, tile), lambda b, i: (b, 0, i)))
            out_shape.append(jax.ShapeDtypeStruct((bsz, C_WIDTH, rows), BF16))
        else:
            out_specs.append(pl.BlockSpec((None, tile,SEGS[n][1]), lambda b, i: (b, i, 0)))
            out_shape.append(jax.ShapeDtypeStruct((bsz, rows, SEGS[n][1]), BF16))
    outs = pl.pallas_call(
        functools.partial(_inproj_kernel, names=tuple(names), use_rope=use_rope),
        grid=(bsz, tiles),
        in_specs=in_specs,
        out_specs=out_specs,
        out_shape=out_shape,
        scratch_shapes=[pltpu.VMEM((B_WIDTH // LANES, (tile // S5_CHUNK) * STAGE_PITCH, LANES), F32)],
        compiler_params=pltpu.CompilerParams(
            dimension_semantics=("parallel", "parallel"), vmem_limit_bytes=VMEM_LIMIT),
        name="in_projection_rope" if use_rope else "in_projection",
    )(*args)
    return dict(zip(names, outs))


def _dot_nt(a, b):
    return lax.dot_general(a, b, (((1,), (1,)), ((), ())), preferred_element_type=F32)


def _half_masked(q, lower):
    lane = lax.broadcasted_iota(jnp.int32, q.shape, 1)
    keep = lane < HEAD_DIM if lower else lane >= HEAD_DIM
    return jnp.where(keep, q.astype(F32), 0.0).astype(q.dtype)


def _swa_kernel(*refs, has_local, lat_blocks):
    if has_local:
        sink_ref, q_ref, kc_ref, vtc_ref, kl_ref, vtl_ref, o_ref = refs
    else:
        sink_ref, q_ref, kc_ref, vtc_ref, o_ref = refs
    n_ctx = kc_ref.shape[0]
    n_items = q_ref.shape[0] // Q_BLOCK
    sink = jnp.concatenate(
        [jnp.full((1, Q_BLOCK), sink_ref[kh * A_GROUP + v] * math.log2(math.e), F32)
         for kh in range(A_KV_HEADS) for v in range(A_GROUP)], axis=1)

    def scores(t):
        qs = slice(t * Q_BLOCK, (t + 1) * Q_BLOCK)
        qblk = jnp.concatenate(
            [_half_masked(q_ref[qs, v * LANES:(v + 1) * LANES], kh == 0)
             for kh in range(A_KV_HEADS) for v in range(A_GROUP)], axis=0)
        keys = [kc_ref[...]]
        vts = [vtc_ref[c] for c in range(n_ctx // Q_BLOCK)]
        if has_local:
            n = pl.program_id(1) * n_items + t
            sb = jnp.clip(n - 1, 0, lat_blocks - 3)
            keys.append(kl_ref[pl.ds(pl.multiple_of(sb * Q_BLOCK, Q_BLOCK), 3 * Q_BLOCK), :])
            vts += [vtl_ref[sb + c] for c in range(3)]
        s_t = _dot_nt(jnp.concatenate(keys, axis=0), qblk)
        if has_local:
            r = lax.broadcasted_iota(jnp.int32, (3 * Q_BLOCK, Q_BLOCK), 0)
            i = lax.broadcasted_iota(jnp.int32, (3 * Q_BLOCK, Q_BLOCK), 1)
            in_band = jnp.abs((n - sb) * Q_BLOCK + i - r) <= WINDOW
            bias = jnp.where(in_band, 0.0, NEG_INF)
            s_t = jnp.concatenate([s_t[:n_ctx], s_t[n_ctx:] + jnp.concatenate([bias] * A_HEADS, axis=1)],
                                  axis=0)
        return s_t, jnp.concatenate(vts, axis=1)

    lane = lax.broadcasted_iota(jnp.int32, (Q_BLOCK, LANES), 1)
    half = A_GROUP * Q_BLOCK
    queue = [scores(t) for t in range(min(SWA_LOOKAHEAD, n_items))]
    for t in range(n_items):
        s_t, vt = queue.pop(0)
        if t + SWA_LOOKAHEAD < n_items:
            queue.append(scores(t + SWA_LOOKAHEAD))
        m = jnp.maximum(jnp.max(s_t, axis=0, keepdims=True), sink)
        e_t = jnp.exp2(s_t - m)
        den = jnp.sum(e_t, axis=0, keepdims=True) + jnp.exp2(sink - m)
        o_t = jnp.dot(vt, e_t.astype(BF16), preferred_element_type=F32) * (1.0 / den)
        qs = slice(t * Q_BLOCK, (t + 1) * Q_BLOCK)
        for v in range(A_GROUP):
            lower = o_t[:, v * Q_BLOCK:(v + 1) * Q_BLOCK].T
            upper = o_t[:, half + v * Q_BLOCK:half + (v + 1) * Q_BLOCK].T
            o_ref[qs, v * LANES:(v + 1) * LANES] = jnp.where(lane < HEAD_DIM, lower, upper).astype(o_ref.dtype)


def _swa(sink, q, k_ctx, vt_ctx, k_lat=None, vt_lat=None):
    bsz, rows, _ = q.shape
    qrows = min(SWA_Q_ROWS, rows)
    n_ctx = k_ctx.shape[1]
    has_local = k_lat is not None
    in_specs = [
        pl.BlockSpec(memory_space=pltpu.SMEM),
        pl.BlockSpec((None, qrows, A_WIDTH), lambda b, n: (b, n, 0)),
        pl.BlockSpec((None, n_ctx, A_KV_WIDTH), lambda b, n: (b, 0, 0)),
        pl.BlockSpec((None, n_ctx // Q_BLOCK, A_KV_WIDTH, Q_BLOCK), lambda b, n: (b, 0, 0, 0)),
    ]
    args = [sink, q, k_ctx, vt_ctx]
    lat_blocks = 0
    if has_local:
        n_lat = k_lat.shape[1]
        lat_blocks = n_lat // Q_BLOCK
        in_specs += [pl.BlockSpec((None, n_lat, A_KV_WIDTH), lambda b, n: (b, 0, 0)),
                     pl.BlockSpec((None, lat_blocks, A_KV_WIDTH, Q_BLOCK), lambda b, n: (b, 0, 0, 0))]
        args += [k_lat, vt_lat]
    return pl.pallas_call(
        functools.partial(_swa_kernel, has_local=has_local, lat_blocks=lat_blocks),
        grid=(bsz, rows // qrows),
        in_specs=in_specs,
        out_specs=pl.BlockSpec((None, qrows, A_WIDTH), lambda b, n: (b, n, 0)),
        out_shape=jax.ShapeDtypeStruct((bsz, rows, A_WIDTH), F32),
        compiler_params=pltpu.CompilerParams(
            dimension_semantics=("parallel", "arbitrary"), vmem_limit_bytes=VMEM_LIMIT),
        name="swa_latent" if has_local else "swa_context",
    )(*args)


def _diff_kernel(*refs, has_lat, lam_init):
    if has_lat:
        lam_ref, sub_ref, q_ref, kc_ref, vtc_ref, kl_ref, vtl_ref, o_ref = refs
    else:
        lam_ref, sub_ref, q_ref, kc_ref, vtc_ref, o_ref = refs
    lp = lam_ref[...]
    lam = (jnp.exp(jnp.sum(lp[0:1] * lp[1:2], axis=1, keepdims=True))
           - jnp.exp(jnp.sum(lp[2:3] * lp[3:4], axis=1, keepdims=True)) + lam_init)
    sub_w = sub_ref[...] * (1.0 - lam_init)
    width = 2 * HEAD_DIM
    chunks = [(kc_ref, vtc_ref, 0, kc_ref.shape[0])]
    if has_lat:
        chunks += [(kl_ref, vtl_ref, c, DIFF_KEY_CHUNK) for c in range(0, kl_ref.shape[0], DIFF_KEY_CHUNK)]
    heads = [slice(h * width, (h + 1) * width) for h in range(C_HEADS)]
    items = [(hs, slice(qb, qb + Q_BLOCK)) for qb in range(0, q_ref.shape[0], Q_BLOCK) for hs in heads]
    units = [(it, ci) for it in range(len(items)) for ci in range(len(chunks))]
    qblks = {}

    def scores(unit):
        it, ci = unit
        hs, qs = items[it]
        if it not in qblks:
            qblks[it] = jnp.concatenate(
                [_half_masked(q_ref[qs, hs], True), _half_masked(q_ref[qs, hs], False)], axis=0)
        k_ref, _, start, size = chunks[ci]
        return _dot_nt(k_ref[start:start + size, hs], qblks[it])

    queue = [scores(unit) for unit in units[:DIFF_LOOKAHEAD]]
    m = den = acc = None
    for u, (it, ci) in enumerate(units):
        hs, qs = items[it]
        _, vt_ref, start, size = chunks[ci]
        s_t = queue.pop(0)
        if u + DIFF_LOOKAHEAD < len(units):
            queue.append(scores(units[u + DIFF_LOOKAHEAD]))
        m_c = jnp.max(s_t, axis=0, keepdims=True)
        m_new = m_c if ci == 0 else jnp.maximum(m, m_c)
        p_t = jnp.exp2(s_t - m_new)
        den_c = jnp.sum(p_t, axis=0, keepdims=True)
        pv = jnp.dot(vt_ref[hs, start:start + size], p_t.astype(BF16), preferred_element_type=F32)
        if ci == 0:
            den, acc = den_c, pv
        else:
            alpha = jnp.exp2(m - m_new)
            den, acc = den * alpha + den_c, acc * alpha + pv
        m = m_new
        if ci == len(chunks) - 1:
            inv = 1.0 / den
            o_t = acc[:, :Q_BLOCK] * inv[:, :Q_BLOCK] - acc[:, Q_BLOCK:] * (lam * inv[:, Q_BLOCK:])
            o = o_t.T
            ms = jnp.mean(o * o, axis=1, keepdims=True)
            o_ref[qs, hs] = (o * lax.rsqrt(ms + EPS) * sub_w).astype(o_ref.dtype)


def _diff_attention(lam_params, subln, lam_init, q, k_ctx, vt_ctx, k_lat=None, vt_lat=None):
    bsz, rows, _ = q.shape
    qrows = min(DIFF_Q_ROWS, rows)
    n_ctx = k_ctx.shape[1]
    has_lat = k_lat is not None
    in_specs = [
        _const_spec((4, HEAD_DIM)),
        _const_spec((1, 2 * HEAD_DIM)),
        pl.BlockSpec((None, qrows, C_WIDTH), lambda b, n: (b, n, 0)),
        pl.BlockSpec((None, n_ctx, C_WIDTH), lambda b, n: (b, 0, 0)),
        pl.BlockSpec((None, C_WIDTH, n_ctx), lambda b, n: (b, 0, 0)),
    ]
    args = [lam_params, subln.reshape(1, 2 * HEAD_DIM), q, k_ctx, vt_ctx]
    if has_lat:
        n_lat = k_lat.shape[1]
        in_specs += [pl.BlockSpec((None, n_lat, C_WIDTH), lambda b, n: (b, 0, 0)),
                     pl.BlockSpec((None, C_WIDTH, n_lat), lambda b, n: (b, 0, 0))]
        args += [k_lat, vt_lat]
    return pl.pallas_call(
        functools.partial(_diff_kernel, has_lat=has_lat, lam_init=lam_init),
        grid=(bsz, rows // qrows),
        in_specs=in_specs,
        out_specs=pl.BlockSpec((None, qrows, C_WIDTH), lambda b, n: (b, n, 0)),
        out_shape=jax.ShapeDtypeStruct((bsz, rows, C_WIDTH), F32),
        compiler_params=pltpu.CompilerParams(
            dimension_semantics=("parallel", "arbitrary"), vmem_limit_bytes=VMEM_LIMIT),
        name="diff_latent" if has_lat else "diff_context",
    )(*args)


def _s5_operators(a_re, a_im, log_dt, b_re, b_im, c_re, c_im, d_skip):
    f = lambda t: t.astype(F32)
    quad = lambda t: jnp.concatenate([t, t, t, t], axis=-1)
    a_re, a_im, b_re, b_im, c_re, c_im = map(f, (a_re, a_im, b_re, b_im, c_re, c_im))
    g, p, h = a_re.shape[1], S5_STATE, S5_GROUP
    dt = jnp.exp(f(log_dt))[..., None]
    mag = jnp.exp(a_re * dt)
    abar_re = mag * jnp.cos(a_im * dt)
    abar_im = mag * jnp.sin(a_im * dt)
    den = a_re * a_re + a_im * a_im
    nr = abar_re - 1.0
    ni = abar_im
    f_re = (nr * a_re + ni * a_im) / den
    f_im = (ni * a_re - nr * a_im) / den
    pw_re, pw_im = [jnp.ones_like(abar_re)], [jnp.zeros_like(abar_im)]
    for _ in range(S5_CHUNK):
        pre, pim = pw_re[-1], pw_im[-1]
        pw_re.append(pre * abar_re - pim * abar_im)
        pw_im.append(pre * abar_im + pim * abar_re)
    pad = [jnp.zeros_like(abar_re)] * (3 * SUBLANES - S5_CHUNK - 1)
    pw = jnp.stack([jnp.stack(pw_re + pad, axis=2), jnp.stack(pw_im + pad, axis=2)], axis=1)
    pw = jnp.transpose(quad(pw), (2, 0, 1, 3, 4))
    dsk = jnp.tile(f(d_skip).reshape(g, h), (1, S5_CW // h))
    sgn = jnp.broadcast_to(jnp.repeat(jnp.array([-1.0, 1.0, 1.0, -1.0], F32), p), (g, 4 * p))
    zero = jnp.zeros_like(dsk)
    rows = jnp.stack([quad(f_re[0]), quad(f_im[0]), quad(f_re[1]), quad(f_im[1]), dsk, sgn, zero, zero],
                     axis=1)
    bt_re = jnp.swapaxes(b_re, -1, -2)
    bt_im = jnp.swapaxes(b_im, -1, -2)
    bpk = jnp.stack([jnp.concatenate([bt_re, bt_im, bt_im, bt_re], axis=-1),
                     jnp.concatenate([bt_im, bt_re, bt_re, bt_im], axis=-1)], axis=1)
    cpk = jnp.stack([jnp.concatenate([c_re, -c_im], axis=-1),
                     jnp.concatenate([-c_im, -c_re], axis=-1)], axis=1)
    bpk = jnp.transpose(bpk, (2, 0, 1, 3, 4))
    cpk = jnp.transpose(cpk, (2, 0, 1, 3, 4))
    per = S5_OP_GROUPS_PER_STEP
    grp = lambda shape: pl.BlockSpec((per,) + shape, lambda i: (i,) + (0,) * len(shape))
    return pl.pallas_call(
        _s5_operator_kernel,
        grid=(g // per,),
        in_specs=[grp((8, 4 * p)), grp((2, 2, 3 * SUBLANES, 4 * p)), grp((2, 2, h, 4 * p)),
                  grp((2, 2, h, 2 * p))],
        out_specs=[grp((S5_CW, 3 * S5_CW)), grp((4 * p, S5_CW)), grp((SUBLANES, 2 * p))],
        out_shape=[jax.ShapeDtypeStruct((g, S5_CW, 3 * S5_CW), BF16),
                   jax.ShapeDtypeStruct((g, 4 * p, S5_CW), BF16),
                   jax.ShapeDtypeStruct((g, SUBLANES, 2 * p), F32)],
        compiler_params=pltpu.CompilerParams(dimension_semantics=("parallel",)),
        name="s5_operators",
    )(rows, pw, bpk, cpk)


def _shift_lanes(x, s):
    lo, hi = x[:, :LANES], x[:, LANES:]
    lane = lax.broadcasted_iota(jnp.int32, lo.shape, 1)
    zero = jnp.zeros_like(lo)
    rot = lambda t, r: pltpu.roll(t, r, 1) if r % LANES else t
    if s >= 0:
        if s < LANES:
            rl, rh = rot(lo, s), rot(hi, s)
            out = (jnp.where(lane >= s, rl, 0.0), jnp.where(lane >= s, rh, rl))
        else:
            rl = rot(lo, s - LANES)
            out = (zero, jnp.where(lane >= s - LANES, rl, 0.0))
    else:
        s = -s
        if s < LANES:
            rl, rh = rot(lo, LANES - s), rot(hi, LANES - s)
            out = (jnp.where(lane < LANES - s, rl, rh), jnp.where(lane < LANES - s, rh, 0.0))
        else:
            rh = rot(hi, 2 * LANES - s)
            out = (jnp.where(lane < 2 * LANES - s, rh, 0.0), zero)
    return jnp.concatenate(out, axis=1)


def _s5_operator_kernel(*refs):
    for gi in range(S5_OP_GROUPS_PER_STEP):
        _s5_group_operators(*(r.at[gi] for r in refs))


def _s5_group_operators(rows_ref, pw_ref, b_ref, c_ref, w1_ref, w2_ref, coef_ref):
    t_len = S5_CHUNK
    half = 2 * S5_STATE
    sgn = rows_ref[5:6, :]
    mxu_operand = lambda t: t.astype(BF16).astype(F32)
    a4, bx4, pr4, pi4, g2 = [], [], [], [], []
    for d in range(2):
        f_re = rows_ref[2 * d:2 * d + 1, :]
        f_im = rows_ref[2 * d + 1:2 * d + 2, :]
        p1 = b_ref[d, 0]
        p2 = b_ref[d, 1]
        a4.append(mxu_operand(f_re * p1 + sgn * f_im * p2))
        bx4.append(mxu_operand(f_re * p2 - sgn * f_im * p1))
        pr4.append(pw_ref[d, 0])
        pi4.append(pw_ref[d, 1])
        cx = jnp.concatenate([mxu_operand(c_ref[d, 0])] * t_len, axis=0)
        cy = jnp.concatenate([mxu_operand(c_ref[d, 1])] * t_len, axis=0)

        def ca(order, d=d, cx=cx, cy=cy):
            pr = jnp.concatenate(
                [jnp.broadcast_to(pr4[d][t:t + 1, :half], (S5_GROUP, half)) for t in order], axis=0)
            pi = jnp.concatenate(
                [jnp.broadcast_to(pi4[d][t:t + 1, :half], (S5_GROUP, half)) for t in order], axis=0)
            return cx * pr + cy * pi
        g2.append(ca)

    def strip(d, order):
        return lax.dot_general(a4[d][:, :half], g2[d](order), (((1,), (1,)), ((), ())),
                               precision=lax.Precision.HIGHEST, preferred_element_type=F32)

    row = lax.broadcasted_iota(jnp.int32, (S5_GROUP, S5_CW), 0)
    col = lax.broadcasted_iota(jnp.int32, (S5_GROUP, S5_CW), 1)
    k_fwd = strip(0, range(t_len)) + jnp.where(row == col, rows_ref[4:5, :], 0.0)
    k_bwd = strip(1, [t_len - 1 - k for k in range(t_len)])
    for j in range(t_len):
        lo, hi = j * S5_GROUP, (j + 1) * S5_GROUP
        m = _shift_lanes(k_fwd, j * S5_GROUP) + _shift_lanes(k_bwd, -(t_len - 1 - j) * S5_GROUP)
        w1_ref[lo:hi, 0:S5_CW] = m.astype(w1_ref.dtype)
        tf, tb = t_len - 1 - j, j
        s_f = a4[0] * pr4[0][tf:tf + 1, :] + bx4[0] * (sgn * pi4[0][tf:tf + 1, :])
        s_b = a4[1] * pr4[1][tb:tb + 1, :] + bx4[1] * (sgn * pi4[1][tb:tb + 1, :])
        w1_ref[lo:hi, S5_CW:2 * S5_CW] = s_f.astype(w1_ref.dtype)
        w1_ref[lo:hi, 2 * S5_CW:3 * S5_CW] = s_b.astype(w1_ref.dtype)
    w2_ref[0:half, :] = g2[0]([i + 1 for i in range(t_len)]).T.astype(w2_ref.dtype)
    w2_ref[half:2 * half, :] = g2[1]([t_len - i for i in range(t_len)]).T.astype(w2_ref.dtype)
    zero = jnp.zeros((1, half), F32)
    coef_ref[...] = jnp.concatenate(
        [pr4[0][t_len:t_len + 1, :half], (sgn * pi4[0][t_len:t_len + 1, :])[:, :half],
         pr4[1][t_len:t_len + 1, :half], (sgn * pi4[1][t_len:t_len + 1, :])[:, :half],
         zero, zero, zero, zero], axis=0)


def _s5_kernel(*refs, bsz, ctx_chunks, lat_chunks):
    sw = 2 * S5_STATE
    n_chunks = ctx_chunks + lat_chunks
    shape = (bsz, sw)
    groups = []
    for gi in range(S5_SCAN_GROUPS_PER_STEP):
        (xc_ref, xl_ref, w1_ref, w2_ref, coef_ref, yc_ref, yl_ref,
         zic_ref, zil_ref, zsc_ref, zsl_ref, hc_ref, hl_ref) = (r.at[gi] for r in refs)
        streams = ((xc_ref, zic_ref, zsc_ref, hc_ref, yc_ref, ctx_chunks),
                   (xl_ref, zil_ref, zsl_ref, hl_ref, yl_ref, lat_chunks))
        groups.append((streams, w2_ref, coef_ref))
        w1 = w1_ref[...]
        for x_ref, zi_ref, zs_ref, _, _, n in streams:
            pitch = n + SUBLANES
            per_dot = min(bsz, max(1, S5_DOT_ROWS // n))
            for b0 in range(0, bsz, per_dot):
                zz = jnp.dot(x_ref[b0 * n:(b0 + per_dot) * n, :], w1, preferred_element_type=F32)
                for b in range(b0, b0 + per_dot):
                    z = zz[(b - b0) * n:(b - b0 + 1) * n]
                    zi_ref[b * n:(b + 1) * n, :] = z[:, 0:S5_CW]
                    for k in range(4):
                        zs_ref[k, b * pitch:b * pitch + n, :] = z[:, S5_CW + k * sw:S5_CW + (k + 1) * sw]

    def locate(streams, c):
        (_, _, zsc_ref, hc_ref, _, _), (_, _, zsl_ref, hl_ref, _, _) = streams
        if c < ctx_chunks:
            return zsc_ref, hc_ref, c, ctx_chunks + SUBLANES
        return zsl_ref, hl_ref, c - ctx_chunks, lat_chunks + SUBLANES

    coefs = [[jnp.broadcast_to(coef_ref[k:k + 1, :], shape) for k in range(4)] for _, _, coef_ref in groups]
    state = [[jnp.zeros(shape, F32)] * 4 for _ in groups]
    for t in range(n_chunks):
        cb = ctx_chunks - 1 - t if t < ctx_chunks else n_chunks + ctx_chunks - 1 - t
        for gi, (streams, _, _) in enumerate(groups):
            a1f, a2f, a1b, a2b = coefs[gi]
            hf, hfs, hb, hbs = state[gi]
            zs_ref, h_ref, c, pitch = locate(streams, t)
            rows = pl.ds(c, bsz, stride=pitch)
            h_ref[0, rows, :] = hf
            lf = zs_ref[0, rows, :]
            lfs = zs_ref[1, rows, :]
            hf, hfs = a1f * hf + a2f * hfs + lf, a1f * hfs - a2f * hf + lfs
            zs_ref, h_ref, c, pitch = locate(streams, cb)
            rows = pl.ds(c, bsz, stride=pitch)
            h_ref[1, rows, :] = hb
            lb = zs_ref[2, rows, :]
            lbs = zs_ref[3, rows, :]
            hb, hbs = a1b * hb + a2b * hbs + lb, a1b * hbs - a2b * hb + lbs
            state[gi] = [hf, hfs, hb, hbs]

    for streams, w2_ref, _ in groups:
        w2 = w2_ref[...]
        for _, zi_ref, _, h_ref, y_ref, n in streams:
            pitch = n + SUBLANES
            per_dot = min(bsz, max(1, S5_DOT_ROWS // n))
            for b0 in range(0, bsz, per_dot):
                hin = jnp.concatenate(
                    [jnp.concatenate([h_ref[0, b * pitch:b * pitch + n, :], h_ref[1, b * pitch:b * pitch + n, :]],
                                     axis=1) for b in range(b0, b0 + per_dot)], axis=0)
                rows = slice(b0 * n, (b0 + per_dot) * n)
                y = zi_ref[rows, :] + jnp.dot(hin.astype(BF16), w2, preferred_element_type=F32)
                y_ref[rows, :] = y.astype(y_ref.dtype)


def _s5_scan(x_ctx, x_lat, w1, w2, coef, layer, bsz):
    ctx_chunks = x_ctx.shape[1] // bsz
    lat_chunks = x_lat.shape[1] // bsz
    sw = 2 * S5_STATE
    per = S5_SCAN_GROUPS_PER_STEP
    grp = lambda rows, cols: pl.BlockSpec((per, rows, cols), lambda g: (g, 0, 0))
    opr = lambda rows, cols: pl.BlockSpec((None, per, rows, cols), lambda g: (layer, g, 0, 0))
    slab = lambda k, n: pltpu.VMEM((per, k, bsz * (n + SUBLANES), sw), F32)
    return pl.pallas_call(
        functools.partial(_s5_kernel, bsz=bsz, ctx_chunks=ctx_chunks, lat_chunks=lat_chunks),
        grid=(S5_GROUPS // per,),
        in_specs=[
            grp(bsz * ctx_chunks, S5_CW), grp(bsz * lat_chunks, S5_CW),
            opr(S5_CW, 3 * S5_CW), opr(4 * S5_STATE, S5_CW), opr(SUBLANES, sw),
        ],
        out_specs=[grp(bsz * ctx_chunks, S5_CW), grp(bsz * lat_chunks, S5_CW)],
        out_shape=[jax.ShapeDtypeStruct(x_ctx.shape, F32), jax.ShapeDtypeStruct(x_lat.shape, F32)],
        scratch_shapes=[
            pltpu.VMEM((per, bsz * ctx_chunks, S5_CW), F32), pltpu.VMEM((per, bsz * lat_chunks, S5_CW), F32),
            slab(4, ctx_chunks), slab(4, lat_chunks), slab(2, ctx_chunks), slab(2, lat_chunks),
        ],
        compiler_params=pltpu.CompilerParams(
            dimension_semantics=("parallel",), vmem_limit_bytes=VMEM_LIMIT),
        name="s5_scan",
    )(x_ctx, x_lat, w1, w2, coef)


def _gelu_tanh(x):
    return 0.5 * x * (1.0 + jnp.tanh(math.sqrt(2.0 / math.pi) * (x + 0.044715 * (x * x * x))))


def _silu(x):
    return x * jax.nn.sigmoid(x)


def _merge_kernel(h_ref, mod_ref, npre_ref, win_ref, wa_ref, oa_ref, yb_ref, oc_ref,
                  wglu_ref, bglu_ref, woa_ref, wob_ref, woc_ref, wout_ref, npost_ref, o_ref, stage_ref):
    gate_cols = (None, SEGS["bg"], SEGS["cg"])
    rows = h_ref.shape[0]
    parts = [slice(r, r + rows // MERGE_PARTS) for r in range(0, rows, rows // MERGE_PARTS)]
    hn = [_modulated_norm(h_ref[rs, :], mod_ref, npre_ref) for rs in parts]

    def branch(k, i, o, w_ref):
        if i == 0:
            g = jnp.dot(hn[k], wa_ref[:, A_WIDTH:2 * A_WIDTH], preferred_element_type=F32)
        else:
            off, width = gate_cols[i]
            g = jnp.dot(hn[k], win_ref[:, off:off + width], preferred_element_type=F32)
        t = jnp.dot((o * _silu(g)).astype(BF16), w_ref[...], preferred_element_type=F32)
        m_lo = SEGS["mg"][0] + i * D_MODEL
        gate = jax.nn.sigmoid(jnp.dot(hn[k], win_ref[:, m_lo:m_lo + D_MODEL], preferred_element_type=F32))
        return gate * t

    y = [branch(k, 0, oa_ref[rs, :], woa_ref) + branch(k, 2, oc_ref[rs, :], woc_ref)
         for k, rs in enumerate(parts)]
    yb_all = _gelu_tanh(_groups_to_tokens(yb_ref, stage_ref))
    for k, rs in enumerate(parts):
        yb = yb_all[rs]
        glu = jnp.dot(yb.astype(BF16), wglu_ref[...], preferred_element_type=F32) + bglu_ref[...]
        y[k] = y[k] + branch(k, 1, yb * jax.nn.sigmoid(glu), wob_ref)
    gate = mod_ref[:, 2 * D_MODEL:3 * D_MODEL]
    for k, rs in enumerate(parts):
        yo = jnp.dot(y[k].astype(BF16), wout_ref[...], preferred_element_type=F32)
        ms = jnp.mean(yo * yo, axis=-1, keepdims=True)
        yn = yo * lax.rsqrt(ms + EPS) * npost_ref[...]
        o_ref[rs, :] = h_ref[rs, :] + gate * yn


def _merge(h, mod, mod_row, norm_pre, layer, w_bf16, w_a, o_a, y_b, o_c, wts):
    bsz, rows, _ = h.shape
    tile = min(ROW_TILE, rows)
    tiles = rows // tile
    if mod_row is None:
        mod_map = lambda b, i: (b, 0, 0)
    else:
        mod_map = lambda b, i: (mod_row, 0, 0)
    tok = lambda w: pl.BlockSpec((None, tile,w), lambda b, i: (b, i, 0))
    in_specs = [
        tok(D_MODEL),
        pl.BlockSpec((None, 1, 3 * D_MODEL), mod_map),
        _const_spec((1, D_MODEL)), _const_spec((D_MODEL, IN_WIDTH), layer),
        _const_spec((D_MODEL, 2 * A_WIDTH), layer),
        tok(A_WIDTH),
        pl.BlockSpec((S5_GROUPS, tile // S5_CHUNK, S5_CW), lambda b, i: (0, b * tiles + i, 0)),
        tok(C_WIDTH),
        _const_spec((B_WIDTH, B_WIDTH), layer), _const_spec((1, B_WIDTH)),
        _const_spec((A_WIDTH, D_MODEL), layer), _const_spec((B_WIDTH, D_MODEL), layer),
        _const_spec((C_WIDTH, D_MODEL), layer),
        _const_spec((D_MODEL, D_MODEL), layer), _const_spec((1, D_MODEL)),
    ]
    return pl.pallas_call(
        _merge_kernel,
        grid=(bsz, rows // tile),
        in_specs=in_specs,
        out_specs=tok(D_MODEL),
        out_shape=jax.ShapeDtypeStruct((bsz, rows, D_MODEL), F32),
        scratch_shapes=[pltpu.VMEM((B_WIDTH // LANES, (tile // S5_CHUNK) * STAGE_PITCH, LANES), F32)],
        compiler_params=pltpu.CompilerParams(
            dimension_semantics=("parallel", "parallel"), vmem_limit_bytes=VMEM_LIMIT),
        name="merge_out",
    )(h, mod, norm_pre.reshape(1, D_MODEL), w_bf16, w_a, o_a, y_b, o_c, *wts)


def _rope_tables(n_tokens):
    rows = n_tokens // GRID_W
    pos_r = jnp.repeat(jnp.arange(rows, dtype=F32), GRID_W)
    pos_c = jnp.tile(jnp.arange(GRID_W, dtype=F32), rows)
    inv = ROPE_BASE ** (-jnp.arange(AX_FREQS, dtype=F32) / AX_FREQS)
    ang_r = pos_r[:, None] * inv[None]
    ang_c = pos_c[:, None] * inv[None]
    ang = jnp.concatenate([ang_r, ang_r, ang_c, ang_c], axis=-1)
    cos, sin = jnp.cos(ang), jnp.sin(ang)
    first = (jnp.arange(HEAD_DIM) % (2 * AX_FREQS)) < AX_FREQS
    sa = jnp.where(first[None], -sin, 0.0)
    sb = jnp.where(first[None], 0.0, sin)
    tile2 = lambda t: jnp.concatenate([t, t], axis=-1)
    return tile2(cos), tile2(sa), tile2(sb)


def _reorder_a_heads(w, axis):
    axis = axis % w.ndim
    shape = w.shape
    split = shape[:axis] + (A_HEADS, HEAD_DIM) + shape[axis + 1:]
    return jnp.take(w.reshape(split), jnp.array(A_HEAD_ORDER), axis=axis).reshape(shape)


def _layers_as_groups(t):
    t = jnp.swapaxes(t, 0, 1)
    return t.reshape((2, DEPTH * S5_GROUPS) + t.shape[3:])


def kernel(x, c, ctx, c_ctx, w_mod, b_mod, norm_pre, norm_post, w_in, swa_sink, s5_a_re, s5_a_im, s5_log_dt, s5_b_re, s5_b_im, s5_c_re, s5_c_im, s5_d, s5_w_glu, s5_b_glu, diff_lq1, diff_lk1, diff_lq2, diff_lk2, diff_subln, w_o_a, w_o_b, w_o_c, w_out):
    bsz, n_lat, _ = x.shape
    n_ctx = ctx.shape[1]
    ctx_row = bsz
    cc = jnp.zeros((16, D_MODEL), F32).at[:bsz].set(c).at[ctx_row].set(c_ctx)
    mod_all = _modulation(cc, w_mod, b_mod)
    rope_tabs = _rope_tables(n_lat)

    cols = lambda name: w_in[..., SEGS[name][0]:SEGS[name][0] + SEGS[name][1]]
    w_in_all = w_in.astype(BF16)
    w_a_all = jnp.concatenate(
        [_reorder_a_heads(cols("aq"), -1), _reorder_a_heads(cols("ag"), -1)], axis=-1).astype(BF16)
    w_glu_all, w_ob_all, w_oc_all, w_out_all = (t.astype(BF16) for t in (s5_w_glu, w_o_b, w_o_c, w_out))
    w_oa_all = _reorder_a_heads(w_o_a, -2).astype(BF16)
    lam_all = jnp.stack([diff_lq1, diff_lk1, diff_lq2, diff_lk2], axis=1).astype(F32)
    s5_ops = _s5_operators(*(_layers_as_groups(t) for t in (s5_a_re, s5_a_im, s5_log_dt, s5_b_re, s5_b_im,
                                                            s5_c_re, s5_c_im)), s5_d.reshape(-1))
    w1_all, w2_all, coef_all = (t.reshape((DEPTH, S5_GROUPS) + t.shape[1:]) for t in s5_ops)

    h_lat, h_ctx = x, ctx
    for l in range(DEPTH):
        last = l == DEPTH - 1
        lam_init = 0.8 - 0.6 * math.exp(-0.3 * l)
        mod = mod_all[l].reshape(16, 1, 3 * D_MODEL)
        zl = _in_projection(h_lat, mod, None, norm_pre[l], w_in_all, w_a_all, l, PROJ_NAMES, rope_tabs)
        ctx_names = ("ak", "av", "bu", "ck", "cv") if last else PROJ_NAMES
        zc = _in_projection(h_ctx, mod, ctx_row, norm_pre[l], w_in_all, w_a_all, l, ctx_names, None)

        o_a_l = _swa(swa_sink[l], zl["aq"], zc["ak"], zc["av"], zl["ak"], zl["av"])

        y_b_c, y_b_l = _s5_scan(zc["bu"], zl["bu"], w1_all, w2_all, coef_all, l, bsz)

        lam_params = lam_all[l]
        o_c_l = _diff_attention(lam_params, diff_subln[l], lam_init, zl["cq"], zc["ck"], zc["cv"],
                                zl["ck"], zl["cv"])

        wts = (w_glu_all, s5_b_glu[l].reshape(1, B_WIDTH), w_oa_all, w_ob_all, w_oc_all,
               w_out_all, norm_post[l].reshape(1, D_MODEL))
        h_lat_new = _merge(h_lat, mod, None, norm_pre[l], l, w_in_all, w_a_all, o_a_l, y_b_l, o_c_l, wts)
        if not last:
            o_a_c = _swa(swa_sink[l], zc["aq"], zc["ak"], zc["av"])
            o_c_c = _diff_attention(lam_params, diff_subln[l], lam_init, zc["cq"], zc["ck"], zc["cv"])
            h_ctx = _merge(h_ctx, mod, ctx_row, norm_pre[l], l, w_in_all, w_a_all, o_a_c, y_b_c, o_c_c, wts)
        h_lat = h_lat_new
    return h_lat
```

```python
import functools
import math

import jax
import jax.numpy as jnp
from jax import lax
from jax.experimental import pallas as pl
from jax.experimental.pallas import tpu as pltpu

F32 = jnp.float32
BF16 = jnp.bfloat16

D_MODEL = 1024
DEPTH = 2
GRID_W = 64
HEAD_DIM = 64
WINDOW = 128
ROPE_BASE = 10000.0
AX_FREQS = HEAD_DIM // 4
NEG_INF = -1e30
A_HEADS = 8
A_KV_HEADS = 2
A_GROUP = A_HEADS // A_KV_HEADS
A_WIDTH = A_HEADS * HEAD_DIM
A_KV_WIDTH = A_KV_HEADS * HEAD_DIM
B_WIDTH = 512
S5_GROUP = 16
S5_GROUPS = B_WIDTH // S5_GROUP
S5_STATE = 64
C_HEADS = 4
C_WIDTH = C_HEADS * 2 * HEAD_DIM
EPS = 1e-6

LANES = 128
SUBLANES = 8

MOD_ROWS = 2 * SUBLANES
PROJ_ROW_TILE = 1024
ROW_TILE = 512
Q_BLOCK = 128
SWA_Q_ROWS = 512
DIFF_Q_ROWS = 512
DIFF_KEY_CHUNK = 512
DIFF_LOOKAHEAD = 8
SWA_LOOKAHEAD = 1
A_HEAD_ORDER = tuple(h for v in range(A_GROUP) for h in (v, A_GROUP + v))
S5_CHUNK = 16
S5_CW = S5_CHUNK * S5_GROUP
S5_DOT_ROWS = 256
S5_SCAN_GROUPS_PER_STEP = 4
S5_OP_GROUPS_PER_STEP = 4
STAGE_PITCH = S5_CHUNK + SUBLANES
VMEM_LIMIT = 56 * 1024 * 1024

_SEG_NAMES = ("aq", "ak", "av", "ag", "bu", "bg", "cq", "ck", "cv", "cg", "mg")
_SEG_WIDTHS = (A_WIDTH, A_KV_WIDTH, A_KV_WIDTH, A_WIDTH, B_WIDTH, B_WIDTH,
               C_WIDTH, C_WIDTH, C_WIDTH, C_WIDTH, 3 * D_MODEL)
SEGS = {}
_off = 0
for _n, _w in zip(_SEG_NAMES, _SEG_WIDTHS):
    SEGS[_n] = (_off, _w)
    _off += _w
IN_WIDTH = _off
PROJ_NAMES = ("aq", "ak", "av", "bu", "cq", "ck", "cv")
Q_SCALE = HEAD_DIM ** -0.5 * math.log2(math.e)
ROPE_SEGS = {"aq": Q_SCALE, "ak": 1.0, "cq": Q_SCALE, "ck": 1.0}


def _const_spec(shape, layer=None):
    nd = len(shape)
    if layer is None:
        return pl.BlockSpec(shape, lambda *_: (0,) * nd, pipeline_mode=pl.Buffered(1))
    return pl.BlockSpec((None,) + shape, lambda *_: (layer,) + (0,) * nd, pipeline_mode=pl.Buffered(1))


def _mod_kernel(c_ref, w_ref, b_ref, o_ref):
    c = c_ref[...]
    a = c * jax.nn.sigmoid(c)
    o_ref[...] = jnp.dot(a, w_ref[...], preferred_element_type=F32) + b_ref[...]


def _modulation(cc, w_mod, b_mod):
    n_tiles = 3
    return pl.pallas_call(
        _mod_kernel,
        grid=(DEPTH, n_tiles),
        in_specs=[
            pl.BlockSpec((MOD_ROWS, D_MODEL), lambda l, j: (0, 0)),
            pl.BlockSpec((None, D_MODEL, D_MODEL), lambda l, j: (l, 0, j)),
            pl.BlockSpec((None, 1, D_MODEL), lambda l, j: (l, 0, j)),
        ],
        out_specs=pl.BlockSpec((None, MOD_ROWS, D_MODEL), lambda l, j: (l, 0, j)),
        out_shape=jax.ShapeDtypeStruct((DEPTH, MOD_ROWS, 3 * D_MODEL), F32),
        name="modulation",
    )(cc, w_mod, b_mod.reshape(DEPTH, 1, 3 * D_MODEL))


def _block_transpose8(xs):
    lane_blk = lax.broadcasted_iota(jnp.int32, xs[0].shape, 1) // S5_GROUP
    for s in (4, 2, 1):
        keep = (lane_blk & s) == 0
        out = list(xs)
        for p in range(8):
            if p & s:
                continue
            a, b = xs[p], xs[p + s]
            out[p] = jnp.where(keep, a, pltpu.roll(b, S5_GROUP * s, 1))
            out[p + s] = jnp.where(keep, pltpu.roll(a, LANES - S5_GROUP * s, 1), b)
        xs = out
    return xs


def _tokens_to_groups(z, stage_ref, o_ref):
    n_chunks = z.shape[0] // S5_CHUNK
    for v in range(B_WIDTH // LANES):
        for c in range(n_chunks):
            stage_ref[v, c * STAGE_PITCH:c * STAGE_PITCH + S5_CHUNK, :] = \
                z[c * S5_CHUNK:(c + 1) * S5_CHUNK, v * LANES:(v + 1) * LANES]
    for w in range(S5_CW // LANES):
        for v in range(B_WIDTH // LANES):
            src = [stage_ref[v, pl.ds(8 * w + jj, n_chunks, stride=STAGE_PITCH), :] for jj in range(8)]
            dst = _block_transpose8(src)
            for gg in range(8):
                o_ref[8 * v + gg, :, w * LANES:(w + 1) * LANES] = dst[gg].astype(o_ref.dtype)


def _groups_to_tokens(y_ref, stage_ref):
    n_chunks = y_ref.shape[1]
    for w in range(S5_CW // LANES):
        for v in range(B_WIDTH // LANES):
            src = [y_ref[8 * v + gg, :, w * LANES:(w + 1) * LANES].astype(F32) for gg in range(8)]
            dst = _block_transpose8(src)
            for jj in range(8):
                stage_ref[v, pl.ds(8 * w + jj, n_chunks, stride=STAGE_PITCH), :] = dst[jj]
    rows = []
    for c in range(n_chunks):
        rows.append(jnp.concatenate(
            [stage_ref[v, c * STAGE_PITCH:c * STAGE_PITCH + S5_CHUNK, :] for v in range(B_WIDTH // LANES)], axis=1))
    return jnp.concatenate(rows, axis=0)


def _modulated_norm(x, mod_ref, nw_ref):
    ms = jnp.mean(x * x, axis=-1, keepdims=True)
    xn = x * lax.rsqrt(ms + EPS) * nw_ref[...]
    shift = mod_ref[:, 0:D_MODEL]
    scale = mod_ref[:, D_MODEL:2 * D_MODEL]
    return (xn * (1.0 + scale) + shift).astype(BF16)


def _inproj_kernel(*refs, names, use_rope):
    h_ref, mod_ref, nw_ref, w_ref, wa_ref = refs[:5]
    stage_ref = refs[-1]
    refs = refs[:-1]
    if use_rope:
        cos_ref, sa_ref, sb_ref = refs[5:8]
        outs = refs[8:]
    else:
        outs = refs[5:]
    hn = _modulated_norm(h_ref[...], mod_ref, nw_ref)
    kv_off = SEGS["ak"][0]
    assert SEGS["av"][0] == kv_off + A_KV_WIDTH
    kv = jnp.dot(hn, w_ref[:, kv_off:kv_off + 2 * A_KV_WIDTH], preferred_element_type=F32)

    def project(name):
        off, width = SEGS[name]
        if name in ("ak", "av"):
            return kv[:, off - kv_off:off - kv_off + width]
        if name == "aq":
            return jnp.dot(hn, wa_ref[:, 0:width], preferred_element_type=F32)
        return jnp.dot(hn, w_ref[:, off:off + width], preferred_element_type=F32)

    for name, o_ref in zip(names, outs):
        z = project(name)
        mul = ROPE_SEGS.get(name, 1.0)
        if use_rope and name in ROPE_SEGS:
            cos = cos_ref[...]
            sa = sa_ref[...]
            sb = sb_ref[...]
            for c in range(z.shape[1] // LANES):
                t = z[:, c * LANES:(c + 1) * LANES]
                r = t * cos + pltpu.roll(t, LANES - AX_FREQS, 1) * sa + pltpu.roll(t, AX_FREQS, 1) * sb
                if mul != 1.0:
                    r = r * mul
                o_ref[:, c * LANES:(c + 1) * LANES] = r.astype(o_ref.dtype)
        elif name == "bu":
            _tokens_to_groups(z, stage_ref, o_ref)
        elif name == "av":
            for t in range(z.shape[0] // Q_BLOCK):
                o_ref[t] = z[t * Q_BLOCK:(t + 1) * Q_BLOCK, :].T.astype(o_ref.dtype)
        elif name == "cv":
            o_ref[...] = z.T.astype(o_ref.dtype)
        else:
            if mul != 1.0:
                z = z * mul
            o_ref[...] = z.astype(o_ref.dtype)


def _in_projection(h, mod, mod_row, norm_w, w_bf16, w_a, layer, names, rope_tabs):
    bsz, rows, _ = h.shape
    tile = min(PROJ_ROW_TILE, rows)
    use_rope = rope_tabs is not None
    if mod_row is None:
        mod_map = lambda b, i: (b, 0, 0)
    else:
        mod_map = lambda b, i: (mod_row, 0, 0)
    in_specs = [
        pl.BlockSpec((None, tile,D_MODEL), lambda b, i: (b, i, 0)),
        pl.BlockSpec((None, 1, 3 * D_MODEL), mod_map),
        _const_spec((1, D_MODEL)),
        _const_spec((D_MODEL, IN_WIDTH), layer),
        _const_spec((D_MODEL, 2 * A_WIDTH), layer),
    ]
    args = [h, mod, norm_w.reshape(1, D_MODEL), w_bf16, w_a]
    if use_rope:
        for t in rope_tabs:
            in_specs.append(pl.BlockSpec((tile, LANES), lambda b, i: (i, 0)))
            args.append(t)
    tiles = rows // tile
    chunk_rows = tile // S5_CHUNK
    out_specs, out_shape = [], []
    for n in names:
        if n == "bu":
            out_specs.append(pl.BlockSpec((S5_GROUPS, chunk_rows, S5_CW), lambda b, i: (0, b * tiles + i, 0)))
            out_shape.append(jax.ShapeDtypeStruct((S5_GROUPS, bsz * rows // S5_CHUNK, S5_CW), BF16))
        elif n == "av":
            kb = tile // Q_BLOCK
            out_specs.append(pl.BlockSpec((None, kb, A_KV_WIDTH, Q_BLOCK), lambda b, i: (b, i, 0, 0)))
            out_shape.append(jax.ShapeDtypeStruct((bsz, rows // Q_BLOCK, A_KV_WIDTH, Q_BLOCK), BF16))
        elif n == "cv":
            out_specs.append(pl.BlockSpec((None, C_WIDTH, tile), lambda b, i: (b, 0, i)))
            out_shape.append(jax.ShapeDtypeStruct((bsz, C_WIDTH, rows), BF16))
        else:
            out_specs.append(pl.BlockSpec((None, tile,SEGS[n][1]), lambda b, i: (b, i, 0)))
            out_shape.append(jax.ShapeDtypeStruct((bsz, rows, SEGS[n][1]), BF16))
    outs = pl.pallas_call(
        functools.partial(_inproj_kernel, names=tuple(names), use_rope=use_rope),
        grid=(bsz, tiles),
        in_specs=in_specs,
        out_specs=out_specs,
        out_shape=out_shape,
        scratch_shapes=[pltpu.VMEM((B_WIDTH // LANES, (tile // S5_CHUNK) * STAGE_PITCH, LANES), F32)],
        compiler_params=pltpu.CompilerParams(
            dimension_semantics=("parallel", "parallel"), vmem_limit_bytes=VMEM_LIMIT),
        name="in_projection_rope" if use_rope else "in_projection",
    )(*args)
    return dict(zip(names, outs))


def _dot_nt(a, b):
    return lax.dot_general(a, b, (((1,), (1,)), ((), ())), preferred_element_type=F32)


def _half_masked(q, lower):
    lane = lax.broadcasted_iota(jnp.int32, q.shape, 1)
    keep = lane < HEAD_DIM if lower else lane >= HEAD_DIM
    return jnp.where(keep, q.astype(F32), 0.0).astype(q.dtype)


def _swa_kernel(*refs, has_local, lat_blocks):
    if has_local:
        sink_ref, q_ref, kc_ref, vtc_ref, kl_ref, vtl_ref, o_ref = refs
    else:
        sink_ref, q_ref, kc_ref, vtc_ref, o_ref = refs
    n_ctx = kc_ref.shape[0]
    n_items = q_ref.shape[0] // Q_BLOCK
    sink = jnp.concatenate(
        [jnp.full((1, Q_BLOCK), sink_ref[kh * A_GROUP + v] * math.log2(math.e), F32)
         for kh in range(A_KV_HEADS) for v in range(A_GROUP)], axis=1)

    def scores(t):
        qs = slice(t * Q_BLOCK, (t + 1) * Q_BLOCK)
        qblk = jnp.concatenate(
            [_half_masked(q_ref[qs, v * LANES:(v + 1) * LANES], kh == 0)
             for kh in range(A_KV_HEADS) for v in range(A_GROUP)], axis=0)
        keys = [kc_ref[...]]
        vts = [vtc_ref[c] for c in range(n_ctx // Q_BLOCK)]
        if has_local:
            n = pl.program_id(1) * n_items + t
            sb = jnp.clip(n - 1, 0, lat_blocks - 3)
            keys.append(kl_ref[pl.ds(pl.multiple_of(sb * Q_BLOCK, Q_BLOCK), 3 * Q_BLOCK), :])
            vts += [vtl_ref[sb + c] for c in range(3)]
        s_t = _dot_nt(jnp.concatenate(keys, axis=0), qblk)
        if has_local:
            r = lax.broadcasted_iota(jnp.int32, (3 * Q_BLOCK, Q_BLOCK), 0)
            i = lax.broadcasted_iota(jnp.int32, (3 * Q_BLOCK, Q_BLOCK), 1)
            in_band = jnp.abs((n - sb) * Q_BLOCK + i - r) <= WINDOW
            bias = jnp.where(in_band, 0.0, NEG_INF)
            s_t = jnp.concatenate([s_t[:n_ctx], s_t[n_ctx:] + jnp.concatenate([bias] * A_HEADS, axis=1)],
                                  axis=0)
        return s_t, jnp.concatenate(vts, axis=1)

    lane = lax.broadcasted_iota(jnp.int32, (Q_BLOCK, LANES), 1)
    half = A_GROUP * Q_BLOCK
    queue = [scores(t) for t in range(min(SWA_LOOKAHEAD, n_items))]
    for t in range(n_items):
        s_t, vt = queue.pop(0)
        if t + SWA_LOOKAHEAD < n_items:
            queue.append(scores(t + SWA_LOOKAHEAD))
        m = jnp.maximum(jnp.max(s_t, axis=0, keepdims=True), sink)
        e_t = jnp.exp2(s_t - m)
        den = jnp.sum(e_t, axis=0, keepdims=True) + jnp.exp2(sink - m)
        o_t = jnp.dot(vt, e_t.astype(BF16), preferred_element_type=F32) * (1.0 / den)
        qs = slice(t * Q_BLOCK, (t + 1) * Q_BLOCK)
        for v in range(A_GROUP):
            lower = o_t[:, v * Q_BLOCK:(v + 1) * Q_BLOCK].T
            upper = o_t[:, half + v * Q_BLOCK:half + (v + 1) * Q_BLOCK].T
            o_ref[qs, v * LANES:(v + 1) * LANES] = jnp.where(lane < HEAD_DIM, lower, upper).astype(o_ref.dtype)


def _swa(sink, q, k_ctx, vt_ctx, k_lat=None, vt_lat=None):
    bsz, rows, _ = q.shape
    qrows = min(SWA_Q_ROWS, rows)
    n_ctx = k_ctx.shape[1]
    has_local = k_lat is not None
    in_specs = [
        pl.BlockSpec(memory_space=pltpu.SMEM),
        pl.BlockSpec((None, qrows, A_WIDTH), lambda b, n: (b, n, 0)),
        pl.BlockSpec((None, n_ctx, A_KV_WIDTH), lambda b, n: (b, 0, 0)),
        pl.BlockSpec((None, n_ctx // Q_BLOCK, A_KV_WIDTH, Q_BLOCK), lambda b, n: (b, 0, 0, 0)),
    ]
    args = [sink, q, k_ctx, vt_ctx]
    lat_blocks = 0
    if has_local:
        n_lat = k_lat.shape[1]
        lat_blocks = n_lat // Q_BLOCK
        in_specs += [pl.BlockSpec((None, n_lat, A_KV_WIDTH), lambda b, n: (b, 0, 0)),
                     pl.BlockSpec((None, lat_blocks, A_KV_WIDTH, Q_BLOCK), lambda b, n: (b, 0, 0, 0))]
        args += [k_lat, vt_lat]
    return pl.pallas_call(
        functools.partial(_swa_kernel, has_local=has_local, lat_blocks=lat_blocks),
        grid=(bsz, rows // qrows),
        in_specs=in_specs,
        out_specs=pl.BlockSpec((None, qrows, A_WIDTH), lambda b, n: (b, n, 0)),
        out_shape=jax.ShapeDtypeStruct((bsz, rows, A_WIDTH), F32),
        compiler_params=pltpu.CompilerParams(
            dimension_semantics=("parallel", "arbitrary"), vmem_limit_bytes=VMEM_LIMIT),
        name="swa_latent" if has_local else "swa_context",
    )(*args)


def _diff_kernel(*refs, has_lat, lam_init):
    if has_lat:
        lam_ref, sub_ref, q_ref, kc_ref, vtc_ref, kl_ref, vtl_ref, o_ref = refs
    else:
        lam_ref, sub_ref, q_ref, kc_ref, vtc_ref, o_ref = refs
    lp = lam_ref[...]
    lam = (jnp.exp(jnp.sum(lp[0:1] * lp[1:2], axis=1, keepdims=True))
           - jnp.exp(jnp.sum(lp[2:3] * lp[3:4], axis=1, keepdims=True)) + lam_init)
    sub_w = sub_ref[...] * (1.0 - lam_init)
    width = 2 * HEAD_DIM
    chunks = [(kc_ref, vtc_ref, 0, kc_ref.shape[0])]
    if has_lat:
        chunks += [(kl_ref, vtl_ref, c, DIFF_KEY_CHUNK) for c in range(0, kl_ref.shape[0], DIFF_KEY_CHUNK)]
    heads = [slice(h * width, (h + 1) * width) for h in range(C_HEADS)]
    items = [(hs, slice(qb, qb + Q_BLOCK)) for qb in range(0, q_ref.shape[0], Q_BLOCK) for hs in heads]
    units = [(it, ci) for it in range(len(items)) for ci in range(len(chunks))]
    qblks = {}

    def scores(unit):
        it, ci = unit
        hs, qs = items[it]
        if it not in qblks:
            qblks[it] = jnp.concatenate(
                [_half_masked(q_ref[qs, hs], True), _half_masked(q_ref[qs, hs], False)], axis=0)
        k_ref, _, start, size = chunks[ci]
        return _dot_nt(k_ref[start:start + size, hs], qblks[it])

    queue = [scores(unit) for unit in units[:DIFF_LOOKAHEAD]]
    m = den = acc = None
    for u, (it, ci) in enumerate(units):
        hs, qs = items[it]
        _, vt_ref, start, size = chunks[ci]
        s_t = queue.pop(0)
        if u + DIFF_LOOKAHEAD < len(units):
            queue.append(scores(units[u + DIFF_LOOKAHEAD]))
        m_c = jnp.max(s_t, axis=0, keepdims=True)
        m_new = m_c if ci == 0 else jnp.maximum(m, m_c)
        p_t = jnp.exp2(s_t - m_new)
        den_c = jnp.sum(p_t, axis=0, keepdims=True)
        pv = jnp.dot(vt_ref[hs, start:start + size], p_t.astype(BF16), preferred_element_type=F32)
        if ci == 0:
            den, acc = den_c, pv
        else:
            alpha = jnp.exp2(m - m_new)
            den, acc = den * alpha + den_c, acc * alpha + pv
        m = m_new
        if ci == len(chunks) - 1:
            inv = 1.0 / den
            o_t = acc[:, :Q_BLOCK] * inv[:, :Q_BLOCK] - acc[:, Q_BLOCK:] * (lam * inv[:, Q_BLOCK:])
            o = o_t.T
            ms = jnp.mean(o * o, axis=1, keepdims=True)
            o_ref[qs, hs] = (o * lax.rsqrt(ms + EPS) * sub_w).astype(o_ref.dtype)


def _diff_attention(lam_params, subln, lam_init, q, k_ctx, vt_ctx, k_lat=None, vt_lat=None):
    bsz, rows, _ = q.shape
    qrows = min(DIFF_Q_ROWS, rows)
    n_ctx = k_ctx.shape[1]
    has_lat = k_lat is not None
    in_specs = [
        _const_spec((4, HEAD_DIM)),
        _const_spec((1, 2 * HEAD_DIM)),
        pl.BlockSpec((None, qrows, C_WIDTH), lambda b, n: (b, n, 0)),
        pl.BlockSpec((None, n_ctx, C_WIDTH), lambda b, n: (b, 0, 0)),
        pl.BlockSpec((None, C_WIDTH, n_ctx), lambda b, n: (b, 0, 0)),
    ]
    args = [lam_params, subln.reshape(1, 2 * HEAD_DIM), q, k_ctx, vt_ctx]
    if has_lat:
        n_lat = k_lat.shape[1]
        in_specs += [pl.BlockSpec((None, n_lat, C_WIDTH), lambda b, n: (b, 0, 0)),
                     pl.BlockSpec((None, C_WIDTH, n_lat), lambda b, n: (b, 0, 0))]
        args += [k_lat, vt_lat]
    return pl.pallas_call(
        functools.partial(_diff_kernel, has_lat=has_lat, lam_init=lam_init),
        grid=(bsz, rows // qrows),
        in_specs=in_specs,
        out_specs=pl.BlockSpec((None, qrows, C_WIDTH), lambda b, n: (b, n, 0)),
        out_shape=jax.ShapeDtypeStruct((bsz, rows, C_WIDTH), F32),
        compiler_params=pltpu.CompilerParams(
            dimension_semantics=("parallel", "arbitrary"), vmem_limit_bytes=VMEM_LIMIT),
        name="diff_latent" if has_lat else "diff_context",
    )(*args)


def _s5_operators(a_re, a_im, log_dt, b_re, b_im, c_re, c_im, d_skip):
    f = lambda t: t.astype(F32)
    quad = lambda t: jnp.concatenate([t, t, t, t], axis=-1)
    a_re, a_im, b_re, b_im, c_re, c_im = map(f, (a_re, a_im, b_re, b_im, c_re, c_im))
    g, p, h = a_re.shape[1], S5_STATE, S5_GROUP
    dt = jnp.exp(f(log_dt))[..., None]
    mag = jnp.exp(a_re * dt)
    abar_re = mag * jnp.cos(a_im * dt)
    abar_im = mag * jnp.sin(a_im * dt)
    den = a_re * a_re + a_im * a_im
    nr = abar_re - 1.0
    ni = abar_im
    f_re = (nr * a_re + ni * a_im) / den
    f_im = (ni * a_re - nr * a_im) / den
    pw_re, pw_im = [jnp.ones_like(abar_re)], [jnp.zeros_like(abar_im)]
    for _ in range(S5_CHUNK):
        pre, pim = pw_re[-1], pw_im[-1]
        pw_re.append(pre * abar_re - pim * abar_im)
        pw_im.append(pre * abar_im + pim * abar_re)
    pad = [jnp.zeros_like(abar_re)] * (3 * SUBLANES - S5_CHUNK - 1)
    pw = jnp.stack([jnp.stack(pw_re + pad, axis=2), jnp.stack(pw_im + pad, axis=2)], axis=1)
    pw = jnp.transpose(quad(pw), (2, 0, 1, 3, 4))
    dsk = jnp.tile(f(d_skip).reshape(g, h), (1, S5_CW // h))
    sgn = jnp.broadcast_to(jnp.repeat(jnp.array([-1.0, 1.0, 1.0, -1.0], F32), p), (g, 4 * p))
    zero = jnp.zeros_like(dsk)
    rows = jnp.stack([quad(f_re[0]), quad(f_im[0]), quad(f_re[1]), quad(f_im[1]), dsk, sgn, zero, zero],
                     axis=1)
    bt_re = jnp.swapaxes(b_re, -1, -2)
    bt_im = jnp.swapaxes(b_im, -1, -2)
    bpk = jnp.stack([jnp.concatenate([bt_re, bt_im, bt_im, bt_re], axis=-1),
                     jnp.concatenate([bt_im, bt_re, bt_re, bt_im], axis=-1)], axis=1)
    cpk = jnp.stack([jnp.concatenate([c_re, -c_im], axis=-1),
                     jnp.concatenate([-c_im, -c_re], axis=-1)], axis=1)
    bpk = jnp.transpose(bpk, (2, 0, 1, 3, 4))
    cpk = jnp.transpose(cpk, (2, 0, 1, 3, 4))
    per = S5_OP_GROUPS_PER_STEP
    grp = lambda shape: pl.BlockSpec((per,) + shape, lambda i: (i,) + (0,) * len(shape))
    return pl.pallas_call(
        _s5_operator_kernel,
        grid=(g // per,),
        in_specs=[grp((8, 4 * p)), grp((2, 2, 3 * SUBLANES, 4 * p)), grp((2, 2, h, 4 * p)),
                  grp((2, 2, h, 2 * p))],
        out_specs=[grp((S5_CW, 3 * S5_CW)), grp((4 * p, S5_CW)), grp((SUBLANES, 2 * p))],
        out_shape=[jax.ShapeDtypeStruct((g, S5_CW, 3 * S5_CW), BF16),
                   jax.ShapeDtypeStruct((g, 4 * p, S5_CW), BF16),
                   jax.ShapeDtypeStruct((g, SUBLANES, 2 * p), F32)],
        compiler_params=pltpu.CompilerParams(dimension_semantics=("parallel",)),
        name="s5_operators",
    )(rows, pw, bpk, cpk)


def _shift_lanes(x, s):
    lo, hi = x[:, :LANES], x[:, LANES:]
    lane = lax.broadcasted_iota(jnp.int32, lo.shape, 1)
    zero = jnp.zeros_like(lo)
    rot = lambda t, r: pltpu.roll(t, r, 1) if r % LANES else t
    if s >= 0:
        if s < LANES:
            rl, rh = rot(lo, s), rot(hi, s)
            out = (jnp.where(lane >= s, rl, 0.0), jnp.where(lane >= s, rh, rl))
        else:
            rl = rot(lo, s - LANES)
            out = (zero, jnp.where(lane >= s - LANES, rl, 0.0))
    else:
        s = -s
        if s < LANES:
            rl, rh = rot(lo, LANES - s), rot(hi, LANES - s)
            out = (jnp.where(lane < LANES - s, rl, rh), jnp.where(lane < LANES - s, rh, 0.0))
        else:
            rh = rot(hi, 2 * LANES - s)
            out = (jnp.where(lane < 2 * LANES - s, rh, 0.0), zero)
    return jnp.concatenate(out, axis=1)


def _s5_operator_kernel(*refs):
    for gi in range(S5_OP_GROUPS_PER_STEP):
        _s5_group_operators(*(r.at[gi] for r in refs))


def _s5_group_operators(rows_ref, pw_ref, b_ref, c_ref, w1_ref, w2_ref, coef_ref):
    t_len = S5_CHUNK
    half = 2 * S5_STATE
    sgn = rows_ref[5:6, :]
    mxu_operand = lambda t: t.astype(BF16).astype(F32)
    a4, bx4, pr4, pi4, g2 = [], [], [], [], []
    for d in range(2):
        f_re = rows_ref[2 * d:2 * d + 1, :]
        f_im = rows_ref[2 * d + 1:2 * d + 2, :]
        p1 = b_ref[d, 0]
        p2 = b_ref[d, 1]
        a4.append(mxu_operand(f_re * p1 + sgn * f_im * p2))
        bx4.append(mxu_operand(f_re * p2 - sgn * f_im * p1))
        pr4.append(pw_ref[d, 0])
        pi4.append(pw_ref[d, 1])
        cx = jnp.concatenate([mxu_operand(c_ref[d, 0])] * t_len, axis=0)
        cy = jnp.concatenate([mxu_operand(c_ref[d, 1])] * t_len, axis=0)

        def ca(order, d=d, cx=cx, cy=cy):
            pr = jnp.concatenate(
                [jnp.broadcast_to(pr4[d][t:t + 1, :half], (S5_GROUP, half)) for t in order], axis=0)
            pi = jnp.concatenate(
                [jnp.broadcast_to(pi4[d][t:t + 1, :half], (S5_GROUP, half)) for t in order], axis=0)
            return cx * pr + cy * pi
        g2.append(ca)

    def strip(d, order):
        return lax.dot_general(a4[d][:, :half], g2[d](order), (((1,), (1,)), ((), ())),
                               precision=lax.Precision.HIGHEST, preferred_element_type=F32)

    row = lax.broadcasted_iota(jnp.int32, (S5_GROUP, S5_CW), 0)
    col = lax.broadcasted_iota(jnp.int32, (S5_GROUP, S5_CW), 1)
    k_fwd = strip(0, range(t_len)) + jnp.where(row == col, rows_ref[4:5, :], 0.0)
    k_bwd = strip(1, [t_len - 1 - k for k in range(t_len)])
    for j in range(t_len):
        lo, hi = j * S5_GROUP, (j + 1) * S5_GROUP
        m = _shift_lanes(k_fwd, j * S5_GROUP) + _shift_lanes(k_bwd, -(t_len - 1 - j) * S5_GROUP)
        w1_ref[lo:hi, 0:S5_CW] = m.astype(w1_ref.dtype)
        tf, tb = t_len - 1 - j, j
        s_f = a4[0] * pr4[0][tf:tf + 1, :] + bx4[0] * (sgn * pi4[0][tf:tf + 1, :])
        s_b = a4[1] * pr4[1][tb:tb + 1, :] + bx4[1] * (sgn * pi4[1][tb:tb + 1, :])
        w1_ref[lo:hi, S5_CW:2 * S5_CW] = s_f.astype(w1_ref.dtype)
        w1_ref[lo:hi, 2 * S5_CW:3 * S5_CW] = s_b.astype(w1_ref.dtype)
    w2_ref[0:half, :] = g2[0]([i + 1 for i in range(t_len)]).T.astype(w2_ref.dtype)
    w2_ref[half:2 * half, :] = g2[1]([t_len - i for i in range(t_len)]).T.astype(w2_ref.dtype)
    zero = jnp.zeros((1, half), F32)
    coef_ref[...] = jnp.concatenate(
        [pr4[0][t_len:t_len + 1, :half], (sgn * pi4[0][t_len:t_len + 1, :])[:, :half],
         pr4[1][t_len:t_len + 1, :half], (sgn * pi4[1][t_len:t_len + 1, :])[:, :half],
         zero, zero, zero, zero], axis=0)


def _s5_kernel(*refs, bsz, ctx_chunks, lat_chunks):
    sw = 2 * S5_STATE
    n_chunks = ctx_chunks + lat_chunks
    shape = (bsz, sw)
    groups = []
    for gi in range(S5_SCAN_GROUPS_PER_STEP):
        (xc_ref, xl_ref, w1_ref, w2_ref, coef_ref, yc_ref, yl_ref,
         zic_ref, zil_ref, zsc_ref, zsl_ref, hc_ref, hl_ref) = (r.at[gi] for r in refs)
        streams = ((xc_ref, zic_ref, zsc_ref, hc_ref, yc_ref, ctx_chunks),
                   (xl_ref, zil_ref, zsl_ref, hl_ref, yl_ref, lat_chunks))
        groups.append((streams, w2_ref, coef_ref))
        w1 = w1_ref[...]
        for x_ref, zi_ref, zs_ref, _, _, n in streams:
            pitch = n + SUBLANES
            per_dot = min(bsz, max(1, S5_DOT_ROWS // n))
            for b0 in range(0, bsz, per_dot):
                zz = jnp.dot(x_ref[b0 * n:(b0 + per_dot) * n, :], w1, preferred_element_type=F32)
                for b in range(b0, b0 + per_dot):
                    z = zz[(b - b0) * n:(b - b0 + 1) * n]
                    zi_ref[b * n:(b + 1) * n, :] = z[:, 0:S5_CW]
                    for k in range(4):
                        zs_ref[k, b * pitch:b * pitch + n, :] = z[:, S5_CW + k * sw:S5_CW + (k + 1) * sw]

    def locate(streams, c):
        (_, _, zsc_ref, hc_ref, _, _), (_, _, zsl_ref, hl_ref, _, _) = streams
        if c < ctx_chunks:
            return zsc_ref, hc_ref, c, ctx_chunks + SUBLANES
        return zsl_ref, hl_ref, c - ctx_chunks, lat_chunks + SUBLANES

    coefs = [[jnp.broadcast_to(coef_ref[k:k + 1, :], shape) for k in range(4)] for _, _, coef_ref in groups]
    state = [[jnp.zeros(shape, F32)] * 4 for _ in groups]
    for t in range(n_chunks):
        cb = ctx_chunks - 1 - t if t < ctx_chunks else n_chunks + ctx_chunks - 1 - t
        for gi, (streams, _, _) in enumerate(groups):
            a1f, a2f, a1b, a2b = coefs[gi]
            hf, hfs, hb, hbs = state[gi]
            zs_ref, h_ref, c, pitch = locate(streams, t)
            rows = pl.ds(c, bsz, stride=pitch)
            h_ref[0, rows, :] = hf
            lf = zs_ref[0, rows, :]
            lfs = zs_ref[1, rows, :]
            hf, hfs = a1f * hf + a2f * hfs + lf, a1f * hfs - a2f * hf + lfs
            zs_ref, h_ref, c, pitch = locate(streams, cb)
            rows = pl.ds(c, bsz, stride=pitch)
            h_ref[1, rows, :] = hb
            lb = zs_ref[2, rows, :]
            lbs = zs_ref[3, rows, :]
            hb, hbs = a1b * hb + a2b * hbs + lb, a1b * hbs - a2b * hb + lbs
            state[gi] = [hf, hfs, hb, hbs]

    for streams, w2_ref, _ in groups:
        w2 = w2_ref[...]
        for _, zi_ref, _, h_ref, y_ref, n in streams:
            pitch = n + SUBLANES
            per_dot = min(bsz, max(1, S5_DOT_ROWS // n))
            for b0 in range(0, bsz, per_dot):
                hin = jnp.concatenate(
                    [jnp.concatenate([h_ref[0, b * pitch:b * pitch + n, :], h_ref[1, b * pitch:b * pitch + n, :]],
                                     axis=1) for b in range(b0, b0 + per_dot)], axis=0)
                rows = slice(b0 * n, (b0 + per_dot) * n)
                y = zi_ref[rows, :] + jnp.dot(hin.astype(BF16), w2, preferred_element_type=F32)
                y_ref[rows, :] = y.astype(y_ref.dtype)


def _s5_scan(x_ctx, x_lat, w1, w2, coef, layer, bsz):
    ctx_chunks = x_ctx.shape[1] // bsz
    lat_chunks = x_lat.shape[1] // bsz
    sw = 2 * S5_STATE
    per = S5_SCAN_GROUPS_PER_STEP
    grp = lambda rows, cols: pl.BlockSpec((per, rows, cols), lambda g: (g, 0, 0))
    opr = lambda rows, cols: pl.BlockSpec((None, per, rows, cols), lambda g: (layer, g, 0, 0))
    slab = lambda k, n: pltpu.VMEM((per, k, bsz * (n + SUBLANES), sw), F32)
    return pl.pallas_call(
        functools.partial(_s5_kernel, bsz=bsz, ctx_chunks=ctx_chunks, lat_chunks=lat_chunks),
        grid=(S5_GROUPS // per,),
        in_specs=[
            grp(bsz * ctx_chunks, S5_CW), grp(bsz * lat_chunks, S5_CW),
            opr(S5_CW, 3 * S5_CW), opr(4 * S5_STATE, S5_CW), opr(SUBLANES, sw),
        ],
        out_specs=[grp(bsz * ctx_chunks, S5_CW), grp(bsz * lat_chunks, S5_CW)],
        out_shape=[jax.ShapeDtypeStruct(x_ctx.shape, F32), jax.ShapeDtypeStruct(x_lat.shape, F32)],
        scratch_shapes=[
            pltpu.VMEM((per, bsz * ctx_chunks, S5_CW), F32), pltpu.VMEM((per, bsz * lat_chunks, S5_CW), F32),
            slab(4, ctx_chunks), slab(4, lat_chunks), slab(2, ctx_chunks), slab(2, lat_chunks),
        ],
        compiler_params=pltpu.CompilerParams(
            dimension_semantics=("parallel",), vmem_limit_bytes=VMEM_LIMIT),
        name="s5_scan",
    )(x_ctx, x_lat, w1, w2, coef)


def _gelu_tanh(x):
    return 0.5 * x * (1.0 + jnp.tanh(math.sqrt(2.0 / math.pi) * (x + 0.044715 * (x * x * x))))


def _silu(x):
    return x * jax.nn.sigmoid(x)


def _merge_kernel(h_ref, mod_ref, npre_ref, win_ref, wa_ref, oa_ref, yb_ref, oc_ref,
                  wglu_ref, bglu_ref, woa_ref, wob_ref, woc_ref, wout_ref, npost_ref, o_ref, stage_ref):
    gate_cols = (None, SEGS["bg"], SEGS["cg"])
    hn = _modulated_norm(h_ref[...], mod_ref, npre_ref)

    def branch(i, o, w_ref):
        if i == 0:
            g = jnp.dot(hn, wa_ref[:, A_WIDTH:2 * A_WIDTH], preferred_element_type=F32)
        else:
            off, width = gate_cols[i]
            g = jnp.dot(hn, win_ref[:, off:off + width], preferred_element_type=F32)
        t = jnp.dot((o * _silu(g)).astype(BF16), w_ref[...], preferred_element_type=F32)
        m_lo = SEGS["mg"][0] + i * D_MODEL
        gate = jax.nn.sigmoid(jnp.dot(hn, win_ref[:, m_lo:m_lo + D_MODEL], preferred_element_type=F32))
        return gate * t

    y = branch(0, oa_ref[...], woa_ref) + branch(2, oc_ref[...], woc_ref)
    yb = _gelu_tanh(_groups_to_tokens(yb_ref, stage_ref))
    glu = jnp.dot(yb.astype(BF16), wglu_ref[...], preferred_element_type=F32) + bglu_ref[...]
    y = y + branch(1, yb * jax.nn.sigmoid(glu), wob_ref)
    y = jnp.dot(y.astype(BF16), wout_ref[...], preferred_element_type=F32)
    ms = jnp.mean(y * y, axis=-1, keepdims=True)
    yn = y * lax.rsqrt(ms + EPS) * npost_ref[...]
    o_ref[...] = h_ref[...] + mod_ref[:, 2 * D_MODEL:3 * D_MODEL] * yn


def _merge(h, mod, mod_row, norm_pre, layer, w_bf16, w_a, o_a, y_b, o_c, wts):
    bsz, rows, _ = h.shape
    tile = min(ROW_TILE, rows)
    tiles = rows // tile
    if mod_row is None:
        mod_map = lambda b, i: (b, 0, 0)
    else:
        mod_map = lambda b, i: (mod_row, 0, 0)
    tok = lambda w: pl.BlockSpec((None, tile,w), lambda b, i: (b, i, 0))
    in_specs = [
        tok(D_MODEL),
        pl.BlockSpec((None, 1, 3 * D_MODEL), mod_map),
        _const_spec((1, D_MODEL)), _const_spec((D_MODEL, IN_WIDTH), layer),
        _const_spec((D_MODEL, 2 * A_WIDTH), layer),
        tok(A_WIDTH),
        pl.BlockSpec((S5_GROUPS, tile // S5_CHUNK, S5_CW), lambda b, i: (0, b * tiles + i, 0)),
        tok(C_WIDTH),
        _const_spec((B_WIDTH, B_WIDTH), layer), _const_spec((1, B_WIDTH)),
        _const_spec((A_WIDTH, D_MODEL), layer), _const_spec((B_WIDTH, D_MODEL), layer),
        _const_spec((C_WIDTH, D_MODEL), layer),
        _const_spec((D_MODEL, D_MODEL), layer), _const_spec((1, D_MODEL)),
    ]
    return pl.pallas_call(
        _merge_kernel,
        grid=(bsz, rows // tile),
        in_specs=in_specs,
        out_specs=tok(D_MODEL),
        out_shape=jax.ShapeDtypeStruct((bsz, rows, D_MODEL), F32),
        scratch_shapes=[pltpu.VMEM((B_WIDTH // LANES, (tile // S5_CHUNK) * STAGE_PITCH, LANES), F32)],
        compiler_params=pltpu.CompilerParams(
            dimension_semantics=("parallel", "parallel"), vmem_limit_bytes=VMEM_LIMIT),
        name="merge_out",
    )(h, mod, norm_pre.reshape(1, D_MODEL), w_bf16, w_a, o_a, y_b, o_c, *wts)


def _rope_tables(n_tokens):
    rows = n_tokens // GRID_W
    pos_r = jnp.repeat(jnp.arange(rows, dtype=F32), GRID_W)
    pos_c = jnp.tile(jnp.arange(GRID_W, dtype=F32), rows)
    inv = ROPE_BASE ** (-jnp.arange(AX_FREQS, dtype=F32) / AX_FREQS)
    ang_r = pos_r[:, None] * inv[None]
    ang_c = pos_c[:, None] * inv[None]
    ang = jnp.concatenate([ang_r, ang_r, ang_c, ang_c], axis=-1)
    cos, sin = jnp.cos(ang), jnp.sin(ang)
    first = (jnp.arange(HEAD_DIM) % (2 * AX_FREQS)) < AX_FREQS
    sa = jnp.where(first[None], -sin, 0.0)
    sb = jnp.where(first[None], 0.0, sin)
    tile2 = lambda t: jnp.concatenate([t, t], axis=-1)
    return tile2(cos), tile2(sa), tile2(sb)


def _reorder_a_heads(w, axis):
    axis = axis % w.ndim
    shape = w.shape
    split = shape[:axis] + (A_HEADS, HEAD_DIM) + shape[axis + 1:]
    return jnp.take(w.reshape(split), jnp.array(A_HEAD_ORDER), axis=axis).reshape(shape)


def _layers_as_groups(t):
    t = jnp.swapaxes(t, 0, 1)
    return t.reshape((2, DEPTH * S5_GROUPS) + t.shape[3:])


def kernel(x, c, ctx, c_ctx, w_mod, b_mod, norm_pre, norm_post, w_in, swa_sink, s5_a_re, s5_a_im, s5_log_dt, s5_b_re, s5_b_im, s5_c_re, s5_c_im, s5_d, s5_w_glu, s5_b_glu, diff_lq1, diff_lk1, diff_lq2, diff_lk2, diff_subln, w_o_a, w_o_b, w_o_c, w_out):
    bsz, n_lat, _ = x.shape
    n_ctx = ctx.shape[1]
    ctx_row = bsz
    assert bsz == SUBLANES, "the S5 scan kernel steps one sublane row per batch element"
    cc = jnp.zeros((MOD_ROWS, D_MODEL), F32).at[:bsz].set(c).at[ctx_row].set(c_ctx)
    mod_all = _modulation(cc, w_mod, b_mod)
    rope_tabs = _rope_tables(n_lat)

    cols = lambda name: w_in[..., SEGS[name][0]:SEGS[name][0] + SEGS[name][1]]
    w_in_all = w_in.astype(BF16)
    w_a_all = jnp.concatenate(
        [_reorder_a_heads(cols("aq"), -1), _reorder_a_heads(cols("ag"), -1)], axis=-1).astype(BF16)
    w_glu_all, w_ob_all, w_oc_all, w_out_all = (t.astype(BF16) for t in (s5_w_glu, w_o_b, w_o_c, w_out))
    w_oa_all = _reorder_a_heads(w_o_a, -2).astype(BF16)
    lam_all = jnp.stack([diff_lq1, diff_lk1, diff_lq2, diff_lk2], axis=1).astype(F32)
    s5_ops = _s5_operators(*(_layers_as_groups(t) for t in (s5_a_re, s5_a_im, s5_log_dt, s5_b_re, s5_b_im,
                                                            s5_c_re, s5_c_im)), s5_d.reshape(-1))
    w1_all, w2_all, coef_all = (t.reshape((DEPTH, S5_GROUPS) + t.shape[1:]) for t in s5_ops)

    h_lat, h_ctx = x, ctx
    for l in range(DEPTH):
        last = l == DEPTH - 1
        lam_init = 0.8 - 0.6 * math.exp(-0.3 * l)
        mod = mod_all[l].reshape(MOD_ROWS, 1, 3 * D_MODEL)
        zl = _in_projection(h_lat, mod, None, norm_pre[l], w_in_all, w_a_all, l, PROJ_NAMES, rope_tabs)
        ctx_names = ("ak", "av", "bu", "ck", "cv") if last else PROJ_NAMES
        zc = _in_projection(h_ctx, mod, ctx_row, norm_pre[l], w_in_all, w_a_all, l, ctx_names, None)

        o_a_l = _swa(swa_sink[l], zl["aq"], zc["ak"], zc["av"], zl["ak"], zl["av"])

        y_b_c, y_b_l = _s5_scan(zc["bu"], zl["bu"], w1_all, w2_all, coef_all, l, bsz)

        lam_params = lam_all[l]
        o_c_l = _diff_attention(lam_params, diff_subln[l], lam_init, zl["cq"], zc["ck"], zc["cv"],
                                zl["ck"], zl["cv"])

        wts = (w_glu_all, s5_b_glu[l].reshape(1, B_WIDTH), w_oa_all, w_ob_all, w_oc_all,
               w_out_all, norm_post[l].reshape(1, D_MODEL))
        h_lat_new = _merge(h_lat, mod, None, norm_pre[l], l, w_in_all, w_a_all, o_a_l, y_b_l, o_c_l, wts)
        if not last:
            o_a_c = _swa(swa_sink[l], zc["aq"], zc["ak"], zc["av"])
            o_c_c = _diff_attention(lam_params, diff_subln[l], lam_init, zc["cq"], zc["ck"], zc["cv"])
            h_ctx = _merge(h_ctx, mod, ctx_row, norm_pre[l], l, w_in_all, w_a_all, o_a_c, y_b_c, o_c_c, wts)
        h_lat = h_lat_new
    return h_lat
```

```python
import functools
import math

import jax
import jax.numpy as jnp
from jax import lax
from jax.experimental import pallas as pl
from jax.experimental.pallas import tpu as pltpu

F32 = jnp.float32
BF16 = jnp.bfloat16

D_MODEL = 1024
DEPTH = 2
GRID_W = 64
HEAD_DIM = 64
WINDOW = 128
ROPE_BASE = 10000.0
AX_FREQS = HEAD_DIM // 4
NEG_INF = -1e30
A_HEADS = 8
A_KV_HEADS = 2
A_GROUP = A_HEADS // A_KV_HEADS
A_WIDTH = A_HEADS * HEAD_DIM
A_KV_WIDTH = A_KV_HEADS * HEAD_DIM
B_WIDTH = 512
S5_GROUP = 16
S5_GROUPS = B_WIDTH // S5_GROUP
S5_STATE = 64
C_HEADS = 4
C_WIDTH = C_HEADS * 2 * HEAD_DIM
EPS = 1e-6

LANES = 128
SUBLANES = 8

MOD_ROWS = 2 * SUBLANES
PROJ_ROW_TILE = 1024
ROW_TILE = 512
Q_BLOCK = 128
SWA_Q_ROWS = 512
DIFF_Q_ROWS = 512
DIFF_KEY_CHUNK = 512
DIFF_LOOKAHEAD = 8
SWA_LOOKAHEAD = 1
A_HEAD_ORDER = tuple(h for v in range(A_GROUP) for h in (v, A_GROUP + v))
S5_CHUNK = 16
S5_CW = S5_CHUNK * S5_GROUP
S5_DOT_ROWS = 256
S5_SCAN_GROUPS_PER_STEP = 4
S5_OP_GROUPS_PER_STEP = 4
STAGE_PITCH = S5_CHUNK + SUBLANES
VMEM_LIMIT = 56 * 1024 * 1024

_SEG_NAMES = ("aq", "ak", "av", "ag", "bu", "bg", "cq", "ck", "cv", "cg", "mg")
_SEG_WIDTHS = (A_WIDTH, A_KV_WIDTH, A_KV_WIDTH, A_WIDTH, B_WIDTH, B_WIDTH,
               C_WIDTH, C_WIDTH, C_WIDTH, C_WIDTH, 3 * D_MODEL)
SEGS = {}
_off = 0
for _n, _w in zip(_SEG_NAMES, _SEG_WIDTHS):
    SEGS[_n] = (_off, _w)
    _off += _w
IN_WIDTH = _off
PROJ_NAMES = ("aq", "ak", "av", "bu", "cq", "ck", "cv")
Q_SCALE = HEAD_DIM ** -0.5 * math.log2(math.e)
ROPE_SEGS = {"aq": Q_SCALE, "ak": 1.0, "cq": Q_SCALE, "ck": 1.0}


def _const_spec(shape, layer=None):
    nd = len(shape)
    if layer is None:
        return pl.BlockSpec(shape, lambda *_: (0,) * nd, pipeline_mode=pl.Buffered(1))
    return pl.BlockSpec((None,) + shape, lambda *_: (layer,) + (0,) * nd, pipeline_mode=pl.Buffered(1))


def _mod_kernel(c_ref, w_ref, b_ref, o_ref):
    c = c_ref[...]
    a = c * jax.nn.sigmoid(c)
    o_ref[...] = jnp.dot(a, w_ref[...], preferred_element_type=F32) + b_ref[...]


def _modulation(cc, w_mod, b_mod):
    n_tiles = 3
    return pl.pallas_call(
        _mod_kernel,
        grid=(DEPTH, n_tiles),
        in_specs=[
            pl.BlockSpec((MOD_ROWS, D_MODEL), lambda l, j: (0, 0)),
            pl.BlockSpec((None, D_MODEL, D_MODEL), lambda l, j: (l, 0, j)),
            pl.BlockSpec((None, 1, D_MODEL), lambda l, j: (l, 0, j)),
        ],
        out_specs=pl.BlockSpec((None, MOD_ROWS, D_MODEL), lambda l, j: (l, 0, j)),
        out_shape=jax.ShapeDtypeStruct((DEPTH, MOD_ROWS, 3 * D_MODEL), F32),
        name="modulation",
    )(cc, w_mod, b_mod.reshape(DEPTH, 1, 3 * D_MODEL))


def _block_transpose8(xs):
    lane_blk = lax.broadcasted_iota(jnp.int32, xs[0].shape, 1) // S5_GROUP
    for s in (4, 2, 1):
        keep = (lane_blk & s) == 0
        out = list(xs)
        for p in range(8):
            if p & s:
                continue
            a, b = xs[p], xs[p + s]
            out[p] = jnp.where(keep, a, pltpu.roll(b, S5_GROUP * s, 1))
            out[p + s] = jnp.where(keep, pltpu.roll(a, LANES - S5_GROUP * s, 1), b)
        xs = out
    return xs


def _tokens_to_groups(z, stage_ref, o_ref):
    n_chunks = z.shape[0] // S5_CHUNK
    for v in range(B_WIDTH // LANES):
        for c in range(n_chunks):
            stage_ref[v, c * STAGE_PITCH:c * STAGE_PITCH + S5_CHUNK, :] = \
                z[c * S5_CHUNK:(c + 1) * S5_CHUNK, v * LANES:(v + 1) * LANES]
    for w in range(S5_CW // LANES):
        for v in range(B_WIDTH // LANES):
            src = [stage_ref[v, pl.ds(8 * w + jj, n_chunks, stride=STAGE_PITCH), :] for jj in range(8)]
            dst = _block_transpose8(src)
            for gg in range(8):
                o_ref[8 * v + gg, :, w * LANES:(w + 1) * LANES] = dst[gg].astype(o_ref.dtype)


def _groups_to_tokens(y_ref, stage_ref):
    n_chunks = y_ref.shape[1]
    for w in range(S5_CW // LANES):
        for v in range(B_WIDTH // LANES):
            src = [y_ref[8 * v + gg, :, w * LANES:(w + 1) * LANES].astype(F32) for gg in range(8)]
            dst = _block_transpose8(src)
            for jj in range(8):
                stage_ref[v, pl.ds(8 * w + jj, n_chunks, stride=STAGE_PITCH), :] = dst[jj]
    rows = []
    for c in range(n_chunks):
        rows.append(jnp.concatenate(
            [stage_ref[v, c * STAGE_PITCH:c * STAGE_PITCH + S5_CHUNK, :] for v in range(B_WIDTH // LANES)], axis=1))
    return jnp.concatenate(rows, axis=0)


def _modulated_norm(x, mod_ref, nw_ref):
    ms = jnp.mean(x * x, axis=-1, keepdims=True)
    xn = x * lax.rsqrt(ms + EPS) * nw_ref[...]
    shift = mod_ref[:, 0:D_MODEL]
    scale = mod_ref[:, D_MODEL:2 * D_MODEL]
    return (xn * (1.0 + scale) + shift).astype(BF16)


def _inproj_kernel(*refs, names, use_rope):
    h_ref, mod_ref, nw_ref, w_ref, wa_ref = refs[:5]
    stage_ref = refs[-1]
    refs = refs[:-1]
    if use_rope:
        cos_ref, sa_ref, sb_ref = refs[5:8]
        outs = refs[8:]
    else:
        outs = refs[5:]
    hn = _modulated_norm(h_ref[...], mod_ref, nw_ref)
    kv_off = SEGS["ak"][0]
    assert SEGS["av"][0] == kv_off + A_KV_WIDTH
    kv = jnp.dot(hn, w_ref[:, kv_off:kv_off + 2 * A_KV_WIDTH], preferred_element_type=F32)

    def project(name):
        off, width = SEGS[name]
        if name in ("ak", "av"):
            return kv[:, off - kv_off:off - kv_off + width]
        if name == "aq":
            return jnp.dot(hn, wa_ref[:, 0:width], preferred_element_type=F32)
        return jnp.dot(hn, w_ref[:, off:off + width], preferred_element_type=F32)

    for name, o_ref in zip(names, outs):
        z = project(name)
        mul = ROPE_SEGS.get(name, 1.0)
        if use_rope and name in ROPE_SEGS:
            cos = cos_ref[...]
            sa = sa_ref[...]
            sb = sb_ref[...]
            for c in range(z.shape[1] // LANES):
                t = z[:, c * LANES:(c + 1) * LANES]
                r = t * cos + pltpu.roll(t, LANES - AX_FREQS, 1) * sa + pltpu.roll(t, AX_FREQS, 1) * sb
                if mul != 1.0:
                    r = r * mul
                o_ref[:, c * LANES:(c + 1) * LANES] = r.astype(o_ref.dtype)
        elif name == "bu":
            _tokens_to_groups(z, stage_ref, o_ref)
        elif name == "av":
            for t in range(z.shape[0] // Q_BLOCK):
                o_ref[t] = z[t * Q_BLOCK:(t + 1) * Q_BLOCK, :].T.astype(o_ref.dtype)
        elif name == "cv":
            o_ref[...] = z.T.astype(o_ref.dtype)
        else:
            if mul != 1.0:
                z = z * mul
            o_ref[...] = z.astype(o_ref.dtype)


def _in_projection(h, mod, mod_row, norm_w, w_bf16, w_a, layer, names, rope_tabs):
    bsz, rows, _ = h.shape
    tile = min(PROJ_ROW_TILE, rows)
    use_rope = rope_tabs is not None
    if mod_row is None:
        mod_map = lambda b, i: (b, 0, 0)
    else:
        mod_map = lambda b, i: (mod_row, 0, 0)
    in_specs = [
        pl.BlockSpec((None, tile,D_MODEL), lambda b, i: (b, i, 0)),
        pl.BlockSpec((None, 1, 3 * D_MODEL), mod_map),
        _const_spec((1, D_MODEL)),
        _const_spec((D_MODEL, IN_WIDTH), layer),
        _const_spec((D_MODEL, 2 * A_WIDTH), layer),
    ]
    args = [h, mod, norm_w.reshape(1, D_MODEL), w_bf16, w_a]
    if use_rope:
        for t in rope_tabs:
            in_specs.append(pl.BlockSpec((tile, LANES), lambda b, i: (i, 0)))
            args.append(t)
    tiles = rows // tile
    chunk_rows = tile // S5_CHUNK
    out_specs, out_shape = [], []
    for n in names:
        if n == "bu":
            out_specs.append(pl.BlockSpec((S5_GROUPS, chunk_rows, S5_CW), lambda b, i: (0, b * tiles + i, 0)))
            out_shape.append(jax.ShapeDtypeStruct((S5_GROUPS, bsz * rows // S5_CHUNK, S5_CW), BF16))
        elif n == "av":
            kb = tile // Q_BLOCK
            out_specs.append(pl.BlockSpec((None, kb, A_KV_WIDTH, Q_BLOCK), lambda b, i: (b, i, 0, 0)))
            out_shape.append(jax.ShapeDtypeStruct((bsz, rows // Q_BLOCK, A_KV_WIDTH, Q_BLOCK), BF16))
        elif n == "cv":
            out_specs.append(pl.BlockSpec((None, C_WIDTH, tile), lambda b, i: (b, 0, i)))
            out_shape.append(jax.ShapeDtypeStruct((bsz, C_WIDTH, rows), BF16))
        else:
            out_specs.append(pl.BlockSpec((None, tile,SEGS[n][1]), lambda b, i: (b, i, 0)))
            out_shape.append(jax.ShapeDtypeStruct((bsz, rows, SEGS[n][1]), BF16))
    outs = pl.pallas_call(
        functools.partial(_inproj_kernel, names=tuple(names), use_rope=use_rope),
        grid=(bsz, tiles),
        in_specs=in_specs,
        out_specs=out_specs,
        out_shape=out_shape,
        scratch_shapes=[pltpu.VMEM((B_WIDTH // LANES, (tile // S5_CHUNK) * STAGE_PITCH, LANES), F32)],
        compiler_params=pltpu.CompilerParams(
            dimension_semantics=("parallel", "parallel"), vmem_limit_bytes=VMEM_LIMIT),
        name="in_projection_rope" if use_rope else "in_projection",
    )(*args)
    return dict(zip(names, outs))


def _dot_nt(a, b):
    return lax.dot_general(a, b, (((1,), (1,)), ((), ())), preferred_element_type=F32)


def _half_masked(q, lower):
    lane = lax.broadcasted_iota(jnp.int32, q.shape, 1)
    keep = lane < HEAD_DIM if lower else lane >= HEAD_DIM
    return jnp.where(keep, q.astype(F32), 0.0).astype(q.dtype)


def _swa_kernel(*refs, has_local, lat_blocks):
    if has_local:
        sink_ref, q_ref, kc_ref, vtc_ref, kl_ref, vtl_ref, o_ref = refs
    else:
        sink_ref, q_ref, kc_ref, vtc_ref, o_ref = refs
    n_ctx = kc_ref.shape[0]
    n_items = q_ref.shape[0] // Q_BLOCK
    sink = jnp.concatenate(
        [jnp.full((1, Q_BLOCK), sink_ref[kh * A_GROUP + v] * math.log2(math.e), F32)
         for kh in range(A_KV_HEADS) for v in range(A_GROUP)], axis=1)

    def scores(t):
        qs = slice(t * Q_BLOCK, (t + 1) * Q_BLOCK)
        qblk = jnp.concatenate(
            [_half_masked(q_ref[qs, v * LANES:(v + 1) * LANES], kh == 0)
             for kh in range(A_KV_HEADS) for v in range(A_GROUP)], axis=0)
        keys = [kc_ref[...]]
        vts = [vtc_ref[c] for c in range(n_ctx // Q_BLOCK)]
        if has_local:
            n = pl.program_id(1) * n_items + t
            sb = jnp.clip(n - 1, 0, lat_blocks - 3)
            keys.append(kl_ref[pl.ds(pl.multiple_of(sb * Q_BLOCK, Q_BLOCK), 3 * Q_BLOCK), :])
            vts += [vtl_ref[sb + c] for c in range(3)]
        s_t = _dot_nt(jnp.concatenate(keys, axis=0), qblk)
        if has_local:
            r = lax.broadcasted_iota(jnp.int32, (3 * Q_BLOCK, Q_BLOCK), 0)
            i = lax.broadcasted_iota(jnp.int32, (3 * Q_BLOCK, Q_BLOCK), 1)
            in_band = jnp.abs((n - sb) * Q_BLOCK + i - r) <= WINDOW
            bias = jnp.where(in_band, 0.0, NEG_INF)
            s_t = jnp.concatenate([s_t[:n_ctx], s_t[n_ctx:] + jnp.concatenate([bias] * A_HEADS, axis=1)],
                                  axis=0)
        return s_t, jnp.concatenate(vts, axis=1)

    lane = lax.broadcasted_iota(jnp.int32, (Q_BLOCK, LANES), 1)
    half = A_GROUP * Q_BLOCK
    queue = [scores(t) for t in range(min(SWA_LOOKAHEAD, n_items))]
    for t in range(n_items):
        s_t, vt = queue.pop(0)
        if t + SWA_LOOKAHEAD < n_items:
            queue.append(scores(t + SWA_LOOKAHEAD))
        m = jnp.maximum(jnp.max(s_t, axis=0, keepdims=True), sink)
        e_t = jnp.exp2(s_t - m)
        den = jnp.sum(e_t, axis=0, keepdims=True) + jnp.exp2(sink - m)
        o_t = jnp.dot(vt, e_t.astype(BF16), preferred_element_type=F32) * (1.0 / den)
        qs = slice(t * Q_BLOCK, (t + 1) * Q_BLOCK)
        for v in range(A_GROUP):
            lower = o_t[:, v * Q_BLOCK:(v + 1) * Q_BLOCK].T
            upper = o_t[:, half + v * Q_BLOCK:half + (v + 1) * Q_BLOCK].T
            o_ref[qs, v * LANES:(v + 1) * LANES] = jnp.where(lane < HEAD_DIM, lower, upper).astype(o_ref.dtype)


def _swa(sink, q, k_ctx, vt_ctx, k_lat=None, vt_lat=None):
    bsz, rows, _ = q.shape
    qrows = min(SWA_Q_ROWS, rows)
    n_ctx = k_ctx.shape[1]
    has_local = k_lat is not None
    in_specs = [
        pl.BlockSpec(memory_space=pltpu.SMEM),
        pl.BlockSpec((None, qrows, A_WIDTH), lambda b, n: (b, n, 0)),
        pl.BlockSpec((None, n_ctx, A_KV_WIDTH), lambda b, n: (b, 0, 0)),
        pl.BlockSpec((None, n_ctx // Q_BLOCK, A_KV_WIDTH, Q_BLOCK), lambda b, n: (b, 0, 0, 0)),
    ]
    args = [sink, q, k_ctx, vt_ctx]
    lat_blocks = 0
    if has_local:
        n_lat = k_lat.shape[1]
        lat_blocks = n_lat // Q_BLOCK
        in_specs += [pl.BlockSpec((None, n_lat, A_KV_WIDTH), lambda b, n: (b, 0, 0)),
                     pl.BlockSpec((None, lat_blocks, A_KV_WIDTH, Q_BLOCK), lambda b, n: (b, 0, 0, 0))]
        args += [k_lat, vt_lat]
    return pl.pallas_call(
        functools.partial(_swa_kernel, has_local=has_local, lat_blocks=lat_blocks),
        grid=(bsz, rows // qrows),
        in_specs=in_specs,
        out_specs=pl.BlockSpec((None, qrows, A_WIDTH), lambda b, n: (b, n, 0)),
        out_shape=jax.ShapeDtypeStruct((bsz, rows, A_WIDTH), F32),
        compiler_params=pltpu.CompilerParams(
            dimension_semantics=("parallel", "arbitrary"), vmem_limit_bytes=VMEM_LIMIT),
        name="swa_latent" if has_local else "swa_context",
    )(*args)


def _diff_kernel(*refs, has_lat, lam_init):
    if has_lat:
        lam_ref, sub_ref, q_ref, kc_ref, vtc_ref, kl_ref, vtl_ref, o_ref = refs
    else:
        lam_ref, sub_ref, q_ref, kc_ref, vtc_ref, o_ref = refs
    lp = lam_ref[...]
    lam = (jnp.exp(jnp.sum(lp[0:1] * lp[1:2], axis=1, keepdims=True))
           - jnp.exp(jnp.sum(lp[2:3] * lp[3:4], axis=1, keepdims=True)) + lam_init)
    sub_w = sub_ref[...] * (1.0 - lam_init)
    width = 2 * HEAD_DIM
    chunks = [(kc_ref, vtc_ref, 0, kc_ref.shape[0])]
    if has_lat:
        chunks += [(kl_ref, vtl_ref, c, DIFF_KEY_CHUNK) for c in range(0, kl_ref.shape[0], DIFF_KEY_CHUNK)]
    heads = [slice(h * width, (h + 1) * width) for h in range(C_HEADS)]
    items = [(hs, slice(qb, qb + Q_BLOCK)) for qb in range(0, q_ref.shape[0], Q_BLOCK) for hs in heads]
    units = [(it, ci) for it in range(len(items)) for ci in range(len(chunks))]
    qblks = {}

    def scores(unit):
        it, ci = unit
        hs, qs = items[it]
        if it not in qblks:
            qblks[it] = jnp.concatenate(
                [_half_masked(q_ref[qs, hs], True), _half_masked(q_ref[qs, hs], False)], axis=0)
        k_ref, _, start, size = chunks[ci]
        return _dot_nt(k_ref[start:start + size, hs], qblks[it])

    queue = [scores(unit) for unit in units[:DIFF_LOOKAHEAD]]
    m = den = acc = None
    for u, (it, ci) in enumerate(units):
        hs, qs = items[it]
        _, vt_ref, start, size = chunks[ci]
        s_t = queue.pop(0)
        if u + DIFF_LOOKAHEAD < len(units):
            queue.append(scores(units[u + DIFF_LOOKAHEAD]))
        m_c = jnp.max(s_t, axis=0, keepdims=True)
        m_new = m_c if ci == 0 else jnp.maximum(m, m_c)
        p_t = jnp.exp2(s_t - m_new)
        den_c = jnp.sum(p_t, axis=0, keepdims=True)
        pv = jnp.dot(vt_ref[hs, start:start + size], p_t.astype(BF16), preferred_element_type=F32)
        if ci == 0:
            den, acc = den_c, pv
        else:
            alpha = jnp.exp2(m - m_new)
            den, acc = den * alpha + den_c, acc * alpha + pv
        m = m_new
        if ci == len(chunks) - 1:
            inv = 1.0 / den
            o_t = acc[:, :Q_BLOCK] * inv[:, :Q_BLOCK] - acc[:, Q_BLOCK:] * (lam * inv[:, Q_BLOCK:])
            o = o_t.T
            ms = jnp.mean(o * o, axis=1, keepdims=True)
            o_ref[qs, hs] = (o * lax.rsqrt(ms + EPS) * sub_w).astype(o_ref.dtype)


def _diff_attention(lam_params, subln, lam_init, q, k_ctx, vt_ctx, k_lat=None, vt_lat=None):
    bsz, rows, _ = q.shape
    qrows = min(DIFF_Q_ROWS, rows)
    n_ctx = k_ctx.shape[1]
    has_lat = k_lat is not None
    in_specs = [
        _const_spec((4, HEAD_DIM)),
        _const_spec((1, 2 * HEAD_DIM)),
        pl.BlockSpec((None, qrows, C_WIDTH), lambda b, n: (b, n, 0)),
        pl.BlockSpec((None, n_ctx, C_WIDTH), lambda b, n: (b, 0, 0)),
        pl.BlockSpec((None, C_WIDTH, n_ctx), lambda b, n: (b, 0, 0)),
    ]
    args = [lam_params, subln.reshape(1, 2 * HEAD_DIM), q, k_ctx, vt_ctx]
    if has_lat:
        n_lat = k_lat.shape[1]
        in_specs += [pl.BlockSpec((None, n_lat, C_WIDTH), lambda b, n: (b, 0, 0)),
                     pl.BlockSpec((None, C_WIDTH, n_lat), lambda b, n: (b, 0, 0))]
        args += [k_lat, vt_lat]
    return pl.pallas_call(
        functools.partial(_diff_kernel, has_lat=has_lat, lam_init=lam_init),
        grid=(bsz, rows // qrows),
        in_specs=in_specs,
        out_specs=pl.BlockSpec((None, qrows, C_WIDTH), lambda b, n: (b, n, 0)),
        out_shape=jax.ShapeDtypeStruct((bsz, rows, C_WIDTH), F32),
        compiler_params=pltpu.CompilerParams(
            dimension_semantics=("parallel", "arbitrary"), vmem_limit_bytes=VMEM_LIMIT),
        name="diff_latent" if has_lat else "diff_context",
    )(*args)


def _s5_operators(a_re, a_im, log_dt, b_re, b_im, c_re, c_im, d_skip):
    f = lambda t: t.astype(F32)
    quad = lambda t: jnp.concatenate([t, t, t, t], axis=-1)
    a_re, a_im, b_re, b_im, c_re, c_im = map(f, (a_re, a_im, b_re, b_im, c_re, c_im))
    g, p, h = a_re.shape[1], S5_STATE, S5_GROUP
    dt = jnp.exp(f(log_dt))[..., None]
    mag = jnp.exp(a_re * dt)
    abar_re = mag * jnp.cos(a_im * dt)
    abar_im = mag * jnp.sin(a_im * dt)
    den = a_re * a_re + a_im * a_im
    nr = abar_re - 1.0
    ni = abar_im
    f_re = (nr * a_re + ni * a_im) / den
    f_im = (ni * a_re - nr * a_im) / den
    pw_re, pw_im = [jnp.ones_like(abar_re)], [jnp.zeros_like(abar_im)]
    for _ in range(S5_CHUNK):
        pre, pim = pw_re[-1], pw_im[-1]
        pw_re.append(pre * abar_re - pim * abar_im)
        pw_im.append(pre * abar_im + pim * abar_re)
    pad = [jnp.zeros_like(abar_re)] * (3 * SUBLANES - S5_CHUNK - 1)
    pw = jnp.stack([jnp.stack(pw_re + pad, axis=2), jnp.stack(pw_im + pad, axis=2)], axis=1)
    pw = jnp.transpose(quad(pw), (2, 0, 1, 3, 4))
    dsk = jnp.tile(f(d_skip).reshape(g, h), (1, S5_CW // h))
    sgn = jnp.broadcast_to(jnp.repeat(jnp.array([-1.0, 1.0, 1.0, -1.0], F32), p), (g, 4 * p))
    zero = jnp.zeros_like(dsk)
    rows = jnp.stack([quad(f_re[0]), quad(f_im[0]), quad(f_re[1]), quad(f_im[1]), dsk, sgn, zero, zero],
                     axis=1)
    bt_re = jnp.swapaxes(b_re, -1, -2)
    bt_im = jnp.swapaxes(b_im, -1, -2)
    bpk = jnp.stack([jnp.concatenate([bt_re, bt_im, bt_im, bt_re], axis=-1),
                     jnp.concatenate([bt_im, bt_re, bt_re, bt_im], axis=-1)], axis=1)
    cpk = jnp.stack([jnp.concatenate([c_re, -c_im], axis=-1),
                     jnp.concatenate([-c_im, -c_re], axis=-1)], axis=1)
    bpk = jnp.transpose(bpk, (2, 0, 1, 3, 4))
    cpk = jnp.transpose(cpk, (2, 0, 1, 3, 4))
    per = S5_OP_GROUPS_PER_STEP
    grp = lambda shape: pl.BlockSpec((per,) + shape, lambda i: (i,) + (0,) * len(shape))
    return pl.pallas_call(
        _s5_operator_kernel,
        grid=(g // per,),
        in_specs=[grp((8, 4 * p)), grp((2, 2, 3 * SUBLANES, 4 * p)), grp((2, 2, h, 4 * p)),
                  grp((2, 2, h, 2 * p))],
        out_specs=[grp((S5_CW, 3 * S5_CW)), grp((4 * p, S5_CW)), grp((SUBLANES, 2 * p))],
        out_shape=[jax.ShapeDtypeStruct((g, S5_CW, 3 * S5_CW), BF16),
                   jax.ShapeDtypeStruct((g, 4 * p, S5_CW), BF16),
                   jax.ShapeDtypeStruct((g, SUBLANES, 2 * p), F32)],
        compiler_params=pltpu.CompilerParams(dimension_semantics=("parallel",)),
        name="s5_operators",
    )(rows, pw, bpk, cpk)


def _shift_lanes(x, s):
    lo, hi = x[:, :LANES], x[:, LANES:]
    lane = lax.broadcasted_iota(jnp.int32, lo.shape, 1)
    zero = jnp.zeros_like(lo)
    rot = lambda t, r: pltpu.roll(t, r, 1) if r % LANES else t
    if s >= 0:
        if s < LANES:
            rl, rh = rot(lo, s), rot(hi, s)
            out = (jnp.where(lane >= s, rl, 0.0), jnp.where(lane >= s, rh, rl))
        else:
            rl = rot(lo, s - LANES)
            out = (zero, jnp.where(lane >= s - LANES, rl, 0.0))
    else:
        s = -s
        if s < LANES:
            rl, rh = rot(lo, LANES - s), rot(hi, LANES - s)
            out = (jnp.where(lane < LANES - s, rl, rh), jnp.where(lane < LANES - s, rh, 0.0))
        else:
            rh = rot(hi, 2 * LANES - s)
            out = (jnp.where(lane < 2 * LANES - s, rh, 0.0), zero)
    return jnp.concatenate(out, axis=1)


def _s5_operator_kernel(*refs):
    for gi in range(S5_OP_GROUPS_PER_STEP):
        _s5_group_operators(*(r.at[gi] for r in refs))


def _s5_group_operators(rows_ref, pw_ref, b_ref, c_ref, w1_ref, w2_ref, coef_ref):
    t_len = S5_CHUNK
    half = 2 * S5_STATE
    sgn = rows_ref[5:6, :]
    mxu_operand = lambda t: t.astype(BF16).astype(F32)
    a4, bx4, pr4, pi4, g2 = [], [], [], [], []
    for d in range(2):
        f_re = rows_ref[2 * d:2 * d + 1, :]
        f_im = rows_ref[2 * d + 1:2 * d + 2, :]
        p1 = b_ref[d, 0]
        p2 = b_ref[d, 1]
        a4.append(mxu_operand(f_re * p1 + sgn * f_im * p2))
        bx4.append(mxu_operand(f_re * p2 - sgn * f_im * p1))
        pr4.append(pw_ref[d, 0])
        pi4.append(pw_ref[d, 1])
        cx = jnp.concatenate([mxu_operand(c_ref[d, 0])] * t_len, axis=0)
        cy = jnp.concatenate([mxu_operand(c_ref[d, 1])] * t_len, axis=0)

        def ca(order, d=d, cx=cx, cy=cy):
            pr = jnp.concatenate(
                [jnp.broadcast_to(pr4[d][t:t + 1, :half], (S5_GROUP, half)) for t in order], axis=0)
            pi = jnp.concatenate(
                [jnp.broadcast_to(pi4[d][t:t + 1, :half], (S5_GROUP, half)) for t in order], axis=0)
            return cx * pr + cy * pi
        g2.append(ca)

    def strip(d, order):
        return lax.dot_general(a4[d][:, :half], g2[d](order), (((1,), (1,)), ((), ())),
                               precision=lax.Precision.HIGHEST, preferred_element_type=F32)

    row = lax.broadcasted_iota(jnp.int32, (S5_GROUP, S5_CW), 0)
    col = lax.broadcasted_iota(jnp.int32, (S5_GROUP, S5_CW), 1)
    k_fwd = strip(0, range(t_len)) + jnp.where(row == col, rows_ref[4:5, :], 0.0)
    k_bwd = strip(1, [t_len - 1 - k for k in range(t_len)])
    for j in range(t_len):
        lo, hi = j * S5_GROUP, (j + 1) * S5_GROUP
        m = _shift_lanes(k_fwd, j * S5_GROUP) + _shift_lanes(k_bwd, -(t_len - 1 - j) * S5_GROUP)
        w1_ref[lo:hi, 0:S5_CW] = m.astype(w1_ref.dtype)
        tf, tb = t_len - 1 - j, j
        s_f = a4[0] * pr4[0][tf:tf + 1, :] + bx4[0] * (sgn * pi4[0][tf:tf + 1, :])
        s_b = a4[1] * pr4[1][tb:tb + 1, :] + bx4[1] * (sgn * pi4[1][tb:tb + 1, :])
        w1_ref[lo:hi, S5_CW:2 * S5_CW] = s_f.astype(w1_ref.dtype)
        w1_ref[lo:hi, 2 * S5_CW:3 * S5_CW] = s_b.astype(w1_ref.dtype)
    w2_ref[0:half, :] = g2[0]([i + 1 for i in range(t_len)]).T.astype(w2_ref.dtype)
    w2_ref[half:2 * half, :] = g2[1]([t_len - i for i in range(t_len)]).T.astype(w2_ref.dtype)
    zero = jnp.zeros((1, half), F32)
    coef_ref[...] = jnp.concatenate(
        [pr4[0][t_len:t_len + 1, :half], (sgn * pi4[0][t_len:t_len + 1, :])[:, :half],
         pr4[1][t_len:t_len + 1, :half], (sgn * pi4[1][t_len:t_len + 1, :])[:, :half],
         zero, zero, zero, zero], axis=0)


def _s5_kernel(*refs, bsz, ctx_chunks, lat_chunks):
    sw = 2 * S5_STATE
    n_chunks = ctx_chunks + lat_chunks
    shape = (bsz, sw)
    groups = []
    for gi in range(S5_SCAN_GROUPS_PER_STEP):
        (xc_ref, xl_ref, w1_ref, w2_ref, coef_ref, yc_ref, yl_ref,
         zic_ref, zil_ref, zsc_ref, zsl_ref, hc_ref, hl_ref) = (r.at[gi] for r in refs)
        streams = ((xc_ref, zic_ref, zsc_ref, hc_ref, yc_ref, ctx_chunks),
                   (xl_ref, zil_ref, zsl_ref, hl_ref, yl_ref, lat_chunks))
        groups.append((streams, w2_ref, coef_ref))
        w1 = w1_ref[...]
        for x_ref, zi_ref, zs_ref, _, _, n in streams:
            pitch = n + SUBLANES
            per_dot = min(bsz, max(1, S5_DOT_ROWS // n))
            for b0 in range(0, bsz, per_dot):
                zz = jnp.dot(x_ref[b0 * n:(b0 + per_dot) * n, :], w1, preferred_element_type=F32)
                for b in range(b0, b0 + per_dot):
                    z = zz[(b - b0) * n:(b - b0 + 1) * n]
                    zi_ref[b * n:(b + 1) * n, :] = z[:, 0:S5_CW]
                    for k in range(4):
                        zs_ref[k, b * pitch:b * pitch + n, :] = z[:, S5_CW + k * sw:S5_CW + (k + 1) * sw]

    def locate(streams, c):
        (_, _, zsc_ref, hc_ref, _, _), (_, _, zsl_ref, hl_ref, _, _) = streams
        if c < ctx_chunks:
            return zsc_ref, hc_ref, c, ctx_chunks + SUBLANES
        return zsl_ref, hl_ref, c - ctx_chunks, lat_chunks + SUBLANES

    coefs = [[jnp.broadcast_to(coef_ref[k:k + 1, :], shape) for k in range(4)] for _, _, coef_ref in groups]
    state = [[jnp.zeros(shape, F32)] * 4 for _ in groups]
    for t in range(n_chunks):
        cb = ctx_chunks - 1 - t if t < ctx_chunks else n_chunks + ctx_chunks - 1 - t
        for gi, (streams, _, _) in enumerate(groups):
            a1f, a2f, a1b, a2b = coefs[gi]
            hf, hfs, hb, hbs = state[gi]
            zs_ref, h_ref, c, pitch = locate(streams, t)
            rows = pl.ds(c, bsz, stride=pitch)
            h_ref[0, rows, :] = hf
            lf = zs_ref[0, rows, :]
            lfs = zs_ref[1, rows, :]
            hf, hfs = a1f * hf + a2f * hfs + lf, a1f * hfs - a2f * hf + lfs
            zs_ref, h_ref, c, pitch = locate(streams, cb)
            rows = pl.ds(c, bsz, stride=pitch)
            h_ref[1, rows, :] = hb
            lb = zs_ref[2, rows, :]
            lbs = zs_ref[3, rows, :]
            hb, hbs = a1b * hb + a2b * hbs + lb, a1b * hbs - a2b * hb + lbs
            state[gi] = [hf, hfs, hb, hbs]

    for streams, w2_ref, _ in groups:
        w2 = w2_ref[...]
        for _, zi_ref, _, h_ref, y_ref, n in streams:
            pitch = n + SUBLANES
            per_dot = min(bsz, max(1, S5_DOT_ROWS // n))
            for b0 in range(0, bsz, per_dot):
                hin = jnp.concatenate(
                    [jnp.concatenate([h_ref[0, b * pitch:b * pitch + n, :], h_ref[1, b * pitch:b * pitch + n, :]],
                                     axis=1) for b in range(b0, b0 + per_dot)], axis=0)
                rows = slice(b0 * n, (b0 + per_dot) * n)
                y = zi_ref[rows, :] + jnp.dot(hin.astype(BF16), w2, preferred_element_type=F32)
                y_ref[rows, :] = y.astype(y_ref.dtype)


def _s5_scan(x_ctx, x_lat, w1, w2, coef, layer, bsz):
    ctx_chunks = x_ctx.shape[1] // bsz
    lat_chunks = x_lat.shape[1] // bsz
    sw = 2 * S5_STATE
    per = S5_SCAN_GROUPS_PER_STEP
    grp = lambda rows, cols: pl.BlockSpec((per, rows, cols), lambda g: (g, 0, 0))
    opr = lambda rows, cols: pl.BlockSpec((None, per, rows, cols), lambda g: (layer, g, 0, 0))
    slab = lambda k, n: pltpu.VMEM((per, k, bsz * (n + SUBLANES), sw), F32)
    return pl.pallas_call(
        functools.partial(_s5_kernel, bsz=bsz, ctx_chunks=ctx_chunks, lat_chunks=lat_chunks),
        grid=(S5_GROUPS // per,),
        in_specs=[
            grp(bsz * ctx_chunks, S5_CW), grp(bsz * lat_chunks, S5_CW),
            opr(S5_CW, 3 * S5_CW), opr(4 * S5_STATE, S5_CW), opr(SUBLANES, sw),
        ],
        out_specs=[grp(bsz * ctx_chunks, S5_CW), grp(bsz * lat_chunks, S5_CW)],
        out_shape=[jax.ShapeDtypeStruct(x_ctx.shape, F32), jax.ShapeDtypeStruct(x_lat.shape, F32)],
        scratch_shapes=[
            pltpu.VMEM((per, bsz * ctx_chunks, S5_CW), F32), pltpu.VMEM((per, bsz * lat_chunks, S5_CW), F32),
            slab(4, ctx_chunks), slab(4, lat_chunks), slab(2, ctx_chunks), slab(2, lat_chunks),
        ],
        compiler_params=pltpu.CompilerParams(
            dimension_semantics=("parallel",), vmem_limit_bytes=VMEM_LIMIT),
        name="s5_scan",
    )(x_ctx, x_lat, w1, w2, coef)


def _gelu_tanh(x):
    return 0.5 * x * (1.0 + jnp.tanh(math.sqrt(2.0 / math.pi) * (x + 0.044715 * (x * x * x))))


def _silu(x):
    return x * jax.nn.sigmoid(x)


def _merge_kernel(h_ref, mod_ref, npre_ref, win_ref, wa_ref, oa_ref, yb_ref, oc_ref,
                  wglu_ref, bglu_ref, woa_ref, wob_ref, woc_ref, wout_ref, npost_ref, o_ref, stage_ref):
    gate_cols = (None, SEGS["bg"], SEGS["cg"])
    hn = _modulated_norm(h_ref[...], mod_ref, npre_ref)

    def gates(i):
        if i == 0:
            g = jnp.dot(hn, wa_ref[:, A_WIDTH:2 * A_WIDTH], preferred_element_type=F32)
        else:
            off, width = gate_cols[i]
            g = jnp.dot(hn, win_ref[:, off:off + width], preferred_element_type=F32)
        m_lo = SEGS["mg"][0] + i * D_MODEL
        return g, jnp.dot(hn, win_ref[:, m_lo:m_lo + D_MODEL], preferred_element_type=F32)

    def branch(gm, o, w_ref):
        g, m = gm
        t = jnp.dot((o * _silu(g)).astype(BF16), w_ref[...], preferred_element_type=F32)
        return jax.nn.sigmoid(m) * t

    gm = [gates(i) for i in (0, 2, 1)]
    y = branch(gm[0], oa_ref[...], woa_ref) + branch(gm[1], oc_ref[...], woc_ref)
    yb = _gelu_tanh(_groups_to_tokens(yb_ref, stage_ref))
    glu = jnp.dot(yb.astype(BF16), wglu_ref[...], preferred_element_type=F32) + bglu_ref[...]
    y = y + branch(gm[2], yb * jax.nn.sigmoid(glu), wob_ref)
    y = jnp.dot(y.astype(BF16), wout_ref[...], preferred_element_type=F32)
    ms = jnp.mean(y * y, axis=-1, keepdims=True)
    yn = y * lax.rsqrt(ms + EPS) * npost_ref[...]
    o_ref[...] = h_ref[...] + mod_ref[:, 2 * D_MODEL:3 * D_MODEL] * yn


def _merge(h, mod, mod_row, norm_pre, layer, w_bf16, w_a, o_a, y_b, o_c, wts):
    bsz, rows, _ = h.shape
    tile = min(ROW_TILE, rows)
    tiles = rows // tile
    if mod_row is None:
        mod_map = lambda b, i: (b, 0, 0)
    else:
        mod_map = lambda b, i: (mod_row, 0, 0)
    tok = lambda w: pl.BlockSpec((None, tile,w), lambda b, i: (b, i, 0))
    in_specs = [
        tok(D_MODEL),
        pl.BlockSpec((None, 1, 3 * D_MODEL), mod_map),
        _const_spec((1, D_MODEL)), _const_spec((D_MODEL, IN_WIDTH), layer),
        _const_spec((D_MODEL, 2 * A_WIDTH), layer),
        tok(A_WIDTH),
        pl.BlockSpec((S5_GROUPS, tile // S5_CHUNK, S5_CW), lambda b, i: (0, b * tiles + i, 0)),
        tok(C_WIDTH),
        _const_spec((B_WIDTH, B_WIDTH), layer), _const_spec((1, B_WIDTH)),
        _const_spec((A_WIDTH, D_MODEL), layer), _const_spec((B_WIDTH, D_MODEL), layer),
        _const_spec((C_WIDTH, D_MODEL), layer),
        _const_spec((D_MODEL, D_MODEL), layer), _const_spec((1, D_MODEL)),
    ]
    return pl.pallas_call(
        _merge_kernel,
        grid=(bsz, rows // tile),
        in_specs=in_specs,
        out_specs=tok(D_MODEL),
        out_shape=jax.ShapeDtypeStruct((bsz, rows, D_MODEL), F32),
        scratch_shapes=[pltpu.VMEM((B_WIDTH // LANES, (tile // S5_CHUNK) * STAGE_PITCH, LANES), F32)],
        compiler_params=pltpu.CompilerParams(
            dimension_semantics=("parallel", "parallel"), vmem_limit_bytes=VMEM_LIMIT),
        name="merge_out",
    )(h, mod, norm_pre.reshape(1, D_MODEL), w_bf16, w_a, o_a, y_b, o_c, *wts)


def _rope_tables(n_tokens):
    rows = n_tokens // GRID_W
    pos_r = jnp.repeat(jnp.arange(rows, dtype=F32), GRID_W)
    pos_c = jnp.tile(jnp.arange(GRID_W, dtype=F32), rows)
    inv = ROPE_BASE ** (-jnp.arange(AX_FREQS, dtype=F32) / AX_FREQS)
    ang_r = pos_r[:, None] * inv[None]
    ang_c = pos_c[:, None] * inv[None]
    ang = jnp.concatenate([ang_r, ang_r, ang_c, ang_c], axis=-1)
    cos, sin = jnp.cos(ang), jnp.sin(ang)
    first = (jnp.arange(HEAD_DIM) % (2 * AX_FREQS)) < AX_FREQS
    sa = jnp.where(first[None], -sin, 0.0)
    sb = jnp.where(first[None], 0.0, sin)
    tile2 = lambda t: jnp.concatenate([t, t], axis=-1)
    return tile2(cos), tile2(sa), tile2(sb)


def _reorder_a_heads(w, axis):
    axis = axis % w.ndim
    shape = w.shape
    split = shape[:axis] + (A_HEADS, HEAD_DIM) + shape[axis + 1:]
    return jnp.take(w.reshape(split), jnp.array(A_HEAD_ORDER), axis=axis).reshape(shape)


def _layers_as_groups(t):
    t = jnp.swapaxes(t, 0, 1)
    return t.reshape((2, DEPTH * S5_GROUPS) + t.shape[3:])


def kernel(x, c, ctx, c_ctx, w_mod, b_mod, norm_pre, norm_post, w_in, swa_sink, s5_a_re, s5_a_im, s5_log_dt, s5_b_re, s5_b_im, s5_c_re, s5_c_im, s5_d, s5_w_glu, s5_b_glu, diff_lq1, diff_lk1, diff_lq2, diff_lk2, diff_subln, w_o_a, w_o_b, w_o_c, w_out):
    bsz, n_lat, _ = x.shape
    n_ctx = ctx.shape[1]
    ctx_row = bsz
    assert bsz == SUBLANES, "the S5 scan kernel steps one sublane row per batch element"
    cc = jnp.zeros((MOD_ROWS, D_MODEL), F32).at[:bsz].set(c).at[ctx_row].set(c_ctx)
    mod_all = _modulation(cc, w_mod, b_mod)
    rope_tabs = _rope_tables(n_lat)

    cols = lambda name: w_in[..., SEGS[name][0]:SEGS[name][0] + SEGS[name][1]]
    w_in_all = w_in.astype(BF16)
    w_a_all = jnp.concatenate(
        [_reorder_a_heads(cols("aq"), -1), _reorder_a_heads(cols("ag"), -1)], axis=-1).astype(BF16)
    w_glu_all, w_ob_all, w_oc_all, w_out_all = (t.astype(BF16) for t in (s5_w_glu, w_o_b, w_o_c, w_out))
    w_oa_all = _reorder_a_heads(w_o_a, -2).astype(BF16)
    lam_all = jnp.stack([diff_lq1, diff_lk1, diff_lq2, diff_lk2], axis=1).astype(F32)
    s5_ops = _s5_operators(*(_layers_as_groups(t) for t in (s5_a_re, s5_a_im, s5_log_dt, s5_b_re, s5_b_im,
                                                            s5_c_re, s5_c_im)), s5_d.reshape(-1))
    w1_all, w2_all, coef_all = (t.reshape((DEPTH, S5_GROUPS) + t.shape[1:]) for t in s5_ops)

    h_lat, h_ctx = x, ctx
    for l in range(DEPTH):
        last = l == DEPTH - 1
        lam_init = 0.8 - 0.6 * math.exp(-0.3 * l)
        mod = mod_all[l].reshape(MOD_ROWS, 1, 3 * D_MODEL)
        zl = _in_projection(h_lat, mod, None, norm_pre[l], w_in_all, w_a_all, l, PROJ_NAMES, rope_tabs)
        ctx_names = ("ak", "av", "bu", "ck", "cv") if last else PROJ_NAMES
        zc = _in_projection(h_ctx, mod, ctx_row, norm_pre[l], w_in_all, w_a_all, l, ctx_names, None)

        o_a_l = _swa(swa_sink[l], zl["aq"], zc["ak"], zc["av"], zl["ak"], zl["av"])

        y_b_c, y_b_l = _s5_scan(zc["bu"], zl["bu"], w1_all, w2_all, coef_all, l, bsz)

        lam_params = lam_all[l]
        o_c_l = _diff_attention(lam_params, diff_subln[l], lam_init, zl["cq"], zc["ck"], zc["cv"],
                                zl["ck"], zl["cv"])

        wts = (w_glu_all, s5_b_glu[l].reshape(1, B_WIDTH), w_oa_all, w_ob_all, w_oc_all,
               w_out_all, norm_post[l].reshape(1, D_MODEL))
        h_lat_new = _merge(h_lat, mod, None, norm_pre[l], l, w_in_all, w_a_all, o_a_l, y_b_l, o_c_l, wts)
        if not last:
            o_a_c = _swa(swa_sink[l], zc["aq"], zc["ak"], zc["av"])
            o_c_c = _diff_attention(lam_params, diff_subln[l], lam_init, zc["cq"], zc["ck"], zc["cv"])
            h_ctx = _merge(h_ctx, mod, ctx_row, norm_pre[l], l, w_in_all, w_a_all, o_a_c, y_b_c, o_c_c, wts)
        h_lat = h_lat_new
    return h_lat
```

```python
import functools
import math

import jax
import jax.numpy as jnp
from jax import lax
from jax.experimental import pallas as pl
from jax.experimental.pallas import tpu as pltpu

F32 = jnp.float32
BF16 = jnp.bfloat16

D_MODEL = 1024
DEPTH = 2
GRID_W = 64
HEAD_DIM = 64
WINDOW = 128
ROPE_BASE = 10000.0
AX_FREQS = HEAD_DIM // 4
NEG_INF = -1e30
A_HEADS = 8
A_KV_HEADS = 2
A_GROUP = A_HEADS // A_KV_HEADS
A_WIDTH = A_HEADS * HEAD_DIM
A_KV_WIDTH = A_KV_HEADS * HEAD_DIM
B_WIDTH = 512
S5_GROUP = 16
S5_GROUPS = B_WIDTH // S5_GROUP
S5_STATE = 64
C_HEADS = 4
C_WIDTH = C_HEADS * 2 * HEAD_DIM
EPS = 1e-6

LANES = 128
SUBLANES = 8

MOD_ROWS = 2 * SUBLANES
PROJ_ROW_TILE = 1024
ROW_TILE = 512
Q_BLOCK = 128
SWA_Q_ROWS = 1024
DIFF_Q_ROWS = 512
DIFF_KEY_CHUNK = 512
DIFF_LOOKAHEAD = 8
SWA_LOOKAHEAD = 1
A_HEAD_ORDER = tuple(h for v in range(A_GROUP) for h in (v, A_GROUP + v))
S5_CHUNK = 16
S5_CW = S5_CHUNK * S5_GROUP
S5_DOT_ROWS = 256
S5_SCAN_GROUPS_PER_STEP = 4
S5_OP_GROUPS_PER_STEP = 4
STAGE_PITCH = S5_CHUNK + SUBLANES
VMEM_LIMIT = 56 * 1024 * 1024

_SEG_NAMES = ("aq", "ak", "av", "ag", "bu", "bg", "cq", "ck", "cv", "cg", "mg")
_SEG_WIDTHS = (A_WIDTH, A_KV_WIDTH, A_KV_WIDTH, A_WIDTH, B_WIDTH, B_WIDTH,
               C_WIDTH, C_WIDTH, C_WIDTH, C_WIDTH, 3 * D_MODEL)
SEGS = {}
_off = 0
for _n, _w in zip(_SEG_NAMES, _SEG_WIDTHS):
    SEGS[_n] = (_off, _w)
    _off += _w
IN_WIDTH = _off
PROJ_NAMES = ("aq", "ak", "av", "bu", "cq", "ck", "cv")
Q_SCALE = HEAD_DIM ** -0.5 * math.log2(math.e)
ROPE_SEGS = {"aq": Q_SCALE, "ak": 1.0, "cq": Q_SCALE, "ck": 1.0}


def _const_spec(shape, layer=None):
    nd = len(shape)
    if layer is None:
        return pl.BlockSpec(shape, lambda *_: (0,) * nd, pipeline_mode=pl.Buffered(1))
    return pl.BlockSpec((None,) + shape, lambda *_: (layer,) + (0,) * nd, pipeline_mode=pl.Buffered(1))


def _mod_kernel(c_ref, w_ref, b_ref, o_ref):
    c = c_ref[...]
    a = c * jax.nn.sigmoid(c)
    o_ref[...] = jnp.dot(a, w_ref[...], preferred_element_type=F32) + b_ref[...]


def _modulation(cc, w_mod, b_mod):
    n_tiles = 3
    return pl.pallas_call(
        _mod_kernel,
        grid=(DEPTH, n_tiles),
        in_specs=[
            pl.BlockSpec((MOD_ROWS, D_MODEL), lambda l, j: (0, 0)),
            pl.BlockSpec((None, D_MODEL, D_MODEL), lambda l, j: (l, 0, j)),
            pl.BlockSpec((None, 1, D_MODEL), lambda l, j: (l, 0, j)),
        ],
        out_specs=pl.BlockSpec((None, MOD_ROWS, D_MODEL), lambda l, j: (l, 0, j)),
        out_shape=jax.ShapeDtypeStruct((DEPTH, MOD_ROWS, 3 * D_MODEL), F32),
        name="modulation",
    )(cc, w_mod, b_mod.reshape(DEPTH, 1, 3 * D_MODEL))


def _block_transpose8(xs):
    lane_blk = lax.broadcasted_iota(jnp.int32, xs[0].shape, 1) // S5_GROUP
    for s in (4, 2, 1):
        keep = (lane_blk & s) == 0
        out = list(xs)
        for p in range(8):
            if p & s:
                continue
            a, b = xs[p], xs[p + s]
            out[p] = jnp.where(keep, a, pltpu.roll(b, S5_GROUP * s, 1))
            out[p + s] = jnp.where(keep, pltpu.roll(a, LANES - S5_GROUP * s, 1), b)
        xs = out
    return xs


def _tokens_to_groups(z, stage_ref, o_ref):
    n_chunks = z.shape[0] // S5_CHUNK
    for v in range(B_WIDTH // LANES):
        for c in range(n_chunks):
            stage_ref[v, c * STAGE_PITCH:c * STAGE_PITCH + S5_CHUNK, :] = \
                z[c * S5_CHUNK:(c + 1) * S5_CHUNK, v * LANES:(v + 1) * LANES]
    for w in range(S5_CW // LANES):
        for v in range(B_WIDTH // LANES):
            src = [stage_ref[v, pl.ds(8 * w + jj, n_chunks, stride=STAGE_PITCH), :] for jj in range(8)]
            dst = _block_transpose8(src)
            for gg in range(8):
                o_ref[8 * v + gg, :, w * LANES:(w + 1) * LANES] = dst[gg].astype(o_ref.dtype)


def _groups_to_tokens(y_ref, stage_ref):
    n_chunks = y_ref.shape[1]
    for w in range(S5_CW // LANES):
        for v in range(B_WIDTH // LANES):
            src = [y_ref[8 * v + gg, :, w * LANES:(w + 1) * LANES].astype(F32) for gg in range(8)]
            dst = _block_transpose8(src)
            for jj in range(8):
                stage_ref[v, pl.ds(8 * w + jj, n_chunks, stride=STAGE_PITCH), :] = dst[jj]
    rows = []
    for c in range(n_chunks):
        rows.append(jnp.concatenate(
            [stage_ref[v, c * STAGE_PITCH:c * STAGE_PITCH + S5_CHUNK, :] for v in range(B_WIDTH // LANES)], axis=1))
    return jnp.concatenate(rows, axis=0)


def _modulated_norm(x, mod_ref, nw_ref):
    ms = jnp.mean(x * x, axis=-1, keepdims=True)
    xn = x * lax.rsqrt(ms + EPS) * nw_ref[...]
    shift = mod_ref[:, 0:D_MODEL]
    scale = mod_ref[:, D_MODEL:2 * D_MODEL]
    return (xn * (1.0 + scale) + shift).astype(BF16)


def _inproj_kernel(*refs, names, use_rope):
    h_ref, mod_ref, nw_ref, w_ref, wa_ref = refs[:5]
    stage_ref = refs[-1]
    refs = refs[:-1]
    if use_rope:
        cos_ref, sa_ref, sb_ref = refs[5:8]
        outs = refs[8:]
    else:
        outs = refs[5:]
    hn = _modulated_norm(h_ref[...], mod_ref, nw_ref)
    kv_off = SEGS["ak"][0]
    assert SEGS["av"][0] == kv_off + A_KV_WIDTH
    kv = jnp.dot(hn, w_ref[:, kv_off:kv_off + 2 * A_KV_WIDTH], preferred_element_type=F32)

    def project(name):
        off, width = SEGS[name]
        if name in ("ak", "av"):
            return kv[:, off - kv_off:off - kv_off + width]
        if name == "aq":
            return jnp.dot(hn, wa_ref[:, 0:width], preferred_element_type=F32)
        return jnp.dot(hn, w_ref[:, off:off + width], preferred_element_type=F32)

    for name, o_ref in zip(names, outs):
        z = project(name)
        mul = ROPE_SEGS.get(name, 1.0)
        if use_rope and name in ROPE_SEGS:
            cos = cos_ref[...]
            sa = sa_ref[...]
            sb = sb_ref[...]
            for c in range(z.shape[1] // LANES):
                t = z[:, c * LANES:(c + 1) * LANES]
                r = t * cos + pltpu.roll(t, LANES - AX_FREQS, 1) * sa + pltpu.roll(t, AX_FREQS, 1) * sb
                if mul != 1.0:
                    r = r * mul
                o_ref[:, c * LANES:(c + 1) * LANES] = r.astype(o_ref.dtype)
        elif name == "bu":
            _tokens_to_groups(z, stage_ref, o_ref)
        elif name == "av":
            for t in range(z.shape[0] // Q_BLOCK):
                o_ref[t] = z[t * Q_BLOCK:(t + 1) * Q_BLOCK, :].T.astype(o_ref.dtype)
        elif name == "cv":
            o_ref[...] = z.T.astype(o_ref.dtype)
        else:
            if mul != 1.0:
                z = z * mul
            o_ref[...] = z.astype(o_ref.dtype)


def _in_projection(h, mod, mod_row, norm_w, w_bf16, w_a, layer, names, rope_tabs):
    bsz, rows, _ = h.shape
    tile = min(PROJ_ROW_TILE, rows)
    use_rope = rope_tabs is not None
    if mod_row is None:
        mod_map = lambda b, i: (b, 0, 0)
    else:
        mod_map = lambda b, i: (mod_row, 0, 0)
    in_specs = [
        pl.BlockSpec((None, tile,D_MODEL), lambda b, i: (b, i, 0)),
        pl.BlockSpec((None, 1, 3 * D_MODEL), mod_map),
        _const_spec((1, D_MODEL)),
        _const_spec((D_MODEL, IN_WIDTH), layer),
        _const_spec((D_MODEL, 2 * A_WIDTH), layer),
    ]
    args = [h, mod, norm_w.reshape(1, D_MODEL), w_bf16, w_a]
    if use_rope:
        for t in rope_tabs:
            in_specs.append(pl.BlockSpec((tile, LANES), lambda b, i: (i, 0)))
            args.append(t)
    tiles = rows // tile
    chunk_rows = tile // S5_CHUNK
    out_specs, out_shape = [], []
    for n in names:
        if n == "bu":
            out_specs.append(pl.BlockSpec((S5_GROUPS, chunk_rows, S5_CW), lambda b, i: (0, b * tiles + i, 0)))
            out_shape.append(jax.ShapeDtypeStruct((S5_GROUPS, bsz * rows // S5_CHUNK, S5_CW), BF16))
        elif n == "av":
            kb = tile // Q_BLOCK
            out_specs.append(pl.BlockSpec((None, kb, A_KV_WIDTH, Q_BLOCK), lambda b, i: (b, i, 0, 0)))
            out_shape.append(jax.ShapeDtypeStruct((bsz, rows // Q_BLOCK, A_KV_WIDTH, Q_BLOCK), BF16))
        elif n == "cv":
            out_specs.append(pl.BlockSpec((None, C_WIDTH, tile), lambda b, i: (b, 0, i)))
            out_shape.append(jax.ShapeDtypeStruct((bsz, C_WIDTH, rows), BF16))
        else:
            out_specs.append(pl.BlockSpec((None, tile,SEGS[n][1]), lambda b, i: (b, i, 0)))
            out_shape.append(jax.ShapeDtypeStruct((bsz, rows, SEGS[n][1]), BF16))
    outs = pl.pallas_call(
        functools.partial(_inproj_kernel, names=tuple(names), use_rope=use_rope),
        grid=(bsz, tiles),
        in_specs=in_specs,
        out_specs=out_specs,
        out_shape=out_shape,
        scratch_shapes=[pltpu.VMEM((B_WIDTH // LANES, (tile // S5_CHUNK) * STAGE_PITCH, LANES), F32)],
        compiler_params=pltpu.CompilerParams(
            dimension_semantics=("parallel", "parallel"), vmem_limit_bytes=VMEM_LIMIT),
        name="in_projection_rope" if use_rope else "in_projection",
    )(*args)
    return dict(zip(names, outs))


def _dot_nt(a, b):
    return lax.dot_general(a, b, (((1,), (1,)), ((), ())), preferred_element_type=F32)


def _half_masked(q, lower):
    lane = lax.broadcasted_iota(jnp.int32, q.shape, 1)
    keep = lane < HEAD_DIM if lower else lane >= HEAD_DIM
    return jnp.where(keep, q.astype(F32), 0.0).astype(q.dtype)


def _swa_kernel(*refs, has_local, lat_blocks):
    if has_local:
        sink_ref, q_ref, kc_ref, vtc_ref, kl_ref, vtl_ref, o_ref = refs
    else:
        sink_ref, q_ref, kc_ref, vtc_ref, o_ref = refs
    n_ctx = kc_ref.shape[0]
    n_items = q_ref.shape[0] // Q_BLOCK
    sink = jnp.concatenate(
        [jnp.full((1, Q_BLOCK), sink_ref[kh * A_GROUP + v] * math.log2(math.e), F32)
         for kh in range(A_KV_HEADS) for v in range(A_GROUP)], axis=1)

    def scores(t):
        qs = slice(t * Q_BLOCK, (t + 1) * Q_BLOCK)
        qblk = jnp.concatenate(
            [_half_masked(q_ref[qs, v * LANES:(v + 1) * LANES], kh == 0)
             for kh in range(A_KV_HEADS) for v in range(A_GROUP)], axis=0)
        keys = [kc_ref[...]]
        vts = [vtc_ref[c] for c in range(n_ctx // Q_BLOCK)]
        if has_local:
            n = pl.program_id(1) * n_items + t
            sb = jnp.clip(n - 1, 0, lat_blocks - 3)
            keys.append(kl_ref[pl.ds(pl.multiple_of(sb * Q_BLOCK, Q_BLOCK), 3 * Q_BLOCK), :])
            vts += [vtl_ref[sb + c] for c in range(3)]
        s_t = _dot_nt(jnp.concatenate(keys, axis=0), qblk)
        if has_local:
            r = lax.broadcasted_iota(jnp.int32, (3 * Q_BLOCK, Q_BLOCK), 0)
            i = lax.broadcasted_iota(jnp.int32, (3 * Q_BLOCK, Q_BLOCK), 1)
            in_band = jnp.abs((n - sb) * Q_BLOCK + i - r) <= WINDOW
            bias = jnp.where(in_band, 0.0, NEG_INF)
            s_t = jnp.concatenate([s_t[:n_ctx], s_t[n_ctx:] + jnp.concatenate([bias] * A_HEADS, axis=1)],
                                  axis=0)
        return s_t, jnp.concatenate(vts, axis=1)

    lane = lax.broadcasted_iota(jnp.int32, (Q_BLOCK, LANES), 1)
    half = A_GROUP * Q_BLOCK
    queue = [scores(t) for t in range(min(SWA_LOOKAHEAD, n_items))]
    for t in range(n_items):
        s_t, vt = queue.pop(0)
        if t + SWA_LOOKAHEAD < n_items:
            queue.append(scores(t + SWA_LOOKAHEAD))
        m = jnp.maximum(jnp.max(s_t, axis=0, keepdims=True), sink)
        e_t = jnp.exp2(s_t - m)
        den = jnp.sum(e_t, axis=0, keepdims=True) + jnp.exp2(sink - m)
        o_t = jnp.dot(vt, e_t.astype(BF16), preferred_element_type=F32) * (1.0 / den)
        qs = slice(t * Q_BLOCK, (t + 1) * Q_BLOCK)
        for v in range(A_GROUP):
            lower = o_t[:, v * Q_BLOCK:(v + 1) * Q_BLOCK].T
            upper = o_t[:, half + v * Q_BLOCK:half + (v + 1) * Q_BLOCK].T
            o_ref[qs, v * LANES:(v + 1) * LANES] = jnp.where(lane < HEAD_DIM, lower, upper).astype(o_ref.dtype)


def _swa(sink, q, k_ctx, vt_ctx, k_lat=None, vt_lat=None):
    bsz, rows, _ = q.shape
    qrows = min(SWA_Q_ROWS, rows)
    n_ctx = k_ctx.shape[1]
    has_local = k_lat is not None
    in_specs = [
        pl.BlockSpec(memory_space=pltpu.SMEM),
        pl.BlockSpec((None, qrows, A_WIDTH), lambda b, n: (b, n, 0)),
        pl.BlockSpec((None, n_ctx, A_KV_WIDTH), lambda b, n: (b, 0, 0)),
        pl.BlockSpec((None, n_ctx // Q_BLOCK, A_KV_WIDTH, Q_BLOCK), lambda b, n: (b, 0, 0, 0)),
    ]
    args = [sink, q, k_ctx, vt_ctx]
    lat_blocks = 0
    if has_local:
        n_lat = k_lat.shape[1]
        lat_blocks = n_lat // Q_BLOCK
        in_specs += [pl.BlockSpec((None, n_lat, A_KV_WIDTH), lambda b, n: (b, 0, 0)),
                     pl.BlockSpec((None, lat_blocks, A_KV_WIDTH, Q_BLOCK), lambda b, n: (b, 0, 0, 0))]
        args += [k_lat, vt_lat]
    return pl.pallas_call(
        functools.partial(_swa_kernel, has_local=has_local, lat_blocks=lat_blocks),
        grid=(bsz, rows // qrows),
        in_specs=in_specs,
        out_specs=pl.BlockSpec((None, qrows, A_WIDTH), lambda b, n: (b, n, 0)),
        out_shape=jax.ShapeDtypeStruct((bsz, rows, A_WIDTH), F32),
        compiler_params=pltpu.CompilerParams(
            dimension_semantics=("parallel", "arbitrary"), vmem_limit_bytes=VMEM_LIMIT),
        name="swa_latent" if has_local else "swa_context",
    )(*args)


def _diff_kernel(*refs, has_lat, lam_init):
    if has_lat:
        lam_ref, sub_ref, q_ref, kc_ref, vtc_ref, kl_ref, vtl_ref, o_ref = refs
    else:
        lam_ref, sub_ref, q_ref, kc_ref, vtc_ref, o_ref = refs
    lp = lam_ref[...]
    lam = (jnp.exp(jnp.sum(lp[0:1] * lp[1:2], axis=1, keepdims=True))
           - jnp.exp(jnp.sum(lp[2:3] * lp[3:4], axis=1, keepdims=True)) + lam_init)
    sub_w = sub_ref[...] * (1.0 - lam_init)
    width = 2 * HEAD_DIM
    chunks = [(kc_ref, vtc_ref, 0, kc_ref.shape[0])]
    if has_lat:
        chunks += [(kl_ref, vtl_ref, c, DIFF_KEY_CHUNK) for c in range(0, kl_ref.shape[0], DIFF_KEY_CHUNK)]
    heads = [slice(h * width, (h + 1) * width) for h in range(C_HEADS)]
    items = [(hs, slice(qb, qb + Q_BLOCK)) for qb in range(0, q_ref.shape[0], Q_BLOCK) for hs in heads]
    units = [(it, ci) for it in range(len(items)) for ci in range(len(chunks))]
    qblks = {}

    def scores(unit):
        it, ci = unit
        hs, qs = items[it]
        if it not in qblks:
            qblks[it] = jnp.concatenate(
                [_half_masked(q_ref[qs, hs], True), _half_masked(q_ref[qs, hs], False)], axis=0)
        k_ref, _, start, size = chunks[ci]
        return _dot_nt(k_ref[start:start + size, hs], qblks[it])

    queue = [scores(unit) for unit in units[:DIFF_LOOKAHEAD]]
    m = den = acc = None
    for u, (it, ci) in enumerate(units):
        hs, qs = items[it]
        _, vt_ref, start, size = chunks[ci]
        s_t = queue.pop(0)
        if u + DIFF_LOOKAHEAD < len(units):
            queue.append(scores(units[u + DIFF_LOOKAHEAD]))
        m_c = jnp.max(s_t, axis=0, keepdims=True)
        m_new = m_c if ci == 0 else jnp.maximum(m, m_c)
        p_t = jnp.exp2(s_t - m_new)
        den_c = jnp.sum(p_t, axis=0, keepdims=True)
        pv = jnp.dot(vt_ref[hs, start:start + size], p_t.astype(BF16), preferred_element_type=F32)
        if ci == 0:
            den, acc = den_c, pv
        else:
            alpha = jnp.exp2(m - m_new)
            den, acc = den * alpha + den_c, acc * alpha + pv
        m = m_new
        if ci == len(chunks) - 1:
            inv = 1.0 / den
            o_t = acc[:, :Q_BLOCK] * inv[:, :Q_BLOCK] - acc[:, Q_BLOCK:] * (lam * inv[:, Q_BLOCK:])
            o = o_t.T
            ms = jnp.mean(o * o, axis=1, keepdims=True)
            o_ref[qs, hs] = (o * lax.rsqrt(ms + EPS) * sub_w).astype(o_ref.dtype)


def _diff_attention(lam_params, subln, lam_init, q, k_ctx, vt_ctx, k_lat=None, vt_lat=None):
    bsz, rows, _ = q.shape
    qrows = min(DIFF_Q_ROWS, rows)
    n_ctx = k_ctx.shape[1]
    has_lat = k_lat is not None
    in_specs = [
        _const_spec((4, HEAD_DIM)),
        _const_spec((1, 2 * HEAD_DIM)),
        pl.BlockSpec((None, qrows, C_WIDTH), lambda b, n: (b, n, 0)),
        pl.BlockSpec((None, n_ctx, C_WIDTH), lambda b, n: (b, 0, 0)),
        pl.BlockSpec((None, C_WIDTH, n_ctx), lambda b, n: (b, 0, 0)),
    ]
    args = [lam_params, subln.reshape(1, 2 * HEAD_DIM), q, k_ctx, vt_ctx]
    if has_lat:
        n_lat = k_lat.shape[1]
        in_specs += [pl.BlockSpec((None, n_lat, C_WIDTH), lambda b, n: (b, 0, 0)),
                     pl.BlockSpec((None, C_WIDTH, n_lat), lambda b, n: (b, 0, 0))]
        args += [k_lat, vt_lat]
    return pl.pallas_call(
        functools.partial(_diff_kernel, has_lat=has_lat, lam_init=lam_init),
        grid=(bsz, rows // qrows),
        in_specs=in_specs,
        out_specs=pl.BlockSpec((None, qrows, C_WIDTH), lambda b, n: (b, n, 0)),
        out_shape=jax.ShapeDtypeStruct((bsz, rows, C_WIDTH), F32),
        compiler_params=pltpu.CompilerParams(
            dimension_semantics=("parallel", "arbitrary"), vmem_limit_bytes=VMEM_LIMIT),
        name="diff_latent" if has_lat else "diff_context",
    )(*args)


def _s5_operators(a_re, a_im, log_dt, b_re, b_im, c_re, c_im, d_skip):
    f = lambda t: t.astype(F32)
    quad = lambda t: jnp.concatenate([t, t, t, t], axis=-1)
    a_re, a_im, b_re, b_im, c_re, c_im = map(f, (a_re, a_im, b_re, b_im, c_re, c_im))
    g, p, h = a_re.shape[1], S5_STATE, S5_GROUP
    dt = jnp.exp(f(log_dt))[..., None]
    mag = jnp.exp(a_re * dt)
    abar_re = mag * jnp.cos(a_im * dt)
    abar_im = mag * jnp.sin(a_im * dt)
    den = a_re * a_re + a_im * a_im
    nr = abar_re - 1.0
    ni = abar_im
    f_re = (nr * a_re + ni * a_im) / den
    f_im = (ni * a_re - nr * a_im) / den
    pw_re, pw_im = [jnp.ones_like(abar_re)], [jnp.zeros_like(abar_im)]
    for _ in range(S5_CHUNK):
        pre, pim = pw_re[-1], pw_im[-1]
        pw_re.append(pre * abar_re - pim * abar_im)
        pw_im.append(pre * abar_im + pim * abar_re)
    pad = [jnp.zeros_like(abar_re)] * (3 * SUBLANES - S5_CHUNK - 1)
    pw = jnp.stack([jnp.stack(pw_re + pad, axis=2), jnp.stack(pw_im + pad, axis=2)], axis=1)
    pw = jnp.transpose(quad(pw), (2, 0, 1, 3, 4))
    dsk = jnp.tile(f(d_skip).reshape(g, h), (1, S5_CW // h))
    sgn = jnp.broadcast_to(jnp.repeat(jnp.array([-1.0, 1.0, 1.0, -1.0], F32), p), (g, 4 * p))
    zero = jnp.zeros_like(dsk)
    rows = jnp.stack([quad(f_re[0]), quad(f_im[0]), quad(f_re[1]), quad(f_im[1]), dsk, sgn, zero, zero],
                     axis=1)
    bt_re = jnp.swapaxes(b_re, -1, -2)
    bt_im = jnp.swapaxes(b_im, -1, -2)
    bpk = jnp.stack([jnp.concatenate([bt_re, bt_im, bt_im, bt_re], axis=-1),
                     jnp.concatenate([bt_im, bt_re, bt_re, bt_im], axis=-1)], axis=1)
    cpk = jnp.stack([jnp.concatenate([c_re, -c_im], axis=-1),
                     jnp.concatenate([-c_im, -c_re], axis=-1)], axis=1)
    bpk = jnp.transpose(bpk, (2, 0, 1, 3, 4))
    cpk = jnp.transpose(cpk, (2, 0, 1, 3, 4))
    per = S5_OP_GROUPS_PER_STEP
    grp = lambda shape: pl.BlockSpec((per,) + shape, lambda i: (i,) + (0,) * len(shape))
    return pl.pallas_call(
        _s5_operator_kernel,
        grid=(g // per,),
        in_specs=[grp((8, 4 * p)), grp((2, 2, 3 * SUBLANES, 4 * p)), grp((2, 2, h, 4 * p)),
                  grp((2, 2, h, 2 * p))],
        out_specs=[grp((S5_CW, 3 * S5_CW)), grp((4 * p, S5_CW)), grp((SUBLANES, 2 * p))],
        out_shape=[jax.ShapeDtypeStruct((g, S5_CW, 3 * S5_CW), BF16),
                   jax.ShapeDtypeStruct((g, 4 * p, S5_CW), BF16),
                   jax.ShapeDtypeStruct((g, SUBLANES, 2 * p), F32)],
        compiler_params=pltpu.CompilerParams(dimension_semantics=("parallel",)),
        name="s5_operators",
    )(rows, pw, bpk, cpk)


def _shift_lanes(x, s):
    lo, hi = x[:, :LANES], x[:, LANES:]
    lane = lax.broadcasted_iota(jnp.int32, lo.shape, 1)
    zero = jnp.zeros_like(lo)
    rot = lambda t, r: pltpu.roll(t, r, 1) if r % LANES else t
    if s >= 0:
        if s < LANES:
            rl, rh = rot(lo, s), rot(hi, s)
            out = (jnp.where(lane >= s, rl, 0.0), jnp.where(lane >= s, rh, rl))
        else:
            rl = rot(lo, s - LANES)
            out = (zero, jnp.where(lane >= s - LANES, rl, 0.0))
    else:
        s = -s
        if s < LANES:
            rl, rh = rot(lo, LANES - s), rot(hi, LANES - s)
            out = (jnp.where(lane < LANES - s, rl, rh), jnp.where(lane < LANES - s, rh, 0.0))
        else:
            rh = rot(hi, 2 * LANES - s)
            out = (jnp.where(lane < 2 * LANES - s, rh, 0.0), zero)
    return jnp.concatenate(out, axis=1)


def _s5_operator_kernel(*refs):
    for gi in range(S5_OP_GROUPS_PER_STEP):
        _s5_group_operators(*(r.at[gi] for r in refs))


def _s5_group_operators(rows_ref, pw_ref, b_ref, c_ref, w1_ref, w2_ref, coef_ref):
    t_len = S5_CHUNK
    half = 2 * S5_STATE
    sgn = rows_ref[5:6, :]
    mxu_operand = lambda t: t.astype(BF16).astype(F32)
    a4, bx4, pr4, pi4, g2 = [], [], [], [], []
    for d in range(2):
        f_re = rows_ref[2 * d:2 * d + 1, :]
        f_im = rows_ref[2 * d + 1:2 * d + 2, :]
        p1 = b_ref[d, 0]
        p2 = b_ref[d, 1]
        a4.append(mxu_operand(f_re * p1 + sgn * f_im * p2))
        bx4.append(mxu_operand(f_re * p2 - sgn * f_im * p1))
        pr4.append(pw_ref[d, 0])
        pi4.append(pw_ref[d, 1])
        cx = jnp.concatenate([mxu_operand(c_ref[d, 0])] * t_len, axis=0)
        cy = jnp.concatenate([mxu_operand(c_ref[d, 1])] * t_len, axis=0)

        def ca(order, d=d, cx=cx, cy=cy):
            pr = jnp.concatenate(
                [jnp.broadcast_to(pr4[d][t:t + 1, :half], (S5_GROUP, half)) for t in order], axis=0)
            pi = jnp.concatenate(
                [jnp.broadcast_to(pi4[d][t:t + 1, :half], (S5_GROUP, half)) for t in order], axis=0)
            return cx * pr + cy * pi
        g2.append(ca)

    def strip(d, order):
        return lax.dot_general(a4[d][:, :half], g2[d](order), (((1,), (1,)), ((), ())),
                               precision=lax.Precision.HIGHEST, preferred_element_type=F32)

    row = lax.broadcasted_iota(jnp.int32, (S5_GROUP, S5_CW), 0)
    col = lax.broadcasted_iota(jnp.int32, (S5_GROUP, S5_CW), 1)
    k_fwd = strip(0, range(t_len)) + jnp.where(row == col, rows_ref[4:5, :], 0.0)
    k_bwd = strip(1, [t_len - 1 - k for k in range(t_len)])
    for j in range(t_len):
        lo, hi = j * S5_GROUP, (j + 1) * S5_GROUP
        m = _shift_lanes(k_fwd, j * S5_GROUP) + _shift_lanes(k_bwd, -(t_len - 1 - j) * S5_GROUP)
        w1_ref[lo:hi, 0:S5_CW] = m.astype(w1_ref.dtype)
        tf, tb = t_len - 1 - j, j
        s_f = a4[0] * pr4[0][tf:tf + 1, :] + bx4[0] * (sgn * pi4[0][tf:tf + 1, :])
        s_b = a4[1] * pr4[1][tb:tb + 1, :] + bx4[1] * (sgn * pi4[1][tb:tb + 1, :])
        w1_ref[lo:hi, S5_CW:2 * S5_CW] = s_f.astype(w1_ref.dtype)
        w1_ref[lo:hi, 2 * S5_CW:3 * S5_CW] = s_b.astype(w1_ref.dtype)
    w2_ref[0:half, :] = g2[0]([i + 1 for i in range(t_len)]).T.astype(w2_ref.dtype)
    w2_ref[half:2 * half, :] = g2[1]([t_len - i for i in range(t_len)]).T.astype(w2_ref.dtype)
    zero = jnp.zeros((1, half), F32)
    coef_ref[...] = jnp.concatenate(
        [pr4[0][t_len:t_len + 1, :half], (sgn * pi4[0][t_len:t_len + 1, :])[:, :half],
         pr4[1][t_len:t_len + 1, :half], (sgn * pi4[1][t_len:t_len + 1, :])[:, :half],
         zero, zero, zero, zero], axis=0)


def _s5_kernel(*refs, bsz, ctx_chunks, lat_chunks):
    sw = 2 * S5_STATE
    n_chunks = ctx_chunks + lat_chunks
    shape = (bsz, sw)
    groups = []
    for gi in range(S5_SCAN_GROUPS_PER_STEP):
        (xc_ref, xl_ref, w1_ref, w2_ref, coef_ref, yc_ref, yl_ref,
         zic_ref, zil_ref, zsc_ref, zsl_ref, hc_ref, hl_ref) = (r.at[gi] for r in refs)
        streams = ((xc_ref, zic_ref, zsc_ref, hc_ref, yc_ref, ctx_chunks),
                   (xl_ref, zil_ref, zsl_ref, hl_ref, yl_ref, lat_chunks))
        groups.append((streams, w2_ref, coef_ref))
        w1 = w1_ref[...]
        for x_ref, zi_ref, zs_ref, _, _, n in streams:
            pitch = n + SUBLANES
            per_dot = min(bsz, max(1, S5_DOT_ROWS // n))
            for b0 in range(0, bsz, per_dot):
                zz = jnp.dot(x_ref[b0 * n:(b0 + per_dot) * n, :], w1, preferred_element_type=F32)
                for b in range(b0, b0 + per_dot):
                    z = zz[(b - b0) * n:(b - b0 + 1) * n]
                    zi_ref[b * n:(b + 1) * n, :] = z[:, 0:S5_CW]
                    for k in range(4):
                        zs_ref[k, b * pitch:b * pitch + n, :] = z[:, S5_CW + k * sw:S5_CW + (k + 1) * sw]

    def locate(streams, c):
        (_, _, zsc_ref, hc_ref, _, _), (_, _, zsl_ref, hl_ref, _, _) = streams
        if c < ctx_chunks:
            return zsc_ref, hc_ref, c, ctx_chunks + SUBLANES
        return zsl_ref, hl_ref, c - ctx_chunks, lat_chunks + SUBLANES

    coefs = [[jnp.broadcast_to(coef_ref[k:k + 1, :], shape) for k in range(4)] for _, _, coef_ref in groups]
    state = [[jnp.zeros(shape, F32)] * 4 for _ in groups]
    for t in range(n_chunks):
        cb = ctx_chunks - 1 - t if t < ctx_chunks else n_chunks + ctx_chunks - 1 - t
        for gi, (streams, _, _) in enumerate(groups):
            a1f, a2f, a1b, a2b = coefs[gi]
            hf, hfs, hb, hbs = state[gi]
            zs_ref, h_ref, c, pitch = locate(streams, t)
            rows = pl.ds(c, bsz, stride=pitch)
            h_ref[0, rows, :] = hf
            lf = zs_ref[0, rows, :]
            lfs = zs_ref[1, rows, :]
            hf, hfs = a1f * hf + a2f * hfs + lf, a1f * hfs - a2f * hf + lfs
            zs_ref, h_ref, c, pitch = locate(streams, cb)
            rows = pl.ds(c, bsz, stride=pitch)
            h_ref[1, rows, :] = hb
            lb = zs_ref[2, rows, :]
            lbs = zs_ref[3, rows, :]
            hb, hbs = a1b * hb + a2b * hbs + lb, a1b * hbs - a2b * hb + lbs
            state[gi] = [hf, hfs, hb, hbs]

    for streams, w2_ref, _ in groups:
        w2 = w2_ref[...]
        for _, zi_ref, _, h_ref, y_ref, n in streams:
            pitch = n + SUBLANES
            per_dot = min(bsz, max(1, S5_DOT_ROWS // n))
            for b0 in range(0, bsz, per_dot):
                hin = jnp.concatenate(
                    [jnp.concatenate([h_ref[0, b * pitch:b * pitch + n, :], h_ref[1, b * pitch:b * pitch + n, :]],
                                     axis=1) for b in range(b0, b0 + per_dot)], axis=0)
                rows = slice(b0 * n, (b0 + per_dot) * n)
                y = zi_ref[rows, :] + jnp.dot(hin.astype(BF16), w2, preferred_element_type=F32)
                y_ref[rows, :] = y.astype(y_ref.dtype)


def _s5_scan(x_ctx, x_lat, w1, w2, coef, layer, bsz):
    ctx_chunks = x_ctx.shape[1] // bsz
    lat_chunks = x_lat.shape[1] // bsz
    sw = 2 * S5_STATE
    per = S5_SCAN_GROUPS_PER_STEP
    grp = lambda rows, cols: pl.BlockSpec((per, rows, cols), lambda g: (g, 0, 0))
    opr = lambda rows, cols: pl.BlockSpec((None, per, rows, cols), lambda g: (layer, g, 0, 0))
    slab = lambda k, n: pltpu.VMEM((per, k, bsz * (n + SUBLANES), sw), F32)
    return pl.pallas_call(
        functools.partial(_s5_kernel, bsz=bsz, ctx_chunks=ctx_chunks, lat_chunks=lat_chunks),
        grid=(S5_GROUPS // per,),
        in_specs=[
            grp(bsz * ctx_chunks, S5_CW), grp(bsz * lat_chunks, S5_CW),
            opr(S5_CW, 3 * S5_CW), opr(4 * S5_STATE, S5_CW), opr(SUBLANES, sw),
        ],
        out_specs=[grp(bsz * ctx_chunks, S5_CW), grp(bsz * lat_chunks, S5_CW)],
        out_shape=[jax.ShapeDtypeStruct(x_ctx.shape, F32), jax.ShapeDtypeStruct(x_lat.shape, F32)],
        scratch_shapes=[
            pltpu.VMEM((per, bsz * ctx_chunks, S5_CW), F32), pltpu.VMEM((per, bsz * lat_chunks, S5_CW), F32),
            slab(4, ctx_chunks), slab(4, lat_chunks), slab(2, ctx_chunks), slab(2, lat_chunks),
        ],
        compiler_params=pltpu.CompilerParams(
            dimension_semantics=("parallel",), vmem_limit_bytes=VMEM_LIMIT),
        name="s5_scan",
    )(x_ctx, x_lat, w1, w2, coef)


def _gelu_tanh(x):
    return 0.5 * x * (1.0 + jnp.tanh(math.sqrt(2.0 / math.pi) * (x + 0.044715 * (x * x * x))))


def _silu(x):
    return x * jax.nn.sigmoid(x)


def _merge_kernel(h_ref, mod_ref, npre_ref, win_ref, wa_ref, oa_ref, yb_ref, oc_ref,
                  wglu_ref, bglu_ref, woa_ref, wob_ref, woc_ref, wout_ref, npost_ref, o_ref, stage_ref):
    gate_cols = (None, SEGS["bg"], SEGS["cg"])
    hn = _modulated_norm(h_ref[...], mod_ref, npre_ref)

    def gates(i):
        if i == 0:
            g = jnp.dot(hn, wa_ref[:, A_WIDTH:2 * A_WIDTH], preferred_element_type=F32)
        else:
            off, width = gate_cols[i]
            g = jnp.dot(hn, win_ref[:, off:off + width], preferred_element_type=F32)
        m_lo = SEGS["mg"][0] + i * D_MODEL
        return g, jnp.dot(hn, win_ref[:, m_lo:m_lo + D_MODEL], preferred_element_type=F32)

    def branch(gm, o, w_ref):
        g, m = gm
        t = jnp.dot((o * _silu(g)).astype(BF16), w_ref[...], preferred_element_type=F32)
        return jax.nn.sigmoid(m) * t

    gm = [gates(i) for i in (0, 2, 1)]
    y = branch(gm[0], oa_ref[...], woa_ref) + branch(gm[1], oc_ref[...], woc_ref)
    yb = _gelu_tanh(_groups_to_tokens(yb_ref, stage_ref))
    glu = jnp.dot(yb.astype(BF16), wglu_ref[...], preferred_element_type=F32) + bglu_ref[...]
    y = y + branch(gm[2], yb * jax.nn.sigmoid(glu), wob_ref)
    y = jnp.dot(y.astype(BF16), wout_ref[...], preferred_element_type=F32)
    ms = jnp.mean(y * y, axis=-1, keepdims=True)
    yn = y * lax.rsqrt(ms + EPS) * npost_ref[...]
    o_ref[...] = h_ref[...] + mod_ref[:, 2 * D_MODEL:3 * D_MODEL] * yn


def _merge(h, mod, mod_row, norm_pre, layer, w_bf16, w_a, o_a, y_b, o_c, wts):
    bsz, rows, _ = h.shape
    tile = min(ROW_TILE, rows)
    tiles = rows // tile
    if mod_row is None:
        mod_map = lambda b, i: (b, 0, 0)
    else:
        mod_map = lambda b, i: (mod_row, 0, 0)
    tok = lambda w: pl.BlockSpec((None, tile,w), lambda b, i: (b, i, 0))
    in_specs = [
        tok(D_MODEL),
        pl.BlockSpec((None, 1, 3 * D_MODEL), mod_map),
        _const_spec((1, D_MODEL)), _const_spec((D_MODEL, IN_WIDTH), layer),
        _const_spec((D_MODEL, 2 * A_WIDTH), layer),
        tok(A_WIDTH),
        pl.BlockSpec((S5_GROUPS, tile // S5_CHUNK, S5_CW), lambda b, i: (0, b * tiles + i, 0)),
        tok(C_WIDTH),
        _const_spec((B_WIDTH, B_WIDTH), layer), _const_spec((1, B_WIDTH)),
        _const_spec((A_WIDTH, D_MODEL), layer), _const_spec((B_WIDTH, D_MODEL), layer),
        _const_spec((C_WIDTH, D_MODEL), layer),
        _const_spec((D_MODEL, D_MODEL), layer), _const_spec((1, D_MODEL)),
    ]
    return pl.pallas_call(
        _merge_kernel,
        grid=(bsz, rows // tile),
        in_specs=in_specs,
        out_specs=tok(D_MODEL),
        out_shape=jax.ShapeDtypeStruct((bsz, rows, D_MODEL), F32),
        scratch_shapes=[pltpu.VMEM((B_WIDTH // LANES, (tile // S5_CHUNK) * STAGE_PITCH, LANES), F32)],
        compiler_params=pltpu.CompilerParams(
            dimension_semantics=("parallel", "parallel"), vmem_limit_bytes=VMEM_LIMIT),
        name="merge_out",
    )(h, mod, norm_pre.reshape(1, D_MODEL), w_bf16, w_a, o_a, y_b, o_c, *wts)


def _rope_tables(n_tokens):
    rows = n_tokens // GRID_W
    pos_r = jnp.repeat(jnp.arange(rows, dtype=F32), GRID_W)
    pos_c = jnp.tile(jnp.arange(GRID_W, dtype=F32), rows)
    inv = ROPE_BASE ** (-jnp.arange(AX_FREQS, dtype=F32) / AX_FREQS)
    ang_r = pos_r[:, None] * inv[None]
    ang_c = pos_c[:, None] * inv[None]
    ang = jnp.concatenate([ang_r, ang_r, ang_c, ang_c], axis=-1)
    cos, sin = jnp.cos(ang), jnp.sin(ang)
    first = (jnp.arange(HEAD_DIM) % (2 * AX_FREQS)) < AX_FREQS
    sa = jnp.where(first[None], -sin, 0.0)
    sb = jnp.where(first[None], 0.0, sin)
    tile2 = lambda t: jnp.concatenate([t, t], axis=-1)
    return tile2(cos), tile2(sa), tile2(sb)


def _reorder_a_heads(w, axis):
    axis = axis % w.ndim
    shape = w.shape
    split = shape[:axis] + (A_HEADS, HEAD_DIM) + shape[axis + 1:]
    return jnp.take(w.reshape(split), jnp.array(A_HEAD_ORDER), axis=axis).reshape(shape)


def _layers_as_groups(t):
    t = jnp.swapaxes(t, 0, 1)
    return t.reshape((2, DEPTH * S5_GROUPS) + t.shape[3:])


def kernel(x, c, ctx, c_ctx, w_mod, b_mod, norm_pre, norm_post, w_in, swa_sink, s5_a_re, s5_a_im, s5_log_dt, s5_b_re, s5_b_im, s5_c_re, s5_c_im, s5_d, s5_w_glu, s5_b_glu, diff_lq1, diff_lk1, diff_lq2, diff_lk2, diff_subln, w_o_a, w_o_b, w_o_c, w_out):
    bsz, n_lat, _ = x.shape
    n_ctx = ctx.shape[1]
    ctx_row = bsz
    assert bsz == SUBLANES, "the S5 scan kernel steps one sublane row per batch element"
    cc = jnp.zeros((MOD_ROWS, D_MODEL), F32).at[:bsz].set(c).at[ctx_row].set(c_ctx)
    mod_all = _modulation(cc, w_mod, b_mod)
    rope_tabs = _rope_tables(n_lat)

    cols = lambda name: w_in[..., SEGS[name][0]:SEGS[name][0] + SEGS[name][1]]
    w_in_all = w_in.astype(BF16)
    w_a_all = jnp.concatenate(
        [_reorder_a_heads(cols("aq"), -1), _reorder_a_heads(cols("ag"), -1)], axis=-1).astype(BF16)
    w_glu_all, w_ob_all, w_oc_all, w_out_all = (t.astype(BF16) for t in (s5_w_glu, w_o_b, w_o_c, w_out))
    w_oa_all = _reorder_a_heads(w_o_a, -2).astype(BF16)
    lam_all = jnp.stack([diff_lq1, diff_lk1, diff_lq2, diff_lk2], axis=1).astype(F32)
    s5_ops = _s5_operators(*(_layers_as_groups(t) for t in (s5_a_re, s5_a_im, s5_log_dt, s5_b_re, s5_b_im,
                                                            s5_c_re, s5_c_im)), s5_d.reshape(-1))
    w1_all, w2_all, coef_all = (t.reshape((DEPTH, S5_GROUPS) + t.shape[1:]) for t in s5_ops)

    h_lat, h_ctx = x, ctx
    for l in range(DEPTH):
        last = l == DEPTH - 1
        lam_init = 0.8 - 0.6 * math.exp(-0.3 * l)
        mod = mod_all[l].reshape(MOD_ROWS, 1, 3 * D_MODEL)
        zl = _in_projection(h_lat, mod, None, norm_pre[l], w_in_all, w_a_all, l, PROJ_NAMES, rope_tabs)
        ctx_names = ("ak", "av", "bu", "ck", "cv") if last else PROJ_NAMES
        zc = _in_projection(h_ctx, mod, ctx_row, norm_pre[l], w_in_all, w_a_all, l, ctx_names, None)

        o_a_l = _swa(swa_sink[l], zl["aq"], zc["ak"], zc["av"], zl["ak"], zl["av"])

        y_b_c, y_b_l = _s5_scan(zc["bu"], zl["bu"], w1_all, w2_all, coef_all, l, bsz)

        lam_params = lam_all[l]
        o_c_l = _diff_attention(lam_params, diff_subln[l], lam_init, zl["cq"], zc["ck"], zc["cv"],
                                zl["ck"], zl["cv"])

        wts = (w_glu_all, s5_b_glu[l].reshape(1, B_WIDTH), w_oa_all, w_ob_all, w_oc_all,
               w_out_all, norm_post[l].reshape(1, D_MODEL))
        h_lat_new = _merge(h_lat, mod, None, norm_pre[l], l, w_in_all, w_a_all, o_a_l, y_b_l, o_c_l, wts)
        if not last:
            o_a_c = _swa(swa_sink[l], zc["aq"], zc["ak"], zc["av"])
            o_c_c = _diff_attention(lam_params, diff_subln[l], lam_init, zc["cq"], zc["ck"], zc["cv"])
            h_ctx = _merge(h_ctx, mod, ctx_row, norm_pre[l], l, w_in_all, w_a_all, o_a_c, y_b_c, o_c_c, wts)
        h_lat = h_lat_new
    return h_lat
```
